```python
import jax
import jax.numpy as jnp
from jax import lax
import numpy as np

D_MODEL = 1024
BATCH = 2
SEQ = 16384
DEPTH = 2
DEC_BATCH = 32
DEC_SEQ = 64
PAST_LEN = 2048

CHUNK = 64
HEAD_DIM = 64
H_A = 4
H_B = 4
H_C = 4
H_D = 4
GROUP_W = 256
N_GROUPS = 4
MIX_W = N_GROUPS * GROUP_W
A_LEFT_CHUNKS = 8
A_WINDOW = A_LEFT_CHUNKS * CHUNK
A_BAND = (A_LEFT_CHUNKS + 1) * CHUNK
REL_CLIP = 128
Q_BLOCK = 128
MLA_NOPE = 64
MLA_ROPE = 32
MLA_V = 64
KV_RANK = 128
ROPE_BASE = 10000.0
N_EXP = 32
TOP_K = 4
D_FF = 1024
SWIGLU_LIMIT = 7.0
SWIGLU_ALPHA = 1.702
MOE_BLOCK = 512
PLE_DIM = 256
FORGET_BIAS_MEAN = 2.0
NORM_EPS = 1e-6
NEG_INF = -1e30
N_STATE = 9

A_Q = 0
A_K = A_Q + H_A * HEAD_DIM
A_V = A_K + H_A * HEAD_DIM
B_Q = A_V + H_A * HEAD_DIM
B_K = B_Q + H_B * HEAD_DIM
B_V = B_K + H_B * HEAD_DIM
C_Q = B_V + H_B * HEAD_DIM
C_K = C_Q + H_C * HEAD_DIM
C_V = C_K + H_C * HEAD_DIM
C_F = C_V + H_C * HEAD_DIM
D_Q = C_F + H_C
D_CKV = D_Q + H_D * (MLA_NOPE + MLA_ROPE)
D_KR = D_CKV + KV_RANK
IN_COLS = D_KR + MLA_ROPE

kernel_name = 'hybrid_streaming_encoder_step'


def rms_norm(x, g):
    xf = x.astype(jnp.float32)
    y = xf * lax.rsqrt(jnp.mean(xf * xf, axis=-1, keepdims=True) + NORM_EPS)
    return (y * g.astype(jnp.float32)).astype(x.dtype)


def apply_rope(x, pos):
    half = x.shape[-1] // 2
    inv = ROPE_BASE ** (-jnp.arange(half, dtype=jnp.float32) / half)
    ang = pos.astype(jnp.float32)[:, None] * inv
    ang = ang.reshape(ang.shape[:1] + (1,) * (x.ndim - 3) + (half,))
    cos, sin = jnp.cos(ang), jnp.sin(ang)
    x1 = x[..., :half].astype(jnp.float32)
    x2 = x[..., half:].astype(jnp.float32)
    return jnp.concatenate([x1 * cos - x2 * sin, x2 * cos + x1 * sin], axis=-1).astype(x.dtype)


def project_groups(hn, pos, w_in, b_forget, qk_gain, rope_gain, kv_gain):
    nb, s, _ = hn.shape
    z = hn @ w_in

    def heads(lo, n, d):
        return z[..., lo:lo + n * d].reshape(nb, s, n, d)

    grp_a = (rms_norm(heads(A_Q, H_A, HEAD_DIM), qk_gain[0]),
             rms_norm(heads(A_K, H_A, HEAD_DIM), qk_gain[1]),
             heads(A_V, H_A, HEAD_DIM))
    grp_b = (heads(B_Q, H_B, HEAD_DIM), heads(B_K, H_B, HEAD_DIM), heads(B_V, H_B, HEAD_DIM))
    c_logf = jax.nn.log_sigmoid(z[..., C_F:C_F + H_C].astype(jnp.float32) + b_forget.astype(jnp.float32))
    grp_c = (rms_norm(heads(C_Q, H_C, HEAD_DIM), qk_gain[2]),
             rms_norm(heads(C_K, H_C, HEAD_DIM), qk_gain[3]),
             heads(C_V, H_C, HEAD_DIM), c_logf)
    dq = heads(D_Q, H_D, MLA_NOPE + MLA_ROPE)
    d_qn = rms_norm(dq[..., :MLA_NOPE], qk_gain[4])
    d_qr = apply_rope(rms_norm(dq[..., MLA_NOPE:], rope_gain[0]), pos)
    d_ckv = rms_norm(z[..., D_CKV:D_CKV + KV_RANK], kv_gain)
    d_kr = apply_rope(rms_norm(z[..., D_KR:D_KR + MLA_ROPE], rope_gain[1]), pos)
    return grp_a, grp_b, grp_c, (d_qn, d_qr, d_ckv, d_kr)


def mla_expand(ckv, w_uk, w_uv, k_gain):
    nb, s, _ = ckv.shape
    k_nope = rms_norm((ckv @ w_uk).reshape(nb, s, H_D, MLA_NOPE), k_gain)
    v = (ckv @ w_uv).reshape(nb, s, H_D, MLA_V)
    return k_nope, v


def band_attention_prompt(q, k, v, rel_bias):
    nb, s, h, dh = q.shape
    nc = s // CHUNK

    def chunks(a):
        return a.reshape(nb, nc, CHUNK, h, dh)

    pad = jnp.zeros((nb, A_LEFT_CHUNKS, CHUNK, h, dh), k.dtype)
    kp = jnp.concatenate([pad, chunks(k)], axis=1)
    vp = jnp.concatenate([pad, chunks(v)], axis=1)
    kb = jnp.concatenate([kp[:, o:o + nc] for o in range(A_LEFT_CHUNKS + 1)], axis=2)
    vb = jnp.concatenate([vp[:, o:o + nc] for o in range(A_LEFT_CHUNKS + 1)], axis=2)
    sc = jnp.einsum('bnqhd,bnkhd->bnhqk', chunks(q), kb).astype(jnp.float32) * (HEAD_DIM ** -0.5)
    m = jnp.arange(A_BAND)
    rel = jnp.clip(jnp.arange(CHUNK)[:, None] + A_WINDOW - m[None, :], -REL_CLIP, REL_CLIP) + REL_CLIP
    sc = sc + rel_bias.astype(jnp.float32)[:, rel]
    ok = (jnp.arange(nc)[:, None] - A_LEFT_CHUNKS + m[None, :] // CHUNK) >= 0
    p = jax.nn.softmax(jnp.where(ok[None, :, None, None, :], sc, NEG_INF), axis=-1)
    o = jnp.einsum('bnhqk,bnkhd->bnqhd', p, vb.astype(jnp.float32))
    return o.reshape(nb, s, h, dh)


def band_attention(q, k, v, qpos, kpos, rel_bias):
    sc = jnp.einsum('bqhd,bkhd->bhqk', q, k).astype(jnp.float32) * (HEAD_DIM ** -0.5)
    rel = jnp.clip(qpos[:, None] - kpos[None, :], -REL_CLIP, REL_CLIP) + REL_CLIP
    sc = sc + rel_bias.astype(jnp.float32)[:, rel]
    qc, kc = qpos // CHUNK, kpos // CHUNK
    ok = (kc[None, :] <= qc[:, None]) & (kc[None, :] >= qc[:, None] - A_LEFT_CHUNKS)
    p = jax.nn.softmax(jnp.where(ok, sc, NEG_INF), axis=-1)
    return jnp.einsum('bhqk,bkhd->bqhd', p, v.astype(jnp.float32))


def stick_breaking_attention(q, k, v, qpos, kpos):
    z = jnp.einsum('bqhd,bkhd->bhqk', q, k).astype(jnp.float32) * (HEAD_DIM ** -0.5)
    vis = kpos[None, :] < qpos[:, None]
    log_beta = jax.nn.log_sigmoid(z)
    log_rest = jnp.where(vis, jax.nn.log_sigmoid(-z), 0.0)
    between = lax.cumsum(log_rest, axis=3, reverse=True) - log_rest
    w = jnp.where(vis, jnp.exp(log_beta + between), 0.0)
    return jnp.einsum('bhqk,bkhd->bqhd', w, v.astype(jnp.float32))


def forgetting_attention(q, fq, k, v, fk, qpos, kpos):
    sc = jnp.einsum('bqhd,bkhd->bhqk', q, k).astype(jnp.float32) * (HEAD_DIM ** -0.5)
    sc = sc + jnp.swapaxes(fq, 1, 2)[:, :, :, None] - jnp.swapaxes(fk, 1, 2)[:, :, None, :]
    ok = kpos[None, :] <= qpos[:, None]
    p = jax.nn.softmax(jnp.where(ok, sc, NEG_INF), axis=-1)
    return jnp.einsum('bhqk,bkhd->bqhd', p, v.astype(jnp.float32))


def mla_attention(qn, qr, kn, kr, v, qpos, kpos):
    sc = (jnp.einsum('bqhd,bkhd->bhqk', qn, kn) + jnp.einsum('bqhr,bkr->bhqk', qr, kr)).astype(jnp.float32)
    sc = sc * ((MLA_NOPE + MLA_ROPE) ** -0.5)
    ok = (kpos // CHUNK)[None, :] <= (qpos // CHUNK)[:, None]
    p = jax.nn.softmax(jnp.where(ok, sc, NEG_INF), axis=-1)
    return jnp.einsum('bhqk,bkhd->bqhd', p, v.astype(jnp.float32))


def sweep_query_blocks(fn, q_side):
    nb, s = q_side[0].shape[:2]
    n_blk = s // Q_BLOCK
    blocked = tuple(a.reshape((nb, n_blk, Q_BLOCK) + a.shape[2:]).swapaxes(0, 1) for a in q_side)

    def body(args):
        blk, qs = args
        return fn(qs, blk * Q_BLOCK + jnp.arange(Q_BLOCK))

    out = lax.map(body, (jnp.arange(n_blk), blocked))
    return out.swapaxes(0, 1).reshape((nb, s) + out.shape[3:])


def prompt_mixers(hn, w_in, b_forget, qk_gain, rope_gain, kv_gain, w_uk, w_uv, rel_bias):
    nb, s, _ = hn.shape
    pos = jnp.arange(s)
    (aq, ak, av), (bq, bk, bv), (cq, ck, cv, clf), (dqn, dqr, dckv, dkr) = project_groups(
        hn, pos, w_in, b_forget, qk_gain, rope_gain, kv_gain)
    o_a = band_attention_prompt(aq, ak, av, rel_bias)
    o_b = sweep_query_blocks(lambda qs, qp: stick_breaking_attention(qs[0], bk, bv, qp, pos), (bq,))
    c_cum = jnp.cumsum(clf, axis=1)
    o_c = sweep_query_blocks(lambda qs, qp: forgetting_attention(qs[0], qs[1], ck, cv, c_cum, qp, pos), (cq, c_cum))
    dkn, dv = mla_expand(dckv, w_uk, w_uv, qk_gain[5])
    o_d = sweep_query_blocks(lambda qs, qp: mla_attention(qs[0], qs[1], dkn, dkr, dv, qp, pos), (dqn, dqr))
    keep = min(A_WINDOW, s)
    state = (ak[:, s - keep:], av[:, s - keep:], bk, bv, ck, cv, clf, dckv, dkr)
    return (o_a, o_b, o_c, o_d), state


def sample_mixers(hn, a_k, a_v, b_k, b_v, c_k, c_v, c_lf, d_ckv, d_kr,
                  w_in, b_forget, qk_gain, rope_gain, kv_gain, w_uk, w_uv, rel_bias):
    nb, t, _ = hn.shape
    past = b_k.shape[1]
    win = a_k.shape[1]
    qpos = past + jnp.arange(t)
    kpos = jnp.arange(past + t)
    (aq, ak, av), (bq, bk, bv), (cq, ck, cv, clf), (dqn, dqr, dckv, dkr) = project_groups(
        hn, qpos, w_in, b_forget, qk_gain, rope_gain, kv_gain)
    ka = jnp.concatenate([a_k, ak], axis=1)
    va = jnp.concatenate([a_v, av], axis=1)
    o_a = band_attention(aq, ka, va, qpos, past - win + jnp.arange(win + t), rel_bias)
    o_b = stick_breaking_attention(bq, jnp.concatenate([b_k, bk], axis=1), jnp.concatenate([b_v, bv], axis=1), qpos, kpos)
    c_cum = jnp.cumsum(jnp.concatenate([c_lf.astype(jnp.float32), clf], axis=1), axis=1)
    o_c = forgetting_attention(cq, c_cum[:, past:], jnp.concatenate([c_k, ck], axis=1),
                               jnp.concatenate([c_v, cv], axis=1), c_cum, qpos, kpos)
    dkn, dv = mla_expand(jnp.concatenate([d_ckv, dckv], axis=1), w_uk, w_uv, qk_gain[5])
    o_d = mla_attention(dqn, dqr, dkn, jnp.concatenate([d_kr, dkr], axis=1), dv, qpos, kpos)
    state = (ka[:, t:], va[:, t:], bk, bv, ck, cv, clf, dckv, dkr)
    return (o_a, o_b, o_c, o_d), state


def moe_ffn(h, w_router, b_router, w_up, b_up, w_down, b_down):
    n_tok, d = h.shape
    logits = (h @ w_router + b_router).astype(jnp.float32)
    top_v, top_i = lax.top_k(logits, TOP_K)
    gates = jax.nn.softmax(top_v, axis=-1)
    n = n_tok * TOP_K
    e_flat = top_i.reshape(-1)
    tok_flat = jnp.arange(n) // TOP_K
    order = jnp.argsort(e_flat)
    e_sorted = e_flat[order]
    counts = jnp.bincount(e_flat, length=N_EXP)
    start = jnp.cumsum(counts) - counts
    padded = (counts + MOE_BLOCK - 1) // MOE_BLOCK * MOE_BLOCK
    p_end = jnp.cumsum(padded)
    p_start = p_end - padded
    dest = p_start[e_sorted] + jnp.arange(n) - start[e_sorted]
    n_blk = -(-n // MOE_BLOCK) + N_EXP
    rows = n_blk * MOE_BLOCK
    row_tok = jnp.full((rows,), n_tok, jnp.int32).at[dest].set(tok_flat[order].astype(jnp.int32))
    row_gate = jnp.zeros((rows,), jnp.float32).at[dest].set(gates.reshape(-1)[order])
    blk_exp = jnp.clip(jnp.searchsorted(p_end, jnp.arange(n_blk) * MOE_BLOCK, side='right'), 0, N_EXP - 1)
    h_pad = jnp.concatenate([h, jnp.zeros((1, d), h.dtype)], axis=0)

    def expert_block(args):
        toks, gate, e = args
        u = h_pad[toks] @ w_up[e] + b_up[e]
        glu = jnp.minimum(u[:, :D_FF], SWIGLU_LIMIT)
        lin = jnp.clip(u[:, D_FF:], -SWIGLU_LIMIT, SWIGLU_LIMIT)
        act = glu * jax.nn.sigmoid(SWIGLU_ALPHA * glu) * (lin + 1.0)
        y = act @ w_down[e] + b_down[e]
        return (y.astype(jnp.float32) * gate[:, None]).astype(h.dtype)

    y = lax.map(expert_block, (row_tok.reshape(n_blk, MOE_BLOCK), row_gate.reshape(n_blk, MOE_BLOCK), blk_exp))
    return jax.ops.segment_sum(y.reshape(rows, d), row_tok, num_segments=n_tok + 1)[:n_tok]


def finish_layer(h, outs, p, group_gain, w_out, norm_ffn, w_router, b_router, w_up, b_up, w_down, b_down,
                 norm_ple, w_ple_gate, w_ple_proj):
    nb, s, d = h.shape
    parts = [rms_norm(o.reshape(nb, s, GROUP_W), group_gain[g]).astype(h.dtype) for g, o in enumerate(outs)]
    h = h + jnp.concatenate(parts, axis=-1) @ w_out
    hn = rms_norm(h, norm_ffn)
    h = h + moe_ffn(hn.reshape(nb * s, d), w_router, b_router, w_up, b_up, w_down, b_down).reshape(nb, s, d)
    gate = jax.nn.sigmoid(rms_norm(h, norm_ple) @ w_ple_gate)
    return h + gate * (p @ w_ple_proj)


def setup_inputs(seed: int = 0) -> dict:
    key = jax.random.key(seed)
    ks = iter(jax.random.split(key, 40))

    def nrm(shape, scale=1.0):
        return scale * jax.random.normal(next(ks), shape, jnp.float32)

    def gain(shape):
        return 1.0 + nrm(shape, 0.05)

    a_win = min(A_WINDOW, PAST_LEN)
    return {
        'x_prompt': nrm((BATCH, SEQ, D_MODEL)),
        'x_sample': nrm((DEC_BATCH, DEC_SEQ, D_MODEL)),
        'p_prompt': nrm((DEPTH, BATCH, SEQ, PLE_DIM)),
        'p_sample': nrm((DEPTH, DEC_BATCH, DEC_SEQ, PLE_DIM)),
        'cache_a_k': nrm((DEPTH, DEC_BATCH, a_win, H_A, HEAD_DIM)),
        'cache_a_v': nrm((DEPTH, DEC_BATCH, a_win, H_A, HEAD_DIM)),
        'cache_b_k': nrm((DEPTH, DEC_BATCH, PAST_LEN, H_B, HEAD_DIM)),
        'cache_b_v': nrm((DEPTH, DEC_BATCH, PAST_LEN, H_B, HEAD_DIM)),
        'cache_c_k': nrm((DEPTH, DEC_BATCH, PAST_LEN, H_C, HEAD_DIM)),
        'cache_c_v': nrm((DEPTH, DEC_BATCH, PAST_LEN, H_C, HEAD_DIM)),
        'cache_c_logf': jax.nn.log_sigmoid(FORGET_BIAS_MEAN + nrm((DEPTH, DEC_BATCH, PAST_LEN, H_C))),
        'cache_d_ckv': nrm((DEPTH, DEC_BATCH, PAST_LEN, KV_RANK)),
        'cache_d_krope': nrm((DEPTH, DEC_BATCH, PAST_LEN, MLA_ROPE)),
        'norm_mix': gain((DEPTH, D_MODEL)),
        'w_in': nrm((DEPTH, D_MODEL, IN_COLS), D_MODEL ** -0.5),
        'b_forget': FORGET_BIAS_MEAN + nrm((DEPTH, H_C), 0.1),
        'qk_gain': gain((DEPTH, 6, HEAD_DIM)),
        'rope_gain': gain((DEPTH, 2, MLA_ROPE)),
        'kv_gain': gain((DEPTH, KV_RANK)),
        'w_uk': nrm((DEPTH, KV_RANK, H_D * MLA_NOPE), KV_RANK ** -0.5),
        'w_uv': nrm((DEPTH, KV_RANK, H_D * MLA_V), KV_RANK ** -0.5),
        'rel_bias': nrm((DEPTH, H_A, 2 * REL_CLIP + 1), 0.1),
        'group_gain': gain((DEPTH, N_GROUPS, GROUP_W)),
        'w_out': nrm((DEPTH, MIX_W, D_MODEL), MIX_W ** -0.5),
        'norm_ffn': gain((DEPTH, D_MODEL)),
        'w_router': nrm((DEPTH, D_MODEL, N_EXP), D_MODEL ** -0.5),
        'b_router': nrm((DEPTH, N_EXP), 0.01),
        'w_up': nrm((DEPTH, N_EXP, D_MODEL, 2 * D_FF), D_MODEL ** -0.5),
        'b_up': nrm((DEPTH, N_EXP, 2 * D_FF), 0.01),
        'w_down': nrm((DEPTH, N_EXP, D_FF, D_MODEL), D_FF ** -0.5),
        'b_down': nrm((DEPTH, N_EXP, D_MODEL), 0.01),
        'norm_ple': gain((DEPTH, D_MODEL)),
        'w_ple_gate': nrm((DEPTH, D_MODEL, D_MODEL), D_MODEL ** -0.5),
        'w_ple_proj': nrm((DEPTH, PLE_DIM, D_MODEL), PLE_DIM ** -0.5),
    }


def reference(x_prompt, x_sample, p_prompt, p_sample, cache_a_k, cache_a_v, cache_b_k, cache_b_v,
              cache_c_k, cache_c_v, cache_c_logf, cache_d_ckv, cache_d_krope,
              norm_mix, w_in, b_forget, qk_gain, rope_gain, kv_gain, w_uk, w_uv, rel_bias,
              group_gain, w_out, norm_ffn, w_router, b_router, w_up, b_up, w_down, b_down,
              norm_ple, w_ple_gate, w_ple_proj):
    hp, hs = x_prompt, x_sample
    st_p, st_s = [], []
    for i in range(DEPTH):
        outs_p, new_p = prompt_mixers(rms_norm(hp, norm_mix[i]), w_in[i], b_forget[i], qk_gain[i], rope_gain[i],
                                      kv_gain[i], w_uk[i], w_uv[i], rel_bias[i])
        hp = finish_layer(hp, outs_p, p_prompt[i], group_gain[i], w_out[i], norm_ffn[i], w_router[i], b_router[i],
                          w_up[i], b_up[i], w_down[i], b_down[i], norm_ple[i], w_ple_gate[i], w_ple_proj[i])
        outs_s, new_s = sample_mixers(rms_norm(hs, norm_mix[i]), cache_a_k[i], cache_a_v[i], cache_b_k[i],
                                      cache_b_v[i], cache_c_k[i], cache_c_v[i], cache_c_logf[i], cache_d_ckv[i],
                                      cache_d_krope[i], w_in[i], b_forget[i], qk_gain[i], rope_gain[i],
                                      kv_gain[i], w_uk[i], w_uv[i], rel_bias[i])
        hs = finish_layer(hs, outs_s, p_sample[i], group_gain[i], w_out[i], norm_ffn[i], w_router[i], b_router[i],
                          w_up[i], b_up[i], w_down[i], b_down[i], norm_ple[i], w_ple_gate[i], w_ple_proj[i])
        st_p.append(new_p)
        st_s.append(new_s)
    (new_a_k_prompt, new_a_v_prompt, new_b_k_prompt, new_b_v_prompt, new_c_k_prompt, new_c_v_prompt,
     new_c_logf_prompt, new_d_ckv_prompt, new_d_krope_prompt) = [jnp.stack([s[j] for s in st_p]) for j in range(N_STATE)]
    (new_a_k_sample, new_a_v_sample, new_b_k_sample, new_b_v_sample, new_c_k_sample, new_c_v_sample,
     new_c_logf_sample, new_d_ckv_sample, new_d_krope_sample) = [jnp.stack([s[j] for s in st_s]) for j in range(N_STATE)]
    return (hp, hs,
            new_a_k_prompt, new_a_v_prompt, new_b_k_prompt, new_b_v_prompt, new_c_k_prompt, new_c_v_prompt,
            new_c_logf_prompt, new_d_ckv_prompt, new_d_krope_prompt,
            new_a_k_sample, new_a_v_sample, new_b_k_sample, new_b_v_sample, new_c_k_sample, new_c_v_sample,
            new_c_logf_sample, new_d_ckv_sample, new_d_krope_sample)
```

```python
import functools

import numpy as np
import jax
import jax.numpy as jnp
from jax import lax
from jax.experimental import pallas as pl
from jax.experimental.pallas import tpu as pltpu

F32 = jnp.float32
BF16 = jnp.bfloat16

CHUNK = 64
HEAD_DIM = 64
N_HEADS = 4
GROUP_W = 256
A_LEFT_CHUNKS = 8
A_WINDOW = A_LEFT_CHUNKS * CHUNK
REL_CLIP = 128
MLA_NOPE = 64
MLA_ROPE = 32
KV_RANK = 128
ROPE_BASE = 10000.0
N_EXP = 32
TOP_K = 4
D_FF = 1024
SWIGLU_LIMIT = 7.0
SWIGLU_ALPHA = 1.702
MOE_BLOCK = 512
NORM_EPS = 1e-6
NEG_INF = -1e30
LOG2E = 1.4426950408889634

A_Q = 0
C_F = 2304
D_Q = 2308
D_CKV = D_Q + N_HEADS * (MLA_NOPE + MLA_ROPE)
D_KR = D_CKV + KV_RANK

LANE = 128
SEG_ABC = 0
SEG_DQ = 2304
SEG_CKV = 2816
SEG_MISC = 2944
N_COLS = 3072
ROPE_LO = 64
ROPE_HALF = MLA_ROPE // 2

VMEM_LIMIT = 56 * 1024 * 1024

PROJ_ROWS = 512
ATTN_TQ = 512
ATTN_TK = 512
STICK_KS = 256
DENSE_ROWS = 512


def _cparams(n_axes):
    return pltpu.CompilerParams(dimension_semantics=("arbitrary",) * n_axes,
                                vmem_limit_bytes=VMEM_LIMIT)


def _nt_dot(a, b):
    return lax.dot_general(a, b, (((1,), (1,)), ((), ())), preferred_element_type=F32)


def _dot(a, b):
    return jnp.dot(a, b, preferred_element_type=F32)


def _split2(x):
    hi = x.astype(BF16)
    lo = (x - hi.astype(F32)).astype(BF16)
    return hi, lo


def _split3(x):
    hi = x.astype(BF16)
    r = x - hi.astype(F32)
    mid = r.astype(BF16)
    lo = (r - mid.astype(F32)).astype(BF16)
    return hi, mid, lo


def _in_col_map():
    cols = np.full((N_COLS,), -1, np.int64)
    cols[0:2304] = np.arange(2304)
    for h in range(N_HEADS):
        base = D_Q + (MLA_NOPE + MLA_ROPE) * h
        cols[SEG_DQ + LANE * h: SEG_DQ + LANE * h + MLA_NOPE + MLA_ROPE] = base + np.arange(MLA_NOPE + MLA_ROPE)
    cols[SEG_CKV:SEG_CKV + KV_RANK] = D_CKV + np.arange(KV_RANK)
    cols[SEG_MISC:SEG_MISC + N_HEADS] = C_F + np.arange(N_HEADS)
    cols[SEG_MISC + ROPE_LO:SEG_MISC + ROPE_LO + MLA_ROPE] = D_KR + np.arange(MLA_ROPE)
    return cols


def _rope_tables(pos):
    inv = ROPE_BASE ** (-jnp.arange(ROPE_HALF, dtype=F32) / ROPE_HALF)
    ang = pos.astype(F32)[:, None] * inv
    cos, sin = jnp.cos(ang), jnp.sin(ang)
    n = pos.shape[0]
    one = jnp.ones((n, ROPE_LO), F32)
    z16 = jnp.zeros((n, ROPE_HALF), F32)
    z64 = jnp.zeros((n, ROPE_LO), F32)
    z32 = jnp.zeros((n, LANE - ROPE_LO - MLA_ROPE), F32)
    cos_t = jnp.concatenate([one, cos, cos, z32 + 1.0], axis=1)
    sin_a = jnp.concatenate([z64, -sin, z16, z32], axis=1)
    sin_b = jnp.concatenate([z64, z16, sin, z32], axis=1)
    return cos_t, sin_a, sin_b


def _rope(y, cos_t, sin_a, sin_b):
    left = pltpu.roll(y, LANE - ROPE_HALF, axis=1)
    right = pltpu.roll(y, ROPE_HALF, axis=1)
    return y * cos_t + left * sin_a + right * sin_b


def _head_norm(x, m_ref, gain):
    hi, lo = _split2(x * x)
    ssq = _dot(hi, m_ref[...]) + _dot(lo, m_ref[...])
    return x * lax.rsqrt(ssq * (1.0 / HEAD_DIM) + NORM_EPS) * gain


def _store_padded_q(q_ref, q, scale):
    lane = lax.broadcasted_iota(jnp.int32, (q.shape[0], LANE), 1)
    low = lane < HEAD_DIM
    qs = q * scale
    for h in range(N_HEADS):
        pair = qs[:, LANE * (h // 2): LANE * (h // 2) + LANE]
        keep = low if h % 2 == 0 else jnp.logical_not(low)
        q_ref[:, LANE * h: LANE * h + LANE] = jnp.where(keep, pair, 0.0).astype(BF16)


def _proj_kernel(x_ref, gmix_ref, w_ref, cos_ref, sa_ref, sb_ref, gv_ref, m64_ref, wuk_ref, wuv_ref,
                 ak_ref, av_ref, bk_ref, bv_ref, ck_ref, cv_ref, ckv_ref, misc_ref,
                 qa_ref, ka_ref, va_ref, qb_ref, kb_ref, vb_ref, qc_ref, kc_ref, vc_ref,
                 qd_ref, kd_ref, vd_ref):
    x = x_ref[...]
    ms = jnp.mean(x * x, axis=-1, keepdims=True)
    hn = (x * lax.rsqrt(ms + NORM_EPS) * gmix_ref[...]).astype(BF16)
    sm_scale = (HEAD_DIM ** -0.5) * LOG2E
    rows = x.shape[0]
    lane = lax.broadcasted_iota(jnp.int32, (rows, LANE), 1)
    is_nope = lane < MLA_NOPE
    is_rope = jnp.logical_and(lane >= ROPE_LO, lane < ROPE_LO + MLA_ROPE)
    cos_t, sin_a, sin_b = cos_ref[...], sa_ref[...], sb_ref[...]

    z = _dot(hn, w_ref[:, 0:768])
    aq = _head_norm(z[:, 0:256], m64_ref, gv_ref[0:1, :])
    ak = _head_norm(z[:, 256:512], m64_ref, gv_ref[1:2, :])
    av = z[:, 512:768]
    ak_ref[...] = ak
    av_ref[...] = av
    _store_padded_q(qa_ref, aq, sm_scale)
    ka_ref[...] = ak.astype(BF16)
    va_ref[...] = av.astype(BF16)

    z = _dot(hn, w_ref[:, 768:1536])
    bk_ref[...] = z[:, 256:512]
    bv_ref[...] = z[:, 512:768]
    _store_padded_q(qb_ref, z[:, 0:256], sm_scale)
    kb_ref[...] = z[:, 256:512].astype(BF16)
    vb_ref[...] = z[:, 512:768].astype(BF16)

    z = _dot(hn, w_ref[:, 1536:2304])
    cq = _head_norm(z[:, 0:256], m64_ref, gv_ref[2:3, :])
    ck = _head_norm(z[:, 256:512], m64_ref, gv_ref[3:4, :])
    cv = z[:, 512:768]
    ck_ref[...] = ck
    cv_ref[...] = cv
    _store_padded_q(qc_ref, cq, sm_scale)
    kc_ref[...] = ck.astype(BF16)
    vc_ref[...] = cv.astype(BF16)

    z = _dot(hn, w_ref[:, SEG_CKV:N_COLS])
    zc = z[:, 0:KV_RANK]
    ckv = zc * lax.rsqrt(jnp.mean(zc * zc, axis=-1, keepdims=True) + NORM_EPS) * gv_ref[7:8, 0:LANE]
    ckv_ref[...] = ckv
    zm = z[:, KV_RANK:2 * KV_RANK]
    ssr = jnp.sum(jnp.where(is_rope, zm * zm, 0.0), axis=-1, keepdims=True)
    kr = zm * lax.rsqrt(ssr * (1.0 / MLA_ROPE) + NORM_EPS) * gv_ref[6:7, 0:LANE]
    kr = _rope(kr, cos_t, sin_a, sin_b)
    zf = zm + gv_ref[8:9, 0:LANE]
    clf = jnp.minimum(zf, 0.0) - jnp.log1p(jnp.exp(-jnp.abs(zf)))
    misc_ref[...] = jnp.where(lane < N_HEADS, clf, kr)

    ckv_b = ckv.astype(BF16)
    kn = _dot(ckv_b, wuk_ref[...])
    vd_ref[...] = _dot(ckv_b, wuv_ref[...]).astype(BF16)
    for h in range(N_HEADS):
        xh = kn[:, LANE * h: LANE * h + LANE]
        ss = jnp.sum(xh * xh, axis=-1, keepdims=True)
        yh = xh * lax.rsqrt(ss * (1.0 / MLA_NOPE) + NORM_EPS) * gv_ref[5:6, 0:LANE]
        kd_ref[:, LANE * h: LANE * h + LANE] = (yh + kr).astype(BF16)

    z = _dot(hn, w_ref[:, SEG_DQ:SEG_CKV])
    d_scale = ((MLA_NOPE + MLA_ROPE) ** -0.5) * LOG2E
    for h in range(N_HEADS):
        xh = z[:, LANE * h: LANE * h + LANE]
        x2 = xh * xh
        ssn = jnp.sum(jnp.where(is_nope, x2, 0.0), axis=-1, keepdims=True)
        ssr = jnp.sum(jnp.where(is_rope, x2, 0.0), axis=-1, keepdims=True)
        rn = lax.rsqrt(ssn * (1.0 / MLA_NOPE) + NORM_EPS)
        rr = lax.rsqrt(ssr * (1.0 / MLA_ROPE) + NORM_EPS)
        yh = xh * jnp.where(is_nope, rn, rr) * gv_ref[4:5, 0:LANE]
        yh = _rope(yh, cos_t, sin_a, sin_b)
        qd_ref[:, LANE * h: LANE * h + LANE] = (yh * d_scale).astype(BF16)


def _proj_weights(w_in, b_forget, qk_gain, rope_gain, kv_gain, w_uk, w_uv):
    cols = _in_col_map()
    valid = jnp.asarray(cols >= 0)
    w = jnp.where(valid[None, :], w_in[:, np.maximum(cols, 0)], 0.0).astype(BF16)

    def tile4(g):
        return jnp.tile(g, N_HEADS)

    gv = jnp.zeros((16, GROUP_W), F32)
    gv = gv.at[0].set(tile4(qk_gain[0])).at[1].set(tile4(qk_gain[1]))
    gv = gv.at[2].set(tile4(qk_gain[2])).at[3].set(tile4(qk_gain[3]))
    gv = gv.at[4, 0:MLA_NOPE].set(qk_gain[4]).at[4, ROPE_LO:ROPE_LO + MLA_ROPE].set(rope_gain[0])
    gv = gv.at[5, 0:MLA_NOPE].set(qk_gain[5])
    gv = gv.at[6, ROPE_LO:ROPE_LO + MLA_ROPE].set(rope_gain[1])
    gv = gv.at[7, 0:KV_RANK].set(kv_gain)
    gv = gv.at[8, 0:N_HEADS].set(b_forget)
    head = np.arange(GROUP_W) // HEAD_DIM
    m64 = jnp.asarray((head[:, None] == head[None, :]).astype(np.float32), BF16)
    wuk = jnp.zeros((KV_RANK, N_HEADS * LANE), F32)
    for h in range(N_HEADS):
        wuk = wuk.at[:, LANE * h: LANE * h + MLA_NOPE].set(w_uk[:, MLA_NOPE * h: MLA_NOPE * (h + 1)])
    return w, gv, m64, wuk.astype(BF16), w_uv.astype(BF16)


def _project(x, gmix, pw, tables, n_tab_blocks):
    w, gv, m64, wuk, wuv = pw
    t, d = x.shape
    ts = PROJ_ROWS
    assert t % ts == 0
    row = lambda i: (i, 0)
    full = lambda i: (0, 0)
    tab = lambda i: (i % n_tab_blocks, 0)
    f32_w = [GROUP_W] * 6 + [KV_RANK, LANE]
    bf_w = [512, 256, 256, 512, 256, 256, 512, 256, 256, 512, 512, 256]
    out_shape = [jax.ShapeDtypeStruct((t, c), F32) for c in f32_w] + \
                [jax.ShapeDtypeStruct((t, c), BF16) for c in bf_w]
    out_specs = [pl.BlockSpec((ts, c), row) for c in f32_w + bf_w]
    in_specs = [pl.BlockSpec((ts, d), row), pl.BlockSpec((1, d), full), pl.BlockSpec((d, N_COLS), full),
                pl.BlockSpec((ts, LANE), tab), pl.BlockSpec((ts, LANE), tab), pl.BlockSpec((ts, LANE), tab),
                pl.BlockSpec(gv.shape, full), pl.BlockSpec(m64.shape, full),
                pl.BlockSpec(wuk.shape, full), pl.BlockSpec(wuv.shape, full)]
    return pl.pallas_call(
        _proj_kernel, grid=(t // ts,), in_specs=in_specs, out_specs=out_specs, out_shape=out_shape,
        compiler_params=_cparams(1), name="proj",
    )(x, gmix.reshape(1, d), w, *tables, gv, m64, wuk, wuv)


def _softmax_step(q, kb, vb, carry, bias=None, ok=None):
    m, l, acc = carry
    s = _nt_dot(q, kb)
    if bias is not None:
        s = s + bias
    if ok is not None:
        s = jnp.where(ok, s, NEG_INF)
    m_new = jnp.maximum(m, jnp.max(s, axis=1, keepdims=True))
    alpha = jnp.exp2(m - m_new)
    p = jnp.exp2(s - m_new)
    l = alpha * l + jnp.sum(p, axis=1, keepdims=True)
    acc = alpha * acc + _dot(p.astype(BF16), vb)
    return m_new, l, acc


def _softmax_init(tq):
    return (jnp.full((tq, 1), NEG_INF, F32), jnp.zeros((tq, 1), F32), jnp.zeros((tq, LANE), F32))


def _pick_lane(block, h):
    col = lax.broadcasted_iota(jnp.int32, block.shape, 1)
    return jnp.sum(jnp.where(col == h, block, 0.0), axis=1, keepdims=True)


def _flash_softmax_kernel(*refs, tq, tk, chunk_mask, with_decay):
    if with_decay:
        q_ref, k_ref, v_ref, fq_ref, fk_ref, o_ref = refs
    else:
        q_ref, k_ref, v_ref, o_ref = refs
    h = pl.program_id(1)
    qi = pl.program_id(2)
    q = q_ref[0]
    if with_decay:
        fq = _pick_lane(fq_ref[0], h) * LOG2E

    def step(j, carry, diag):
        off = pl.multiple_of(j * tk, tk)
        kb = k_ref[0, pl.ds(off, tk), :]
        vb = v_ref[0, pl.ds(off, tk), :]
        bias = None
        if with_decay:
            bias = fq - fk_ref[0, 0, pl.ds(j, 1), :] * LOG2E
        ok = None
        if diag:
            row = lax.broadcasted_iota(jnp.int32, (tq, tk), 0)
            col = lax.broadcasted_iota(jnp.int32, (tq, tk), 1)
            ok = (col // CHUNK <= row // CHUNK) if chunk_mask else (col <= row)
        return _softmax_step(q, kb, vb, carry, bias, ok)

    carry = lax.fori_loop(0, qi, lambda j, c: step(j, c, False), _softmax_init(tq))
    m, l, acc = step(qi, carry, True)
    o_ref[0] = acc / l


def _flash_stick_kernel(q_ref, k_ref, v_ref, u_ref, o_ref, *, tq, ks):
    qi = pl.program_id(2)
    q = q_ref[0]
    n_sub = tq // ks

    def step(jb, carry, diag):
        c, acc = carry
        off = pl.multiple_of(jb * ks, ks)
        kb = k_ref[0, pl.ds(off, ks), :]
        vb = v_ref[0, pl.ds(off, ks), :]
        vis = None
        if diag:
            row = lax.broadcasted_iota(jnp.int32, (tq, ks), 0) + qi * tq
            col = lax.broadcasted_iota(jnp.int32, (tq, ks), 1) + jb * ks
            vis = col < row
        c_new, w = _stick_weights(_nt_dot(q, kb), c, u_ref[...], vis)
        return c_new, acc + _dot(w.astype(BF16), vb)

    carry = (jnp.zeros((tq, 1), F32), jnp.zeros((tq, LANE), F32))
    for r in range(n_sub):
        carry = step(qi * n_sub + (n_sub - 1 - r), carry, True)
    n_old = qi * n_sub
    _, acc = lax.fori_loop(0, n_old, lambda t, cr: step(n_old - 1 - t, cr, False), carry)
    o_ref[0] = acc


def _stick_weights(z, c, u, vis):
    sp = jnp.log(1.0 + jnp.exp2(-jnp.abs(z))) * LOG2E
    log_rest = jnp.minimum(-z, 0.0) - sp
    log_beta = log_rest + z
    if vis is not None:
        log_rest = jnp.where(vis, log_rest, 0.0)
    hi, lo = _split2(log_rest)
    between = _dot(hi, u) + _dot(lo, u)
    w = jnp.exp2(log_beta + between + c)
    if vis is not None:
        w = jnp.where(vis, w, 0.0)
    return c + jnp.sum(log_rest, axis=1, keepdims=True), w


def _band_kernel(q_ref, k_ref, v_ref, bd_ref, bp_ref, o_ref, *, tq):
    qi = pl.program_id(2)
    q = q_ref[0]
    off = pl.multiple_of(qi * tq, tq)
    carry = _softmax_step(q, k_ref[0, pl.ds(off, tq), :], v_ref[0, pl.ds(off, tq), :],
                          _softmax_init(tq), bd_ref[0])
    offp = pl.multiple_of(jnp.maximum(qi - 1, 0) * tq, tq)
    no_prev = jnp.where(qi == 0, NEG_INF, 0.0)
    m, l, acc = _softmax_step(q, k_ref[0, pl.ds(offp, tq), :], v_ref[0, pl.ds(offp, tq), :],
                              carry, bp_ref[0] + no_prev)
    o_ref[0] = acc / l


def _attn_specs(s, tq, k_per_head):
    q_spec = pl.BlockSpec((1, tq, LANE), lambda b, h, i: (b, i, h))
    k_spec = pl.BlockSpec((1, s, LANE), (lambda b, h, i: (b, 0, h)) if k_per_head else (lambda b, h, i: (b, 0, h // 2)))
    v_spec = pl.BlockSpec((1, s, LANE), lambda b, h, i: (b, 0, h // 2))
    return q_spec, k_spec, v_spec


def _attn_call(kern, nb, s, tq, in_specs, operands, name):
    return pl.pallas_call(
        kern, grid=(nb, N_HEADS, s // tq), in_specs=in_specs,
        out_specs=pl.BlockSpec((1, tq, LANE), lambda b, h, i: (b, i, h)),
        out_shape=jax.ShapeDtypeStruct((nb, s, N_HEADS * LANE), F32),
        compiler_params=_cparams(3), name=name,
    )(*operands)


def _strict_upper(n):
    idx = np.arange(n)
    return jnp.asarray((idx[:, None] > idx[None, :]).astype(np.float32), BF16)


def _band_bias_tiles(rel_bias, tq):
    i = np.arange(tq)[:, None]
    j = np.arange(tq)[None, :]
    out = []
    for shift, ok in ((0, (j // CHUNK) <= (i // CHUNK)),
                      (tq, (j // CHUNK) >= (i // CHUNK) + tq // CHUNK - A_LEFT_CHUNKS)):
        rel = np.clip(i - j + shift, -REL_CLIP, REL_CLIP) + REL_CLIP
        bias = rel_bias.astype(F32)[:, rel] * LOG2E
        out.append(jnp.where(jnp.asarray(ok)[None], bias, NEG_INF))
    return out


def _prompt_attention(pr, nb, s, rel_bias):
    tq, tk = ATTN_TQ, ATTN_TK
    assert s % tq == 0 and tq == tk and tq == A_WINDOW
    r3 = lambda a: a.reshape(nb, s, a.shape[-1])
    qa, ka, va, qb, kb, vb, qc, kc, vc, qd, kd, vd = [r3(a) for a in pr[8:20]]
    q_spec, kp_spec, v_spec = _attn_specs(s, tq, False)
    _, kh_spec, _ = _attn_specs(s, tq, True)

    bd, bp = _band_bias_tiles(rel_bias, tq)
    b_spec = pl.BlockSpec((1, tq, tq), lambda b, h, i: (h, 0, 0))
    o_a = _attn_call(functools.partial(_band_kernel, tq=tq), nb, s, tq,
                     [q_spec, kp_spec, v_spec, b_spec, b_spec], (qa, ka, va, bd, bp), "attn_band")

    u = _strict_upper(STICK_KS)
    o_b = _attn_call(functools.partial(_flash_stick_kernel, tq=tq, ks=STICK_KS), nb, s, tq,
                     [q_spec, kp_spec, v_spec, pl.BlockSpec(u.shape, lambda b, h, i: (0, 0))],
                     (qb, kb, vb, u), "attn_stick")

    clf = r3(pr[7])[..., 0:N_HEADS]
    c_cum = jnp.cumsum(clf, axis=1)
    fk = jnp.swapaxes(c_cum, 1, 2).reshape(nb, N_HEADS, s // tk, tk)
    fq_spec = pl.BlockSpec((1, tq, N_HEADS), lambda b, h, i: (b, i, 0))
    fk_spec = pl.BlockSpec((1, 1, s // tk, tk), lambda b, h, i: (b, h, 0, 0))
    o_c = _attn_call(functools.partial(_flash_softmax_kernel, tq=tq, tk=tk, chunk_mask=False, with_decay=True),
                     nb, s, tq, [q_spec, kp_spec, v_spec, fq_spec, fk_spec], (qc, kc, vc, c_cum, fk),
                     "attn_forget")

    o_d = _attn_call(functools.partial(_flash_softmax_kernel, tq=tq, tk=tk, chunk_mask=True, with_decay=False),
                     nb, s, tq, [q_spec, kh_spec, v_spec], (qd, kd, vd), "attn_mla")
    return o_a, o_b, o_c, o_d


def _two_block_softmax(s1, s2, v1, v2):
    m = jnp.maximum(jnp.max(s1, axis=1, keepdims=True), jnp.max(s2, axis=1, keepdims=True))
    p1 = jnp.exp2(s1 - m)
    p2 = jnp.exp2(s2 - m)
    l = jnp.sum(p1, axis=1, keepdims=True) + jnp.sum(p2, axis=1, keepdims=True)
    return (_dot(p1.astype(BF16), v1) + _dot(p2.astype(BF16), v2)) / l


def _sample_kernel(qa_ref, ka_ref, va_ref, qb_ref, kb_ref, vb_ref, qc_ref, kc_ref, vc_ref, qd_ref, kd_ref, vd_ref,
                   cak_ref, cav_ref, cbk_ref, cbv_ref, cck_ref, ccv_ref, cckv_ref, ckr_ref,
                   ba_c_ref, ba_n_ref, fq_ref, fkc_ref, fkn_ref,
                   u_ref, un_ref, wuk_ref, wuv_ref, gk_ref, place_ref,
                   oa_ref, ob_ref, oc_ref, od_ref, *, t, past, ks):
    row = lax.broadcasted_iota(jnp.int32, (t, t), 0)
    col = lax.broadcasted_iota(jnp.int32, (t, t), 1)
    causal_bias = jnp.where(col <= row, 0.0, NEG_INF)
    chunk_bias = jnp.where((past + col) // CHUNK <= (past + row) // CHUNK, 0.0, NEG_INF)
    strict = col < row

    ckv_c = cckv_ref[0].astype(BF16)
    kn_c = _dot(ckv_c, wuk_ref[...])
    vd_c = _dot(ckv_c, wuv_ref[...]).astype(BF16)
    kr_c = _dot(ckr_ref[0].astype(BF16), place_ref[...])

    for h in range(N_HEADS):
        hs = slice(LANE * h, LANE * h + LANE)
        ps = slice(LANE * (h // 2), LANE * (h // 2) + LANE)

        q = qa_ref[0][:, hs]
        s1 = _nt_dot(q, cak_ref[0][:, ps].astype(BF16)) + ba_c_ref[h]
        s2 = _nt_dot(q, ka_ref[0][:, ps]) + ba_n_ref[h]
        oa_ref[0, :, hs] = _two_block_softmax(s1, s2, cav_ref[0][:, ps].astype(BF16), va_ref[0][:, ps])

        q = qb_ref[0][:, hs]
        c, w = _stick_weights(_nt_dot(q, kb_ref[0][:, ps]), jnp.zeros((t, 1), F32), un_ref[...], strict)
        acc = _dot(w.astype(BF16), vb_ref[0][:, ps])

        def b_step(i, carry, q=q, ps=ps):
            c, acc = carry
            off = pl.multiple_of(past - ks - i * ks, ks)
            kb = cbk_ref[0, pl.ds(off, ks), ps].astype(BF16)
            vb = cbv_ref[0, pl.ds(off, ks), ps].astype(BF16)
            c, w = _stick_weights(_nt_dot(q, kb), c, u_ref[...], None)
            return c, acc + _dot(w.astype(BF16), vb)

        _, acc = lax.fori_loop(0, past // ks, b_step, (c, acc))
        ob_ref[0, :, hs] = acc

        q = qc_ref[0][:, hs]
        fq = fq_ref[0][:, h:h + 1] * LOG2E
        s1 = _nt_dot(q, cck_ref[0][:, ps].astype(BF16)) + (fq - fkc_ref[0, h:h + 1, :] * LOG2E)
        s2 = _nt_dot(q, kc_ref[0][:, ps]) + (fq - fkn_ref[0, h:h + 1, :] * LOG2E) + causal_bias
        oc_ref[0, :, hs] = _two_block_softmax(s1, s2, ccv_ref[0][:, ps].astype(BF16), vc_ref[0][:, ps])

        q = qd_ref[0][:, hs]
        xh = kn_c[:, hs]
        ss = jnp.sum(xh * xh, axis=-1, keepdims=True)
        k_c = (xh * lax.rsqrt(ss * (1.0 / MLA_NOPE) + NORM_EPS) * gk_ref[...] + kr_c).astype(BF16)
        s1 = _nt_dot(q, k_c)
        s2 = _nt_dot(q, kd_ref[0][:, hs]) + chunk_bias
        od_ref[0, :, hs] = _two_block_softmax(s1, s2, vd_c[:, ps], vd_ref[0][:, ps])


def _sample_attention(pr, caches, pw, rel_bias, nb, t):
    a_k, a_v, b_k, b_v, c_k, c_v, c_lf, d_ckv, d_kr = caches
    _, gv, _, wuk, wuv = pw
    past = b_k.shape[1]
    win = a_k.shape[1]
    ks = STICK_KS
    assert past % ks == 0
    r3 = lambda a: a.reshape(nb, t, a.shape[-1])
    news = [r3(a) for a in pr[8:20]]
    flat = lambda a: a.reshape(nb, a.shape[1], GROUP_W)

    qpos = past + np.arange(t)
    kpos = past - win + np.arange(win + t)
    rel = np.clip(qpos[:, None] - kpos[None, :], -REL_CLIP, REL_CLIP) + REL_CLIP
    qc, kc = qpos // CHUNK, kpos // CHUNK
    ok = (kc[None, :] <= qc[:, None]) & (kc[None, :] >= qc[:, None] - A_LEFT_CHUNKS)
    ba = jnp.where(jnp.asarray(ok)[None], rel_bias.astype(F32)[:, rel] * LOG2E, NEG_INF)
    ba_c, ba_n = ba[:, :, :win], ba[:, :, win:]

    clf_new = r3(pr[7])[..., 0:N_HEADS]
    c_cum = jnp.cumsum(jnp.concatenate([c_lf.astype(F32), clf_new], axis=1), axis=1)
    fq = c_cum[:, past:]
    fk = jnp.swapaxes(c_cum, 1, 2)
    fk_c, fk_n = fk[:, :, :past], fk[:, :, past:]

    place = np.zeros((MLA_ROPE, LANE), np.float32)
    place[np.arange(MLA_ROPE), ROPE_LO + np.arange(MLA_ROPE)] = 1.0
    place = jnp.asarray(place, BF16)
    gk = gv[5:6, 0:LANE]
    u, un = _strict_upper(ks), _strict_upper(t)

    per_b = lambda shape: pl.BlockSpec((1,) + shape, lambda b: (b,) + (0,) * len(shape))
    const = lambda a: pl.BlockSpec(a.shape, lambda b: (0,) * a.ndim)
    operands = news + [flat(a_k), flat(a_v), flat(b_k), flat(b_v), flat(c_k), flat(c_v), d_ckv, d_kr,
                       ba_c, ba_n, fq, fk_c, fk_n, u, un, wuk, wuv, gk, place]
    in_specs = [per_b(a.shape[1:]) for a in operands[:20]] + [const(ba_c), const(ba_n)] + \
               [per_b(a.shape[1:]) for a in (fq, fk_c, fk_n)] + [const(a) for a in (u, un, wuk, wuv, gk, place)]
    out_shape = [jax.ShapeDtypeStruct((nb, t, N_HEADS * LANE), F32)] * 4
    out_specs = [per_b((t, N_HEADS * LANE))] * 4
    return pl.pallas_call(
        functools.partial(_sample_kernel, t=t, past=past, ks=ks), grid=(nb,),
        in_specs=in_specs, out_specs=out_specs, out_shape=out_shape,
        compiler_params=_cparams(1), name="attn_sample",
    )(*operands)


def _out_kernel(oa_ref, ob_ref, oc_ref, od_ref, h_ref, gg_ref, wo_ref, nf_ref,
                wr1_ref, wr2_ref, wr3_ref, br_ref, h1_ref, hn_ref, lg_ref):
    rows = h_ref.shape[0]
    low = lax.broadcasted_iota(jnp.int32, (rows, LANE), 1) < HEAD_DIM
    h1 = h_ref[...]
    for g, o_ref in enumerate((oa_ref, ob_ref, oc_ref, od_ref)):
        p0 = jnp.where(low, o_ref[:, 0:LANE], o_ref[:, LANE:2 * LANE])
        p1 = jnp.where(low, o_ref[:, 2 * LANE:3 * LANE], o_ref[:, 3 * LANE:4 * LANE])
        og = jnp.concatenate([p0, p1], axis=1)
        ms = jnp.mean(og * og, axis=-1, keepdims=True)
        y = (og * lax.rsqrt(ms + NORM_EPS) * gg_ref[g:g + 1, :]).astype(BF16)
        h1 = h1 + _dot(y, wo_ref[GROUP_W * g: GROUP_W * (g + 1), :])
    h1_ref[...] = h1
    ms = jnp.mean(h1 * h1, axis=-1, keepdims=True)
    hn = h1 * lax.rsqrt(ms + NORM_EPS) * nf_ref[...]
    hn_ref[...] = hn.astype(BF16)
    a1, a2, a3 = _split3(hn)
    w1, w2, w3 = wr1_ref[...], wr2_ref[...], wr3_ref[...]
    lg = _dot(a1, w1) + _dot(a1, w2) + _dot(a2, w1) + _dot(a1, w3) + _dot(a2, w2) + _dot(a3, w1)
    lg_ref[...] = lg + br_ref[...]


def _out_project(outs, h, group_gain, w_out_b, norm_ffn, wr_parts, br_pad):
    t, d = h.shape
    ts = DENSE_ROWS
    assert t % ts == 0
    row = lambda i: (i, 0)
    full = lambda i: (0, 0)
    o_spec = pl.BlockSpec((ts, N_HEADS * LANE), row)
    in_specs = [o_spec] * 4 + [pl.BlockSpec((ts, d), row), pl.BlockSpec(group_gain.shape, full),
                               pl.BlockSpec(w_out_b.shape, full), pl.BlockSpec((1, d), full)] + \
               [pl.BlockSpec((d, LANE), full)] * 3 + [pl.BlockSpec((1, LANE), full)]
    out_shape = [jax.ShapeDtypeStruct((t, d), F32), jax.ShapeDtypeStruct((t, d), BF16),
                 jax.ShapeDtypeStruct((t, LANE), F32)]
    out_specs = [pl.BlockSpec((ts, d), row), pl.BlockSpec((ts, d), row), pl.BlockSpec((ts, LANE), row)]
    return pl.pallas_call(
        _out_kernel, grid=(t // ts,), in_specs=in_specs, out_specs=out_specs, out_shape=out_shape,
        compiler_params=_cparams(1), name="out_proj",
    )(*outs, h, group_gain, w_out_b, norm_ffn.reshape(1, d), *wr_parts, br_pad)


def _expert_kernel(be_ref, nu_ref, x_ref, wu_ref, bu_ref, wd_ref, bd_ref, y_ref):
    @pl.when(pl.program_id(0) < nu_ref[0])
    def _():
        u = _dot(x_ref[...], wu_ref[0]) + bu_ref[0]
        glu = jnp.minimum(u[:, :D_FF], SWIGLU_LIMIT)
        lin = jnp.clip(u[:, D_FF:], -SWIGLU_LIMIT, SWIGLU_LIMIT)
        act = glu * jax.nn.sigmoid(SWIGLU_ALPHA * glu) * (lin + 1.0)
        y_ref[...] = _dot(act.astype(BF16), wd_ref[0]) + bd_ref[0]


def _expert_ffn(x_sorted, blk_exp, n_used, w_up_b, b_up, w_down_b, b_down):
    rows, d = x_sorted.shape
    n_blk = rows // MOE_BLOCK
    last = lambda i, nu: jnp.minimum(i, nu[0] - 1)
    grid_spec = pltpu.PrefetchScalarGridSpec(
        num_scalar_prefetch=2, grid=(n_blk,),
        in_specs=[pl.BlockSpec((MOE_BLOCK, d), lambda i, be, nu: (last(i, nu), 0)),
                  pl.BlockSpec((1, d, 2 * D_FF), lambda i, be, nu: (be[last(i, nu)], 0, 0)),
                  pl.BlockSpec((1, 1, 2 * D_FF), lambda i, be, nu: (be[last(i, nu)], 0, 0)),
                  pl.BlockSpec((1, D_FF, d), lambda i, be, nu: (be[last(i, nu)], 0, 0)),
                  pl.BlockSpec((1, 1, d), lambda i, be, nu: (be[last(i, nu)], 0, 0))],
        out_specs=pl.BlockSpec((MOE_BLOCK, d), lambda i, be, nu: (last(i, nu), 0)))
    return pl.pallas_call(
        _expert_kernel, grid_spec=grid_spec, out_shape=jax.ShapeDtypeStruct((rows, d), F32),
        compiler_params=_cparams(1), name="expert_ffn",
    )(blk_exp, n_used, x_sorted, w_up_b, b_up.reshape(N_EXP, 1, 2 * D_FF), w_down_b, b_down.reshape(N_EXP, 1, d))


def _moe(hn, logits, w_up_b, b_up, w_down_b, b_down):
    n_tok, d = hn.shape
    top_v, top_i = lax.top_k(logits, TOP_K)
    gates = jax.nn.softmax(top_v, axis=-1)
    n = n_tok * TOP_K
    e_flat = top_i.reshape(-1)
    onehot = (e_flat[:, None] == jnp.arange(N_EXP)[None, :]).astype(jnp.int32)
    rank = jnp.sum((jnp.cumsum(onehot, axis=0) - onehot) * onehot, axis=1)
    counts = jnp.sum(onehot, axis=0)
    padded = (counts + MOE_BLOCK - 1) // MOE_BLOCK * MOE_BLOCK
    p_end = jnp.cumsum(padded)
    p_start = p_end - padded
    dest = p_start[e_flat] + rank
    n_blk = -(-n // MOE_BLOCK) + N_EXP
    rows = n_blk * MOE_BLOCK
    row_tok = jnp.full((rows,), n_tok, jnp.int32).at[dest].set(jnp.arange(n, dtype=jnp.int32) // TOP_K)
    blk_exp = jnp.clip(jnp.searchsorted(p_end, jnp.arange(n_blk) * MOE_BLOCK, side='right'), 0, N_EXP - 1)
    n_used = (p_end[-1] // MOE_BLOCK).astype(jnp.int32).reshape(1)
    hn_pad = jnp.concatenate([hn, jnp.zeros((1, d), hn.dtype)], axis=0)
    x_sorted = hn_pad[row_tok]
    y = _expert_ffn(x_sorted, blk_exp.astype(jnp.int32), n_used, w_up_b, b_up, w_down_b, b_down)
    picked = y[dest.reshape(n_tok, TOP_K)]
    return jnp.sum(picked * gates[..., None], axis=1)


def _ple_kernel(h_ref, m_ref, p_ref, np_ref, wg_ref, wp_ref, o_ref):
    h2 = h_ref[...] + m_ref[...]
    ms = jnp.mean(h2 * h2, axis=-1, keepdims=True)
    hn = (h2 * lax.rsqrt(ms + NORM_EPS) * np_ref[...]).astype(BF16)
    gate = jax.nn.sigmoid(_dot(hn, wg_ref[...]))
    o_ref[...] = h2 + gate * _dot(p_ref[...].astype(BF16), wp_ref[...])


def _ple(h1, moe, p, norm_ple, wg_b, wp_b):
    t, d = h1.shape
    ts = DENSE_ROWS
    row = lambda i: (i, 0)
    full = lambda i: (0, 0)
    return pl.pallas_call(
        _ple_kernel, grid=(t // ts,),
        in_specs=[pl.BlockSpec((ts, d), row), pl.BlockSpec((ts, d), row), pl.BlockSpec((ts, p.shape[1]), row),
                  pl.BlockSpec((1, d), full), pl.BlockSpec(wg_b.shape, full), pl.BlockSpec(wp_b.shape, full)],
        out_specs=pl.BlockSpec((ts, d), row), out_shape=jax.ShapeDtypeStruct((t, d), F32),
        compiler_params=_cparams(1), name="ple",
    )(h1, moe, p, norm_ple.reshape(1, d), wg_b, wp_b)


def kernel(x_prompt, x_sample, p_prompt, p_sample, cache_a_k, cache_a_v, cache_b_k, cache_b_v, cache_c_k, cache_c_v, cache_c_logf, cache_d_ckv, cache_d_krope, norm_mix, w_in, b_forget, qk_gain, rope_gain, kv_gain, w_uk, w_uv, rel_bias, group_gain, w_out, norm_ffn, w_router, b_router, w_up, b_up, w_down, b_down, norm_ple, w_ple_gate, w_ple_proj):
    nb, s, d = x_prompt.shape
    nd, t, _ = x_sample.shape
    depth = w_in.shape[0]
    past = cache_b_k.shape[2]
    assert PROJ_ROWS % t == 0 and s % PROJ_ROWS == 0

    tab_p = _rope_tables(jnp.arange(s))
    tab_s = _rope_tables(past + jnp.arange(PROJ_ROWS) % t)
    hp = x_prompt.reshape(nb * s, d)
    hs = x_sample.reshape(nd * t, d)
    st_p, st_s = [], []
    keep = min(A_WINDOW, s)
    for i in range(depth):
        pw = _proj_weights(w_in[i], b_forget[i], qk_gain[i], rope_gain[i], kv_gain[i], w_uk[i], w_uv[i])
        pr_p = _project(hp, norm_mix[i], pw, tab_p, s // PROJ_ROWS)
        pr_s = _project(hs, norm_mix[i], pw, tab_s, 1)
        outs_p = _prompt_attention(pr_p, nb, s, rel_bias[i])
        caches = (cache_a_k[i], cache_a_v[i], cache_b_k[i], cache_b_v[i], cache_c_k[i], cache_c_v[i],
                  cache_c_logf[i], cache_d_ckv[i], cache_d_krope[i])
        outs_s = _sample_attention(pr_s, caches, pw, rel_bias[i], nd, t)

        def heads(a, n, rows):
            return a.reshape(n, rows, N_HEADS, HEAD_DIM)

        p3 = lambda a: a.reshape(nb, s, a.shape[-1])
        s3 = lambda a: a.reshape(nd, t, a.shape[-1])
        st_p.append((heads(p3(pr_p[0])[:, s - keep:], nb, keep), heads(p3(pr_p[1])[:, s - keep:], nb, keep),
                     heads(pr_p[2], nb, s), heads(pr_p[3], nb, s), heads(pr_p[4], nb, s), heads(pr_p[5], nb, s),
                     p3(pr_p[7])[..., 0:N_HEADS], p3(pr_p[6]), p3(pr_p[7])[..., ROPE_LO:ROPE_LO + MLA_ROPE]))
        ka_all = jnp.concatenate([cache_a_k[i], heads(pr_s[0], nd, t)], axis=1)[:, t:]
        va_all = jnp.concatenate([cache_a_v[i], heads(pr_s[1], nd, t)], axis=1)[:, t:]
        st_s.append((ka_all, va_all,
                     heads(pr_s[2], nd, t), heads(pr_s[3], nd, t), heads(pr_s[4], nd, t), heads(pr_s[5], nd, t),
                     s3(pr_s[7])[..., 0:N_HEADS], s3(pr_s[6]), s3(pr_s[7])[..., ROPE_LO:ROPE_LO + MLA_ROPE]))

        w_out_b = w_out[i].astype(BF16)
        wr_pad = jnp.zeros((d, LANE), F32).at[:, 0:N_EXP].set(w_router[i])
        wr_parts = _split3(wr_pad)
        br_pad = jnp.zeros((1, LANE), F32).at[0, 0:N_EXP].set(b_router[i])
        flat4 = lambda o: o.reshape(-1, N_HEADS * LANE)
        h1_p, hn_p, lg_p = _out_project([flat4(o) for o in outs_p], hp, group_gain[i], w_out_b, norm_ffn[i],
                                        wr_parts, br_pad)
        h1_s, hn_s, lg_s = _out_project([flat4(o) for o in outs_s], hs, group_gain[i], w_out_b, norm_ffn[i],
                                        wr_parts, br_pad)

        hn_all = jnp.concatenate([hn_p, hn_s], axis=0)
        lg_all = jnp.concatenate([lg_p, lg_s], axis=0)[:, 0:N_EXP]
        moe = _moe(hn_all, lg_all, w_up[i].astype(BF16), b_up[i], w_down[i].astype(BF16), b_down[i])

        wg_b = w_ple_gate[i].astype(BF16)
        wp_b = w_ple_proj[i].astype(BF16)
        hp = _ple(h1_p, moe[:nb * s], p_prompt[i].reshape(nb * s, -1), norm_ple[i], wg_b, wp_b)
        hs = _ple(h1_s, moe[nb * s:], p_sample[i].reshape(nd * t, -1), norm_ple[i], wg_b, wp_b)

    state_p = [jnp.stack([st[j] for st in st_p]) for j in range(9)]
    state_s = [jnp.stack([st[j] for st in st_s]) for j in range(9)]
    return (hp.reshape(nb, s, d), hs.reshape(nd, t, d), *state_p, *state_s)
```

```python
import functools

import numpy as np
import jax
import jax.numpy as jnp
from jax import lax
from jax.experimental import pallas as pl
from jax.experimental.pallas import tpu as pltpu

F32 = jnp.float32
BF16 = jnp.bfloat16

CHUNK = 64
HEAD_DIM = 64
N_HEADS = 4
GROUP_W = 256
A_LEFT_CHUNKS = 8
A_WINDOW = A_LEFT_CHUNKS * CHUNK
REL_CLIP = 128
MLA_NOPE = 64
MLA_ROPE = 32
KV_RANK = 128
ROPE_BASE = 10000.0
N_EXP = 32
TOP_K = 4
D_FF = 1024
SWIGLU_LIMIT = 7.0
SWIGLU_ALPHA = 1.702
MOE_BLOCK = 512
NORM_EPS = 1e-6
NEG_INF = -1e30
LOG2E = 1.4426950408889634

A_Q = 0
C_F = 2304
D_Q = 2308
D_CKV = D_Q + N_HEADS * (MLA_NOPE + MLA_ROPE)
D_KR = D_CKV + KV_RANK

LANE = 128
SEG_ABC = 0
SEG_DQ = 2304
SEG_CKV = 2816
SEG_MISC = 2944
N_COLS = 3072
ROPE_LO = 64
ROPE_HALF = MLA_ROPE // 2

VMEM_LIMIT = 56 * 1024 * 1024

PROJ_ROWS = 512
ATTN_TQ = 512
ATTN_TK = 512
MLA_TQ = 1024
STICK_KS = 256
DENSE_ROWS = 512
PLE_ROWS = 256

DEAD_LOG2 = -160.0
BOUND_MARGIN = 1.001


def _cparams(n_axes):
    return pltpu.CompilerParams(dimension_semantics=("arbitrary",) * n_axes,
                                vmem_limit_bytes=VMEM_LIMIT)


def _nt_dot(a, b):
    return lax.dot_general(a, b, (((1,), (1,)), ((), ())), preferred_element_type=F32)


def _dot(a, b):
    return jnp.dot(a, b, preferred_element_type=F32)


def _split2(x):
    hi = x.astype(BF16)
    lo = (x - hi.astype(F32)).astype(BF16)
    return hi, lo


def _split3(x):
    hi = x.astype(BF16)
    r = x - hi.astype(F32)
    mid = r.astype(BF16)
    lo = (r - mid.astype(F32)).astype(BF16)
    return hi, mid, lo


def _in_col_map():
    cols = np.full((N_COLS,), -1, np.int64)
    cols[0:2304] = np.arange(2304)
    for h in range(N_HEADS):
        base = D_Q + (MLA_NOPE + MLA_ROPE) * h
        cols[SEG_DQ + LANE * h: SEG_DQ + LANE * h + MLA_NOPE + MLA_ROPE] = base + np.arange(MLA_NOPE + MLA_ROPE)
    cols[SEG_CKV:SEG_CKV + KV_RANK] = D_CKV + np.arange(KV_RANK)
    cols[SEG_MISC:SEG_MISC + N_HEADS] = C_F + np.arange(N_HEADS)
    cols[SEG_MISC + ROPE_LO:SEG_MISC + ROPE_LO + MLA_ROPE] = D_KR + np.arange(MLA_ROPE)
    return cols


def _rope_tables(pos):
    inv = ROPE_BASE ** (-jnp.arange(ROPE_HALF, dtype=F32) / ROPE_HALF)
    ang = pos.astype(F32)[:, None] * inv
    cos, sin = jnp.cos(ang), jnp.sin(ang)
    n = pos.shape[0]
    one = jnp.ones((n, ROPE_LO), F32)
    z16 = jnp.zeros((n, ROPE_HALF), F32)
    z64 = jnp.zeros((n, ROPE_LO), F32)
    z32 = jnp.zeros((n, LANE - ROPE_LO - MLA_ROPE), F32)
    cos_t = jnp.concatenate([one, cos, cos, z32 + 1.0], axis=1)
    sin_a = jnp.concatenate([z64, -sin, z16, z32], axis=1)
    sin_b = jnp.concatenate([z64, z16, sin, z32], axis=1)
    return cos_t, sin_a, sin_b


def _rope(y, cos_t, sin_a, sin_b):
    left = pltpu.roll(y, LANE - ROPE_HALF, axis=1)
    right = pltpu.roll(y, ROPE_HALF, axis=1)
    return y * cos_t + left * sin_a + right * sin_b


def _head_norm(x, m_ref, gain):
    hi, lo = _split2(x * x)
    ssq = _dot(hi, m_ref[...]) + _dot(lo, m_ref[...])
    return x * lax.rsqrt(ssq * (1.0 / HEAD_DIM) + NORM_EPS) * gain


def _value_lane0(h):
    return HEAD_DIM * (h % 2)


def _ones_lane(h):
    return HEAD_DIM - _value_lane0(h)


def _store_padded_q(q_ref, q, scale):
    lane = lax.broadcasted_iota(jnp.int32, (q.shape[0], LANE), 1)
    low = lane < HEAD_DIM
    qs = q * scale
    for h in range(N_HEADS):
        pair = qs[:, LANE * (h // 2): LANE * (h // 2) + LANE]
        keep = low if h % 2 == 0 else jnp.logical_not(low)
        q_ref[:, LANE * h: LANE * h + LANE] = jnp.where(keep, pair, 0.0).astype(BF16)


def _proj_kernel(x_ref, gmix_ref, w_ref, cos_ref, sa_ref, sb_ref, gv_ref, m64_ref, wuk_ref, wuv_ref,
                 ak_ref, av_ref, bk_ref, bv_ref, ck_ref, cv_ref, ckv_ref, misc_ref,
                 qa_ref, ka_ref, va_ref, qb_ref, kb_ref, vb_ref, qc_ref, kc_ref, vc_ref,
                 qd_ref, kd_ref, vd_ref):
    x = x_ref[...]
    ms = jnp.mean(x * x, axis=-1, keepdims=True)
    hn = (x * lax.rsqrt(ms + NORM_EPS) * gmix_ref[...]).astype(BF16)
    sm_scale = (HEAD_DIM ** -0.5) * LOG2E
    rows = x.shape[0]
    lane = lax.broadcasted_iota(jnp.int32, (rows, LANE), 1)
    is_nope = lane < MLA_NOPE
    is_rope = jnp.logical_and(lane >= ROPE_LO, lane < ROPE_LO + MLA_ROPE)
    cos_t, sin_a, sin_b = cos_ref[...], sa_ref[...], sb_ref[...]

    z = _dot(hn, w_ref[:, 0:768])
    aq = _head_norm(z[:, 0:256], m64_ref, gv_ref[0:1, :])
    ak = _head_norm(z[:, 256:512], m64_ref, gv_ref[1:2, :])
    av = z[:, 512:768]
    ak_ref[...] = ak
    av_ref[...] = av
    _store_padded_q(qa_ref, aq, sm_scale)
    ka_ref[...] = ak.astype(BF16)
    va_ref[...] = av.astype(BF16)

    z = _dot(hn, w_ref[:, 768:1536])
    bk_ref[...] = z[:, 256:512]
    bv_ref[...] = z[:, 512:768]
    _store_padded_q(qb_ref, z[:, 0:256], sm_scale)
    kb_ref[...] = z[:, 256:512].astype(BF16)
    vb_ref[...] = z[:, 512:768].astype(BF16)

    z = _dot(hn, w_ref[:, 1536:2304])
    cq = _head_norm(z[:, 0:256], m64_ref, gv_ref[2:3, :])
    ck = _head_norm(z[:, 256:512], m64_ref, gv_ref[3:4, :])
    cv = z[:, 512:768]
    ck_ref[...] = ck
    cv_ref[...] = cv
    _store_padded_q(qc_ref, cq, sm_scale)
    kc_ref[...] = ck.astype(BF16)
    vc_ref[...] = cv.astype(BF16)

    z = _dot(hn, w_ref[:, SEG_CKV:N_COLS])
    zc = z[:, 0:KV_RANK]
    ckv = zc * lax.rsqrt(jnp.mean(zc * zc, axis=-1, keepdims=True) + NORM_EPS) * gv_ref[7:8, 0:LANE]
    ckv_ref[...] = ckv
    zm = z[:, KV_RANK:2 * KV_RANK]
    ssr = jnp.sum(jnp.where(is_rope, zm * zm, 0.0), axis=-1, keepdims=True)
    kr = zm * lax.rsqrt(ssr * (1.0 / MLA_ROPE) + NORM_EPS) * gv_ref[6:7, 0:LANE]
    kr = _rope(kr, cos_t, sin_a, sin_b)
    zf = zm + gv_ref[8:9, 0:LANE]
    clf = jnp.minimum(zf, 0.0) - jnp.log1p(jnp.exp(-jnp.abs(zf)))
    misc_ref[...] = jnp.where(lane < N_HEADS, clf, kr)

    ckv_b = ckv.astype(BF16)
    kn = _dot(ckv_b, wuk_ref[...])
    for h in range(N_HEADS):
        hs = slice(LANE * h, LANE * h + LANE)
        ones = (lane == _ones_lane(h)).astype(F32)
        vd_ref[:, hs] = (_dot(ckv_b, wuv_ref[:, hs]) + ones).astype(BF16)
    for h in range(N_HEADS):
        xh = kn[:, LANE * h: LANE * h + LANE]
        ss = jnp.sum(xh * xh, axis=-1, keepdims=True)
        yh = xh * lax.rsqrt(ss * (1.0 / MLA_NOPE) + NORM_EPS) * gv_ref[5:6, 0:LANE]
        kd_ref[:, LANE * h: LANE * h + LANE] = (yh + kr).astype(BF16)

    z = _dot(hn, w_ref[:, SEG_DQ:SEG_CKV])
    d_scale = ((MLA_NOPE + MLA_ROPE) ** -0.5) * LOG2E
    for h in range(N_HEADS):
        xh = z[:, LANE * h: LANE * h + LANE]
        x2 = xh * xh
        ssn = jnp.sum(jnp.where(is_nope, x2, 0.0), axis=-1, keepdims=True)
        ssr = jnp.sum(jnp.where(is_rope, x2, 0.0), axis=-1, keepdims=True)
        rn = lax.rsqrt(ssn * (1.0 / MLA_NOPE) + NORM_EPS)
        rr = lax.rsqrt(ssr * (1.0 / MLA_ROPE) + NORM_EPS)
        yh = xh * jnp.where(is_nope, rn, rr) * gv_ref[4:5, 0:LANE]
        yh = _rope(yh, cos_t, sin_a, sin_b)
        qd_ref[:, LANE * h: LANE * h + LANE] = (yh * d_scale).astype(BF16)


def _proj_weights(w_in, b_forget, qk_gain, rope_gain, kv_gain, w_uk, w_uv):
    cols = _in_col_map()
    valid = jnp.asarray(cols >= 0)
    w = jnp.where(valid[None, :], w_in[:, np.maximum(cols, 0)], 0.0).astype(BF16)

    def tile4(g):
        return jnp.tile(g, N_HEADS)

    gv = jnp.zeros((16, GROUP_W), F32)
    gv = gv.at[0].set(tile4(qk_gain[0])).at[1].set(tile4(qk_gain[1]))
    gv = gv.at[2].set(tile4(qk_gain[2])).at[3].set(tile4(qk_gain[3]))
    gv = gv.at[4, 0:MLA_NOPE].set(qk_gain[4]).at[4, ROPE_LO:ROPE_LO + MLA_ROPE].set(rope_gain[0])
    gv = gv.at[5, 0:MLA_NOPE].set(qk_gain[5])
    gv = gv.at[6, ROPE_LO:ROPE_LO + MLA_ROPE].set(rope_gain[1])
    gv = gv.at[7, 0:KV_RANK].set(kv_gain)
    gv = gv.at[8, 0:N_HEADS].set(b_forget)
    head = np.arange(GROUP_W) // HEAD_DIM
    m64 = jnp.asarray((head[:, None] == head[None, :]).astype(np.float32), BF16)
    wuk = jnp.zeros((KV_RANK, N_HEADS * LANE), F32)
    for h in range(N_HEADS):
        wuk = wuk.at[:, LANE * h: LANE * h + MLA_NOPE].set(w_uk[:, MLA_NOPE * h: MLA_NOPE * (h + 1)])
    wuv = jnp.zeros((KV_RANK, N_HEADS * LANE), F32)
    for h in range(N_HEADS):
        lo = LANE * h + _value_lane0(h)
        wuv = wuv.at[:, lo: lo + HEAD_DIM].set(w_uv[:, HEAD_DIM * h: HEAD_DIM * (h + 1)])
    return w, gv, m64, wuk.astype(BF16), wuv.astype(BF16)


def _project(x, gmix, pw, tables, n_tab_blocks):
    w, gv, m64, wuk, wuv = pw
    t, d = x.shape
    ts = PROJ_ROWS
    assert t % ts == 0
    row = lambda i: (i, 0)
    full = lambda i: (0, 0)
    tab = lambda i: (i % n_tab_blocks, 0)
    f32_w = [GROUP_W] * 6 + [KV_RANK, LANE]
    bf_w = [512, 256, 256, 512, 256, 256, 512, 256, 256, 512, 512, 512]
    out_shape = [jax.ShapeDtypeStruct((t, c), F32) for c in f32_w] + \
                [jax.ShapeDtypeStruct((t, c), BF16) for c in bf_w]
    out_specs = [pl.BlockSpec((ts, c), row) for c in f32_w + bf_w]
    in_specs = [pl.BlockSpec((ts, d), row), pl.BlockSpec((1, d), full), pl.BlockSpec((d, N_COLS), full),
                pl.BlockSpec((ts, LANE), tab), pl.BlockSpec((ts, LANE), tab), pl.BlockSpec((ts, LANE), tab),
                pl.BlockSpec(gv.shape, full), pl.BlockSpec(m64.shape, full),
                pl.BlockSpec(wuk.shape, full), pl.BlockSpec(wuv.shape, full)]
    return pl.pallas_call(
        _proj_kernel, grid=(t // ts,), in_specs=in_specs, out_specs=out_specs, out_shape=out_shape,
        compiler_params=_cparams(1), name="proj",
    )(x, gmix.reshape(1, d), w, *tables, gv, m64, wuk, wuv)


def _softmax_step(q, kb, vb, carry, bias=None, ok=None):
    m, l, acc = carry
    s = _nt_dot(q, kb)
    if bias is not None:
        s = s + bias
    if ok is not None:
        s = jnp.where(ok, s, NEG_INF)
    m_new = jnp.maximum(m, jnp.max(s, axis=1, keepdims=True))
    alpha = jnp.exp2(m - m_new)
    p = jnp.exp2(s - m_new)
    l = alpha * l + jnp.sum(p, axis=1, keepdims=True)
    acc = alpha * acc + _dot(p.astype(BF16), vb)
    return m_new, l, acc


def _softmax_init(tq):
    return (jnp.full((tq, 1), NEG_INF, F32), jnp.zeros((tq, 1), F32), jnp.zeros((tq, LANE), F32))


def _pick_lane(block, h):
    col = lax.broadcasted_iota(jnp.int32, block.shape, 1)
    return jnp.sum(jnp.where(col == h, block, 0.0), axis=1, keepdims=True)


def _flash_forget_kernel(kmax_ref, decay_ref, q_ref, k_ref, v_ref, fq_ref, fk_ref, o_ref, *, tq, n_blk):
    b = pl.program_id(0)
    h = pl.program_id(1)
    qi = pl.program_id(2)
    q = q_ref[0]
    fq = _pick_lane(fq_ref[0], h) * LOG2E

    def step(j, carry, diag):
        off = pl.multiple_of(j * tq, tq)
        kb = k_ref[0, pl.ds(off, tq), :]
        vb = v_ref[0, pl.ds(off, tq), :]
        bias = fq - fk_ref[0, 0, pl.ds(j, 1), :] * LOG2E
        ok = None
        if diag:
            row = lax.broadcasted_iota(jnp.int32, (tq, tq), 0)
            col = lax.broadcasted_iota(jnp.int32, (tq, tq), 1)
            ok = col <= row
        return _softmax_step(q, kb, vb, carry, bias, ok)

    carry = step(qi, _softmax_init(tq), True)
    qf = q.astype(F32)
    q_norm = jnp.sqrt(jnp.sum(qf * qf, axis=1, keepdims=True))
    slack = jnp.max(q_norm * kmax_ref[b * N_HEADS + h] + fq - carry[0]) - DEAD_LOG2
    base = (b * N_HEADS + h) * n_blk

    def live(state):
        j = state[0]
        return jnp.logical_and(j >= 0, slack + decay_ref[base + jnp.maximum(j, 0)] >= 0.0)

    def older(state):
        j = state[0]
        return (j - 1,) + step(j, state[1:], False)

    _, m, l, acc = lax.while_loop(live, older, (qi - 1,) + carry)
    o_ref[0] = acc / l


def _flash_mla_kernel(q_ref, k_ref, v_ref, o_ref, *, tq, tk):
    h = pl.program_id(1)
    qi = pl.program_id(2)
    q = q_ref[0]
    n_sub = tq // tk

    def step(j, carry, diag):
        m, acc = carry
        off = pl.multiple_of(j * tk, tk)
        s = _nt_dot(q, k_ref[0, pl.ds(off, tk), :])
        if diag:
            row = lax.broadcasted_iota(jnp.int32, (tq, tk), 0) + qi * tq
            col = lax.broadcasted_iota(jnp.int32, (tq, tk), 1) + j * tk
            s = jnp.where(col // CHUNK <= row // CHUNK, s, NEG_INF)
        m_new = jnp.maximum(m, jnp.max(s, axis=1, keepdims=True))
        p = jnp.exp2(s - m_new)
        acc = jnp.exp2(m - m_new) * acc + _dot(p.astype(BF16), v_ref[0, pl.ds(off, tk), :])
        return m_new, acc

    carry = (jnp.full((tq, 1), NEG_INF, F32), jnp.zeros((tq, LANE), F32))
    carry = lax.fori_loop(0, qi * n_sub, lambda j, c: step(j, c, False), carry)
    for r in range(n_sub):
        carry = step(qi * n_sub + r, carry, True)
    acc = carry[1]
    denom = jnp.where(h % 2 == 0, acc[:, HEAD_DIM:HEAD_DIM + 1], acc[:, 0:1])
    o_ref[0] = acc / denom


def _flash_stick_kernel(q_ref, k_ref, v_ref, u_ref, o_ref, *, tq, ks):
    qi = pl.program_id(2)
    q = q_ref[0]
    n_sub = tq // ks

    def step(jb, carry, diag):
        c, acc = carry
        off = pl.multiple_of(jb * ks, ks)
        kb = k_ref[0, pl.ds(off, ks), :]
        vb = v_ref[0, pl.ds(off, ks), :]
        vis = None
        if diag:
            row = lax.broadcasted_iota(jnp.int32, (tq, ks), 0) + qi * tq
            col = lax.broadcasted_iota(jnp.int32, (tq, ks), 1) + jb * ks
            vis = col < row
        c_new, w = _stick_weights(_nt_dot(q, kb), c, u_ref[...], vis)
        return c_new, acc + _dot(w.astype(BF16), vb)

    carry = (jnp.zeros((tq, 1), F32), jnp.zeros((tq, LANE), F32))
    for r in range(n_sub):
        carry = step(qi * n_sub + (n_sub - 1 - r), carry, True)

    def live(state):
        return jnp.logical_and(state[0] >= 0, state[1] > DEAD_LOG2)

    def older(state):
        c, acc = step(state[0], state[2:], False)
        return state[0] - 1, jnp.max(c), c, acc

    _, _, _, acc = lax.while_loop(live, older, (qi * n_sub - 1, jnp.max(carry[0])) + carry)
    o_ref[0] = acc


def _stick_weights(z, c, u, vis):
    sp = jnp.log(1.0 + jnp.exp2(-jnp.abs(z))) * LOG2E
    log_rest = jnp.minimum(-z, 0.0) - sp
    log_beta = log_rest + z
    if vis is not None:
        log_rest = jnp.where(vis, log_rest, 0.0)
    hi, lo = _split2(log_rest)
    between = _dot(hi, u) + _dot(lo, u)
    w = jnp.exp2(log_beta + between + c)
    if vis is not None:
        w = jnp.where(vis, w, 0.0)
    return c + jnp.sum(log_rest, axis=1, keepdims=True), w


def _band_kernel(q_ref, k_ref, v_ref, bd_ref, bp_ref, o_ref, *, tq):
    qi = pl.program_id(2)
    q = q_ref[0]
    off = pl.multiple_of(qi * tq, tq)
    carry = _softmax_step(q, k_ref[0, pl.ds(off, tq), :], v_ref[0, pl.ds(off, tq), :],
                          _softmax_init(tq), bd_ref[0])
    offp = pl.multiple_of(jnp.maximum(qi - 1, 0) * tq, tq)
    no_prev = jnp.where(qi == 0, NEG_INF, 0.0)
    m, l, acc = _softmax_step(q, k_ref[0, pl.ds(offp, tq), :], v_ref[0, pl.ds(offp, tq), :],
                              carry, bp_ref[0] + no_prev)
    o_ref[0] = acc / l


def _attn_specs(s, tq, k_per_head):
    q_spec = pl.BlockSpec((1, tq, LANE), lambda b, h, i: (b, i, h))
    k_spec = pl.BlockSpec((1, s, LANE), (lambda b, h, i: (b, 0, h)) if k_per_head else (lambda b, h, i: (b, 0, h // 2)))
    v_spec = pl.BlockSpec((1, s, LANE), lambda b, h, i: (b, 0, h // 2))
    return q_spec, k_spec, v_spec


def _attn_call(kern, nb, s, tq, in_specs, operands, name):
    return pl.pallas_call(
        kern, grid=(nb, N_HEADS, s // tq), in_specs=in_specs,
        out_specs=pl.BlockSpec((1, tq, LANE), lambda b, h, i: (b, i, h)),
        out_shape=jax.ShapeDtypeStruct((nb, s, N_HEADS * LANE), F32),
        compiler_params=_cparams(3), name=name,
    )(*operands)


def _strict_upper(n):
    idx = np.arange(n)
    return jnp.asarray((idx[:, None] > idx[None, :]).astype(np.float32), BF16)


def _toeplitz(vec, n_rows, n_cols):
    length = n_rows + n_cols - 1
    assert vec.shape[-1] == length
    lead = vec.shape[:-1]
    rev = jnp.concatenate([vec[..., ::-1], jnp.zeros(lead + (1,), vec.dtype)], axis=-1)
    flat = jnp.tile(rev, (1,) * len(lead) + (n_rows,))[..., :n_rows * length]
    return flat.reshape(lead + (n_rows, length))[..., n_rows - 1: n_rows - 1 + n_cols]


def _rel_bias_tile(rel_bias, n_rows, n_cols, rel00, ok):
    d = np.arange(n_rows + n_cols - 1) - (n_cols - 1) + rel00
    vec = rel_bias.astype(F32)[:, np.clip(d, -REL_CLIP, REL_CLIP) + REL_CLIP] * LOG2E
    return jnp.where(jnp.asarray(ok)[None], _toeplitz(vec, n_rows, n_cols), NEG_INF)


def _band_bias_tiles(rel_bias, tq):
    i = np.arange(tq)[:, None]
    j = np.arange(tq)[None, :]
    own = _rel_bias_tile(rel_bias, tq, tq, 0, (j // CHUNK) <= (i // CHUNK))
    prev = _rel_bias_tile(rel_bias, tq, tq, tq, (j // CHUNK) >= (i // CHUNK) + tq // CHUNK - A_LEFT_CHUNKS)
    return own, prev


def _prompt_attention(pr, nb, s, rel_bias):
    tq = ATTN_TQ
    assert s % tq == 0 and tq == A_WINDOW
    r3 = lambda a: a.reshape(nb, s, a.shape[-1])
    qa, ka, va, qb, kb, vb, qc, kc, vc, qd, kd, vd = [r3(a) for a in pr[8:20]]
    q_spec, kp_spec, v_spec = _attn_specs(s, tq, False)

    bd, bp = _band_bias_tiles(rel_bias, tq)
    b_spec = pl.BlockSpec((1, tq, tq), lambda b, h, i: (h, 0, 0))
    o_a = _attn_call(functools.partial(_band_kernel, tq=tq), nb, s, tq,
                     [q_spec, kp_spec, v_spec, b_spec, b_spec], (qa, ka, va, bd, bp), "attn_band")

    u = _strict_upper(STICK_KS)
    o_b = _attn_call(functools.partial(_flash_stick_kernel, tq=tq, ks=STICK_KS), nb, s, tq,
                     [q_spec, kp_spec, v_spec, pl.BlockSpec(u.shape, lambda b, h, i: (0, 0))],
                     (qb, kb, vb, u), "attn_stick")

    n_blk = s // tq
    clf = r3(pr[7])[..., 0:N_HEADS]
    c_cum = jnp.cumsum(clf, axis=1)
    fk = jnp.swapaxes(c_cum, 1, 2).reshape(nb, N_HEADS, n_blk, tq)
    kc_f = kc.astype(F32).reshape(nb, s, N_HEADS, HEAD_DIM)
    kmax = jnp.sqrt(jnp.max(jnp.sum(kc_f * kc_f, axis=-1), axis=1)) * BOUND_MARGIN
    decay = lax.cummax(jnp.max(-fk, axis=-1), axis=2) * LOG2E
    smem = pl.BlockSpec(memory_space=pltpu.SMEM)
    fq_spec = pl.BlockSpec((1, tq, N_HEADS), lambda b, h, i: (b, i, 0))
    fk_spec = pl.BlockSpec((1, 1, n_blk, tq), lambda b, h, i: (b, h, 0, 0))
    o_c = _attn_call(functools.partial(_flash_forget_kernel, tq=tq, n_blk=n_blk), nb, s, tq,
                     [smem, smem, q_spec, kp_spec, v_spec, fq_spec, fk_spec],
                     (kmax.reshape(-1), decay.reshape(-1), qc, kc, vc, c_cum, fk), "attn_forget")

    tqd = MLA_TQ if s % MLA_TQ == 0 else tq
    qd_spec, kh_spec, _ = _attn_specs(s, tqd, True)
    vh_spec = pl.BlockSpec((1, s, LANE), lambda b, h, i: (b, 0, h))
    o_d = _attn_call(functools.partial(_flash_mla_kernel, tq=tqd, tk=ATTN_TK), nb, s, tqd,
                     [qd_spec, kh_spec, vh_spec], (qd, kd, vd), "attn_mla")
    return o_a, o_b, o_c, o_d


def _two_block_softmax(s1, s2, v1, v2):
    m = jnp.maximum(jnp.max(s1, axis=1, keepdims=True), jnp.max(s2, axis=1, keepdims=True))
    p1 = jnp.exp2(s1 - m)
    p2 = jnp.exp2(s2 - m)
    l = jnp.sum(p1, axis=1, keepdims=True) + jnp.sum(p2, axis=1, keepdims=True)
    return (_dot(p1.astype(BF16), v1) + _dot(p2.astype(BF16), v2)) / l


def _sample_kernel(qa_ref, ka_ref, va_ref, qb_ref, kb_ref, vb_ref, qc_ref, kc_ref, vc_ref, qd_ref, kd_ref, vd_ref,
                   cak_ref, cav_ref, cbk_ref, cbv_ref, cck_ref, ccv_ref, cckv_ref, ckr_ref,
                   ba_c_ref, ba_n_ref, fq_ref, fkc_ref, fkn_ref,
                   u_ref, un_ref, wuk_ref, wuv_ref, gk_ref, place_ref,
                   oa_ref, ob_ref, oc_ref, od_ref, *, t, past, ks):
    row = lax.broadcasted_iota(jnp.int32, (t, t), 0)
    col = lax.broadcasted_iota(jnp.int32, (t, t), 1)
    causal_bias = jnp.where(col <= row, 0.0, NEG_INF)
    chunk_bias = jnp.where((past + col) // CHUNK <= (past + row) // CHUNK, 0.0, NEG_INF)
    strict = col < row

    ckv_c = cckv_ref[0].astype(BF16)
    kn_c = _dot(ckv_c, wuk_ref[...])
    vd_c = _dot(ckv_c, wuv_ref[...]).astype(BF16)
    kr_c = _dot(ckr_ref[0].astype(BF16), place_ref[...])

    for h in range(N_HEADS):
        hs = slice(LANE * h, LANE * h + LANE)
        ps = slice(LANE * (h // 2), LANE * (h // 2) + LANE)

        q = qa_ref[0][:, hs]
        s1 = _nt_dot(q, cak_ref[0][:, ps].astype(BF16)) + ba_c_ref[h]
        s2 = _nt_dot(q, ka_ref[0][:, ps]) + ba_n_ref[h]
        oa_ref[0, :, hs] = _two_block_softmax(s1, s2, cav_ref[0][:, ps].astype(BF16), va_ref[0][:, ps])

        q = qb_ref[0][:, hs]
        c, w = _stick_weights(_nt_dot(q, kb_ref[0][:, ps]), jnp.zeros((t, 1), F32), un_ref[...], strict)
        acc = _dot(w.astype(BF16), vb_ref[0][:, ps])

        def b_live(state):
            return jnp.logical_and(state[0] >= 0, state[1] > DEAD_LOG2)

        def b_older(state, q=q, ps=ps):
            jb, _, c, acc = state
            off = pl.multiple_of(jb * ks, ks)
            kb = cbk_ref[0, pl.ds(off, ks), ps].astype(BF16)
            vb = cbv_ref[0, pl.ds(off, ks), ps].astype(BF16)
            c, w = _stick_weights(_nt_dot(q, kb), c, u_ref[...], None)
            return jb - 1, jnp.max(c), c, acc + _dot(w.astype(BF16), vb)

        _, _, _, acc = lax.while_loop(b_live, b_older, (past // ks - 1, jnp.max(c), c, acc))
        ob_ref[0, :, hs] = acc

        q = qc_ref[0][:, hs]
        fq = fq_ref[0][:, h:h + 1] * LOG2E
        s1 = _nt_dot(q, cck_ref[0][:, ps].astype(BF16)) + (fq - fkc_ref[0, h:h + 1, :] * LOG2E)
        s2 = _nt_dot(q, kc_ref[0][:, ps]) + (fq - fkn_ref[0, h:h + 1, :] * LOG2E) + causal_bias
        oc_ref[0, :, hs] = _two_block_softmax(s1, s2, ccv_ref[0][:, ps].astype(BF16), vc_ref[0][:, ps])

        q = qd_ref[0][:, hs]
        xh = kn_c[:, hs]
        ss = jnp.sum(xh * xh, axis=-1, keepdims=True)
        k_c = (xh * lax.rsqrt(ss * (1.0 / MLA_NOPE) + NORM_EPS) * gk_ref[...] + kr_c).astype(BF16)
        s1 = _nt_dot(q, k_c)
        s2 = _nt_dot(q, kd_ref[0][:, hs]) + chunk_bias
        od_ref[0, :, hs] = _two_block_softmax(s1, s2, vd_c[:, hs], vd_ref[0][:, hs])


def _sample_attention(pr, caches, pw, rel_bias, nb, t):
    a_k, a_v, b_k, b_v, c_k, c_v, c_lf, d_ckv, d_kr = caches
    _, gv, _, wuk, wuv = pw
    past = b_k.shape[1]
    win = a_k.shape[1]
    ks = STICK_KS
    assert past % ks == 0
    r3 = lambda a: a.reshape(nb, t, a.shape[-1])
    news = [r3(a) for a in pr[8:20]]
    flat = lambda a: a.reshape(nb, a.shape[1], GROUP_W)

    qpos = past + np.arange(t)
    kpos = past - win + np.arange(win + t)
    qc, kc = qpos // CHUNK, kpos // CHUNK
    ok = (kc[None, :] <= qc[:, None]) & (kc[None, :] >= qc[:, None] - A_LEFT_CHUNKS)
    ba = _rel_bias_tile(rel_bias, t, win + t, win, ok)
    ba_c, ba_n = ba[:, :, :win], ba[:, :, win:]

    clf_new = r3(pr[7])[..., 0:N_HEADS]
    c_cum = jnp.cumsum(jnp.concatenate([c_lf.astype(F32), clf_new], axis=1), axis=1)
    fq = c_cum[:, past:]
    fk = jnp.swapaxes(c_cum, 1, 2)
    fk_c, fk_n = fk[:, :, :past], fk[:, :, past:]

    place = np.zeros((MLA_ROPE, LANE), np.float32)
    place[np.arange(MLA_ROPE), ROPE_LO + np.arange(MLA_ROPE)] = 1.0
    place = jnp.asarray(place, BF16)
    gk = gv[5:6, 0:LANE]
    u, un = _strict_upper(ks), _strict_upper(t)

    per_b = lambda shape: pl.BlockSpec((1,) + shape, lambda b: (b,) + (0,) * len(shape))
    const = lambda a: pl.BlockSpec(a.shape, lambda b: (0,) * a.ndim)
    operands = news + [flat(a_k), flat(a_v), flat(b_k), flat(b_v), flat(c_k), flat(c_v), d_ckv, d_kr,
                       ba_c, ba_n, fq, fk_c, fk_n, u, un, wuk, wuv, gk, place]
    in_specs = [per_b(a.shape[1:]) for a in operands[:20]] + [const(ba_c), const(ba_n)] + \
               [per_b(a.shape[1:]) for a in (fq, fk_c, fk_n)] + [const(a) for a in (u, un, wuk, wuv, gk, place)]
    out_shape = [jax.ShapeDtypeStruct((nb, t, N_HEADS * LANE), F32)] * 4
    out_specs = [per_b((t, N_HEADS * LANE))] * 4
    return pl.pallas_call(
        functools.partial(_sample_kernel, t=t, past=past, ks=ks), grid=(nb,),
        in_specs=in_specs, out_specs=out_specs, out_shape=out_shape,
        compiler_params=_cparams(1), name="attn_sample",
    )(*operands)


def _out_kernel(oa_ref, ob_ref, oc_ref, od_ref, h_ref, gg_ref, wo_ref, nf_ref,
                wr1_ref, wr2_ref, wr3_ref, br_ref, h1_ref, hn_ref, lg_ref):
    rows = h_ref.shape[0]
    low = lax.broadcasted_iota(jnp.int32, (rows, LANE), 1) < HEAD_DIM
    h1 = h_ref[...]
    for g, o_ref in enumerate((oa_ref, ob_ref, oc_ref, od_ref)):
        p0 = jnp.where(low, o_ref[:, 0:LANE], o_ref[:, LANE:2 * LANE])
        p1 = jnp.where(low, o_ref[:, 2 * LANE:3 * LANE], o_ref[:, 3 * LANE:4 * LANE])
        og = jnp.concatenate([p0, p1], axis=1)
        ms = jnp.mean(og * og, axis=-1, keepdims=True)
        y = (og * lax.rsqrt(ms + NORM_EPS) * gg_ref[g:g + 1, :]).astype(BF16)
        h1 = h1 + _dot(y, wo_ref[GROUP_W * g: GROUP_W * (g + 1), :])
    h1_ref[...] = h1
    ms = jnp.mean(h1 * h1, axis=-1, keepdims=True)
    hn = h1 * lax.rsqrt(ms + NORM_EPS) * nf_ref[...]
    hn_ref[...] = hn.astype(BF16)
    a1, a2, a3 = _split3(hn)
    w1, w2, w3 = wr1_ref[...], wr2_ref[...], wr3_ref[...]
    lg = _dot(a1, w1) + _dot(a1, w2) + _dot(a2, w1) + _dot(a1, w3) + _dot(a2, w2) + _dot(a3, w1)
    lg_ref[...] = lg + br_ref[...]


def _out_project(outs, h, group_gain, w_out_b, norm_ffn, wr_parts, br_pad):
    t, d = h.shape
    ts = DENSE_ROWS
    assert t % ts == 0
    row = lambda i: (i, 0)
    full = lambda i: (0, 0)
    o_spec = pl.BlockSpec((ts, N_HEADS * LANE), row)
    in_specs = [o_spec] * 4 + [pl.BlockSpec((ts, d), row), pl.BlockSpec(group_gain.shape, full),
                               pl.BlockSpec(w_out_b.shape, full), pl.BlockSpec((1, d), full)] + \
               [pl.BlockSpec((d, LANE), full)] * 3 + [pl.BlockSpec((1, LANE), full)]
    out_shape = [jax.ShapeDtypeStruct((t, d), F32), jax.ShapeDtypeStruct((t, d), BF16),
                 jax.ShapeDtypeStruct((t, LANE), F32)]
    out_specs = [pl.BlockSpec((ts, d), row), pl.BlockSpec((ts, d), row), pl.BlockSpec((ts, LANE), row)]
    return pl.pallas_call(
        _out_kernel, grid=(t // ts,), in_specs=in_specs, out_specs=out_specs, out_shape=out_shape,
        compiler_params=_cparams(1), name="out_proj",
    )(*outs, h, group_gain, w_out_b, norm_ffn.reshape(1, d), *wr_parts, br_pad)


def _expert_kernel(be_ref, fe_ref, nu_ref, x_ref, wu_ref, bu_ref, wd_ref, bd_ref, y_ref, wub_ref, wdb_ref):
    i = pl.program_id(0)
    used = i < nu_ref[0]

    @pl.when(jnp.logical_and(used, fe_ref[i] == 1))
    def _():
        wub_ref[...] = wu_ref[0, 0].astype(BF16)
        wdb_ref[...] = wd_ref[0, 0].astype(BF16)

    @pl.when(used)
    def _():
        u = _dot(x_ref[...], wub_ref[...]) + bu_ref[0, 0]
        glu = jnp.minimum(u[:, :D_FF], SWIGLU_LIMIT)
        lin = jnp.clip(u[:, D_FF:], -SWIGLU_LIMIT, SWIGLU_LIMIT)
        act = glu * jax.nn.sigmoid(SWIGLU_ALPHA * glu) * (lin + 1.0)
        y_ref[...] = _dot(act.astype(BF16), wdb_ref[...]) + bd_ref[0, 0]


def _expert_ffn(x_sorted, blk_exp, blk_first, n_used, layer, w_up, b_up, w_down, b_down):
    rows, d = x_sorted.shape
    n_blk = rows // MOE_BLOCK
    last = lambda i, nu: jnp.minimum(i, nu[0] - 1)
    w_idx = lambda i, be, fe, nu: (layer, be[last(i, nu)], 0, 0)
    grid_spec = pltpu.PrefetchScalarGridSpec(
        num_scalar_prefetch=3, grid=(n_blk,),
        in_specs=[pl.BlockSpec((MOE_BLOCK, d), lambda i, be, fe, nu: (last(i, nu), 0)),
                  pl.BlockSpec((1, 1, d, 2 * D_FF), w_idx), pl.BlockSpec((1, 1, 1, 2 * D_FF), w_idx),
                  pl.BlockSpec((1, 1, D_FF, d), w_idx), pl.BlockSpec((1, 1, 1, d), w_idx)],
        out_specs=pl.BlockSpec((MOE_BLOCK, d), lambda i, be, fe, nu: (last(i, nu), 0)),
        scratch_shapes=[pltpu.VMEM((d, 2 * D_FF), BF16), pltpu.VMEM((D_FF, d), BF16)])
    depth = w_up.shape[0]
    return pl.pallas_call(
        _expert_kernel, grid_spec=grid_spec, out_shape=jax.ShapeDtypeStruct((rows, d), F32),
        compiler_params=_cparams(1), name="expert_ffn",
    )(blk_exp, blk_first, n_used, x_sorted, w_up, b_up.reshape(depth, N_EXP, 1, 2 * D_FF),
      w_down, b_down.reshape(depth, N_EXP, 1, d))


def _moe(hn, logits, layer, w_up, b_up, w_down, b_down):
    n_tok, d = hn.shape
    top_v, top_i = lax.top_k(logits, TOP_K)
    gates = jax.nn.softmax(top_v, axis=-1)
    n = n_tok * TOP_K
    e_flat = top_i.reshape(-1)
    onehot = (e_flat[:, None] == jnp.arange(N_EXP)[None, :]).astype(jnp.int32)
    rank = jnp.sum((jnp.cumsum(onehot, axis=0) - onehot) * onehot, axis=1)
    counts = jnp.sum(onehot, axis=0)
    padded = (counts + MOE_BLOCK - 1) // MOE_BLOCK * MOE_BLOCK
    p_end = jnp.cumsum(padded)
    p_start = p_end - padded
    dest = p_start[e_flat] + rank
    n_blk = -(-n // MOE_BLOCK) + N_EXP
    rows = n_blk * MOE_BLOCK
    row_tok = jnp.full((rows,), n_tok, jnp.int32).at[dest].set(jnp.arange(n, dtype=jnp.int32) // TOP_K)
    blk_exp = jnp.clip(jnp.searchsorted(p_end, jnp.arange(n_blk) * MOE_BLOCK, side='right'), 0, N_EXP - 1)
    blk_exp = blk_exp.astype(jnp.int32)
    blk_first = jnp.concatenate([jnp.ones((1,), jnp.int32), (blk_exp[1:] != blk_exp[:-1]).astype(jnp.int32)])
    n_used = (p_end[-1] // MOE_BLOCK).astype(jnp.int32).reshape(1)
    hn_pad = jnp.concatenate([hn, jnp.zeros((1, d), hn.dtype)], axis=0)
    x_sorted = hn_pad[row_tok]
    y = _expert_ffn(x_sorted, blk_exp, blk_first, n_used, layer, w_up, b_up, w_down, b_down)
    picked = y[dest.reshape(n_tok, TOP_K).T]
    return picked, gates


def _ple_kernel(h_ref, y_ref, g_ref, p_ref, np_ref, wg_ref, wp_ref, o_ref):
    h2 = h_ref[...]
    g = g_ref[...]
    for j in range(TOP_K):
        h2 = h2 + y_ref[j] * g[:, j:j + 1]
    ms = jnp.mean(h2 * h2, axis=-1, keepdims=True)
    hn = (h2 * lax.rsqrt(ms + NORM_EPS) * np_ref[...]).astype(BF16)
    gate = jax.nn.sigmoid(_dot(hn, wg_ref[...]))
    o_ref[...] = h2 + gate * _dot(p_ref[...].astype(BF16), wp_ref[...])


def _ple(h1, picked, gates, row0, p, norm_ple, wg_b, wp_b):
    t, d = h1.shape
    ts = PLE_ROWS
    assert t % ts == 0 and row0 % ts == 0
    off = row0 // ts
    row = lambda i: (i, 0)
    full = lambda i: (0, 0)
    return pl.pallas_call(
        _ple_kernel, grid=(t // ts,),
        in_specs=[pl.BlockSpec((ts, d), row), pl.BlockSpec((TOP_K, ts, d), lambda i: (0, i + off, 0)),
                  pl.BlockSpec((ts, TOP_K), lambda i: (i + off, 0)), pl.BlockSpec((ts, p.shape[1]), row),
                  pl.BlockSpec((1, d), full), pl.BlockSpec(wg_b.shape, full), pl.BlockSpec(wp_b.shape, full)],
        out_specs=pl.BlockSpec((ts, d), row), out_shape=jax.ShapeDtypeStruct((t, d), F32),
        compiler_params=_cparams(1), name="ple",
    )(h1, picked, gates, p, norm_ple.reshape(1, d), wg_b, wp_b)


def kernel(x_prompt, x_sample, p_prompt, p_sample, cache_a_k, cache_a_v, cache_b_k, cache_b_v, cache_c_k, cache_c_v, cache_c_logf, cache_d_ckv, cache_d_krope, norm_mix, w_in, b_forget, qk_gain, rope_gain, kv_gain, w_uk, w_uv, rel_bias, group_gain, w_out, norm_ffn, w_router, b_router, w_up, b_up, w_down, b_down, norm_ple, w_ple_gate, w_ple_proj):
    nb, s, d = x_prompt.shape
    nd, t, _ = x_sample.shape
    depth = w_in.shape[0]
    past = cache_b_k.shape[2]
    assert PROJ_ROWS % t == 0 and s % PROJ_ROWS == 0

    tab_p = _rope_tables(jnp.arange(s))
    tab_s = _rope_tables(past + jnp.arange(PROJ_ROWS) % t)
    hp = x_prompt.reshape(nb * s, d)
    hs = x_sample.reshape(nd * t, d)
    st_p, st_s = [], []
    keep = min(A_WINDOW, s)
    for i in range(depth):
        pw = _proj_weights(w_in[i], b_forget[i], qk_gain[i], rope_gain[i], kv_gain[i], w_uk[i], w_uv[i])
        pr_p = _project(hp, norm_mix[i], pw, tab_p, s // PROJ_ROWS)
        pr_s = _project(hs, norm_mix[i], pw, tab_s, 1)
        outs_p = _prompt_attention(pr_p, nb, s, rel_bias[i])
        caches = (cache_a_k[i], cache_a_v[i], cache_b_k[i], cache_b_v[i], cache_c_k[i], cache_c_v[i],
                  cache_c_logf[i], cache_d_ckv[i], cache_d_krope[i])
        outs_s = _sample_attention(pr_s, caches, pw, rel_bias[i], nd, t)

        def heads(a, n, rows):
            return a.reshape(n, rows, N_HEADS, HEAD_DIM)

        p3 = lambda a: a.reshape(nb, s, a.shape[-1])
        s3 = lambda a: a.reshape(nd, t, a.shape[-1])
        st_p.append((heads(p3(pr_p[0])[:, s - keep:], nb, keep), heads(p3(pr_p[1])[:, s - keep:], nb, keep),
                     heads(pr_p[2], nb, s), heads(pr_p[3], nb, s), heads(pr_p[4], nb, s), heads(pr_p[5], nb, s),
                     p3(pr_p[7])[..., 0:N_HEADS], p3(pr_p[6]), p3(pr_p[7])[..., ROPE_LO:ROPE_LO + MLA_ROPE]))
        ka_all = jnp.concatenate([cache_a_k[i], heads(pr_s[0], nd, t)], axis=1)[:, t:]
        va_all = jnp.concatenate([cache_a_v[i], heads(pr_s[1], nd, t)], axis=1)[:, t:]
        st_s.append((ka_all, va_all,
                     heads(pr_s[2], nd, t), heads(pr_s[3], nd, t), heads(pr_s[4], nd, t), heads(pr_s[5], nd, t),
                     s3(pr_s[7])[..., 0:N_HEADS], s3(pr_s[6]), s3(pr_s[7])[..., ROPE_LO:ROPE_LO + MLA_ROPE]))

        w_out_b = w_out[i].astype(BF16)
        wr_pad = jnp.zeros((d, LANE), F32).at[:, 0:N_EXP].set(w_router[i])
        wr_parts = _split3(wr_pad)
        br_pad = jnp.zeros((1, LANE), F32).at[0, 0:N_EXP].set(b_router[i])
        flat4 = lambda o: o.reshape(-1, N_HEADS * LANE)
        h1_p, hn_p, lg_p = _out_project([flat4(o) for o in outs_p], hp, group_gain[i], w_out_b, norm_ffn[i],
                                        wr_parts, br_pad)
        h1_s, hn_s, lg_s = _out_project([flat4(o) for o in outs_s], hs, group_gain[i], w_out_b, norm_ffn[i],
                                        wr_parts, br_pad)

        hn_all = jnp.concatenate([hn_p, hn_s], axis=0)
        lg_all = jnp.concatenate([lg_p, lg_s], axis=0)[:, 0:N_EXP]
        picked, gates = _moe(hn_all, lg_all, i, w_up, b_up, w_down, b_down)

        wg_b = w_ple_gate[i].astype(BF16)
        wp_b = w_ple_proj[i].astype(BF16)
        hp = _ple(h1_p, picked, gates, 0, p_prompt[i].reshape(nb * s, -1), norm_ple[i], wg_b, wp_b)
        hs = _ple(h1_s, picked, gates, nb * s, p_sample[i].reshape(nd * t, -1), norm_ple[i], wg_b, wp_b)

    state_p = [jnp.stack([st[j] for st in st_p]) for j in range(9)]
    state_s = [jnp.stack([st[j] for st in st_s]) for j in range(9)]
    return (hp.reshape(nb, s, d), hs.reshape(nd, t, d), *state_p, *state_s)
```

```python
import functools

import numpy as np
import jax
import jax.numpy as jnp
from jax import lax
from jax.experimental import pallas as pl
from jax.experimental.pallas import tpu as pltpu

F32 = jnp.float32
BF16 = jnp.bfloat16

CHUNK = 64
HEAD_DIM = 64
N_HEADS = 4
GROUP_W = 256
A_LEFT_CHUNKS = 8
A_WINDOW = A_LEFT_CHUNKS * CHUNK
REL_CLIP = 128
MLA_NOPE = 64
MLA_ROPE = 32
KV_RANK = 128
ROPE_BASE = 10000.0
N_EXP = 32
TOP_K = 4
D_FF = 1024
SWIGLU_LIMIT = 7.0
SWIGLU_ALPHA = 1.702
MOE_BLOCK = 512
NORM_EPS = 1e-6
NEG_INF = -1e30
LOG2E = 1.4426950408889634

A_Q = 0
C_F = 2304
D_Q = 2308
D_CKV = D_Q + N_HEADS * (MLA_NOPE + MLA_ROPE)
D_KR = D_CKV + KV_RANK

LANE = 128
SEG_ABC = 0
SEG_DQ = 2304
SEG_CKV = 2816
SEG_MISC = 2944
N_COLS = 3072
ROPE_LO = 64
ROPE_HALF = MLA_ROPE // 2

VMEM_LIMIT = 56 * 1024 * 1024

PROJ_ROWS = 512
ATTN_TQ = 512
ATTN_TK = 512
MLA_TQ = 2048
STICK_KS = 256
DENSE_ROWS = 512
PLE_ROWS = 256

DEAD_LOG2 = -160.0
BOUND_MARGIN = 1.01


def _cparams(n_axes):
    return pltpu.CompilerParams(dimension_semantics=("arbitrary",) * n_axes,
                                vmem_limit_bytes=VMEM_LIMIT)


def _nt_dot(a, b):
    return lax.dot_general(a, b, (((1,), (1,)), ((), ())), preferred_element_type=F32)


def _dot(a, b):
    return jnp.dot(a, b, preferred_element_type=F32)


def _split2(x):
    hi = x.astype(BF16)
    lo = (x - hi.astype(F32)).astype(BF16)
    return hi, lo


def _split3(x):
    hi = x.astype(BF16)
    r = x - hi.astype(F32)
    mid = r.astype(BF16)
    lo = (r - mid.astype(F32)).astype(BF16)
    return hi, mid, lo


def _in_col_map():
    cols = np.full((N_COLS,), -1, np.int64)
    cols[0:2304] = np.arange(2304)
    for h in range(N_HEADS):
        base = D_Q + (MLA_NOPE + MLA_ROPE) * h
        cols[SEG_DQ + LANE * h: SEG_DQ + LANE * h + MLA_NOPE + MLA_ROPE] = base + np.arange(MLA_NOPE + MLA_ROPE)
    cols[SEG_CKV:SEG_CKV + KV_RANK] = D_CKV + np.arange(KV_RANK)
    cols[SEG_MISC:SEG_MISC + N_HEADS] = C_F + np.arange(N_HEADS)
    cols[SEG_MISC + ROPE_LO:SEG_MISC + ROPE_LO + MLA_ROPE] = D_KR + np.arange(MLA_ROPE)
    return cols


def _rope_tables(pos):
    inv = ROPE_BASE ** (-jnp.arange(ROPE_HALF, dtype=F32) / ROPE_HALF)
    ang = pos.astype(F32)[:, None] * inv
    cos, sin = jnp.cos(ang), jnp.sin(ang)
    n = pos.shape[0]
    one = jnp.ones((n, ROPE_LO), F32)
    z16 = jnp.zeros((n, ROPE_HALF), F32)
    z64 = jnp.zeros((n, ROPE_LO), F32)
    z32 = jnp.zeros((n, LANE - ROPE_LO - MLA_ROPE), F32)
    cos_t = jnp.concatenate([one, cos, cos, z32 + 1.0], axis=1)
    sin_a = jnp.concatenate([z64, -sin, z16, z32], axis=1)
    sin_b = jnp.concatenate([z64, z16, sin, z32], axis=1)
    return cos_t, sin_a, sin_b


def _rope(y, cos_t, sin_a, sin_b):
    left = pltpu.roll(y, LANE - ROPE_HALF, axis=1)
    right = pltpu.roll(y, ROPE_HALF, axis=1)
    return y * cos_t + left * sin_a + right * sin_b


def _head_norm(x, m_ref, gain):
    hi, lo = _split2(x * x)
    ssq = _dot(hi, m_ref[...]) + _dot(lo, m_ref[...])
    return x * lax.rsqrt(ssq * (1.0 / HEAD_DIM) + NORM_EPS) * gain


def _value_lane0(h):
    return HEAD_DIM * (h % 2)


def _ones_lane(h):
    return HEAD_DIM - _value_lane0(h)


def _store_padded_q(q_ref, q, scale):
    lane = lax.broadcasted_iota(jnp.int32, (q.shape[0], LANE), 1)
    low = lane < HEAD_DIM
    qs = q * scale
    for h in range(N_HEADS):
        pair = qs[:, LANE * (h // 2): LANE * (h // 2) + LANE]
        keep = low if h % 2 == 0 else jnp.logical_not(low)
        q_ref[:, LANE * h: LANE * h + LANE] = jnp.where(keep, pair, 0.0).astype(BF16)


def _proj_kernel(x_ref, gmix_ref, w_ref, cos_ref, sa_ref, sb_ref, gv_ref, m64_ref, wuk_ref, wuv_ref,
                 ak_ref, av_ref, bk_ref, bv_ref, ck_ref, cv_ref, ckv_ref, misc_ref,
                 qa_ref, ka_ref, va_ref, qb_ref, kb_ref, vb_ref, qc_ref, kc_ref, vc_ref,
                 qd_ref, kd_ref, vd_ref):
    x = x_ref[...]
    ms = jnp.mean(x * x, axis=-1, keepdims=True)
    hn = (x * lax.rsqrt(ms + NORM_EPS) * gmix_ref[...]).astype(BF16)
    sm_scale = (HEAD_DIM ** -0.5) * LOG2E
    rows = x.shape[0]
    lane = lax.broadcasted_iota(jnp.int32, (rows, LANE), 1)
    is_nope = lane < MLA_NOPE
    is_rope = jnp.logical_and(lane >= ROPE_LO, lane < ROPE_LO + MLA_ROPE)
    cos_t, sin_a, sin_b = cos_ref[...], sa_ref[...], sb_ref[...]

    z = _dot(hn, w_ref[:, 0:768])
    aq = _head_norm(z[:, 0:256], m64_ref, gv_ref[0:1, :])
    ak = _head_norm(z[:, 256:512], m64_ref, gv_ref[1:2, :])
    av = z[:, 512:768]
    ak_ref[...] = ak
    av_ref[...] = av
    _store_padded_q(qa_ref, aq, sm_scale)
    ka_ref[...] = ak.astype(BF16)
    va_ref[...] = av.astype(BF16)

    z = _dot(hn, w_ref[:, 768:1536])
    bk_ref[...] = z[:, 256:512]
    bv_ref[...] = z[:, 512:768]
    _store_padded_q(qb_ref, z[:, 0:256], sm_scale)
    kb_ref[...] = z[:, 256:512].astype(BF16)
    vb_ref[...] = z[:, 512:768].astype(BF16)

    z = _dot(hn, w_ref[:, 1536:2304])
    cq = _head_norm(z[:, 0:256], m64_ref, gv_ref[2:3, :])
    ck = _head_norm(z[:, 256:512], m64_ref, gv_ref[3:4, :])
    cv = z[:, 512:768]
    ck_ref[...] = ck
    cv_ref[...] = cv
    _store_padded_q(qc_ref, cq, sm_scale)
    kc_ref[...] = ck.astype(BF16)
    vc_ref[...] = cv.astype(BF16)

    z = _dot(hn, w_ref[:, SEG_CKV:N_COLS])
    zc = z[:, 0:KV_RANK]
    ckv = zc * lax.rsqrt(jnp.mean(zc * zc, axis=-1, keepdims=True) + NORM_EPS) * gv_ref[7:8, 0:LANE]
    ckv_ref[...] = ckv
    zm = z[:, KV_RANK:2 * KV_RANK]
    ssr = jnp.sum(jnp.where(is_rope, zm * zm, 0.0), axis=-1, keepdims=True)
    kr = zm * lax.rsqrt(ssr * (1.0 / MLA_ROPE) + NORM_EPS) * gv_ref[6:7, 0:LANE]
    kr = _rope(kr, cos_t, sin_a, sin_b)
    zf = zm + gv_ref[8:9, 0:LANE]
    clf = jnp.minimum(zf, 0.0) - jnp.log1p(jnp.exp(-jnp.abs(zf)))
    misc_ref[...] = jnp.where(lane < N_HEADS, clf, kr)

    ckv_b = ckv.astype(BF16)
    kn = _dot(ckv_b, wuk_ref[...])
    for h in range(N_HEADS):
        hs = slice(LANE * h, LANE * h + LANE)
        ones = (lane == _ones_lane(h)).astype(F32)
        vd_ref[:, hs] = (_dot(ckv_b, wuv_ref[:, hs]) + ones).astype(BF16)
    for h in range(N_HEADS):
        xh = kn[:, LANE * h: LANE * h + LANE]
        ss = jnp.sum(xh * xh, axis=-1, keepdims=True)
        yh = xh * lax.rsqrt(ss * (1.0 / MLA_NOPE) + NORM_EPS) * gv_ref[5:6, 0:LANE]
        kd_ref[:, LANE * h: LANE * h + LANE] = (yh + kr).astype(BF16)

    z = _dot(hn, w_ref[:, SEG_DQ:SEG_CKV])
    d_scale = ((MLA_NOPE + MLA_ROPE) ** -0.5) * LOG2E
    for h in range(N_HEADS):
        xh = z[:, LANE * h: LANE * h + LANE]
        x2 = xh * xh
        ssn = jnp.sum(jnp.where(is_nope, x2, 0.0), axis=-1, keepdims=True)
        ssr = jnp.sum(jnp.where(is_rope, x2, 0.0), axis=-1, keepdims=True)
        rn = lax.rsqrt(ssn * (1.0 / MLA_NOPE) + NORM_EPS)
        rr = lax.rsqrt(ssr * (1.0 / MLA_ROPE) + NORM_EPS)
        yh = xh * jnp.where(is_nope, rn, rr) * gv_ref[4:5, 0:LANE]
        yh = _rope(yh, cos_t, sin_a, sin_b)
        qd_ref[:, LANE * h: LANE * h + LANE] = (yh * d_scale).astype(BF16)


def _proj_weights(w_in, b_forget, qk_gain, rope_gain, kv_gain, w_uk, w_uv):
    cols = _in_col_map()
    valid = jnp.asarray(cols >= 0)
    w = jnp.where(valid[None, :], w_in[:, np.maximum(cols, 0)], 0.0).astype(BF16)

    def tile4(g):
        return jnp.tile(g, N_HEADS)

    gv = jnp.zeros((16, GROUP_W), F32)
    gv = gv.at[0].set(tile4(qk_gain[0])).at[1].set(tile4(qk_gain[1]))
    gv = gv.at[2].set(tile4(qk_gain[2])).at[3].set(tile4(qk_gain[3]))
    gv = gv.at[4, 0:MLA_NOPE].set(qk_gain[4]).at[4, ROPE_LO:ROPE_LO + MLA_ROPE].set(rope_gain[0])
    gv = gv.at[5, 0:MLA_NOPE].set(qk_gain[5])
    gv = gv.at[6, ROPE_LO:ROPE_LO + MLA_ROPE].set(rope_gain[1])
    gv = gv.at[7, 0:KV_RANK].set(kv_gain)
    gv = gv.at[8, 0:N_HEADS].set(b_forget)
    head = np.arange(GROUP_W) // HEAD_DIM
    m64 = jnp.asarray((head[:, None] == head[None, :]).astype(np.float32), BF16)
    wuk = jnp.zeros((KV_RANK, N_HEADS * LANE), F32)
    for h in range(N_HEADS):
        wuk = wuk.at[:, LANE * h: LANE * h + MLA_NOPE].set(w_uk[:, MLA_NOPE * h: MLA_NOPE * (h + 1)])
    wuv = jnp.zeros((KV_RANK, N_HEADS * LANE), F32)
    for h in range(N_HEADS):
        lo = LANE * h + _value_lane0(h)
        wuv = wuv.at[:, lo: lo + HEAD_DIM].set(w_uv[:, HEAD_DIM * h: HEAD_DIM * (h + 1)])
    return w, gv, m64, wuk.astype(BF16), wuv.astype(BF16)


def _project(x, gmix, pw, tables, n_tab_blocks):
    w, gv, m64, wuk, wuv = pw
    t, d = x.shape
    ts = PROJ_ROWS
    assert t % ts == 0
    row = lambda i: (i, 0)
    full = lambda i: (0, 0)
    tab = lambda i: (i % n_tab_blocks, 0)
    f32_w = [GROUP_W] * 6 + [KV_RANK, LANE]
    bf_w = [512, 256, 256, 512, 256, 256, 512, 256, 256, 512, 512, 512]
    out_shape = [jax.ShapeDtypeStruct((t, c), F32) for c in f32_w] + \
                [jax.ShapeDtypeStruct((t, c), BF16) for c in bf_w]
    out_specs = [pl.BlockSpec((ts, c), row) for c in f32_w + bf_w]
    in_specs = [pl.BlockSpec((ts, d), row), pl.BlockSpec((1, d), full), pl.BlockSpec((d, N_COLS), full),
                pl.BlockSpec((ts, LANE), tab), pl.BlockSpec((ts, LANE), tab), pl.BlockSpec((ts, LANE), tab),
                pl.BlockSpec(gv.shape, full), pl.BlockSpec(m64.shape, full),
                pl.BlockSpec(wuk.shape, full), pl.BlockSpec(wuv.shape, full)]
    return pl.pallas_call(
        _proj_kernel, grid=(t // ts,), in_specs=in_specs, out_specs=out_specs, out_shape=out_shape,
        compiler_params=_cparams(1), name="proj",
    )(x, gmix.reshape(1, d), w, *tables, gv, m64, wuk, wuv)


def _softmax_step(q, kb, vb, carry, bias=None, ok=None):
    m, l, acc = carry
    s = _nt_dot(q, kb)
    if bias is not None:
        s = s + bias
    if ok is not None:
        s = jnp.where(ok, s, NEG_INF)
    m_new = jnp.maximum(m, jnp.max(s, axis=1, keepdims=True))
    alpha = jnp.exp2(m - m_new)
    p = jnp.exp2(s - m_new)
    l = alpha * l + jnp.sum(p, axis=1, keepdims=True)
    acc = alpha * acc + _dot(p.astype(BF16), vb)
    return m_new, l, acc


def _softmax_init(tq):
    return (jnp.full((tq, 1), NEG_INF, F32), jnp.zeros((tq, 1), F32), jnp.zeros((tq, LANE), F32))


def _pick_lane(block, h):
    col = lax.broadcasted_iota(jnp.int32, block.shape, 1)
    return jnp.sum(jnp.where(col == h, block, 0.0), axis=1, keepdims=True)


def _flash_forget_kernel(kmax_ref, decay_ref, q_ref, k_ref, v_ref, fq_ref, fk_ref, o_ref, *, tq, n_blk):
    b = pl.program_id(0)
    h = pl.program_id(1)
    qi = pl.program_id(2)
    q = q_ref[0]
    fq = _pick_lane(fq_ref[0], h) * LOG2E

    def step(j, carry, diag):
        off = pl.multiple_of(j * tq, tq)
        kb = k_ref[0, pl.ds(off, tq), :]
        vb = v_ref[0, pl.ds(off, tq), :]
        bias = fq - fk_ref[0, 0, pl.ds(j, 1), :] * LOG2E
        ok = None
        if diag:
            row = lax.broadcasted_iota(jnp.int32, (tq, tq), 0)
            col = lax.broadcasted_iota(jnp.int32, (tq, tq), 1)
            ok = col <= row
        return _softmax_step(q, kb, vb, carry, bias, ok)

    carry = step(qi, _softmax_init(tq), True)
    qf = q.astype(F32)
    q_norm = jnp.sqrt(jnp.sum(qf * qf, axis=1, keepdims=True))
    slack = jnp.max(q_norm * kmax_ref[b * N_HEADS + h] + fq - carry[0]) - DEAD_LOG2
    base = (b * N_HEADS + h) * n_blk

    def live(state):
        j = state[0]
        return jnp.logical_and(j >= 0, slack + decay_ref[base + jnp.maximum(j, 0)] >= 0.0)

    def older(state):
        j = state[0]
        return (j - 1,) + step(j, state[1:], False)

    _, m, l, acc = lax.while_loop(live, older, (qi - 1,) + carry)
    o_ref[0] = acc / l


def _flash_mla_kernel(q_ref, k_ref, v_ref, o_ref, *, tq, tk):
    h = pl.program_id(1)
    qi = pl.program_id(2)
    q = q_ref[0]
    n_sub = tq // tk

    def step(j, carry, row0=None):
        m, acc = carry
        off = pl.multiple_of(j * tk, tk)
        qs = q if row0 is None else q[row0:]
        s = _nt_dot(qs, k_ref[0, pl.ds(off, tk), :])
        if row0 is not None:
            row = lax.broadcasted_iota(jnp.int32, s.shape, 0)
            col = lax.broadcasted_iota(jnp.int32, s.shape, 1)
            s = jnp.where(col // CHUNK <= row // CHUNK, s, NEG_INF)
        m_new = jnp.maximum(m, jnp.max(s, axis=1, keepdims=True))
        p = jnp.exp2(s - m_new)
        acc = jnp.exp2(m - m_new) * acc + _dot(p.astype(BF16), v_ref[0, pl.ds(off, tk), :])
        return m_new, acc

    carry = (jnp.full((tq, 1), NEG_INF, F32), jnp.zeros((tq, LANE), F32))
    m, acc = lax.fori_loop(0, qi * n_sub, lambda j, c: step(j, c), carry)
    for r in range(n_sub):
        row0 = r * tk
        m_r, acc_r = step(qi * n_sub + r, (m[row0:], acc[row0:]), row0)
        m = m_r if r == 0 else jnp.concatenate([m[:row0], m_r], axis=0)
        acc = acc_r if r == 0 else jnp.concatenate([acc[:row0], acc_r], axis=0)
    denom = jnp.where(h % 2 == 0, acc[:, HEAD_DIM:HEAD_DIM + 1], acc[:, 0:1])
    o_ref[0] = acc / denom


def _flash_stick_kernel(q_ref, k_ref, v_ref, u_ref, o_ref, *, tq, ks):
    qi = pl.program_id(2)
    q = q_ref[0]
    n_sub = tq // ks

    def step(jb, carry, diag):
        c, acc = carry
        off = pl.multiple_of(jb * ks, ks)
        kb = k_ref[0, pl.ds(off, ks), :]
        vb = v_ref[0, pl.ds(off, ks), :]
        vis = None
        if diag:
            row = lax.broadcasted_iota(jnp.int32, (tq, ks), 0) + qi * tq
            col = lax.broadcasted_iota(jnp.int32, (tq, ks), 1) + jb * ks
            vis = col < row
        c_new, w = _stick_weights(_nt_dot(q, kb), c, u_ref[...], vis)
        return c_new, acc + _dot(w.astype(BF16), vb)

    carry = (jnp.zeros((tq, 1), F32), jnp.zeros((tq, LANE), F32))
    for r in range(n_sub):
        carry = step(qi * n_sub + (n_sub - 1 - r), carry, True)

    def live(state):
        return jnp.logical_and(state[0] >= 0, state[1] > DEAD_LOG2)

    def older(state):
        c, acc = step(state[0], state[2:], False)
        return state[0] - 1, jnp.max(c), c, acc

    _, _, _, acc = lax.while_loop(live, older, (qi * n_sub - 1, jnp.max(carry[0])) + carry)
    o_ref[0] = acc


def _stick_weights(z, c, u, vis):
    sp = jnp.log(1.0 + jnp.exp2(-jnp.abs(z))) * LOG2E
    log_rest = jnp.minimum(-z, 0.0) - sp
    log_beta = log_rest + z
    if vis is not None:
        log_rest = jnp.where(vis, log_rest, 0.0)
    hi, lo = _split2(log_rest)
    between = _dot(hi, u) + _dot(lo, u)
    w = jnp.exp2(log_beta + between + c)
    if vis is not None:
        w = jnp.where(vis, w, 0.0)
    return c + jnp.sum(log_rest, axis=1, keepdims=True), w


def _band_kernel(q_ref, k_ref, v_ref, bd_ref, bp_ref, o_ref, *, tq):
    qi = pl.program_id(2)
    q = q_ref[0]
    off = pl.multiple_of(qi * tq, tq)
    carry = _softmax_step(q, k_ref[0, pl.ds(off, tq), :], v_ref[0, pl.ds(off, tq), :],
                          _softmax_init(tq), bd_ref[0])
    offp = pl.multiple_of(jnp.maximum(qi - 1, 0) * tq, tq)
    no_prev = jnp.where(qi == 0, NEG_INF, 0.0)
    m, l, acc = _softmax_step(q, k_ref[0, pl.ds(offp, tq), :], v_ref[0, pl.ds(offp, tq), :],
                              carry, bp_ref[0] + no_prev)
    o_ref[0] = acc / l


def _attn_specs(s, tq, k_per_head):
    q_spec = pl.BlockSpec((1, tq, LANE), lambda b, h, i: (b, i, h))
    k_spec = pl.BlockSpec((1, s, LANE), (lambda b, h, i: (b, 0, h)) if k_per_head else (lambda b, h, i: (b, 0, h // 2)))
    v_spec = pl.BlockSpec((1, s, LANE), lambda b, h, i: (b, 0, h // 2))
    return q_spec, k_spec, v_spec


def _attn_call(kern, nb, s, tq, in_specs, operands, name):
    return pl.pallas_call(
        kern, grid=(nb, N_HEADS, s // tq), in_specs=in_specs,
        out_specs=pl.BlockSpec((1, tq, LANE), lambda b, h, i: (b, i, h)),
        out_shape=jax.ShapeDtypeStruct((nb, s, N_HEADS * LANE), F32),
        compiler_params=_cparams(3), name=name,
    )(*operands)


def _strict_upper(n):
    idx = np.arange(n)
    return jnp.asarray((idx[:, None] > idx[None, :]).astype(np.float32), BF16)


def _toeplitz(vec, n_rows, n_cols):
    length = n_rows + n_cols - 1
    assert vec.shape[-1] == length
    lead = vec.shape[:-1]
    rev = jnp.concatenate([vec[..., ::-1], jnp.zeros(lead + (1,), vec.dtype)], axis=-1)
    flat = jnp.tile(rev, (1,) * len(lead) + (n_rows,))[..., :n_rows * length]
    return flat.reshape(lead + (n_rows, length))[..., n_rows - 1: n_rows - 1 + n_cols]


def _rel_bias_tile(rel_bias, n_rows, n_cols, rel00, ok):
    d = np.arange(n_rows + n_cols - 1) - (n_cols - 1) + rel00
    vec = rel_bias.astype(F32)[:, np.clip(d, -REL_CLIP, REL_CLIP) + REL_CLIP] * LOG2E
    return jnp.where(jnp.asarray(ok)[None], _toeplitz(vec, n_rows, n_cols), NEG_INF)


def _band_bias_tiles(rel_bias, tq):
    i = np.arange(tq)[:, None]
    j = np.arange(tq)[None, :]
    own = _rel_bias_tile(rel_bias, tq, tq, 0, (j // CHUNK) <= (i // CHUNK))
    prev = _rel_bias_tile(rel_bias, tq, tq, tq, (j // CHUNK) >= (i // CHUNK) + tq // CHUNK - A_LEFT_CHUNKS)
    return own, prev


def _prompt_attention(pr, nb, s, rel_bias, k_gain):
    tq = ATTN_TQ
    assert s % tq == 0 and tq == A_WINDOW
    r3 = lambda a: a.reshape(nb, s, a.shape[-1])
    qa, ka, va, qb, kb, vb, qc, kc, vc, qd, kd, vd = [r3(a) for a in pr[8:20]]
    q_spec, kp_spec, v_spec = _attn_specs(s, tq, False)

    bd, bp = _band_bias_tiles(rel_bias, tq)
    b_spec = pl.BlockSpec((1, tq, tq), lambda b, h, i: (h, 0, 0))
    o_a = _attn_call(functools.partial(_band_kernel, tq=tq), nb, s, tq,
                     [q_spec, kp_spec, v_spec, b_spec, b_spec], (qa, ka, va, bd, bp), "attn_band")

    u = _strict_upper(STICK_KS)
    o_b = _attn_call(functools.partial(_flash_stick_kernel, tq=tq, ks=STICK_KS), nb, s, tq,
                     [q_spec, kp_spec, v_spec, pl.BlockSpec(u.shape, lambda b, h, i: (0, 0))],
                     (qb, kb, vb, u), "attn_stick")

    n_blk = s // tq
    clf = r3(pr[7])[..., 0:N_HEADS]
    c_cum = jnp.cumsum(clf, axis=1)
    fk = jnp.swapaxes(c_cum, 1, 2).reshape(nb, N_HEADS, n_blk, tq)
    kmax = jnp.full((nb * N_HEADS,), HEAD_DIM ** 0.5 * BOUND_MARGIN, F32) * jnp.max(jnp.abs(k_gain))
    decay = lax.cummax(jnp.max(-fk, axis=-1), axis=2) * LOG2E
    smem = pl.BlockSpec(memory_space=pltpu.SMEM)
    fq_spec = pl.BlockSpec((1, tq, N_HEADS), lambda b, h, i: (b, i, 0))
    fk_spec = pl.BlockSpec((1, 1, n_blk, tq), lambda b, h, i: (b, h, 0, 0))
    o_c = _attn_call(functools.partial(_flash_forget_kernel, tq=tq, n_blk=n_blk), nb, s, tq,
                     [smem, smem, q_spec, kp_spec, v_spec, fq_spec, fk_spec],
                     (kmax.reshape(-1), decay.reshape(-1), qc, kc, vc, c_cum, fk), "attn_forget")

    tqd = MLA_TQ if s % MLA_TQ == 0 else tq
    qd_spec, kh_spec, _ = _attn_specs(s, tqd, True)
    vh_spec = pl.BlockSpec((1, s, LANE), lambda b, h, i: (b, 0, h))
    o_d = _attn_call(functools.partial(_flash_mla_kernel, tq=tqd, tk=ATTN_TK), nb, s, tqd,
                     [qd_spec, kh_spec, vh_spec], (qd, kd, vd), "attn_mla")
    return o_a, o_b, o_c, o_d


def _two_block_softmax(s1, s2, v1_t, v2):
    m = jnp.maximum(jnp.max(s1, axis=1, keepdims=True), jnp.max(s2, axis=1, keepdims=True))
    p1 = jnp.exp2(s1 - m)
    p2 = jnp.exp2(s2 - m)
    l = jnp.sum(p1, axis=1, keepdims=True) + jnp.sum(p2, axis=1, keepdims=True)
    return (_nt_dot(p1.astype(BF16), v1_t) + _dot(p2.astype(BF16), v2)) / l


def _sample_kernel(qa_ref, ka_ref, va_ref, qb_ref, kb_ref, vb_ref, qc_ref, kc_ref, vc_ref, qd_ref, kd_ref, vd_ref,
                   cak_ref, cav_ref, cbk_ref, cbv_ref, cck_ref, ccv_ref, cckv_ref, ckr_ref,
                   ba_c_ref, ba_n_ref, fq_ref, fkc_ref, fkn_ref,
                   u_ref, un_ref, wukt_ref, wuv_ref, gk_ref,
                   oa_ref, ob_ref, oc_ref, od_ref, *, t, past, ks):
    row = lax.broadcasted_iota(jnp.int32, (t, t), 0)
    col = lax.broadcasted_iota(jnp.int32, (t, t), 1)
    causal_bias = jnp.where(col <= row, 0.0, NEG_INF)
    chunk_bias = jnp.where((past + col) // CHUNK <= (past + row) // CHUNK, 0.0, NEG_INF)
    strict = col < row

    ckv_c = cckv_ref[0].astype(BF16)
    kr_t = ckr_ref[0].astype(BF16)
    pad_t = jnp.zeros((LANE - MLA_NOPE - MLA_ROPE, past), BF16)

    for h in range(N_HEADS):
        hs = slice(LANE * h, LANE * h + LANE)
        ps = slice(LANE * (h // 2), LANE * (h // 2) + LANE)

        q = qa_ref[0][:, hs]
        s1 = _dot(q, cak_ref[0, ps, :].astype(BF16)) + ba_c_ref[h]
        s2 = _nt_dot(q, ka_ref[0][:, ps]) + ba_n_ref[h]
        oa_ref[0, :, hs] = _two_block_softmax(s1, s2, cav_ref[0, ps, :].astype(BF16), va_ref[0][:, ps])

        q = qb_ref[0][:, hs]
        c, w = _stick_weights(_nt_dot(q, kb_ref[0][:, ps]), jnp.zeros((t, 1), F32), un_ref[...], strict)
        acc = _dot(w.astype(BF16), vb_ref[0][:, ps])

        def b_live(state):
            return jnp.logical_and(state[0] >= 0, state[1] > DEAD_LOG2)

        def b_older(state, q=q, ps=ps):
            jb, _, c, acc = state
            off = pl.multiple_of(jb * ks, ks)
            kb_t = cbk_ref[0, ps, pl.ds(off, ks)].astype(BF16)
            vb_t = cbv_ref[0, ps, pl.ds(off, ks)].astype(BF16)
            c, w = _stick_weights(_dot(q, kb_t), c, u_ref[...], None)
            return jb - 1, jnp.max(c), c, acc + _nt_dot(w.astype(BF16), vb_t)

        _, _, _, acc = lax.while_loop(b_live, b_older, (past // ks - 1, jnp.max(c), c, acc))
        ob_ref[0, :, hs] = acc

        q = qc_ref[0][:, hs]
        fq = fq_ref[0][:, h:h + 1] * LOG2E
        s1 = _dot(q, cck_ref[0, ps, :].astype(BF16)) + (fq - fkc_ref[0, h:h + 1, :] * LOG2E)
        s2 = _nt_dot(q, kc_ref[0][:, ps]) + (fq - fkn_ref[0, h:h + 1, :] * LOG2E) + causal_bias
        oc_ref[0, :, hs] = _two_block_softmax(s1, s2, ccv_ref[0, ps, :].astype(BF16), vc_ref[0][:, ps])

        q = qd_ref[0][:, hs]
        kn_t = _nt_dot(wukt_ref[MLA_NOPE * h: MLA_NOPE * (h + 1), :], ckv_c)
        ss = jnp.sum(kn_t * kn_t, axis=0, keepdims=True)
        kn_t = kn_t * lax.rsqrt(ss * (1.0 / MLA_NOPE) + NORM_EPS) * gk_ref[...]
        k_t = jnp.concatenate([kn_t.astype(BF16), kr_t, pad_t], axis=0)
        s1 = _dot(q, k_t)
        s2 = _nt_dot(q, kd_ref[0][:, hs]) + chunk_bias
        v_c = _dot(ckv_c, wuv_ref[:, hs]).astype(BF16)
        m = jnp.maximum(jnp.max(s1, axis=1, keepdims=True), jnp.max(s2, axis=1, keepdims=True))
        p1 = jnp.exp2(s1 - m)
        p2 = jnp.exp2(s2 - m)
        l = jnp.sum(p1, axis=1, keepdims=True) + jnp.sum(p2, axis=1, keepdims=True)
        od_ref[0, :, hs] = (_dot(p1.astype(BF16), v_c) + _dot(p2.astype(BF16), vd_ref[0][:, hs])) / l


def _channel_major(cache):
    depth, nb, rows = cache.shape[:3]
    return jnp.transpose(cache, (0, 1, 3, 4, 2)).reshape(depth, nb, GROUP_W, rows)


def _sample_attention(pr, caches, layer, pw, w_uk, k_gain, rel_bias, nb, t):
    a_k, a_v, b_k, b_v, c_k, c_v, c_lf, d_ckv, d_kr = caches
    wuv = pw[4]
    past = b_k.shape[2]
    win = a_k.shape[2]
    ks = STICK_KS
    assert past % ks == 0
    r3 = lambda a: a.reshape(nb, t, a.shape[-1])
    news = [r3(a) for a in pr[8:20]]

    qpos = past + np.arange(t)
    kpos = past - win + np.arange(win + t)
    qc, kc = qpos // CHUNK, kpos // CHUNK
    ok = (kc[None, :] <= qc[:, None]) & (kc[None, :] >= qc[:, None] - A_LEFT_CHUNKS)
    ba = _rel_bias_tile(rel_bias, t, win + t, win, ok)
    ba_c, ba_n = ba[:, :, :win], ba[:, :, win:]

    clf_new = jnp.swapaxes(r3(pr[7])[..., 0:N_HEADS], 1, 2)
    lf_c = jnp.swapaxes(c_lf[layer], 1, 2).astype(F32)
    fk = jnp.cumsum(jnp.concatenate([lf_c, clf_new], axis=2), axis=2)
    fk_c, fk_n = fk[:, :, :past], fk[:, :, past:]
    fq = jnp.swapaxes(fk_n, 1, 2)

    wukt = w_uk.T.astype(BF16)
    gk = k_gain.reshape(MLA_NOPE, 1)
    u, un = _strict_upper(ks), _strict_upper(t)

    per_b = lambda shape: pl.BlockSpec((1,) + shape, lambda b: (b,) + (0,) * len(shape))
    per_lb = lambda shape: pl.BlockSpec((None, 1) + shape, lambda b: (layer, b) + (0,) * len(shape))
    const = lambda a: pl.BlockSpec(a.shape, lambda b: (0,) * a.ndim)
    cache_ops = [_channel_major(c) for c in (a_k, a_v, b_k, b_v, c_k, c_v)] + [d_ckv, jnp.swapaxes(d_kr, 2, 3)]
    operands = news + cache_ops + [ba_c, ba_n, fq, fk_c, fk_n, u, un, wukt, wuv, gk]
    in_specs = [per_b(a.shape[1:]) for a in news] + [per_lb(a.shape[2:]) for a in cache_ops] + \
               [const(ba_c), const(ba_n)] + [per_b(a.shape[1:]) for a in (fq, fk_c, fk_n)] + \
               [const(a) for a in (u, un, wukt, wuv, gk)]
    out_shape = [jax.ShapeDtypeStruct((nb, t, N_HEADS * LANE), F32)] * 4
    out_specs = [per_b((t, N_HEADS * LANE))] * 4
    return pl.pallas_call(
        functools.partial(_sample_kernel, t=t, past=past, ks=ks), grid=(nb,),
        in_specs=in_specs, out_specs=out_specs, out_shape=out_shape,
        compiler_params=_cparams(1), name="attn_sample",
    )(*operands)


def _out_kernel(oa_ref, ob_ref, oc_ref, od_ref, h_ref, gg_ref, wo_ref, nf_ref,
                wr1_ref, wr2_ref, wr3_ref, br_ref, h1_ref, hn_ref, lg_ref):
    rows = h_ref.shape[0]
    low = lax.broadcasted_iota(jnp.int32, (rows, LANE), 1) < HEAD_DIM
    h1 = h_ref[...]
    for g, o_ref in enumerate((oa_ref, ob_ref, oc_ref, od_ref)):
        p0 = jnp.where(low, o_ref[:, 0:LANE], o_ref[:, LANE:2 * LANE])
        p1 = jnp.where(low, o_ref[:, 2 * LANE:3 * LANE], o_ref[:, 3 * LANE:4 * LANE])
        og = jnp.concatenate([p0, p1], axis=1)
        ms = jnp.mean(og * og, axis=-1, keepdims=True)
        y = (og * lax.rsqrt(ms + NORM_EPS) * gg_ref[g:g + 1, :]).astype(BF16)
        h1 = h1 + _dot(y, wo_ref[GROUP_W * g: GROUP_W * (g + 1), :])
    h1_ref[...] = h1
    ms = jnp.mean(h1 * h1, axis=-1, keepdims=True)
    hn = h1 * lax.rsqrt(ms + NORM_EPS) * nf_ref[...]
    hn_ref[...] = hn.astype(BF16)
    a1, a2, a3 = _split3(hn)
    w1, w2, w3 = wr1_ref[...], wr2_ref[...], wr3_ref[...]
    lg = _dot(a1, w1) + _dot(a1, w2) + _dot(a2, w1) + _dot(a1, w3) + _dot(a2, w2) + _dot(a3, w1)
    lg_ref[...] = lg + br_ref[...]


def _out_project(outs, h, group_gain, w_out_b, norm_ffn, wr_parts, br_pad):
    t, d = h.shape
    ts = DENSE_ROWS
    assert t % ts == 0
    row = lambda i: (i, 0)
    full = lambda i: (0, 0)
    o_spec = pl.BlockSpec((ts, N_HEADS * LANE), row)
    in_specs = [o_spec] * 4 + [pl.BlockSpec((ts, d), row), pl.BlockSpec(group_gain.shape, full),
                               pl.BlockSpec(w_out_b.shape, full), pl.BlockSpec((1, d), full)] + \
               [pl.BlockSpec((d, LANE), full)] * 3 + [pl.BlockSpec((1, LANE), full)]
    out_shape = [jax.ShapeDtypeStruct((t, d), F32), jax.ShapeDtypeStruct((t, d), BF16),
                 jax.ShapeDtypeStruct((t, LANE), F32)]
    out_specs = [pl.BlockSpec((ts, d), row), pl.BlockSpec((ts, d), row), pl.BlockSpec((ts, LANE), row)]
    return pl.pallas_call(
        _out_kernel, grid=(t // ts,), in_specs=in_specs, out_specs=out_specs, out_shape=out_shape,
        compiler_params=_cparams(1), name="out_proj",
    )(*outs, h, group_gain, w_out_b, norm_ffn.reshape(1, d), *wr_parts, br_pad)


def _expert_kernel(be_ref, fe_ref, nu_ref, x_ref, wu_ref, bu_ref, wd_ref, bd_ref, y_ref, wub_ref, wdb_ref):
    i = pl.program_id(0)
    used = i < nu_ref[0]

    @pl.when(jnp.logical_and(used, fe_ref[i] == 1))
    def _():
        wub_ref[...] = wu_ref[0, 0].astype(BF16)
        wdb_ref[...] = wd_ref[0, 0].astype(BF16)

    @pl.when(used)
    def _():
        u = _dot(x_ref[...], wub_ref[...]) + bu_ref[0, 0]
        glu = jnp.minimum(u[:, :D_FF], SWIGLU_LIMIT)
        lin = jnp.clip(u[:, D_FF:], -SWIGLU_LIMIT, SWIGLU_LIMIT)
        act = glu * jax.nn.sigmoid(SWIGLU_ALPHA * glu) * (lin + 1.0)
        y_ref[...] = (_dot(act.astype(BF16), wdb_ref[...]) + bd_ref[0, 0]).astype(y_ref.dtype)


def _expert_ffn(x_sorted, blk_exp, blk_first, n_used, layer, w_up, b_up, w_down, b_down):
    rows, d = x_sorted.shape
    n_blk = rows // MOE_BLOCK
    last = lambda i, nu: jnp.minimum(i, nu[0] - 1)
    w_idx = lambda i, be, fe, nu: (layer, be[last(i, nu)], 0, 0)
    grid_spec = pltpu.PrefetchScalarGridSpec(
        num_scalar_prefetch=3, grid=(n_blk,),
        in_specs=[pl.BlockSpec((MOE_BLOCK, d), lambda i, be, fe, nu: (last(i, nu), 0)),
                  pl.BlockSpec((1, 1, d, 2 * D_FF), w_idx), pl.BlockSpec((1, 1, 1, 2 * D_FF), w_idx),
                  pl.BlockSpec((1, 1, D_FF, d), w_idx), pl.BlockSpec((1, 1, 1, d), w_idx)],
        out_specs=pl.BlockSpec((MOE_BLOCK, d), lambda i, be, fe, nu: (last(i, nu), 0)),
        scratch_shapes=[pltpu.VMEM((d, 2 * D_FF), BF16), pltpu.VMEM((D_FF, d), BF16)])
    depth = w_up.shape[0]
    return pl.pallas_call(
        _expert_kernel, grid_spec=grid_spec, out_shape=jax.ShapeDtypeStruct((rows, d), BF16),
        compiler_params=_cparams(1), name="expert_ffn",
    )(blk_exp, blk_first, n_used, x_sorted, w_up, b_up.reshape(depth, N_EXP, 1, 2 * D_FF),
      w_down, b_down.reshape(depth, N_EXP, 1, d))


def _moe(hn, logits, layer, w_up, b_up, w_down, b_down):
    n_tok, d = hn.shape
    top_v, top_i = lax.top_k(logits, TOP_K)
    gates = jax.nn.softmax(top_v, axis=-1)
    n = n_tok * TOP_K
    e_flat = top_i.reshape(-1)
    onehot = (e_flat[:, None] == jnp.arange(N_EXP)[None, :]).astype(jnp.int32)
    rank = jnp.sum((jnp.cumsum(onehot, axis=0) - onehot) * onehot, axis=1)
    counts = jnp.sum(onehot, axis=0)
    padded = (counts + MOE_BLOCK - 1) // MOE_BLOCK * MOE_BLOCK
    p_end = jnp.cumsum(padded)
    p_start = p_end - padded
    dest = p_start[e_flat] + rank
    n_blk = -(-n // MOE_BLOCK) + N_EXP
    rows = n_blk * MOE_BLOCK
    row_tok = jnp.zeros((rows,), jnp.int32).at[dest].set(jnp.arange(n, dtype=jnp.int32) // TOP_K)
    blk_start = jnp.arange(n_blk, dtype=jnp.int32) * MOE_BLOCK
    blk_exp = jnp.sum((p_end[None, :] <= blk_start[:, None]).astype(jnp.int32), axis=1)
    blk_exp = jnp.minimum(blk_exp, N_EXP - 1)
    blk_first = jnp.concatenate([jnp.ones((1,), jnp.int32), (blk_exp[1:] != blk_exp[:-1]).astype(jnp.int32)])
    n_used = (p_end[-1] // MOE_BLOCK).astype(jnp.int32).reshape(1)
    x_sorted = hn[row_tok]
    y = _expert_ffn(x_sorted, blk_exp, blk_first, n_used, layer, w_up, b_up, w_down, b_down)
    picked = y[dest.reshape(n_tok, TOP_K).T]
    return picked, gates


def _ple_kernel(h_ref, y_ref, g_ref, p_ref, np_ref, wg_ref, wp_ref, o_ref):
    h2 = h_ref[...]
    g = g_ref[...]
    for j in range(TOP_K):
        h2 = h2 + y_ref[j].astype(F32) * g[:, j:j + 1]
    ms = jnp.mean(h2 * h2, axis=-1, keepdims=True)
    hn = (h2 * lax.rsqrt(ms + NORM_EPS) * np_ref[...]).astype(BF16)
    gate = jax.nn.sigmoid(_dot(hn, wg_ref[...]))
    o_ref[...] = h2 + gate * _dot(p_ref[...].astype(BF16), wp_ref[...])


def _ple(h1, picked, gates, row0, p, layer, norm_ple, wg_b, wp_b):
    t, d = h1.shape
    ts = PLE_ROWS
    assert t % ts == 0 and row0 % ts == 0
    off = row0 // ts
    row = lambda i: (i, 0)
    full = lambda i: (0, 0)
    return pl.pallas_call(
        _ple_kernel, grid=(t // ts,),
        in_specs=[pl.BlockSpec((ts, d), row), pl.BlockSpec((TOP_K, ts, d), lambda i: (0, i + off, 0)),
                  pl.BlockSpec((ts, TOP_K), lambda i: (i + off, 0)),
                  pl.BlockSpec((None, ts, p.shape[2]), lambda i: (layer, i, 0)),
                  pl.BlockSpec((1, d), full), pl.BlockSpec(wg_b.shape, full), pl.BlockSpec(wp_b.shape, full)],
        out_specs=pl.BlockSpec((ts, d), row), out_shape=jax.ShapeDtypeStruct((t, d), F32),
        compiler_params=_cparams(1), name="ple",
    )(h1, picked, gates, p, norm_ple.reshape(1, d), wg_b, wp_b)


def kernel(x_prompt, x_sample, p_prompt, p_sample, cache_a_k, cache_a_v, cache_b_k, cache_b_v, cache_c_k, cache_c_v, cache_c_logf, cache_d_ckv, cache_d_krope, norm_mix, w_in, b_forget, qk_gain, rope_gain, kv_gain, w_uk, w_uv, rel_bias, group_gain, w_out, norm_ffn, w_router, b_router, w_up, b_up, w_down, b_down, norm_ple, w_ple_gate, w_ple_proj):
    nb, s, d = x_prompt.shape
    nd, t, _ = x_sample.shape
    depth = w_in.shape[0]
    past = cache_b_k.shape[2]
    assert PROJ_ROWS % t == 0 and s % PROJ_ROWS == 0

    tab_p = _rope_tables(jnp.arange(s))
    tab_s = _rope_tables(past + jnp.arange(PROJ_ROWS) % t)
    hp = x_prompt.reshape(nb * s, d)
    hs = x_sample.reshape(nd * t, d)
    st_p, st_s = [], []
    keep = min(A_WINDOW, s)
    for i in range(depth):
        pw = _proj_weights(w_in[i], b_forget[i], qk_gain[i], rope_gain[i], kv_gain[i], w_uk[i], w_uv[i])
        pr_p = _project(hp, norm_mix[i], pw, tab_p, s // PROJ_ROWS)
        pr_s = _project(hs, norm_mix[i], pw, tab_s, 1)
        outs_p = _prompt_attention(pr_p, nb, s, rel_bias[i], qk_gain[i, 3])
        caches = (cache_a_k, cache_a_v, cache_b_k, cache_b_v, cache_c_k, cache_c_v,
                  cache_c_logf, cache_d_ckv, cache_d_krope)
        outs_s = _sample_attention(pr_s, caches, i, pw, w_uk[i], qk_gain[i, 5], rel_bias[i], nd, t)

        def heads(a, n, rows):
            return a.reshape(n, rows, N_HEADS, HEAD_DIM)

        p3 = lambda a: a.reshape(nb, s, a.shape[-1])
        s3 = lambda a: a.reshape(nd, t, a.shape[-1])
        st_p.append((heads(p3(pr_p[0])[:, s - keep:], nb, keep), heads(p3(pr_p[1])[:, s - keep:], nb, keep),
                     heads(pr_p[2], nb, s), heads(pr_p[3], nb, s), heads(pr_p[4], nb, s), heads(pr_p[5], nb, s),
                     p3(pr_p[7])[..., 0:N_HEADS], p3(pr_p[6]), p3(pr_p[7])[..., ROPE_LO:ROPE_LO + MLA_ROPE]))
        ka_all = jnp.concatenate([cache_a_k[i], heads(pr_s[0], nd, t)], axis=1)[:, t:]
        va_all = jnp.concatenate([cache_a_v[i], heads(pr_s[1], nd, t)], axis=1)[:, t:]
        st_s.append((ka_all, va_all,
                     heads(pr_s[2], nd, t), heads(pr_s[3], nd, t), heads(pr_s[4], nd, t), heads(pr_s[5], nd, t),
                     s3(pr_s[7])[..., 0:N_HEADS], s3(pr_s[6]), s3(pr_s[7])[..., ROPE_LO:ROPE_LO + MLA_ROPE]))

        w_out_b = w_out[i].astype(BF16)
        wr_pad = jnp.zeros((d, LANE), F32).at[:, 0:N_EXP].set(w_router[i])
        wr_parts = _split3(wr_pad)
        br_pad = jnp.zeros((1, LANE), F32).at[0, 0:N_EXP].set(b_router[i])
        flat4 = lambda o: o.reshape(-1, N_HEADS * LANE)
        h1_p, hn_p, lg_p = _out_project([flat4(o) for o in outs_p], hp, group_gain[i], w_out_b, norm_ffn[i],
                                        wr_parts, br_pad)
        h1_s, hn_s, lg_s = _out_project([flat4(o) for o in outs_s], hs, group_gain[i], w_out_b, norm_ffn[i],
                                        wr_parts, br_pad)

        hn_all = jnp.concatenate([hn_p, hn_s], axis=0)
        lg_all = jnp.concatenate([lg_p, lg_s], axis=0)[:, 0:N_EXP]
        picked, gates = _moe(hn_all, lg_all, i, w_up, b_up, w_down, b_down)

        wg_b = w_ple_gate[i].astype(BF16)
        wp_b = w_ple_proj[i].astype(BF16)
        hp = _ple(h1_p, picked, gates, 0, p_prompt.reshape(depth, nb * s, -1), i, norm_ple[i], wg_b, wp_b)
        hs = _ple(h1_s, picked, gates, nb * s, p_sample.reshape(depth, nd * t, -1), i, norm_ple[i], wg_b, wp_b)

    state_p = [jnp.stack([st[j] for st in st_p]) for j in range(9)]
    state_s = [jnp.stack([st[j] for st in st_s]) for j in range(9)]
    return (hp.reshape(nb, s, d), hs.reshape(nd, t, d), *state_p, *state_s)
```

```python
import functools

import numpy as np
import jax
import jax.numpy as jnp
from jax import lax
from jax.experimental import pallas as pl
from jax.experimental.pallas import tpu as pltpu

F32 = jnp.float32
BF16 = jnp.bfloat16

CHUNK = 64
HEAD_DIM = 64
N_HEADS = 4
GROUP_W = 256
A_LEFT_CHUNKS = 8
A_WINDOW = A_LEFT_CHUNKS * CHUNK
REL_CLIP = 128
MLA_NOPE = 64
MLA_ROPE = 32
KV_RANK = 128
ROPE_BASE = 10000.0
N_EXP = 32
TOP_K = 4
D_FF = 1024
SWIGLU_LIMIT = 7.0
SWIGLU_ALPHA = 1.702
MOE_BLOCK = 512
NORM_EPS = 1e-6
NEG_INF = -1e30
LOG2E = 1.4426950408889634

A_Q = 0
C_F = 2304
D_Q = 2308
D_CKV = D_Q + N_HEADS * (MLA_NOPE + MLA_ROPE)
D_KR = D_CKV + KV_RANK

LANE = 128
SEG_ABC = 0
SEG_DQ = 2304
SEG_CKV = 2816
SEG_MISC = 2944
N_COLS = 3072
ROPE_LO = 64
ROPE_HALF = MLA_ROPE // 2

VMEM_LIMIT = 56 * 1024 * 1024

PROJ_ROWS = 512
ATTN_TQ = 512
ATTN_TK = 512
MLA_TQ = 2048
STICK_KS = 256
DENSE_ROWS = 512
PLE_ROWS = 256
MOE_GROUPS = 4

DEAD_LOG2 = -160.0
BOUND_MARGIN = 1.01


def _cparams(n_axes):
    return pltpu.CompilerParams(dimension_semantics=("arbitrary",) * n_axes,
                                vmem_limit_bytes=VMEM_LIMIT)


def _nt_dot(a, b):
    return lax.dot_general(a, b, (((1,), (1,)), ((), ())), preferred_element_type=F32)


def _dot(a, b):
    return jnp.dot(a, b, preferred_element_type=F32)


def _split2(x):
    hi = x.astype(BF16)
    lo = (x - hi.astype(F32)).astype(BF16)
    return hi, lo


def _split3(x):
    hi = x.astype(BF16)
    r = x - hi.astype(F32)
    mid = r.astype(BF16)
    lo = (r - mid.astype(F32)).astype(BF16)
    return hi, mid, lo


def _in_col_map():
    cols = np.full((N_COLS,), -1, np.int64)
    cols[0:2304] = np.arange(2304)
    for h in range(N_HEADS):
        base = D_Q + (MLA_NOPE + MLA_ROPE) * h
        cols[SEG_DQ + LANE * h: SEG_DQ + LANE * h + MLA_NOPE + MLA_ROPE] = base + np.arange(MLA_NOPE + MLA_ROPE)
    cols[SEG_CKV:SEG_CKV + KV_RANK] = D_CKV + np.arange(KV_RANK)
    cols[SEG_MISC:SEG_MISC + N_HEADS] = C_F + np.arange(N_HEADS)
    cols[SEG_MISC + ROPE_LO:SEG_MISC + ROPE_LO + MLA_ROPE] = D_KR + np.arange(MLA_ROPE)
    return cols


def _rope_tables(pos):
    inv = ROPE_BASE ** (-jnp.arange(ROPE_HALF, dtype=F32) / ROPE_HALF)
    ang = pos.astype(F32)[:, None] * inv
    cos, sin = jnp.cos(ang), jnp.sin(ang)
    n = pos.shape[0]
    one = jnp.ones((n, ROPE_LO), F32)
    z16 = jnp.zeros((n, ROPE_HALF), F32)
    z64 = jnp.zeros((n, ROPE_LO), F32)
    z32 = jnp.zeros((n, LANE - ROPE_LO - MLA_ROPE), F32)
    cos_t = jnp.concatenate([one, cos, cos, z32 + 1.0], axis=1)
    sin_a = jnp.concatenate([z64, -sin, z16, z32], axis=1)
    sin_b = jnp.concatenate([z64, z16, sin, z32], axis=1)
    return cos_t, sin_a, sin_b


def _rope(y, cos_t, sin_a, sin_b):
    left = pltpu.roll(y, LANE - ROPE_HALF, axis=1)
    right = pltpu.roll(y, ROPE_HALF, axis=1)
    return y * cos_t + left * sin_a + right * sin_b


def _head_norm(x, m_ref, gain):
    hi, lo = _split2(x * x)
    ssq = _dot(hi, m_ref[...]) + _dot(lo, m_ref[...])
    return x * lax.rsqrt(ssq * (1.0 / HEAD_DIM) + NORM_EPS) * gain


def _value_lane0(h):
    return HEAD_DIM * (h % 2)


def _ones_lane(h):
    return HEAD_DIM - _value_lane0(h)


def _store_padded_q(q_ref, q, scale):
    lane = lax.broadcasted_iota(jnp.int32, (q.shape[0], LANE), 1)
    low = lane < HEAD_DIM
    qs = q * scale
    for h in range(N_HEADS):
        pair = qs[:, LANE * (h // 2): LANE * (h // 2) + LANE]
        keep = low if h % 2 == 0 else jnp.logical_not(low)
        q_ref[:, LANE * h: LANE * h + LANE] = jnp.where(keep, pair, 0.0).astype(BF16)


def _proj_kernel(x_ref, gmix_ref, w_ref, cos_ref, sa_ref, sb_ref, gv_ref, m64_ref, wuk_ref, wuv_ref,
                 ak_ref, av_ref, bk_ref, bv_ref, ck_ref, cv_ref, ckv_ref, misc_ref,
                 qa_ref, ka_ref, va_ref, qb_ref, kb_ref, vb_ref, qc_ref, kc_ref, vc_ref,
                 qd_ref, kd_ref, vd_ref):
    x = x_ref[...]
    ms = jnp.mean(x * x, axis=-1, keepdims=True)
    hn = (x * lax.rsqrt(ms + NORM_EPS) * gmix_ref[...]).astype(BF16)
    sm_scale = (HEAD_DIM ** -0.5) * LOG2E
    rows = x.shape[0]
    lane = lax.broadcasted_iota(jnp.int32, (rows, LANE), 1)
    is_nope = lane < MLA_NOPE
    is_rope = jnp.logical_and(lane >= ROPE_LO, lane < ROPE_LO + MLA_ROPE)
    cos_t, sin_a, sin_b = cos_ref[...], sa_ref[...], sb_ref[...]

    z = _dot(hn, w_ref[:, 0:768])
    aq = _head_norm(z[:, 0:256], m64_ref, gv_ref[0:1, :])
    ak = _head_norm(z[:, 256:512], m64_ref, gv_ref[1:2, :])
    av = z[:, 512:768]
    ak_ref[...] = ak
    av_ref[...] = av
    _store_padded_q(qa_ref, aq, sm_scale)
    ka_ref[...] = ak.astype(BF16)
    va_ref[...] = av.astype(BF16)

    z = _dot(hn, w_ref[:, 768:1536])
    bk_ref[...] = z[:, 256:512]
    bv_ref[...] = z[:, 512:768]
    _store_padded_q(qb_ref, z[:, 0:256], sm_scale)
    kb_ref[...] = z[:, 256:512].astype(BF16)
    vb_ref[...] = z[:, 512:768].astype(BF16)

    z = _dot(hn, w_ref[:, 1536:2304])
    cq = _head_norm(z[:, 0:256], m64_ref, gv_ref[2:3, :])
    ck = _head_norm(z[:, 256:512], m64_ref, gv_ref[3:4, :])
    cv = z[:, 512:768]
    ck_ref[...] = ck
    cv_ref[...] = cv
    _store_padded_q(qc_ref, cq, sm_scale)
    kc_ref[...] = ck.astype(BF16)
    vc_ref[...] = cv.astype(BF16)

    z = _dot(hn, w_ref[:, SEG_CKV:N_COLS])
    zc = z[:, 0:KV_RANK]
    ckv = zc * lax.rsqrt(jnp.mean(zc * zc, axis=-1, keepdims=True) + NORM_EPS) * gv_ref[7:8, 0:LANE]
    ckv_ref[...] = ckv
    zm = z[:, KV_RANK:2 * KV_RANK]
    ssr = jnp.sum(jnp.where(is_rope, zm * zm, 0.0), axis=-1, keepdims=True)
    kr = zm * lax.rsqrt(ssr * (1.0 / MLA_ROPE) + NORM_EPS) * gv_ref[6:7, 0:LANE]
    kr = _rope(kr, cos_t, sin_a, sin_b)
    zf = zm + gv_ref[8:9, 0:LANE]
    clf = jnp.minimum(zf, 0.0) - jnp.log1p(jnp.exp(-jnp.abs(zf)))
    misc_ref[...] = jnp.where(lane < N_HEADS, clf, kr)

    ckv_b = ckv.astype(BF16)
    kn = _dot(ckv_b, wuk_ref[...])
    for h in range(N_HEADS):
        hs = slice(LANE * h, LANE * h + LANE)
        ones = (lane == _ones_lane(h)).astype(F32)
        vd_ref[:, hs] = (_dot(ckv_b, wuv_ref[:, hs]) + ones).astype(BF16)
    for h in range(N_HEADS):
        xh = kn[:, LANE * h: LANE * h + LANE]
        ss = jnp.sum(xh * xh, axis=-1, keepdims=True)
        yh = xh * lax.rsqrt(ss * (1.0 / MLA_NOPE) + NORM_EPS) * gv_ref[5:6, 0:LANE]
        kd_ref[:, LANE * h: LANE * h + LANE] = (yh + kr).astype(BF16)

    z = _dot(hn, w_ref[:, SEG_DQ:SEG_CKV])
    d_scale = ((MLA_NOPE + MLA_ROPE) ** -0.5) * LOG2E
    for h in range(N_HEADS):
        xh = z[:, LANE * h: LANE * h + LANE]
        x2 = xh * xh
        ssn = jnp.sum(jnp.where(is_nope, x2, 0.0), axis=-1, keepdims=True)
        ssr = jnp.sum(jnp.where(is_rope, x2, 0.0), axis=-1, keepdims=True)
        rn = lax.rsqrt(ssn * (1.0 / MLA_NOPE) + NORM_EPS)
        rr = lax.rsqrt(ssr * (1.0 / MLA_ROPE) + NORM_EPS)
        yh = xh * jnp.where(is_nope, rn, rr) * gv_ref[4:5, 0:LANE]
        yh = _rope(yh, cos_t, sin_a, sin_b)
        qd_ref[:, LANE * h: LANE * h + LANE] = (yh * d_scale).astype(BF16)


def _proj_weights(w_in, b_forget, qk_gain, rope_gain, kv_gain, w_uk, w_uv):
    cols = _in_col_map()
    valid = jnp.asarray(cols >= 0)
    w = jnp.where(valid[None, :], w_in[:, np.maximum(cols, 0)], 0.0).astype(BF16)

    def tile4(g):
        return jnp.tile(g, N_HEADS)

    gv = jnp.zeros((16, GROUP_W), F32)
    gv = gv.at[0].set(tile4(qk_gain[0])).at[1].set(tile4(qk_gain[1]))
    gv = gv.at[2].set(tile4(qk_gain[2])).at[3].set(tile4(qk_gain[3]))
    gv = gv.at[4, 0:MLA_NOPE].set(qk_gain[4]).at[4, ROPE_LO:ROPE_LO + MLA_ROPE].set(rope_gain[0])
    gv = gv.at[5, 0:MLA_NOPE].set(qk_gain[5])
    gv = gv.at[6, ROPE_LO:ROPE_LO + MLA_ROPE].set(rope_gain[1])
    gv = gv.at[7, 0:KV_RANK].set(kv_gain)
    gv = gv.at[8, 0:N_HEADS].set(b_forget)
    head = np.arange(GROUP_W) // HEAD_DIM
    m64 = jnp.asarray((head[:, None] == head[None, :]).astype(np.float32), BF16)
    wuk = jnp.zeros((KV_RANK, N_HEADS * LANE), F32)
    for h in range(N_HEADS):
        wuk = wuk.at[:, LANE * h: LANE * h + MLA_NOPE].set(w_uk[:, MLA_NOPE * h: MLA_NOPE * (h + 1)])
    wuv = jnp.zeros((KV_RANK, N_HEADS * LANE), F32)
    for h in range(N_HEADS):
        lo = LANE * h + _value_lane0(h)
        wuv = wuv.at[:, lo: lo + HEAD_DIM].set(w_uv[:, HEAD_DIM * h: HEAD_DIM * (h + 1)])
    return w, gv, m64, wuk.astype(BF16), wuv.astype(BF16)


def _project(x, gmix, pw, tables, n_tab_blocks):
    w, gv, m64, wuk, wuv = pw
    t, d = x.shape
    ts = PROJ_ROWS
    assert t % ts == 0
    row = lambda i: (i, 0)
    full = lambda i: (0, 0)
    tab = lambda i: (i % n_tab_blocks, 0)
    f32_w = [GROUP_W] * 6 + [KV_RANK, LANE]
    bf_w = [512, 256, 256, 512, 256, 256, 512, 256, 256, 512, 512, 512]
    out_shape = [jax.ShapeDtypeStruct((t, c), F32) for c in f32_w] + \
                [jax.ShapeDtypeStruct((t, c), BF16) for c in bf_w]
    out_specs = [pl.BlockSpec((ts, c), row) for c in f32_w + bf_w]
    in_specs = [pl.BlockSpec((ts, d), row), pl.BlockSpec((1, d), full), pl.BlockSpec((d, N_COLS), full),
                pl.BlockSpec((ts, LANE), tab), pl.BlockSpec((ts, LANE), tab), pl.BlockSpec((ts, LANE), tab),
                pl.BlockSpec(gv.shape, full), pl.BlockSpec(m64.shape, full),
                pl.BlockSpec(wuk.shape, full), pl.BlockSpec(wuv.shape, full)]
    return pl.pallas_call(
        _proj_kernel, grid=(t // ts,), in_specs=in_specs, out_specs=out_specs, out_shape=out_shape,
        compiler_params=_cparams(1), name="proj",
    )(x, gmix.reshape(1, d), w, *tables, gv, m64, wuk, wuv)


def _softmax_step(q, kb, vb, carry, bias=None, ok=None):
    m, l, acc = carry
    s = _nt_dot(q, kb)
    if bias is not None:
        s = s + bias
    if ok is not None:
        s = jnp.where(ok, s, NEG_INF)
    m_new = jnp.maximum(m, jnp.max(s, axis=1, keepdims=True))
    alpha = jnp.exp2(m - m_new)
    p = jnp.exp2(s - m_new)
    l = alpha * l + jnp.sum(p, axis=1, keepdims=True)
    acc = alpha * acc + _dot(p.astype(BF16), vb)
    return m_new, l, acc


def _softmax_init(tq):
    return (jnp.full((tq, 1), NEG_INF, F32), jnp.zeros((tq, 1), F32), jnp.zeros((tq, LANE), F32))


def _pick_lane(block, h):
    col = lax.broadcasted_iota(jnp.int32, block.shape, 1)
    return jnp.sum(jnp.where(col == h, block, 0.0), axis=1, keepdims=True)


def _stack_pair(q_ref):
    return jnp.concatenate([q_ref[0, :, 0:LANE], q_ref[0, :, LANE:2 * LANE]], axis=0)


def _store_pair(o_ref, o, tq):
    o_ref[0, :, 0:LANE] = o[:tq]
    o_ref[0, :, LANE:2 * LANE] = o[tq:]


def _flash_forget_kernel(kmax_ref, decay_ref, q_ref, k_ref, v_ref, fq_ref, fk_ref, o_ref, *, tq, n_blk):
    b = pl.program_id(0)
    pair = pl.program_id(1)
    qi = pl.program_id(2)
    q = _stack_pair(q_ref)
    fq = jnp.concatenate([_pick_lane(fq_ref[0], 2 * pair), _pick_lane(fq_ref[0], 2 * pair + 1)], axis=0) * LOG2E

    def step(j, carry, diag):
        off = pl.multiple_of(j * tq, tq)
        kb = k_ref[0, pl.ds(off, tq), :]
        vb = v_ref[0, pl.ds(off, tq), :]
        fk = jnp.concatenate([jnp.broadcast_to(fk_ref[0, 0, pl.ds(j, 1), :], (tq, tq)),
                              jnp.broadcast_to(fk_ref[0, 1, pl.ds(j, 1), :], (tq, tq))], axis=0)
        ok = None
        if diag:
            row = lax.broadcasted_iota(jnp.int32, (2 * tq, tq), 0) % tq
            col = lax.broadcasted_iota(jnp.int32, (2 * tq, tq), 1)
            ok = col <= row
        return _softmax_step(q, kb, vb, carry, fq - fk * LOG2E, ok)

    carry = step(qi, _softmax_init(2 * tq), True)
    qf = q.astype(F32)
    q_norm = jnp.sqrt(jnp.sum(qf * qf, axis=1, keepdims=True))
    head0 = b * N_HEADS + 2 * pair
    room = q_norm * jnp.where(lax.broadcasted_iota(jnp.int32, q_norm.shape, 0) < tq,
                              kmax_ref[head0], kmax_ref[head0 + 1]) + fq - carry[0]
    slack0 = jnp.max(room[:tq]) - DEAD_LOG2
    slack1 = jnp.max(room[tq:]) - DEAD_LOG2

    def live(state):
        j = jnp.maximum(state[0], 0)
        alive = jnp.logical_or(slack0 + decay_ref[head0 * n_blk + j] >= 0.0,
                               slack1 + decay_ref[(head0 + 1) * n_blk + j] >= 0.0)
        return jnp.logical_and(state[0] >= 0, alive)

    def older(state):
        j = state[0]
        return (j - 1,) + step(j, state[1:], False)

    _, m, l, acc = lax.while_loop(live, older, (qi - 1,) + carry)
    _store_pair(o_ref, acc / l, tq)


def _flash_mla_kernel(q_ref, k_ref, v_ref, o_ref, *, tq, tk):
    h = pl.program_id(1)
    qi = pl.program_id(2)
    q = q_ref[0]
    n_sub = tq // tk

    def step(j, carry, row0=None):
        m, acc = carry
        off = pl.multiple_of(j * tk, tk)
        qs = q if row0 is None else q[row0:]
        s = _nt_dot(qs, k_ref[0, pl.ds(off, tk), :])
        if row0 is not None:
            row = lax.broadcasted_iota(jnp.int32, s.shape, 0)
            col = lax.broadcasted_iota(jnp.int32, s.shape, 1)
            s = jnp.where(col // CHUNK <= row // CHUNK, s, NEG_INF)
        m_new = jnp.maximum(m, jnp.max(s, axis=1, keepdims=True))
        p = jnp.exp2(s - m_new)
        acc = jnp.exp2(m - m_new) * acc + _dot(p.astype(BF16), v_ref[0, pl.ds(off, tk), :])
        return m_new, acc

    carry = (jnp.full((tq, 1), NEG_INF, F32), jnp.zeros((tq, LANE), F32))
    m, acc = lax.fori_loop(0, qi * n_sub, lambda j, c: step(j, c), carry)
    for r in range(n_sub):
        row0 = r * tk
        m_r, acc_r = step(qi * n_sub + r, (m[row0:], acc[row0:]), row0)
        m = m_r if r == 0 else jnp.concatenate([m[:row0], m_r], axis=0)
        acc = acc_r if r == 0 else jnp.concatenate([acc[:row0], acc_r], axis=0)
    denom = jnp.where(h % 2 == 0, acc[:, HEAD_DIM:HEAD_DIM + 1], acc[:, 0:1])
    o_ref[0] = acc / denom


def _flash_stick_kernel(q_ref, k_ref, v_ref, u_ref, o_ref, *, tq, ks):
    qi = pl.program_id(2)
    q = _stack_pair(q_ref)
    n_sub = tq // ks

    def step(jb, carry, diag):
        c, acc = carry
        off = pl.multiple_of(jb * ks, ks)
        kb = k_ref[0, pl.ds(off, ks), :]
        vb = v_ref[0, pl.ds(off, ks), :]
        vis = None
        if diag:
            row = lax.broadcasted_iota(jnp.int32, (2 * tq, ks), 0) % tq + qi * tq
            col = lax.broadcasted_iota(jnp.int32, (2 * tq, ks), 1) + jb * ks
            vis = col < row
        c_new, w = _stick_weights(_nt_dot(q, kb), c, u_ref[...], vis)
        return c_new, acc + _dot(w.astype(BF16), vb)

    carry = (jnp.zeros((2 * tq, 1), F32), jnp.zeros((2 * tq, LANE), F32))
    for r in range(n_sub):
        carry = step(qi * n_sub + (n_sub - 1 - r), carry, True)

    def live(state):
        return jnp.logical_and(state[0] >= 0, state[1] > DEAD_LOG2)

    def older(state):
        c, acc = step(state[0], state[2:], False)
        return state[0] - 1, jnp.max(c), c, acc

    _, _, _, acc = lax.while_loop(live, older, (qi * n_sub - 1, jnp.max(carry[0])) + carry)
    _store_pair(o_ref, acc, tq)


def _stick_weights(z, c, u, vis):
    sp = jnp.log(1.0 + jnp.exp2(-jnp.abs(z))) * LOG2E
    log_rest = jnp.minimum(-z, 0.0) - sp
    log_beta = log_rest + z
    if vis is not None:
        log_rest = jnp.where(vis, log_rest, 0.0)
    hi, lo = _split2(log_rest)
    between = _dot(hi, u) + _dot(lo, u)
    w = jnp.exp2(log_beta + between + c)
    if vis is not None:
        w = jnp.where(vis, w, 0.0)
    return c + jnp.sum(log_rest, axis=1, keepdims=True), w


def _band_kernel(q_ref, k_ref, v_ref, bd_ref, bp_ref, o_ref, *, tq):
    qi = pl.program_id(2)
    q = _stack_pair(q_ref)
    off = pl.multiple_of(qi * tq, tq)
    carry = _softmax_step(q, k_ref[0, pl.ds(off, tq), :], v_ref[0, pl.ds(off, tq), :],
                          _softmax_init(2 * tq), bd_ref[...].reshape(2 * tq, tq))
    offp = pl.multiple_of(jnp.maximum(qi - 1, 0) * tq, tq)
    no_prev = jnp.where(qi == 0, NEG_INF, 0.0)
    m, l, acc = _softmax_step(q, k_ref[0, pl.ds(offp, tq), :], v_ref[0, pl.ds(offp, tq), :],
                              carry, bp_ref[...].reshape(2 * tq, tq) + no_prev)
    _store_pair(o_ref, acc / l, tq)


def _pair_specs(s, tq):
    q_spec = pl.BlockSpec((1, tq, 2 * LANE), lambda b, p, i: (b, i, p))
    kv_spec = pl.BlockSpec((1, s, LANE), lambda b, p, i: (b, 0, p))
    return q_spec, kv_spec


def _pair_call(kern, nb, s, tq, in_specs, operands, name):
    return pl.pallas_call(
        kern, grid=(nb, N_HEADS // 2, s // tq), in_specs=in_specs,
        out_specs=pl.BlockSpec((1, tq, 2 * LANE), lambda b, p, i: (b, i, p)),
        out_shape=jax.ShapeDtypeStruct((nb, s, N_HEADS * LANE), F32),
        compiler_params=_cparams(3), name=name,
    )(*operands)


def _head_call(kern, nb, s, tq, in_specs, operands, name):
    return pl.pallas_call(
        kern, grid=(nb, N_HEADS, s // tq), in_specs=in_specs,
        out_specs=pl.BlockSpec((1, tq, LANE), lambda b, h, i: (b, i, h)),
        out_shape=jax.ShapeDtypeStruct((nb, s, N_HEADS * LANE), F32),
        compiler_params=_cparams(3), name=name,
    )(*operands)


def _strict_upper(n):
    idx = np.arange(n)
    return jnp.asarray((idx[:, None] > idx[None, :]).astype(np.float32), BF16)


def _toeplitz(vec, n_rows, n_cols):
    length = n_rows + n_cols - 1
    assert vec.shape[-1] == length
    lead = vec.shape[:-1]
    rev = jnp.concatenate([vec[..., ::-1], jnp.zeros(lead + (1,), vec.dtype)], axis=-1)
    flat = jnp.tile(rev, (1,) * len(lead) + (n_rows,))[..., :n_rows * length]
    return flat.reshape(lead + (n_rows, length))[..., n_rows - 1: n_rows - 1 + n_cols]


def _rel_bias_tile(rel_bias, n_rows, n_cols, rel00, ok):
    d = np.arange(n_rows + n_cols - 1) - (n_cols - 1) + rel00
    vec = rel_bias.astype(F32)[:, np.clip(d, -REL_CLIP, REL_CLIP) + REL_CLIP] * LOG2E
    return jnp.where(jnp.asarray(ok)[None], _toeplitz(vec, n_rows, n_cols), NEG_INF)


def _band_bias_tiles(rel_bias, tq):
    i = np.arange(tq)[:, None]
    j = np.arange(tq)[None, :]
    own = _rel_bias_tile(rel_bias, tq, tq, 0, (j // CHUNK) <= (i // CHUNK))
    prev = _rel_bias_tile(rel_bias, tq, tq, tq, (j // CHUNK) >= (i // CHUNK) + tq // CHUNK - A_LEFT_CHUNKS)
    return own, prev


def _prompt_attention(pr, nb, s, rel_bias, k_gain):
    tq = ATTN_TQ
    assert s % tq == 0 and tq == A_WINDOW
    r3 = lambda a: a.reshape(nb, s, a.shape[-1])
    qa, ka, va, qb, kb, vb, qc, kc, vc, qd, kd, vd = [r3(a) for a in pr[8:20]]
    q_spec, kv_spec = _pair_specs(s, tq)

    bd, bp = _band_bias_tiles(rel_bias, tq)
    b_spec = pl.BlockSpec((2, tq, tq), lambda b, p, i: (p, 0, 0))
    o_a = _pair_call(functools.partial(_band_kernel, tq=tq), nb, s, tq,
                     [q_spec, kv_spec, kv_spec, b_spec, b_spec], (qa, ka, va, bd, bp), "attn_band")

    u = _strict_upper(STICK_KS)
    o_b = _pair_call(functools.partial(_flash_stick_kernel, tq=tq, ks=STICK_KS), nb, s, tq,
                     [q_spec, kv_spec, kv_spec, pl.BlockSpec(u.shape, lambda b, p, i: (0, 0))],
                     (qb, kb, vb, u), "attn_stick")

    n_blk = s // tq
    clf = r3(pr[7])[..., 0:N_HEADS]
    c_cum = jnp.cumsum(clf, axis=1)
    fk = jnp.swapaxes(c_cum, 1, 2).reshape(nb, N_HEADS, n_blk, tq)
    kmax = jnp.full((nb * N_HEADS,), HEAD_DIM ** 0.5 * BOUND_MARGIN, F32) * jnp.max(jnp.abs(k_gain))
    decay = lax.cummax(jnp.max(-fk, axis=-1), axis=2) * LOG2E
    smem = pl.BlockSpec(memory_space=pltpu.SMEM)
    fq_spec = pl.BlockSpec((1, tq, N_HEADS), lambda b, p, i: (b, i, 0))
    fk_spec = pl.BlockSpec((1, 2, n_blk, tq), lambda b, p, i: (b, p, 0, 0))
    o_c = _pair_call(functools.partial(_flash_forget_kernel, tq=tq, n_blk=n_blk), nb, s, tq,
                     [smem, smem, q_spec, kv_spec, kv_spec, fq_spec, fk_spec],
                     (kmax, decay.reshape(-1), qc, kc, vc, c_cum, fk), "attn_forget")

    tqd = MLA_TQ if s % MLA_TQ == 0 else tq
    qd_spec = pl.BlockSpec((1, tqd, LANE), lambda b, h, i: (b, i, h))
    kvd_spec = pl.BlockSpec((1, s, LANE), lambda b, h, i: (b, 0, h))
    o_d = _head_call(functools.partial(_flash_mla_kernel, tq=tqd, tk=ATTN_TK), nb, s, tqd,
                     [qd_spec, kvd_spec, kvd_spec], (qd, kd, vd), "attn_mla")
    return o_a, o_b, o_c, o_d


def _two_block_softmax(s1, s2, v1_t, v2):
    m = jnp.maximum(jnp.max(s1, axis=1, keepdims=True), jnp.max(s2, axis=1, keepdims=True))
    p1 = jnp.exp2(s1 - m)
    p2 = jnp.exp2(s2 - m)
    l = jnp.sum(p1, axis=1, keepdims=True) + jnp.sum(p2, axis=1, keepdims=True)
    return (_nt_dot(p1.astype(BF16), v1_t) + _dot(p2.astype(BF16), v2)) / l


def _sample_kernel(qa_ref, ka_ref, va_ref, qb_ref, kb_ref, vb_ref, qc_ref, kc_ref, vc_ref, qd_ref, kd_ref, vd_ref,
                   cak_ref, cav_ref, cbk_ref, cbv_ref, cck_ref, ccv_ref, cckv_ref, ckr_ref,
                   ba_c_ref, ba_n_ref, fq_ref, fkc_ref, fkn_ref,
                   u_ref, un_ref, wukt_ref, wuv_ref, gk_ref,
                   oa_ref, ob_ref, oc_ref, od_ref, *, t, past, ks):
    row = lax.broadcasted_iota(jnp.int32, (t, t), 0)
    col = lax.broadcasted_iota(jnp.int32, (t, t), 1)
    causal_bias = jnp.where(col <= row, 0.0, NEG_INF)
    chunk_bias = jnp.where((past + col) // CHUNK <= (past + row) // CHUNK, 0.0, NEG_INF)
    strict = col < row

    ckv_c = cckv_ref[0].astype(BF16)
    kr_t = ckr_ref[0].astype(BF16)
    pad_t = jnp.zeros((LANE - MLA_NOPE - MLA_ROPE, past), BF16)

    for h in range(N_HEADS):
        hs = slice(LANE * h, LANE * h + LANE)
        ps = slice(LANE * (h // 2), LANE * (h // 2) + LANE)

        q = qa_ref[0][:, hs]
        s1 = _dot(q, cak_ref[0, ps, :].astype(BF16)) + ba_c_ref[h]
        s2 = _nt_dot(q, ka_ref[0][:, ps]) + ba_n_ref[h]
        oa_ref[0, :, hs] = _two_block_softmax(s1, s2, cav_ref[0, ps, :].astype(BF16), va_ref[0][:, ps])

        q = qb_ref[0][:, hs]
        c, w = _stick_weights(_nt_dot(q, kb_ref[0][:, ps]), jnp.zeros((t, 1), F32), un_ref[...], strict)
        acc = _dot(w.astype(BF16), vb_ref[0][:, ps])

        def b_live(state):
            return jnp.logical_and(state[0] >= 0, state[1] > DEAD_LOG2)

        def b_older(state, q=q, ps=ps):
            jb, _, c, acc = state
            off = pl.multiple_of(jb * ks, ks)
            kb_t = cbk_ref[0, ps, pl.ds(off, ks)].astype(BF16)
            vb_t = cbv_ref[0, ps, pl.ds(off, ks)].astype(BF16)
            c, w = _stick_weights(_dot(q, kb_t), c, u_ref[...], None)
            return jb - 1, jnp.max(c), c, acc + _nt_dot(w.astype(BF16), vb_t)

        _, _, _, acc = lax.while_loop(b_live, b_older, (past // ks - 1, jnp.max(c), c, acc))
        ob_ref[0, :, hs] = acc

        q = qc_ref[0][:, hs]
        fq = fq_ref[0][:, h:h + 1] * LOG2E
        s1 = _dot(q, cck_ref[0, ps, :].astype(BF16)) + (fq - fkc_ref[0, h:h + 1, :] * LOG2E)
        s2 = _nt_dot(q, kc_ref[0][:, ps]) + (fq - fkn_ref[0, h:h + 1, :] * LOG2E) + causal_bias
        oc_ref[0, :, hs] = _two_block_softmax(s1, s2, ccv_ref[0, ps, :].astype(BF16), vc_ref[0][:, ps])

        q = qd_ref[0][:, hs]
        kn_t = _nt_dot(wukt_ref[MLA_NOPE * h: MLA_NOPE * (h + 1), :], ckv_c)
        ss = jnp.sum(kn_t * kn_t, axis=0, keepdims=True)
        kn_t = kn_t * lax.rsqrt(ss * (1.0 / MLA_NOPE) + NORM_EPS) * gk_ref[...]
        k_t = jnp.concatenate([kn_t.astype(BF16), kr_t, pad_t], axis=0)
        s1 = _dot(q, k_t)
        s2 = _nt_dot(q, kd_ref[0][:, hs]) + chunk_bias
        v_c = _dot(ckv_c, wuv_ref[:, hs]).astype(BF16)
        m = jnp.maximum(jnp.max(s1, axis=1, keepdims=True), jnp.max(s2, axis=1, keepdims=True))
        p1 = jnp.exp2(s1 - m)
        p2 = jnp.exp2(s2 - m)
        l = jnp.sum(p1, axis=1, keepdims=True) + jnp.sum(p2, axis=1, keepdims=True)
        od_ref[0, :, hs] = (_dot(p1.astype(BF16), v_c) + _dot(p2.astype(BF16), vd_ref[0][:, hs])) / l


def _channel_major(cache):
    depth, nb, rows = cache.shape[:3]
    return jnp.transpose(cache, (0, 1, 3, 4, 2)).reshape(depth, nb, GROUP_W, rows)


def _sample_attention(pr, caches, layer, pw, w_uk, k_gain, rel_bias, nb, t):
    a_k, a_v, b_k, b_v, c_k, c_v, c_lf, d_ckv, d_kr = caches
    wuv = pw[4]
    past = b_k.shape[2]
    win = a_k.shape[2]
    ks = STICK_KS
    assert past % ks == 0
    r3 = lambda a: a.reshape(nb, t, a.shape[-1])
    news = [r3(a) for a in pr[8:20]]

    qpos = past + np.arange(t)
    kpos = past - win + np.arange(win + t)
    qc, kc = qpos // CHUNK, kpos // CHUNK
    ok = (kc[None, :] <= qc[:, None]) & (kc[None, :] >= qc[:, None] - A_LEFT_CHUNKS)
    ba = _rel_bias_tile(rel_bias, t, win + t, win, ok)
    ba_c, ba_n = ba[:, :, :win], ba[:, :, win:]

    clf_new = jnp.swapaxes(r3(pr[7])[..., 0:N_HEADS], 1, 2)
    lf_c = jnp.swapaxes(c_lf[layer], 1, 2).astype(F32)
    fk = jnp.cumsum(jnp.concatenate([lf_c, clf_new], axis=2), axis=2)
    fk_c, fk_n = fk[:, :, :past], fk[:, :, past:]
    fq = jnp.swapaxes(fk_n, 1, 2)

    wukt = w_uk.T.astype(BF16)
    gk = k_gain.reshape(MLA_NOPE, 1)
    u, un = _strict_upper(ks), _strict_upper(t)

    per_b = lambda shape: pl.BlockSpec((1,) + shape, lambda b: (b,) + (0,) * len(shape))
    per_lb = lambda shape: pl.BlockSpec((None, 1) + shape, lambda b: (layer, b) + (0,) * len(shape))
    const = lambda a: pl.BlockSpec(a.shape, lambda b: (0,) * a.ndim)
    cache_ops = [_channel_major(c) for c in (a_k, a_v, b_k, b_v, c_k, c_v)] + [d_ckv, jnp.swapaxes(d_kr, 2, 3)]
    operands = news + cache_ops + [ba_c, ba_n, fq, fk_c, fk_n, u, un, wukt, wuv, gk]
    in_specs = [per_b(a.shape[1:]) for a in news] + [per_lb(a.shape[2:]) for a in cache_ops] + \
               [const(ba_c), const(ba_n)] + [per_b(a.shape[1:]) for a in (fq, fk_c, fk_n)] + \
               [const(a) for a in (u, un, wukt, wuv, gk)]
    out_shape = [jax.ShapeDtypeStruct((nb, t, N_HEADS * LANE), F32)] * 4
    out_specs = [per_b((t, N_HEADS * LANE))] * 4
    return pl.pallas_call(
        functools.partial(_sample_kernel, t=t, past=past, ks=ks), grid=(nb,),
        in_specs=in_specs, out_specs=out_specs, out_shape=out_shape,
        compiler_params=_cparams(1), name="attn_sample",
    )(*operands)


def _out_kernel(oa_ref, ob_ref, oc_ref, od_ref, h_ref, gg_ref, wo_ref, nf_ref,
                wr1_ref, wr2_ref, wr3_ref, br_ref, h1_ref, hn_ref, lg_ref):
    rows = h_ref.shape[0]
    low = lax.broadcasted_iota(jnp.int32, (rows, LANE), 1) < HEAD_DIM
    h1 = h_ref[...]
    for g, o_ref in enumerate((oa_ref, ob_ref, oc_ref, od_ref)):
        p0 = jnp.where(low, o_ref[:, 0:LANE], o_ref[:, LANE:2 * LANE])
        p1 = jnp.where(low, o_ref[:, 2 * LANE:3 * LANE], o_ref[:, 3 * LANE:4 * LANE])
        og = jnp.concatenate([p0, p1], axis=1)
        ms = jnp.mean(og * og, axis=-1, keepdims=True)
        y = (og * lax.rsqrt(ms + NORM_EPS) * gg_ref[g:g + 1, :]).astype(BF16)
        h1 = h1 + _dot(y, wo_ref[GROUP_W * g: GROUP_W * (g + 1), :])
    h1_ref[...] = h1
    ms = jnp.mean(h1 * h1, axis=-1, keepdims=True)
    hn = h1 * lax.rsqrt(ms + NORM_EPS) * nf_ref[...]
    hn_ref[...] = hn.astype(BF16)
    a1, a2, a3 = _split3(hn)
    w1, w2, w3 = wr1_ref[...], wr2_ref[...], wr3_ref[...]
    lg = _dot(a1, w1) + _dot(a1, w2) + _dot(a2, w1) + _dot(a1, w3) + _dot(a2, w2) + _dot(a3, w1)
    lg_ref[...] = lg + br_ref[...]


def _out_project(outs, h, group_gain, w_out_b, norm_ffn, wr_parts, br_pad):
    t, d = h.shape
    ts = DENSE_ROWS
    assert t % ts == 0
    row = lambda i: (i, 0)
    full = lambda i: (0, 0)
    o_spec = pl.BlockSpec((ts, N_HEADS * LANE), row)
    in_specs = [o_spec] * 4 + [pl.BlockSpec((ts, d), row), pl.BlockSpec(group_gain.shape, full),
                               pl.BlockSpec(w_out_b.shape, full), pl.BlockSpec((1, d), full)] + \
               [pl.BlockSpec((d, LANE), full)] * 3 + [pl.BlockSpec((1, LANE), full)]
    out_shape = [jax.ShapeDtypeStruct((t, d), F32), jax.ShapeDtypeStruct((t, d), BF16),
                 jax.ShapeDtypeStruct((t, LANE), F32)]
    out_specs = [pl.BlockSpec((ts, d), row), pl.BlockSpec((ts, d), row), pl.BlockSpec((ts, LANE), row)]
    return pl.pallas_call(
        _out_kernel, grid=(t // ts,), in_specs=in_specs, out_specs=out_specs, out_shape=out_shape,
        compiler_params=_cparams(1), name="out_proj",
    )(*outs, h, group_gain, w_out_b, norm_ffn.reshape(1, d), *wr_parts, br_pad)


def _expert_kernel(be_ref, fe_ref, nu_ref, x_ref, wu_ref, bu_ref, wd_ref, bd_ref, *rest):
    y_ref, wub_ref, wdb_ref = rest[-3:]
    i = pl.program_id(0)
    used = i < nu_ref[0]

    @pl.when(jnp.logical_and(used, fe_ref[i] == 1))
    def _():
        wub_ref[...] = wu_ref[0, 0].astype(BF16)
        wdb_ref[...] = wd_ref[0, 0].astype(BF16)

    @pl.when(used)
    def _():
        u = _dot(x_ref[...], wub_ref[...]) + bu_ref[0, 0]
        glu = jnp.minimum(u[:, :D_FF], SWIGLU_LIMIT)
        lin = jnp.clip(u[:, D_FF:], -SWIGLU_LIMIT, SWIGLU_LIMIT)
        act = glu * jax.nn.sigmoid(SWIGLU_ALPHA * glu) * (lin + 1.0)
        y_ref[...] = (_dot(act.astype(BF16), wdb_ref[...]) + bd_ref[0, 0]).astype(y_ref.dtype)


def _expert_ffn(y_prev, blk0, n_blk_all, x_rows, blk_exp, blk_first, n_used, layer, w_up, b_up, w_down, b_down):
    rows, d = x_rows.shape
    n_blk = rows // MOE_BLOCK
    last = lambda i, nu: jnp.maximum(jnp.minimum(i, nu[0] - 1), 0)
    w_idx = lambda i, be, fe, nu: (layer, be[last(i, nu)], 0, 0)
    in_specs = [pl.BlockSpec((MOE_BLOCK, d), lambda i, be, fe, nu: (last(i, nu), 0)),
                pl.BlockSpec((1, 1, d, 2 * D_FF), w_idx), pl.BlockSpec((1, 1, 1, 2 * D_FF), w_idx),
                pl.BlockSpec((1, 1, D_FF, d), w_idx), pl.BlockSpec((1, 1, 1, d), w_idx)]
    depth = w_up.shape[0]
    operands = [blk_exp, blk_first, n_used, x_rows, w_up, b_up.reshape(depth, N_EXP, 1, 2 * D_FF),
                w_down, b_down.reshape(depth, N_EXP, 1, d)]
    aliases = {}
    if y_prev is not None:
        in_specs.append(pl.BlockSpec(memory_space=pl.ANY))
        aliases = {len(operands): 0}
        operands.append(y_prev)
    grid_spec = pltpu.PrefetchScalarGridSpec(
        num_scalar_prefetch=3, grid=(n_blk,), in_specs=in_specs,
        out_specs=pl.BlockSpec((MOE_BLOCK, d), lambda i, be, fe, nu: (blk0 + last(i, nu), 0)),
        scratch_shapes=[pltpu.VMEM((d, 2 * D_FF), BF16), pltpu.VMEM((D_FF, d), BF16)])
    return pl.pallas_call(
        _expert_kernel, grid_spec=grid_spec,
        out_shape=jax.ShapeDtypeStruct((n_blk_all * MOE_BLOCK, d), BF16), input_output_aliases=aliases,
        compiler_params=_cparams(1), name="expert_ffn",
    )(*operands)


def _moe(hn, logits, layer, w_up, b_up, w_down, b_down):
    n_tok, d = hn.shape
    top_v, top_i = lax.top_k(logits, TOP_K)
    gates = jax.nn.softmax(top_v, axis=-1)
    n = n_tok * TOP_K
    e_flat = top_i.reshape(-1)
    onehot = (e_flat[:, None] == jnp.arange(N_EXP)[None, :]).astype(jnp.int32)
    rank = jnp.sum((jnp.cumsum(onehot, axis=0) - onehot) * onehot, axis=1)
    counts = jnp.sum(onehot, axis=0)
    padded = (counts + MOE_BLOCK - 1) // MOE_BLOCK * MOE_BLOCK
    p_end = jnp.cumsum(padded)
    p_start = p_end - padded
    dest = p_start[e_flat] + rank
    n_blk = -(-n // MOE_BLOCK) + N_EXP
    rows = n_blk * MOE_BLOCK
    row_tok = jnp.zeros((rows,), jnp.int32).at[dest].set(jnp.arange(n, dtype=jnp.int32) // TOP_K)
    blk_start = jnp.arange(n_blk, dtype=jnp.int32) * MOE_BLOCK
    blk_exp = jnp.sum((p_end[None, :] <= blk_start[:, None]).astype(jnp.int32), axis=1)
    blk_exp = jnp.minimum(blk_exp, N_EXP - 1)
    n_used = (p_end[-1] // MOE_BLOCK).astype(jnp.int32)
    n_grp = MOE_GROUPS if n_blk % MOE_GROUPS == 0 else 1
    nbg = n_blk // n_grp
    y = None
    for g in range(n_grp):
        be = blk_exp[g * nbg:(g + 1) * nbg]
        fe = jnp.concatenate([jnp.ones((1,), jnp.int32), (be[1:] != be[:-1]).astype(jnp.int32)])
        x_rows = hn[row_tok[g * nbg * MOE_BLOCK:(g + 1) * nbg * MOE_BLOCK]]
        y = _expert_ffn(y, g * nbg, n_blk, x_rows, be, fe, (n_used - g * nbg).reshape(1), layer,
                        w_up, b_up, w_down, b_down)
    picked = y[dest.reshape(n_tok, TOP_K).T]
    return picked, gates


def _ple_kernel(h_ref, y_ref, g_ref, p_ref, np_ref, wg_ref, wp_ref, o_ref):
    h2 = h_ref[...]
    g = g_ref[...]
    for j in range(TOP_K):
        h2 = h2 + y_ref[j].astype(F32) * g[:, j:j + 1]
    ms = jnp.mean(h2 * h2, axis=-1, keepdims=True)
    hn = (h2 * lax.rsqrt(ms + NORM_EPS) * np_ref[...]).astype(BF16)
    gate = jax.nn.sigmoid(_dot(hn, wg_ref[...]))
    o_ref[...] = h2 + gate * _dot(p_ref[...].astype(BF16), wp_ref[...])


def _ple(h1, picked, gates, row0, p, layer, norm_ple, wg_b, wp_b):
    t, d = h1.shape
    ts = PLE_ROWS
    assert t % ts == 0 and row0 % ts == 0
    off = row0 // ts
    row = lambda i: (i, 0)
    full = lambda i: (0, 0)
    return pl.pallas_call(
        _ple_kernel, grid=(t // ts,),
        in_specs=[pl.BlockSpec((ts, d), row), pl.BlockSpec((TOP_K, ts, d), lambda i: (0, i + off, 0)),
                  pl.BlockSpec((ts, TOP_K), lambda i: (i + off, 0)),
                  pl.BlockSpec((None, ts, p.shape[2]), lambda i: (layer, i, 0)),
                  pl.BlockSpec((1, d), full), pl.BlockSpec(wg_b.shape, full), pl.BlockSpec(wp_b.shape, full)],
        out_specs=pl.BlockSpec((ts, d), row), out_shape=jax.ShapeDtypeStruct((t, d), F32),
        compiler_params=_cparams(1), name="ple",
    )(h1, picked, gates, p, norm_ple.reshape(1, d), wg_b, wp_b)


def kernel(x_prompt, x_sample, p_prompt, p_sample, cache_a_k, cache_a_v, cache_b_k, cache_b_v, cache_c_k, cache_c_v, cache_c_logf, cache_d_ckv, cache_d_krope, norm_mix, w_in, b_forget, qk_gain, rope_gain, kv_gain, w_uk, w_uv, rel_bias, group_gain, w_out, norm_ffn, w_router, b_router, w_up, b_up, w_down, b_down, norm_ple, w_ple_gate, w_ple_proj):
    nb, s, d = x_prompt.shape
    nd, t, _ = x_sample.shape
    depth = w_in.shape[0]
    past = cache_b_k.shape[2]
    assert PROJ_ROWS % t == 0 and s % PROJ_ROWS == 0

    tab_p = _rope_tables(jnp.arange(s))
    tab_s = _rope_tables(past + jnp.arange(PROJ_ROWS) % t)
    hp = x_prompt.reshape(nb * s, d)
    hs = x_sample.reshape(nd * t, d)
    st_p, st_s = [], []
    keep = min(A_WINDOW, s)
    for i in range(depth):
        pw = _proj_weights(w_in[i], b_forget[i], qk_gain[i], rope_gain[i], kv_gain[i], w_uk[i], w_uv[i])
        pr_p = _project(hp, norm_mix[i], pw, tab_p, s // PROJ_ROWS)
        pr_s = _project(hs, norm_mix[i], pw, tab_s, 1)
        outs_p = _prompt_attention(pr_p, nb, s, rel_bias[i], qk_gain[i, 3])
        caches = (cache_a_k, cache_a_v, cache_b_k, cache_b_v, cache_c_k, cache_c_v,
                  cache_c_logf, cache_d_ckv, cache_d_krope)
        outs_s = _sample_attention(pr_s, caches, i, pw, w_uk[i], qk_gain[i, 5], rel_bias[i], nd, t)

        def heads(a, n, rows):
            return a.reshape(n, rows, N_HEADS, HEAD_DIM)

        p3 = lambda a: a.reshape(nb, s, a.shape[-1])
        s3 = lambda a: a.reshape(nd, t, a.shape[-1])
        st_p.append((heads(p3(pr_p[0])[:, s - keep:], nb, keep), heads(p3(pr_p[1])[:, s - keep:], nb, keep),
                     heads(pr_p[2], nb, s), heads(pr_p[3], nb, s), heads(pr_p[4], nb, s), heads(pr_p[5], nb, s),
                     p3(pr_p[7])[..., 0:N_HEADS], p3(pr_p[6]), p3(pr_p[7])[..., ROPE_LO:ROPE_LO + MLA_ROPE]))
        ka_all = jnp.concatenate([cache_a_k[i], heads(pr_s[0], nd, t)], axis=1)[:, t:]
        va_all = jnp.concatenate([cache_a_v[i], heads(pr_s[1], nd, t)], axis=1)[:, t:]
        st_s.append((ka_all, va_all,
                     heads(pr_s[2], nd, t), heads(pr_s[3], nd, t), heads(pr_s[4], nd, t), heads(pr_s[5], nd, t),
                     s3(pr_s[7])[..., 0:N_HEADS], s3(pr_s[6]), s3(pr_s[7])[..., ROPE_LO:ROPE_LO + MLA_ROPE]))

        w_out_b = w_out[i].astype(BF16)
        wr_pad = jnp.zeros((d, LANE), F32).at[:, 0:N_EXP].set(w_router[i])
        wr_parts = _split3(wr_pad)
        br_pad = jnp.zeros((1, LANE), F32).at[0, 0:N_EXP].set(b_router[i])
        flat4 = lambda o: o.reshape(-1, N_HEADS * LANE)
        h1_p, hn_p, lg_p = _out_project([flat4(o) for o in outs_p], hp, group_gain[i], w_out_b, norm_ffn[i],
                                        wr_parts, br_pad)
        h1_s, hn_s, lg_s = _out_project([flat4(o) for o in outs_s], hs, group_gain[i], w_out_b, norm_ffn[i],
                                        wr_parts, br_pad)

        hn_all = jnp.concatenate([hn_p, hn_s], axis=0)
        lg_all = jnp.concatenate([lg_p, lg_s], axis=0)[:, 0:N_EXP]
        picked, gates = _moe(hn_all, lg_all, i, w_up, b_up, w_down, b_down)

        wg_b = w_ple_gate[i].astype(BF16)
        wp_b = w_ple_proj[i].astype(BF16)
        hp = _ple(h1_p, picked, gates, 0, p_prompt.reshape(depth, nb * s, -1), i, norm_ple[i], wg_b, wp_b)
        hs = _ple(h1_s, picked, gates, nb * s, p_sample.reshape(depth, nd * t, -1), i, norm_ple[i], wg_b, wp_b)

    state_p = [jnp.stack([st[j] for st in st_p]) for j in range(9)]
    state_s = [jnp.stack([st[j] for st in st_s]) for j in range(9)]
    return (hp.reshape(nb, s, d), hs.reshape(nd, t, d), *state_p, *state_s)
```

```python
import functools

import numpy as np
import jax
import jax.numpy as jnp
from jax import lax
from jax.experimental import pallas as pl
from jax.experimental.pallas import tpu as pltpu

F32 = jnp.float32
BF16 = jnp.bfloat16

CHUNK = 64
HEAD_DIM = 64
N_HEADS = 4
GROUP_W = 256
A_LEFT_CHUNKS = 8
A_WINDOW = A_LEFT_CHUNKS * CHUNK
REL_CLIP = 128
MLA_NOPE = 64
MLA_ROPE = 32
KV_RANK = 128
ROPE_BASE = 10000.0
N_EXP = 32
TOP_K = 4
D_FF = 1024
SWIGLU_LIMIT = 7.0
SWIGLU_ALPHA = 1.702
MOE_BLOCK = 512
NORM_EPS = 1e-6
NEG_INF = -1e30
LOG2E = 1.4426950408889634

A_Q = 0
C_F = 2304
D_Q = 2308
D_CKV = D_Q + N_HEADS * (MLA_NOPE + MLA_ROPE)
D_KR = D_CKV + KV_RANK

LANE = 128
SEG_ABC = 0
SEG_DQ = 2304
SEG_CKV = 2816
SEG_MISC = 2944
N_COLS = 3072
ROPE_LO = 64
ROPE_HALF = MLA_ROPE // 2

VMEM_LIMIT = 56 * 1024 * 1024

PROJ_ROWS = 512
ATTN_TQ = 512
ATTN_TK = 512
MLA_TQ = 2048
STICK_KS = 256
DENSE_ROWS = 512
PLE_ROWS = 256
MOE_GROUPS = 4

DEAD_LOG2 = -160.0
BOUND_MARGIN = 1.01


def _cparams(n_axes):
    return pltpu.CompilerParams(dimension_semantics=("arbitrary",) * n_axes,
                                vmem_limit_bytes=VMEM_LIMIT)


def _nt_dot(a, b):
    return lax.dot_general(a, b, (((1,), (1,)), ((), ())), preferred_element_type=F32)


def _dot(a, b):
    return jnp.dot(a, b, preferred_element_type=F32)


def _split2(x):
    hi = x.astype(BF16)
    lo = (x - hi.astype(F32)).astype(BF16)
    return hi, lo


def _split3(x):
    hi = x.astype(BF16)
    r = x - hi.astype(F32)
    mid = r.astype(BF16)
    lo = (r - mid.astype(F32)).astype(BF16)
    return hi, mid, lo


def _in_col_map():
    cols = np.full((N_COLS,), -1, np.int64)
    cols[0:2304] = np.arange(2304)
    for h in range(N_HEADS):
        base = D_Q + (MLA_NOPE + MLA_ROPE) * h
        cols[SEG_DQ + LANE * h: SEG_DQ + LANE * h + MLA_NOPE + MLA_ROPE] = base + np.arange(MLA_NOPE + MLA_ROPE)
    cols[SEG_CKV:SEG_CKV + KV_RANK] = D_CKV + np.arange(KV_RANK)
    cols[SEG_MISC:SEG_MISC + N_HEADS] = C_F + np.arange(N_HEADS)
    cols[SEG_MISC + ROPE_LO:SEG_MISC + ROPE_LO + MLA_ROPE] = D_KR + np.arange(MLA_ROPE)
    return cols


def _rope_tables(pos):
    inv = ROPE_BASE ** (-jnp.arange(ROPE_HALF, dtype=F32) / ROPE_HALF)
    ang = pos.astype(F32)[:, None] * inv
    cos, sin = jnp.cos(ang), jnp.sin(ang)
    n = pos.shape[0]
    one = jnp.ones((n, ROPE_LO), F32)
    z16 = jnp.zeros((n, ROPE_HALF), F32)
    z64 = jnp.zeros((n, ROPE_LO), F32)
    z32 = jnp.zeros((n, LANE - ROPE_LO - MLA_ROPE), F32)
    cos_t = jnp.concatenate([one, cos, cos, z32 + 1.0], axis=1)
    sin_a = jnp.concatenate([z64, -sin, z16, z32], axis=1)
    sin_b = jnp.concatenate([z64, z16, sin, z32], axis=1)
    return cos_t, sin_a, sin_b


def _rope(y, cos_t, sin_a, sin_b):
    left = pltpu.roll(y, LANE - ROPE_HALF, axis=1)
    right = pltpu.roll(y, ROPE_HALF, axis=1)
    return y * cos_t + left * sin_a + right * sin_b


def _head_norm(x, m_ref, gain):
    hi, lo = _split2(x * x)
    ssq = _dot(hi, m_ref[...]) + _dot(lo, m_ref[...])
    return x * lax.rsqrt(ssq * (1.0 / HEAD_DIM) + NORM_EPS) * gain


def _value_lane0(h):
    return HEAD_DIM * (h % 2)


def _ones_lane(h):
    return HEAD_DIM - _value_lane0(h)


def _store_padded_q(q_ref, q, scale):
    lane = lax.broadcasted_iota(jnp.int32, (q.shape[0], LANE), 1)
    low = lane < HEAD_DIM
    qs = q * scale
    for h in range(N_HEADS):
        pair = qs[:, LANE * (h // 2): LANE * (h // 2) + LANE]
        keep = low if h % 2 == 0 else jnp.logical_not(low)
        q_ref[:, LANE * h: LANE * h + LANE] = jnp.where(keep, pair, 0.0).astype(BF16)


def _proj_kernel(*refs, channel_major, n_alias):
    x_ref, gmix_ref, w_ref, cos_ref, sa_ref, sb_ref, gv_ref, m64_ref, wuk_ref, wuv_ref = refs[:10]
    (ak_ref, av_ref, bk_ref, bv_ref, ck_ref, cv_ref, ckv_ref, misc_ref,
     qa_ref, ka_ref, va_ref, qb_ref, kb_ref, vb_ref, qc_ref, kc_ref, vc_ref,
     qd_ref, kd_ref, vd_ref) = refs[10 + n_alias:]

    def put_state(ref, val):
        ref[...] = val.T if channel_major else val

    x = x_ref[...]
    ms = jnp.mean(x * x, axis=-1, keepdims=True)
    hn = (x * lax.rsqrt(ms + NORM_EPS) * gmix_ref[...]).astype(BF16)
    sm_scale = (HEAD_DIM ** -0.5) * LOG2E
    rows = x.shape[0]
    lane = lax.broadcasted_iota(jnp.int32, (rows, LANE), 1)
    is_nope = lane < MLA_NOPE
    is_rope = jnp.logical_and(lane >= ROPE_LO, lane < ROPE_LO + MLA_ROPE)
    cos_t, sin_a, sin_b = cos_ref[...], sa_ref[...], sb_ref[...]

    z = _dot(hn, w_ref[:, 0:768])
    aq = _head_norm(z[:, 0:256], m64_ref, gv_ref[0:1, :])
    ak = _head_norm(z[:, 256:512], m64_ref, gv_ref[1:2, :])
    av = z[:, 512:768]
    ak_ref[...] = ak
    av_ref[...] = av
    _store_padded_q(qa_ref, aq, sm_scale)
    ka_ref[...] = ak.astype(BF16)
    va_ref[...] = av.astype(BF16)

    z = _dot(hn, w_ref[:, 768:1536])
    put_state(bk_ref, z[:, 256:512])
    put_state(bv_ref, z[:, 512:768])
    _store_padded_q(qb_ref, z[:, 0:256], sm_scale)
    kb_ref[...] = z[:, 256:512].astype(BF16)
    vb_ref[...] = z[:, 512:768].astype(BF16)

    z = _dot(hn, w_ref[:, 1536:2304])
    cq = _head_norm(z[:, 0:256], m64_ref, gv_ref[2:3, :])
    ck = _head_norm(z[:, 256:512], m64_ref, gv_ref[3:4, :])
    cv = z[:, 512:768]
    put_state(ck_ref, ck)
    put_state(cv_ref, cv)
    _store_padded_q(qc_ref, cq, sm_scale)
    kc_ref[...] = ck.astype(BF16)
    vc_ref[...] = cv.astype(BF16)

    z = _dot(hn, w_ref[:, SEG_CKV:N_COLS])
    zc = z[:, 0:KV_RANK]
    ckv = zc * lax.rsqrt(jnp.mean(zc * zc, axis=-1, keepdims=True) + NORM_EPS) * gv_ref[7:8, 0:LANE]
    ckv_ref[...] = ckv
    zm = z[:, KV_RANK:2 * KV_RANK]
    ssr = jnp.sum(jnp.where(is_rope, zm * zm, 0.0), axis=-1, keepdims=True)
    kr = zm * lax.rsqrt(ssr * (1.0 / MLA_ROPE) + NORM_EPS) * gv_ref[6:7, 0:LANE]
    kr = _rope(kr, cos_t, sin_a, sin_b)
    zf = zm + gv_ref[8:9, 0:LANE]
    clf = jnp.minimum(zf, 0.0) - jnp.log1p(jnp.exp(-jnp.abs(zf)))
    put_state(misc_ref, jnp.where(lane < N_HEADS, clf, kr))

    ckv_b = ckv.astype(BF16)
    kn = _dot(ckv_b, wuk_ref[...])
    for h in range(N_HEADS):
        hs = slice(LANE * h, LANE * h + LANE)
        ones = (lane == _ones_lane(h)).astype(F32)
        vd_ref[:, hs] = (_dot(ckv_b, wuv_ref[:, hs]) + ones).astype(BF16)
    for h in range(N_HEADS):
        xh = kn[:, LANE * h: LANE * h + LANE]
        ss = jnp.sum(xh * xh, axis=-1, keepdims=True)
        yh = xh * lax.rsqrt(ss * (1.0 / MLA_NOPE) + NORM_EPS) * gv_ref[5:6, 0:LANE]
        kd_ref[:, LANE * h: LANE * h + LANE] = (yh + kr).astype(BF16)

    z = _dot(hn, w_ref[:, SEG_DQ:SEG_CKV])
    d_scale = ((MLA_NOPE + MLA_ROPE) ** -0.5) * LOG2E
    for h in range(N_HEADS):
        xh = z[:, LANE * h: LANE * h + LANE]
        x2 = xh * xh
        ssn = jnp.sum(jnp.where(is_nope, x2, 0.0), axis=-1, keepdims=True)
        ssr = jnp.sum(jnp.where(is_rope, x2, 0.0), axis=-1, keepdims=True)
        rn = lax.rsqrt(ssn * (1.0 / MLA_NOPE) + NORM_EPS)
        rr = lax.rsqrt(ssr * (1.0 / MLA_ROPE) + NORM_EPS)
        yh = xh * jnp.where(is_nope, rn, rr) * gv_ref[4:5, 0:LANE]
        yh = _rope(yh, cos_t, sin_a, sin_b)
        qd_ref[:, LANE * h: LANE * h + LANE] = (yh * d_scale).astype(BF16)


def _proj_weights(w_in, b_forget, qk_gain, rope_gain, kv_gain, w_uk, w_uv):
    cols = _in_col_map()
    valid = jnp.asarray(cols >= 0)
    w = jnp.where(valid[None, :], w_in[:, np.maximum(cols, 0)], 0.0).astype(BF16)

    def tile4(g):
        return jnp.tile(g, N_HEADS)

    gv = jnp.zeros((16, GROUP_W), F32)
    gv = gv.at[0].set(tile4(qk_gain[0])).at[1].set(tile4(qk_gain[1]))
    gv = gv.at[2].set(tile4(qk_gain[2])).at[3].set(tile4(qk_gain[3]))
    gv = gv.at[4, 0:MLA_NOPE].set(qk_gain[4]).at[4, ROPE_LO:ROPE_LO + MLA_ROPE].set(rope_gain[0])
    gv = gv.at[5, 0:MLA_NOPE].set(qk_gain[5])
    gv = gv.at[6, ROPE_LO:ROPE_LO + MLA_ROPE].set(rope_gain[1])
    gv = gv.at[7, 0:KV_RANK].set(kv_gain)
    gv = gv.at[8, 0:N_HEADS].set(b_forget)
    head = np.arange(GROUP_W) // HEAD_DIM
    m64 = jnp.asarray((head[:, None] == head[None, :]).astype(np.float32), BF16)
    wuk = jnp.zeros((KV_RANK, N_HEADS * LANE), F32)
    for h in range(N_HEADS):
        wuk = wuk.at[:, LANE * h: LANE * h + MLA_NOPE].set(w_uk[:, MLA_NOPE * h: MLA_NOPE * (h + 1)])
    wuv = jnp.zeros((KV_RANK, N_HEADS * LANE), F32)
    for h in range(N_HEADS):
        lo = LANE * h + _value_lane0(h)
        wuv = wuv.at[:, lo: lo + HEAD_DIM].set(w_uv[:, HEAD_DIM * h: HEAD_DIM * (h + 1)])
    return w, gv, m64, wuk.astype(BF16), wuv.astype(BF16)


STATE_SLOTS = (2, 3, 4, 5, 6, 7)


def _project(x, gmix, pw, tables, n_tab_blocks, layered=None):
    w, gv, m64, wuk, wuv = pw
    t, d = x.shape
    ts = PROJ_ROWS
    assert t % ts == 0
    row = lambda i: (i, 0)
    full = lambda i: (0, 0)
    tab = lambda i: (i % n_tab_blocks, 0)
    f32_w = [GROUP_W] * 6 + [KV_RANK, LANE]
    bf_w = [512, 256, 256, 512, 256, 256, 512, 256, 256, 512, 512, 512]
    out_shape = [jax.ShapeDtypeStruct((t, c), F32) for c in f32_w] + \
                [jax.ShapeDtypeStruct((t, c), BF16) for c in bf_w]
    out_specs = [pl.BlockSpec((ts, c), row) for c in f32_w + bf_w]
    in_specs = [pl.BlockSpec((ts, d), row), pl.BlockSpec((1, d), full), pl.BlockSpec((d, N_COLS), full),
                pl.BlockSpec((ts, LANE), tab), pl.BlockSpec((ts, LANE), tab), pl.BlockSpec((ts, LANE), tab),
                pl.BlockSpec(gv.shape, full), pl.BlockSpec(m64.shape, full),
                pl.BlockSpec(wuk.shape, full), pl.BlockSpec(wuv.shape, full)]
    operands = [x, gmix.reshape(1, d), w, *tables, gv, m64, wuk, wuv]
    aliases = {}
    if layered is not None:
        depth, layer, nb, earlier = layered
        s = t // nb
        nt = s // ts
        for slot in STATE_SLOTS:
            c = f32_w[slot]
            if slot == 6:
                out_shape[slot] = jax.ShapeDtypeStruct((depth, t, c), F32)
                out_specs[slot] = pl.BlockSpec((None, ts, c), lambda i: (layer, i, 0))
            else:
                out_shape[slot] = jax.ShapeDtypeStruct((depth, nb, c, s), F32)
                out_specs[slot] = pl.BlockSpec((None, None, c, ts), lambda i: (layer, i // nt, 0, i % nt))
        if earlier is not None:
            for k, slot in enumerate(STATE_SLOTS):
                aliases[len(operands)] = slot
                in_specs.append(pl.BlockSpec(memory_space=pl.ANY))
                operands.append(earlier[k])
    kern = functools.partial(_proj_kernel, channel_major=layered is not None, n_alias=len(aliases))
    return pl.pallas_call(
        kern, grid=(t // ts,), in_specs=in_specs, out_specs=out_specs, out_shape=out_shape,
        input_output_aliases=aliases, compiler_params=_cparams(1), name="proj",
    )(*operands)


def _softmax_step(q, kb, vb, carry, bias=None, ok=None):
    m, l, acc = carry
    s = _nt_dot(q, kb)
    if bias is not None:
        s = s + bias
    if ok is not None:
        s = jnp.where(ok, s, NEG_INF)
    m_new = jnp.maximum(m, jnp.max(s, axis=1, keepdims=True))
    alpha = jnp.exp2(m - m_new)
    p = jnp.exp2(s - m_new)
    l = alpha * l + jnp.sum(p, axis=1, keepdims=True)
    acc = alpha * acc + _dot(p.astype(BF16), vb)
    return m_new, l, acc


def _softmax_init(tq):
    return (jnp.full((tq, 1), NEG_INF, F32), jnp.zeros((tq, 1), F32), jnp.zeros((tq, LANE), F32))


def _pick_lane(block, h):
    col = lax.broadcasted_iota(jnp.int32, block.shape, 1)
    return jnp.sum(jnp.where(col == h, block, 0.0), axis=1, keepdims=True)


def _stack_pair(q_ref):
    return jnp.concatenate([q_ref[0, :, 0:LANE], q_ref[0, :, LANE:2 * LANE]], axis=0)


def _store_pair(o_ref, o, tq):
    o_ref[0, :, 0:LANE] = o[:tq]
    o_ref[0, :, LANE:2 * LANE] = o[tq:]


def _flash_forget_kernel(kmax_ref, decay_ref, q_ref, k_ref, v_ref, fq_ref, fk_ref, o_ref, *, tq, n_blk):
    b = pl.program_id(0)
    pair = pl.program_id(1)
    qi = pl.program_id(2)
    q = _stack_pair(q_ref)
    fq = jnp.concatenate([_pick_lane(fq_ref[0], 2 * pair), _pick_lane(fq_ref[0], 2 * pair + 1)], axis=0) * LOG2E

    def step(j, carry, diag):
        off = pl.multiple_of(j * tq, tq)
        kb = k_ref[0, pl.ds(off, tq), :]
        vb = v_ref[0, pl.ds(off, tq), :]
        fk = jnp.concatenate([jnp.broadcast_to(fk_ref[0, 0, pl.ds(j, 1), :], (tq, tq)),
                              jnp.broadcast_to(fk_ref[0, 1, pl.ds(j, 1), :], (tq, tq))], axis=0)
        ok = None
        if diag:
            row = lax.broadcasted_iota(jnp.int32, (2 * tq, tq), 0) % tq
            col = lax.broadcasted_iota(jnp.int32, (2 * tq, tq), 1)
            ok = col <= row
        return _softmax_step(q, kb, vb, carry, fq - fk * LOG2E, ok)

    carry = step(qi, _softmax_init(2 * tq), True)
    qf = q.astype(F32)
    q_norm = jnp.sqrt(jnp.sum(qf * qf, axis=1, keepdims=True))
    head0 = b * N_HEADS + 2 * pair
    room = q_norm * jnp.where(lax.broadcasted_iota(jnp.int32, q_norm.shape, 0) < tq,
                              kmax_ref[head0], kmax_ref[head0 + 1]) + fq - carry[0]
    slack0 = jnp.max(room[:tq]) - DEAD_LOG2
    slack1 = jnp.max(room[tq:]) - DEAD_LOG2

    def live(state):
        j = jnp.maximum(state[0], 0)
        alive = jnp.logical_or(slack0 + decay_ref[head0 * n_blk + j] >= 0.0,
                               slack1 + decay_ref[(head0 + 1) * n_blk + j] >= 0.0)
        return jnp.logical_and(state[0] >= 0, alive)

    def older(state):
        j = state[0]
        return (j - 1,) + step(j, state[1:], False)

    _, m, l, acc = lax.while_loop(live, older, (qi - 1,) + carry)
    _store_pair(o_ref, acc / l, tq)


def _flash_mla_kernel(q_ref, k_ref, v_ref, o_ref, *, tq, tk):
    h = pl.program_id(1)
    qi = pl.program_id(2)
    q = q_ref[0]
    n_sub = tq // tk

    def step(j, carry, row0=None):
        m, acc = carry
        off = pl.multiple_of(j * tk, tk)
        qs = q if row0 is None else q[row0:]
        s = _nt_dot(qs, k_ref[0, pl.ds(off, tk), :])
        if row0 is not None:
            row = lax.broadcasted_iota(jnp.int32, s.shape, 0)
            col = lax.broadcasted_iota(jnp.int32, s.shape, 1)
            s = jnp.where(col // CHUNK <= row // CHUNK, s, NEG_INF)
        m_new = jnp.maximum(m, jnp.max(s, axis=1, keepdims=True))
        p = jnp.exp2(s - m_new)
        acc = jnp.exp2(m - m_new) * acc + _dot(p.astype(BF16), v_ref[0, pl.ds(off, tk), :])
        return m_new, acc

    carry = (jnp.full((tq, 1), NEG_INF, F32), jnp.zeros((tq, LANE), F32))
    m, acc = lax.fori_loop(0, qi * n_sub, lambda j, c: step(j, c), carry)
    for r in range(n_sub):
        row0 = r * tk
        m_r, acc_r = step(qi * n_sub + r, (m[row0:], acc[row0:]), row0)
        m = m_r if r == 0 else jnp.concatenate([m[:row0], m_r], axis=0)
        acc = acc_r if r == 0 else jnp.concatenate([acc[:row0], acc_r], axis=0)
    denom = jnp.where(h % 2 == 0, acc[:, HEAD_DIM:HEAD_DIM + 1], acc[:, 0:1])
    o_ref[0] = acc / denom


def _flash_stick_kernel(q_ref, k_ref, v_ref, u_ref, o_ref, *, tq, ks):
    qi = pl.program_id(2)
    q = _stack_pair(q_ref)
    n_sub = tq // ks

    def step(jb, carry, diag):
        c, acc = carry
        off = pl.multiple_of(jb * ks, ks)
        kb = k_ref[0, pl.ds(off, ks), :]
        vb = v_ref[0, pl.ds(off, ks), :]
        vis = None
        if diag:
            row = lax.broadcasted_iota(jnp.int32, (2 * tq, ks), 0) % tq + qi * tq
            col = lax.broadcasted_iota(jnp.int32, (2 * tq, ks), 1) + jb * ks
            vis = col < row
        c_new, w = _stick_weights(_nt_dot(q, kb), c, u_ref[...], vis)
        return c_new, acc + _dot(w.astype(BF16), vb)

    carry = (jnp.zeros((2 * tq, 1), F32), jnp.zeros((2 * tq, LANE), F32))
    for r in range(n_sub):
        carry = step(qi * n_sub + (n_sub - 1 - r), carry, True)

    def live(state):
        return jnp.logical_and(state[0] >= 0, state[1] > DEAD_LOG2)

    def older(state):
        c, acc = step(state[0], state[2:], False)
        return state[0] - 1, jnp.max(c), c, acc

    _, _, _, acc = lax.while_loop(live, older, (qi * n_sub - 1, jnp.max(carry[0])) + carry)
    _store_pair(o_ref, acc, tq)


def _stick_weights(z, c, u, vis):
    sp = jnp.log(1.0 + jnp.exp2(-jnp.abs(z))) * LOG2E
    log_rest = jnp.minimum(-z, 0.0) - sp
    log_beta = log_rest + z
    if vis is not None:
        log_rest = jnp.where(vis, log_rest, 0.0)
    hi, lo = _split2(log_rest)
    between = _dot(hi, u) + _dot(lo, u)
    w = jnp.exp2(log_beta + between + c)
    if vis is not None:
        w = jnp.where(vis, w, 0.0)
    return c + jnp.sum(log_rest, axis=1, keepdims=True), w


def _band_kernel(q_ref, k_ref, v_ref, bd_ref, bp_ref, o_ref, *, tq):
    qi = pl.program_id(2)
    q = _stack_pair(q_ref)
    off = pl.multiple_of(qi * tq, tq)
    carry = _softmax_step(q, k_ref[0, pl.ds(off, tq), :], v_ref[0, pl.ds(off, tq), :],
                          _softmax_init(2 * tq), bd_ref[...].reshape(2 * tq, tq))
    offp = pl.multiple_of(jnp.maximum(qi - 1, 0) * tq, tq)
    no_prev = jnp.where(qi == 0, NEG_INF, 0.0)
    m, l, acc = _softmax_step(q, k_ref[0, pl.ds(offp, tq), :], v_ref[0, pl.ds(offp, tq), :],
                              carry, bp_ref[...].reshape(2 * tq, tq) + no_prev)
    _store_pair(o_ref, acc / l, tq)


def _pair_specs(s, tq):
    q_spec = pl.BlockSpec((1, tq, 2 * LANE), lambda b, p, i: (b, i, p))
    kv_spec = pl.BlockSpec((1, s, LANE), lambda b, p, i: (b, 0, p))
    return q_spec, kv_spec


def _pair_call(kern, nb, s, tq, in_specs, operands, name):
    return pl.pallas_call(
        kern, grid=(nb, N_HEADS // 2, s // tq), in_specs=in_specs,
        out_specs=pl.BlockSpec((1, tq, 2 * LANE), lambda b, p, i: (b, i, p)),
        out_shape=jax.ShapeDtypeStruct((nb, s, N_HEADS * LANE), F32),
        compiler_params=_cparams(3), name=name,
    )(*operands)


def _head_call(kern, nb, s, tq, in_specs, operands, name):
    return pl.pallas_call(
        kern, grid=(nb, N_HEADS, s // tq), in_specs=in_specs,
        out_specs=pl.BlockSpec((1, tq, LANE), lambda b, h, i: (b, i, h)),
        out_shape=jax.ShapeDtypeStruct((nb, s, N_HEADS * LANE), F32),
        compiler_params=_cparams(3), name=name,
    )(*operands)


def _strict_upper(n):
    idx = np.arange(n)
    return jnp.asarray((idx[:, None] > idx[None, :]).astype(np.float32), BF16)


def _toeplitz(vec, n_rows, n_cols):
    length = n_rows + n_cols - 1
    assert vec.shape[-1] == length
    lead = vec.shape[:-1]
    rev = jnp.concatenate([vec[..., ::-1], jnp.zeros(lead + (1,), vec.dtype)], axis=-1)
    flat = jnp.tile(rev, (1,) * len(lead) + (n_rows,))[..., :n_rows * length]
    return flat.reshape(lead + (n_rows, length))[..., n_rows - 1: n_rows - 1 + n_cols]


def _rel_bias_tile(rel_bias, n_rows, n_cols, rel00, ok):
    d = np.arange(n_rows + n_cols - 1) - (n_cols - 1) + rel00
    vec = rel_bias.astype(F32)[:, np.clip(d, -REL_CLIP, REL_CLIP) + REL_CLIP] * LOG2E
    return jnp.where(jnp.asarray(ok)[None], _toeplitz(vec, n_rows, n_cols), NEG_INF)


def _band_bias_tiles(rel_bias, tq):
    i = np.arange(tq)[:, None]
    j = np.arange(tq)[None, :]
    own = _rel_bias_tile(rel_bias, tq, tq, 0, (j // CHUNK) <= (i // CHUNK))
    prev = _rel_bias_tile(rel_bias, tq, tq, tq, (j // CHUNK) >= (i // CHUNK) + tq // CHUNK - A_LEFT_CHUNKS)
    return own, prev


def _prompt_attention(pr, misc_t, nb, s, rel_bias, k_gain):
    tq = ATTN_TQ
    assert s % tq == 0 and tq == A_WINDOW
    r3 = lambda a: a.reshape(nb, s, a.shape[-1])
    qa, ka, va, qb, kb, vb, qc, kc, vc, qd, kd, vd = [r3(a) for a in pr[8:20]]
    q_spec, kv_spec = _pair_specs(s, tq)

    bd, bp = _band_bias_tiles(rel_bias, tq)
    b_spec = pl.BlockSpec((2, tq, tq), lambda b, p, i: (p, 0, 0))
    o_a = _pair_call(functools.partial(_band_kernel, tq=tq), nb, s, tq,
                     [q_spec, kv_spec, kv_spec, b_spec, b_spec], (qa, ka, va, bd, bp), "attn_band")

    u = _strict_upper(STICK_KS)
    o_b = _pair_call(functools.partial(_flash_stick_kernel, tq=tq, ks=STICK_KS), nb, s, tq,
                     [q_spec, kv_spec, kv_spec, pl.BlockSpec(u.shape, lambda b, p, i: (0, 0))],
                     (qb, kb, vb, u), "attn_stick")

    n_blk = s // tq
    fk_t = jnp.cumsum(misc_t[:, 0:N_HEADS, :], axis=2)
    c_cum = jnp.swapaxes(fk_t, 1, 2)
    fk = fk_t.reshape(nb, N_HEADS, n_blk, tq)
    kmax = jnp.full((nb * N_HEADS,), HEAD_DIM ** 0.5 * BOUND_MARGIN, F32) * jnp.max(jnp.abs(k_gain))
    decay = lax.cummax(jnp.max(-fk, axis=-1), axis=2) * LOG2E
    smem = pl.BlockSpec(memory_space=pltpu.SMEM)
    fq_spec = pl.BlockSpec((1, tq, N_HEADS), lambda b, p, i: (b, i, 0))
    fk_spec = pl.BlockSpec((1, 2, n_blk, tq), lambda b, p, i: (b, p, 0, 0))
    o_c = _pair_call(functools.partial(_flash_forget_kernel, tq=tq, n_blk=n_blk), nb, s, tq,
                     [smem, smem, q_spec, kv_spec, kv_spec, fq_spec, fk_spec],
                     (kmax, decay.reshape(-1), qc, kc, vc, c_cum, fk), "attn_forget")

    tqd = MLA_TQ if s % MLA_TQ == 0 else tq
    qd_spec = pl.BlockSpec((1, tqd, LANE), lambda b, h, i: (b, i, h))
    kvd_spec = pl.BlockSpec((1, s, LANE), lambda b, h, i: (b, 0, h))
    o_d = _head_call(functools.partial(_flash_mla_kernel, tq=tqd, tk=ATTN_TK), nb, s, tqd,
                     [qd_spec, kvd_spec, kvd_spec], (qd, kd, vd), "attn_mla")
    return o_a, o_b, o_c, o_d


def _two_block_softmax(s1, s2, v1_t, v2):
    m = jnp.maximum(jnp.max(s1, axis=1, keepdims=True), jnp.max(s2, axis=1, keepdims=True))
    p1 = jnp.exp2(s1 - m)
    p2 = jnp.exp2(s2 - m)
    l = jnp.sum(p1, axis=1, keepdims=True) + jnp.sum(p2, axis=1, keepdims=True)
    return (_nt_dot(p1.astype(BF16), v1_t) + _dot(p2.astype(BF16), v2)) / l


def _sample_kernel(qa_ref, ka_ref, va_ref, qb_ref, kb_ref, vb_ref, qc_ref, kc_ref, vc_ref, qd_ref, kd_ref, vd_ref,
                   cak_ref, cav_ref, cbk_ref, cbv_ref, cck_ref, ccv_ref, cckv_ref, ckr_ref,
                   ba_c_ref, ba_n_ref, fq_ref, fkc_ref, fkn_ref,
                   u_ref, un_ref, wukt_ref, wuv_ref, gk_ref,
                   oa_ref, ob_ref, oc_ref, od_ref, *, t, past, ks):
    row = lax.broadcasted_iota(jnp.int32, (t, t), 0)
    col = lax.broadcasted_iota(jnp.int32, (t, t), 1)
    causal_bias = jnp.where(col <= row, 0.0, NEG_INF)
    chunk_bias = jnp.where((past + col) // CHUNK <= (past + row) // CHUNK, 0.0, NEG_INF)
    strict = col < row

    ckv_c = cckv_ref[0].astype(BF16)
    kr_t = ckr_ref[0].astype(BF16)
    pad_t = jnp.zeros((LANE - MLA_NOPE - MLA_ROPE, past), BF16)

    for h in range(N_HEADS):
        hs = slice(LANE * h, LANE * h + LANE)
        ps = slice(LANE * (h // 2), LANE * (h // 2) + LANE)

        q = qa_ref[0][:, hs]
        s1 = _dot(q, cak_ref[0, ps, :].astype(BF16)) + ba_c_ref[h]
        s2 = _nt_dot(q, ka_ref[0][:, ps]) + ba_n_ref[h]
        oa_ref[0, :, hs] = _two_block_softmax(s1, s2, cav_ref[0, ps, :].astype(BF16), va_ref[0][:, ps])

        q = qb_ref[0][:, hs]
        c, w = _stick_weights(_nt_dot(q, kb_ref[0][:, ps]), jnp.zeros((t, 1), F32), un_ref[...], strict)
        acc = _dot(w.astype(BF16), vb_ref[0][:, ps])

        def b_live(state):
            return jnp.logical_and(state[0] >= 0, state[1] > DEAD_LOG2)

        def b_older(state, q=q, ps=ps):
            jb, _, c, acc = state
            off = pl.multiple_of(jb * ks, ks)
            kb_t = cbk_ref[0, ps, pl.ds(off, ks)].astype(BF16)
            vb_t = cbv_ref[0, ps, pl.ds(off, ks)].astype(BF16)
            c, w = _stick_weights(_dot(q, kb_t), c, u_ref[...], None)
            return jb - 1, jnp.max(c), c, acc + _nt_dot(w.astype(BF16), vb_t)

        _, _, _, acc = lax.while_loop(b_live, b_older, (past // ks - 1, jnp.max(c), c, acc))
        ob_ref[0, :, hs] = acc

        q = qc_ref[0][:, hs]
        fq = fq_ref[0][:, h:h + 1] * LOG2E
        s1 = _dot(q, cck_ref[0, ps, :].astype(BF16)) + (fq - fkc_ref[0, h:h + 1, :] * LOG2E)
        s2 = _nt_dot(q, kc_ref[0][:, ps]) + (fq - fkn_ref[0, h:h + 1, :] * LOG2E) + causal_bias
        oc_ref[0, :, hs] = _two_block_softmax(s1, s2, ccv_ref[0, ps, :].astype(BF16), vc_ref[0][:, ps])

        q = qd_ref[0][:, hs]
        kn_t = _nt_dot(wukt_ref[MLA_NOPE * h: MLA_NOPE * (h + 1), :], ckv_c)
        ss = jnp.sum(kn_t * kn_t, axis=0, keepdims=True)
        kn_t = kn_t * lax.rsqrt(ss * (1.0 / MLA_NOPE) + NORM_EPS) * gk_ref[...]
        k_t = jnp.concatenate([kn_t.astype(BF16), kr_t, pad_t], axis=0)
        s1 = _dot(q, k_t)
        s2 = _nt_dot(q, kd_ref[0][:, hs]) + chunk_bias
        v_c = _dot(ckv_c, wuv_ref[:, hs]).astype(BF16)
        m = jnp.maximum(jnp.max(s1, axis=1, keepdims=True), jnp.max(s2, axis=1, keepdims=True))
        p1 = jnp.exp2(s1 - m)
        p2 = jnp.exp2(s2 - m)
        l = jnp.sum(p1, axis=1, keepdims=True) + jnp.sum(p2, axis=1, keepdims=True)
        od_ref[0, :, hs] = (_dot(p1.astype(BF16), v_c) + _dot(p2.astype(BF16), vd_ref[0][:, hs])) / l


def _channel_major(cache):
    depth, nb, rows = cache.shape[:3]
    return jnp.transpose(cache, (0, 1, 3, 4, 2)).reshape(depth, nb, GROUP_W, rows)


def _sample_attention(pr, caches, layer, pw, w_uk, k_gain, rel_bias, nb, t):
    a_k, a_v, b_k, b_v, c_k, c_v, c_lf, d_ckv, d_kr = caches
    wuv = pw[4]
    past = b_k.shape[2]
    win = a_k.shape[2]
    ks = STICK_KS
    assert past % ks == 0
    r3 = lambda a: a.reshape(nb, t, a.shape[-1])
    news = [r3(a) for a in pr[8:20]]

    qpos = past + np.arange(t)
    kpos = past - win + np.arange(win + t)
    qc, kc = qpos // CHUNK, kpos // CHUNK
    ok = (kc[None, :] <= qc[:, None]) & (kc[None, :] >= qc[:, None] - A_LEFT_CHUNKS)
    ba = _rel_bias_tile(rel_bias, t, win + t, win, ok)
    ba_c, ba_n = ba[:, :, :win], ba[:, :, win:]

    clf_new = jnp.swapaxes(r3(pr[7])[..., 0:N_HEADS], 1, 2)
    lf_c = jnp.swapaxes(c_lf[layer], 1, 2).astype(F32)
    fk = jnp.cumsum(jnp.concatenate([lf_c, clf_new], axis=2), axis=2)
    fk_c, fk_n = fk[:, :, :past], fk[:, :, past:]
    fq = jnp.swapaxes(fk_n, 1, 2)

    wukt = w_uk.T.astype(BF16)
    gk = k_gain.reshape(MLA_NOPE, 1)
    u, un = _strict_upper(ks), _strict_upper(t)

    per_b = lambda shape: pl.BlockSpec((1,) + shape, lambda b: (b,) + (0,) * len(shape))
    per_lb = lambda shape: pl.BlockSpec((None, 1) + shape, lambda b: (layer, b) + (0,) * len(shape))
    const = lambda a: pl.BlockSpec(a.shape, lambda b: (0,) * a.ndim)
    cache_ops = [_channel_major(c) for c in (a_k, a_v, b_k, b_v, c_k, c_v)] + [d_ckv, jnp.swapaxes(d_kr, 2, 3)]
    operands = news + cache_ops + [ba_c, ba_n, fq, fk_c, fk_n, u, un, wukt, wuv, gk]
    in_specs = [per_b(a.shape[1:]) for a in news] + [per_lb(a.shape[2:]) for a in cache_ops] + \
               [const(ba_c), const(ba_n)] + [per_b(a.shape[1:]) for a in (fq, fk_c, fk_n)] + \
               [const(a) for a in (u, un, wukt, wuv, gk)]
    out_shape = [jax.ShapeDtypeStruct((nb, t, N_HEADS * LANE), F32)] * 4
    out_specs = [per_b((t, N_HEADS * LANE))] * 4
    return pl.pallas_call(
        functools.partial(_sample_kernel, t=t, past=past, ks=ks), grid=(nb,),
        in_specs=in_specs, out_specs=out_specs, out_shape=out_shape,
        compiler_params=_cparams(1), name="attn_sample",
    )(*operands)


def _out_kernel(oa_ref, ob_ref, oc_ref, od_ref, h_ref, gg_ref, wo_ref, nf_ref,
                wr1_ref, wr2_ref, wr3_ref, br_ref, h1_ref, hn_ref, lg_ref):
    rows = h_ref.shape[0]
    low = lax.broadcasted_iota(jnp.int32, (rows, LANE), 1) < HEAD_DIM
    h1 = h_ref[...]
    for g, o_ref in enumerate((oa_ref, ob_ref, oc_ref, od_ref)):
        p0 = jnp.where(low, o_ref[:, 0:LANE], o_ref[:, LANE:2 * LANE])
        p1 = jnp.where(low, o_ref[:, 2 * LANE:3 * LANE], o_ref[:, 3 * LANE:4 * LANE])
        og = jnp.concatenate([p0, p1], axis=1)
        ms = jnp.mean(og * og, axis=-1, keepdims=True)
        y = (og * lax.rsqrt(ms + NORM_EPS) * gg_ref[g:g + 1, :]).astype(BF16)
        h1 = h1 + _dot(y, wo_ref[GROUP_W * g: GROUP_W * (g + 1), :])
    h1_ref[...] = h1
    ms = jnp.mean(h1 * h1, axis=-1, keepdims=True)
    hn = h1 * lax.rsqrt(ms + NORM_EPS) * nf_ref[...]
    hn_ref[...] = hn.astype(BF16)
    a1, a2, a3 = _split3(hn)
    w1, w2, w3 = wr1_ref[...], wr2_ref[...], wr3_ref[...]
    lg = _dot(a1, w1) + _dot(a1, w2) + _dot(a2, w1) + _dot(a1, w3) + _dot(a2, w2) + _dot(a3, w1)
    lg_ref[...] = lg + br_ref[...]


def _out_project(outs, h, group_gain, w_out_b, norm_ffn, wr_parts, br_pad):
    t, d = h.shape
    ts = DENSE_ROWS
    assert t % ts == 0
    row = lambda i: (i, 0)
    full = lambda i: (0, 0)
    o_spec = pl.BlockSpec((ts, N_HEADS * LANE), row)
    in_specs = [o_spec] * 4 + [pl.BlockSpec((ts, d), row), pl.BlockSpec(group_gain.shape, full),
                               pl.BlockSpec(w_out_b.shape, full), pl.BlockSpec((1, d), full)] + \
               [pl.BlockSpec((d, LANE), full)] * 3 + [pl.BlockSpec((1, LANE), full)]
    out_shape = [jax.ShapeDtypeStruct((t, d), F32), jax.ShapeDtypeStruct((t, d), BF16),
                 jax.ShapeDtypeStruct((t, LANE), F32)]
    out_specs = [pl.BlockSpec((ts, d), row), pl.BlockSpec((ts, d), row), pl.BlockSpec((ts, LANE), row)]
    return pl.pallas_call(
        _out_kernel, grid=(t // ts,), in_specs=in_specs, out_specs=out_specs, out_shape=out_shape,
        compiler_params=_cparams(1), name="out_proj",
    )(*outs, h, group_gain, w_out_b, norm_ffn.reshape(1, d), *wr_parts, br_pad)


def _expert_kernel(be_ref, fe_ref, nu_ref, x_ref, wu_ref, bu_ref, wd_ref, bd_ref, *rest):
    y_ref, wub_ref, wdb_ref = rest[-3:]
    i = pl.program_id(0)
    used = i < nu_ref[0]

    @pl.when(jnp.logical_and(used, fe_ref[i] == 1))
    def _():
        wub_ref[...] = wu_ref[0, 0].astype(BF16)
        wdb_ref[...] = wd_ref[0, 0].astype(BF16)

    @pl.when(used)
    def _():
        u = _dot(x_ref[...], wub_ref[...]) + bu_ref[0, 0]
        glu = jnp.minimum(u[:, :D_FF], SWIGLU_LIMIT)
        lin = jnp.clip(u[:, D_FF:], -SWIGLU_LIMIT, SWIGLU_LIMIT)
        act = glu * jax.nn.sigmoid(SWIGLU_ALPHA * glu) * (lin + 1.0)
        y_ref[...] = (_dot(act.astype(BF16), wdb_ref[...]) + bd_ref[0, 0]).astype(y_ref.dtype)


def _expert_ffn(y_prev, blk0, n_blk_all, x_rows, blk_exp, blk_first, n_used, layer, w_up, b_up, w_down, b_down):
    rows, d = x_rows.shape
    n_blk = rows // MOE_BLOCK
    last = lambda i, nu: jnp.maximum(jnp.minimum(i, nu[0] - 1), 0)
    w_idx = lambda i, be, fe, nu: (layer, be[last(i, nu)], 0, 0)
    in_specs = [pl.BlockSpec((MOE_BLOCK, d), lambda i, be, fe, nu: (last(i, nu), 0)),
                pl.BlockSpec((1, 1, d, 2 * D_FF), w_idx), pl.BlockSpec((1, 1, 1, 2 * D_FF), w_idx),
                pl.BlockSpec((1, 1, D_FF, d), w_idx), pl.BlockSpec((1, 1, 1, d), w_idx)]
    depth = w_up.shape[0]
    operands = [blk_exp, blk_first, n_used, x_rows, w_up, b_up.reshape(depth, N_EXP, 1, 2 * D_FF),
                w_down, b_down.reshape(depth, N_EXP, 1, d)]
    aliases = {}
    if y_prev is not None:
        in_specs.append(pl.BlockSpec(memory_space=pl.ANY))
        aliases = {len(operands): 0}
        operands.append(y_prev)
    grid_spec = pltpu.PrefetchScalarGridSpec(
        num_scalar_prefetch=3, grid=(n_blk,), in_specs=in_specs,
        out_specs=pl.BlockSpec((MOE_BLOCK, d), lambda i, be, fe, nu: (blk0 + last(i, nu), 0)),
        scratch_shapes=[pltpu.VMEM((d, 2 * D_FF), BF16), pltpu.VMEM((D_FF, d), BF16)])
    return pl.pallas_call(
        _expert_kernel, grid_spec=grid_spec,
        out_shape=jax.ShapeDtypeStruct((n_blk_all * MOE_BLOCK, d), BF16), input_output_aliases=aliases,
        compiler_params=_cparams(1), name="expert_ffn",
    )(*operands)


def _moe(hn, logits, layer, w_up, b_up, w_down, b_down):
    n_tok, d = hn.shape
    top_v, top_i = lax.top_k(logits, TOP_K)
    gates = jax.nn.softmax(top_v, axis=-1)
    n = n_tok * TOP_K
    e_flat = top_i.reshape(-1)
    onehot = (e_flat[:, None] == jnp.arange(N_EXP)[None, :]).astype(jnp.int32)
    rank = jnp.sum((jnp.cumsum(onehot, axis=0) - onehot) * onehot, axis=1)
    counts = jnp.sum(onehot, axis=0)
    padded = (counts + MOE_BLOCK - 1) // MOE_BLOCK * MOE_BLOCK
    p_end = jnp.cumsum(padded)
    p_start = p_end - padded
    dest = p_start[e_flat] + rank
    n_blk = -(-n // MOE_BLOCK) + N_EXP
    rows = n_blk * MOE_BLOCK
    row_tok = jnp.zeros((rows,), jnp.int32).at[dest].set(jnp.arange(n, dtype=jnp.int32) // TOP_K,
                                                         unique_indices=True, mode='promise_in_bounds')
    blk_start = jnp.arange(n_blk, dtype=jnp.int32) * MOE_BLOCK
    blk_exp = jnp.sum((p_end[None, :] <= blk_start[:, None]).astype(jnp.int32), axis=1)
    blk_exp = jnp.minimum(blk_exp, N_EXP - 1)
    n_used = (p_end[-1] // MOE_BLOCK).astype(jnp.int32)
    n_grp = MOE_GROUPS if n_blk % MOE_GROUPS == 0 else 1
    nbg = n_blk // n_grp
    y = None
    for g in range(n_grp):
        be = blk_exp[g * nbg:(g + 1) * nbg]
        fe = jnp.concatenate([jnp.ones((1,), jnp.int32), (be[1:] != be[:-1]).astype(jnp.int32)])
        x_rows = hn.at[row_tok[g * nbg * MOE_BLOCK:(g + 1) * nbg * MOE_BLOCK]].get(mode='promise_in_bounds')
        y = _expert_ffn(y, g * nbg, n_blk, x_rows, be, fe, (n_used - g * nbg).reshape(1), layer,
                        w_up, b_up, w_down, b_down)
    picked = y.at[dest.reshape(n_tok, TOP_K).T].get(mode='promise_in_bounds')
    return picked, gates


def _ple_kernel(h_ref, y_ref, g_ref, p_ref, np_ref, wg_ref, wp_ref, o_ref):
    h2 = h_ref[...]
    g = g_ref[...]
    for j in range(TOP_K):
        h2 = h2 + y_ref[j].astype(F32) * g[:, j:j + 1]
    ms = jnp.mean(h2 * h2, axis=-1, keepdims=True)
    hn = (h2 * lax.rsqrt(ms + NORM_EPS) * np_ref[...]).astype(BF16)
    gate = jax.nn.sigmoid(_dot(hn, wg_ref[...]))
    o_ref[...] = h2 + gate * _dot(p_ref[...].astype(BF16), wp_ref[...])


def _ple(h1, picked, gates, row0, p, layer, norm_ple, wg_b, wp_b):
    t, d = h1.shape
    ts = PLE_ROWS
    assert t % ts == 0 and row0 % ts == 0
    off = row0 // ts
    row = lambda i: (i, 0)
    full = lambda i: (0, 0)
    return pl.pallas_call(
        _ple_kernel, grid=(t // ts,),
        in_specs=[pl.BlockSpec((ts, d), row), pl.BlockSpec((TOP_K, ts, d), lambda i: (0, i + off, 0)),
                  pl.BlockSpec((ts, TOP_K), lambda i: (i + off, 0)),
                  pl.BlockSpec((None, ts, p.shape[2]), lambda i: (layer, i, 0)),
                  pl.BlockSpec((1, d), full), pl.BlockSpec(wg_b.shape, full), pl.BlockSpec(wp_b.shape, full)],
        out_specs=pl.BlockSpec((ts, d), row), out_shape=jax.ShapeDtypeStruct((t, d), F32),
        compiler_params=_cparams(1), name="ple",
    )(h1, picked, gates, p, norm_ple.reshape(1, d), wg_b, wp_b)


def kernel(x_prompt, x_sample, p_prompt, p_sample, cache_a_k, cache_a_v, cache_b_k, cache_b_v, cache_c_k, cache_c_v, cache_c_logf, cache_d_ckv, cache_d_krope, norm_mix, w_in, b_forget, qk_gain, rope_gain, kv_gain, w_uk, w_uv, rel_bias, group_gain, w_out, norm_ffn, w_router, b_router, w_up, b_up, w_down, b_down, norm_ple, w_ple_gate, w_ple_proj):
    nb, s, d = x_prompt.shape
    nd, t, _ = x_sample.shape
    depth = w_in.shape[0]
    past = cache_b_k.shape[2]
    assert PROJ_ROWS % t == 0 and s % PROJ_ROWS == 0

    tab_p = _rope_tables(jnp.arange(s))
    tab_s = _rope_tables(past + jnp.arange(PROJ_ROWS) % t)
    hp = x_prompt.reshape(nb * s, d)
    hs = x_sample.reshape(nd * t, d)
    st_p, st_s = [], []
    keep = min(A_WINDOW, s)
    layered_state = None
    for i in range(depth):
        pw = _proj_weights(w_in[i], b_forget[i], qk_gain[i], rope_gain[i], kv_gain[i], w_uk[i], w_uv[i])
        pr_p = _project(hp, norm_mix[i], pw, tab_p, s // PROJ_ROWS, (depth, i, nb, layered_state))
        layered_state = [pr_p[slot] for slot in STATE_SLOTS]
        pr_s = _project(hs, norm_mix[i], pw, tab_s, 1)
        outs_p = _prompt_attention(pr_p, pr_p[7][i], nb, s, rel_bias[i], qk_gain[i, 3])
        caches = (cache_a_k, cache_a_v, cache_b_k, cache_b_v, cache_c_k, cache_c_v,
                  cache_c_logf, cache_d_ckv, cache_d_krope)
        outs_s = _sample_attention(pr_s, caches, i, pw, w_uk[i], qk_gain[i, 5], rel_bias[i], nd, t)

        def heads(a, n, rows):
            return a.reshape(n, rows, N_HEADS, HEAD_DIM)

        p3 = lambda a: a.reshape(nb, s, a.shape[-1])
        s3 = lambda a: a.reshape(nd, t, a.shape[-1])
        st_p.append((heads(p3(pr_p[0])[:, s - keep:], nb, keep), heads(p3(pr_p[1])[:, s - keep:], nb, keep)))
        ka_all = jnp.concatenate([cache_a_k[i], heads(pr_s[0], nd, t)], axis=1)[:, t:]
        va_all = jnp.concatenate([cache_a_v[i], heads(pr_s[1], nd, t)], axis=1)[:, t:]
        st_s.append((ka_all, va_all,
                     heads(pr_s[2], nd, t), heads(pr_s[3], nd, t), heads(pr_s[4], nd, t), heads(pr_s[5], nd, t),
                     s3(pr_s[7])[..., 0:N_HEADS], s3(pr_s[6]), s3(pr_s[7])[..., ROPE_LO:ROPE_LO + MLA_ROPE]))

        w_out_b = w_out[i].astype(BF16)
        wr_pad = jnp.zeros((d, LANE), F32).at[:, 0:N_EXP].set(w_router[i])
        wr_parts = _split3(wr_pad)
        br_pad = jnp.zeros((1, LANE), F32).at[0, 0:N_EXP].set(b_router[i])
        flat4 = lambda o: o.reshape(-1, N_HEADS * LANE)
        h1_p, hn_p, lg_p = _out_project([flat4(o) for o in outs_p], hp, group_gain[i], w_out_b, norm_ffn[i],
                                        wr_parts, br_pad)
        h1_s, hn_s, lg_s = _out_project([flat4(o) for o in outs_s], hs, group_gain[i], w_out_b, norm_ffn[i],
                                        wr_parts, br_pad)

        hn_all = jnp.concatenate([hn_p, hn_s], axis=0)
        lg_all = jnp.concatenate([lg_p, lg_s], axis=0)[:, 0:N_EXP]
        picked, gates = _moe(hn_all, lg_all, i, w_up, b_up, w_down, b_down)

        wg_b = w_ple_gate[i].astype(BF16)
        wp_b = w_ple_proj[i].astype(BF16)
        hp = _ple(h1_p, picked, gates, 0, p_prompt.reshape(depth, nb * s, -1), i, norm_ple[i], wg_b, wp_b)
        hs = _ple(h1_s, picked, gates, nb * s, p_sample.reshape(depth, nd * t, -1), i, norm_ple[i], wg_b, wp_b)

    bk_t, bv_t, ck_t, cv_t, ckv_all, misc_t = layered_state

    def from_channel_major(a):
        return jnp.transpose(a.reshape(depth, nb, N_HEADS, HEAD_DIM, s), (0, 1, 4, 2, 3))

    state_p = [jnp.stack([st[j] for st in st_p]) for j in range(2)] + \
              [from_channel_major(a) for a in (bk_t, bv_t, ck_t, cv_t)] + \
              [jnp.swapaxes(misc_t[:, :, 0:N_HEADS, :], 2, 3), ckv_all.reshape(depth, nb, s, KV_RANK),
               jnp.swapaxes(misc_t[:, :, ROPE_LO:ROPE_LO + MLA_ROPE, :], 2, 3)]
    state_s = [jnp.stack([st[j] for st in st_s]) for j in range(9)]
    return (hp.reshape(nb, s, d), hs.reshape(nd, t, d), *state_p, *state_s)
```

```python
import functools

import numpy as np
import jax
import jax.numpy as jnp
from jax import lax
from jax.experimental import pallas as pl
from jax.experimental.pallas import tpu as pltpu

F32 = jnp.float32
BF16 = jnp.bfloat16

CHUNK = 64
HEAD_DIM = 64
N_HEADS = 4
GROUP_W = 256
A_LEFT_CHUNKS = 8
A_WINDOW = A_LEFT_CHUNKS * CHUNK
REL_CLIP = 128
MLA_NOPE = 64
MLA_ROPE = 32
KV_RANK = 128
ROPE_BASE = 10000.0
N_EXP = 32
TOP_K = 4
D_FF = 1024
SWIGLU_LIMIT = 7.0
SWIGLU_ALPHA = 1.702
MOE_BLOCK = 512
NORM_EPS = 1e-6
NEG_INF = -1e30
LOG2E = 1.4426950408889634

A_Q = 0
C_F = 2304
D_Q = 2308
D_CKV = D_Q + N_HEADS * (MLA_NOPE + MLA_ROPE)
D_KR = D_CKV + KV_RANK

LANE = 128
SEG_ABC = 0
SEG_DQ = 2304
SEG_CKV = 2816
SEG_MISC = 2944
N_COLS = 3072
ROPE_LO = 64
ROPE_HALF = MLA_ROPE // 2

VMEM_LIMIT = 56 * 1024 * 1024

PROJ_ROWS = 512
ATTN_TQ = 512
ATTN_TK = 1024
MLA_TQ = 2048
STICK_KS = 256
DENSE_ROWS = 512
PLE_ROWS = 256
MOE_GROUPS = 4
SCATTER_CHUNK = 8192

DEAD_LOG2 = -160.0
BOUND_MARGIN = 1.01


def _cparams(n_axes):
    return pltpu.CompilerParams(dimension_semantics=("arbitrary",) * n_axes,
                                vmem_limit_bytes=VMEM_LIMIT)


def _nt_dot(a, b):
    return lax.dot_general(a, b, (((1,), (1,)), ((), ())), preferred_element_type=F32)


def _dot(a, b):
    return jnp.dot(a, b, preferred_element_type=F32)


def _split2(x):
    hi = x.astype(BF16)
    lo = (x - hi.astype(F32)).astype(BF16)
    return hi, lo


def _in_col_map():
    cols = np.full((N_COLS,), -1, np.int64)
    cols[0:2304] = np.arange(2304)
    for h in range(N_HEADS):
        base = D_Q + (MLA_NOPE + MLA_ROPE) * h
        cols[SEG_DQ + LANE * h: SEG_DQ + LANE * h + MLA_NOPE + MLA_ROPE] = base + np.arange(MLA_NOPE + MLA_ROPE)
    cols[SEG_CKV:SEG_CKV + KV_RANK] = D_CKV + np.arange(KV_RANK)
    cols[SEG_MISC:SEG_MISC + N_HEADS] = C_F + np.arange(N_HEADS)
    cols[SEG_MISC + ROPE_LO:SEG_MISC + ROPE_LO + MLA_ROPE] = D_KR + np.arange(MLA_ROPE)
    return cols


def _rope_tables(pos):
    inv = ROPE_BASE ** (-jnp.arange(ROPE_HALF, dtype=F32) / ROPE_HALF)
    ang = pos.astype(F32)[:, None] * inv
    cos, sin = jnp.cos(ang), jnp.sin(ang)
    n = pos.shape[0]
    one = jnp.ones((n, ROPE_LO), F32)
    z16 = jnp.zeros((n, ROPE_HALF), F32)
    z64 = jnp.zeros((n, ROPE_LO), F32)
    z32 = jnp.zeros((n, LANE - ROPE_LO - MLA_ROPE), F32)
    cos_t = jnp.concatenate([one, cos, cos, z32 + 1.0], axis=1)
    sin_a = jnp.concatenate([z64, -sin, z16, z32], axis=1)
    sin_b = jnp.concatenate([z64, z16, sin, z32], axis=1)
    return cos_t, sin_a, sin_b


def _rope(y, cos_t, sin_a, sin_b):
    left = pltpu.roll(y, LANE - ROPE_HALF, axis=1)
    right = pltpu.roll(y, ROPE_HALF, axis=1)
    return y * cos_t + left * sin_a + right * sin_b


def _head_norm(x, m_ref, gain):
    hi, lo = _split2(x * x)
    ssq = _dot(hi, m_ref[...]) + _dot(lo, m_ref[...])
    return x * lax.rsqrt(ssq * (1.0 / HEAD_DIM) + NORM_EPS) * gain


def _value_lane0(h):
    return HEAD_DIM * (h % 2)


def _ones_lane(h):
    return HEAD_DIM - _value_lane0(h)


def _store_padded_q(q_ref, q, scale):
    lane = lax.broadcasted_iota(jnp.int32, (q.shape[0], LANE), 1)
    low = lane < HEAD_DIM
    qs = q * scale
    for h in range(N_HEADS):
        pair = qs[:, LANE * (h // 2): LANE * (h // 2) + LANE]
        keep = low if h % 2 == 0 else jnp.logical_not(low)
        q_ref[:, LANE * h: LANE * h + LANE] = jnp.where(keep, pair, 0.0).astype(BF16)


def _proj_kernel(*refs, channel_major, n_alias):
    x_ref, gmix_ref, w_ref, cos_ref, sa_ref, sb_ref, gv_ref, m64_ref, wuk_ref, wuv_ref = refs[:10]
    (ak_ref, av_ref, bk_ref, bv_ref, ck_ref, cv_ref, ckv_ref, misc_ref,
     qa_ref, ka_ref, va_ref, qb_ref, kb_ref, vb_ref, qc_ref, kc_ref, vc_ref,
     qd_ref, kd_ref, vd_ref) = refs[10 + n_alias:]

    def put_state(ref, val):
        ref[...] = val.T if channel_major else val

    x = x_ref[...]
    ms = jnp.mean(x * x, axis=-1, keepdims=True)
    hn = (x * lax.rsqrt(ms + NORM_EPS) * gmix_ref[...]).astype(BF16)
    sm_scale = (HEAD_DIM ** -0.5) * LOG2E
    rows = x.shape[0]
    lane = lax.broadcasted_iota(jnp.int32, (rows, LANE), 1)
    is_nope = lane < MLA_NOPE
    is_rope = jnp.logical_and(lane >= ROPE_LO, lane < ROPE_LO + MLA_ROPE)
    cos_t, sin_a, sin_b = cos_ref[...], sa_ref[...], sb_ref[...]

    z = _dot(hn, w_ref[:, 0:768])
    aq = _head_norm(z[:, 0:256], m64_ref, gv_ref[0:1, :])
    ak = _head_norm(z[:, 256:512], m64_ref, gv_ref[1:2, :])
    av = z[:, 512:768]
    ak_ref[...] = ak
    av_ref[...] = av
    _store_padded_q(qa_ref, aq, sm_scale)
    ka_ref[...] = ak.astype(BF16)
    va_ref[...] = av.astype(BF16)

    z = _dot(hn, w_ref[:, 768:1536])
    put_state(bk_ref, z[:, 256:512])
    put_state(bv_ref, z[:, 512:768])
    _store_padded_q(qb_ref, z[:, 0:256], sm_scale)
    kb_ref[...] = z[:, 256:512].astype(BF16)
    vb_ref[...] = z[:, 512:768].astype(BF16)

    z = _dot(hn, w_ref[:, 1536:2304])
    cq = _head_norm(z[:, 0:256], m64_ref, gv_ref[2:3, :])
    ck = _head_norm(z[:, 256:512], m64_ref, gv_ref[3:4, :])
    cv = z[:, 512:768]
    put_state(ck_ref, ck)
    put_state(cv_ref, cv)
    _store_padded_q(qc_ref, cq, sm_scale)
    kc_ref[...] = ck.astype(BF16)
    vc_ref[...] = cv.astype(BF16)

    z = _dot(hn, w_ref[:, SEG_CKV:N_COLS])
    zc = z[:, 0:KV_RANK]
    ckv = zc * lax.rsqrt(jnp.mean(zc * zc, axis=-1, keepdims=True) + NORM_EPS) * gv_ref[7:8, 0:LANE]
    ckv_ref[...] = ckv
    zm = z[:, KV_RANK:2 * KV_RANK]
    ssr = jnp.sum(jnp.where(is_rope, zm * zm, 0.0), axis=-1, keepdims=True)
    kr = zm * lax.rsqrt(ssr * (1.0 / MLA_ROPE) + NORM_EPS) * gv_ref[6:7, 0:LANE]
    kr = _rope(kr, cos_t, sin_a, sin_b)
    zf = zm + gv_ref[8:9, 0:LANE]
    clf = jnp.minimum(zf, 0.0) - jnp.log1p(jnp.exp(-jnp.abs(zf)))
    put_state(misc_ref, jnp.where(lane < N_HEADS, clf, kr))

    ckv_b = ckv.astype(BF16)
    kn = _dot(ckv_b, wuk_ref[...])
    for h in range(N_HEADS):
        hs = slice(LANE * h, LANE * h + LANE)
        ones = (lane == _ones_lane(h)).astype(F32)
        vd_ref[:, hs] = (_dot(ckv_b, wuv_ref[:, hs]) + ones).astype(BF16)
    for h in range(N_HEADS):
        xh = kn[:, LANE * h: LANE * h + LANE]
        ss = jnp.sum(xh * xh, axis=-1, keepdims=True)
        yh = xh * lax.rsqrt(ss * (1.0 / MLA_NOPE) + NORM_EPS) * gv_ref[5:6, 0:LANE]
        kd_ref[:, LANE * h: LANE * h + LANE] = (yh + kr).astype(BF16)

    z = _dot(hn, w_ref[:, SEG_DQ:SEG_CKV])
    d_scale = ((MLA_NOPE + MLA_ROPE) ** -0.5) * LOG2E
    for h in range(N_HEADS):
        xh = z[:, LANE * h: LANE * h + LANE]
        x2 = xh * xh
        ssn = jnp.sum(jnp.where(is_nope, x2, 0.0), axis=-1, keepdims=True)
        ssr = jnp.sum(jnp.where(is_rope, x2, 0.0), axis=-1, keepdims=True)
        rn = lax.rsqrt(ssn * (1.0 / MLA_NOPE) + NORM_EPS)
        rr = lax.rsqrt(ssr * (1.0 / MLA_ROPE) + NORM_EPS)
        yh = xh * jnp.where(is_nope, rn, rr) * gv_ref[4:5, 0:LANE]
        yh = _rope(yh, cos_t, sin_a, sin_b)
        qd_ref[:, LANE * h: LANE * h + LANE] = (yh * d_scale).astype(BF16)


def _proj_weights(w_in, b_forget, qk_gain, rope_gain, kv_gain, w_uk, w_uv):
    cols = _in_col_map()
    valid = jnp.asarray(cols >= 0)
    w = jnp.where(valid[None, :], w_in[:, np.maximum(cols, 0)], 0.0).astype(BF16)

    def tile4(g):
        return jnp.tile(g, N_HEADS)

    gv = jnp.zeros((16, GROUP_W), F32)
    gv = gv.at[0].set(tile4(qk_gain[0])).at[1].set(tile4(qk_gain[1]))
    gv = gv.at[2].set(tile4(qk_gain[2])).at[3].set(tile4(qk_gain[3]))
    gv = gv.at[4, 0:MLA_NOPE].set(qk_gain[4]).at[4, ROPE_LO:ROPE_LO + MLA_ROPE].set(rope_gain[0])
    gv = gv.at[5, 0:MLA_NOPE].set(qk_gain[5])
    gv = gv.at[6, ROPE_LO:ROPE_LO + MLA_ROPE].set(rope_gain[1])
    gv = gv.at[7, 0:KV_RANK].set(kv_gain)
    gv = gv.at[8, 0:N_HEADS].set(b_forget)
    head = np.arange(GROUP_W) // HEAD_DIM
    m64 = jnp.asarray((head[:, None] == head[None, :]).astype(np.float32), BF16)
    wuk = jnp.zeros((KV_RANK, N_HEADS * LANE), F32)
    for h in range(N_HEADS):
        wuk = wuk.at[:, LANE * h: LANE * h + MLA_NOPE].set(w_uk[:, MLA_NOPE * h: MLA_NOPE * (h + 1)])
    wuv = jnp.zeros((KV_RANK, N_HEADS * LANE), F32)
    for h in range(N_HEADS):
        lo = LANE * h + _value_lane0(h)
        wuv = wuv.at[:, lo: lo + HEAD_DIM].set(w_uv[:, HEAD_DIM * h: HEAD_DIM * (h + 1)])
    return w, gv, m64, wuk.astype(BF16), wuv.astype(BF16)


STATE_SLOTS = (2, 3, 4, 5, 6, 7)


def _project(x, gmix, pw, tables, n_tab_blocks, layered=None):
    w, gv, m64, wuk, wuv = pw
    t, d = x.shape
    ts = PROJ_ROWS
    assert t % ts == 0
    row = lambda i: (i, 0)
    full = lambda i: (0, 0)
    tab = lambda i: (i % n_tab_blocks, 0)
    f32_w = [GROUP_W] * 6 + [KV_RANK, LANE]
    bf_w = [512, 256, 256, 512, 256, 256, 512, 256, 256, 512, 512, 512]
    out_shape = [jax.ShapeDtypeStruct((t, c), F32) for c in f32_w] + \
                [jax.ShapeDtypeStruct((t, c), BF16) for c in bf_w]
    out_specs = [pl.BlockSpec((ts, c), row) for c in f32_w + bf_w]
    in_specs = [pl.BlockSpec((ts, d), row), pl.BlockSpec((1, d), full), pl.BlockSpec((d, N_COLS), full),
                pl.BlockSpec((ts, LANE), tab), pl.BlockSpec((ts, LANE), tab), pl.BlockSpec((ts, LANE), tab),
                pl.BlockSpec(gv.shape, full), pl.BlockSpec(m64.shape, full),
                pl.BlockSpec(wuk.shape, full), pl.BlockSpec(wuv.shape, full)]
    operands = [x, gmix.reshape(1, d), w, *tables, gv, m64, wuk, wuv]
    aliases = {}
    if layered is not None:
        depth, layer, nb, earlier = layered
        s = t // nb
        nt = s // ts
        for slot in STATE_SLOTS:
            c = f32_w[slot]
            if slot == 6:
                out_shape[slot] = jax.ShapeDtypeStruct((depth, t, c), F32)
                out_specs[slot] = pl.BlockSpec((None, ts, c), lambda i: (layer, i, 0))
            else:
                out_shape[slot] = jax.ShapeDtypeStruct((depth, nb, c, s), F32)
                out_specs[slot] = pl.BlockSpec((None, None, c, ts), lambda i: (layer, i // nt, 0, i % nt))
        if earlier is not None:
            for k, slot in enumerate(STATE_SLOTS):
                aliases[len(operands)] = slot
                in_specs.append(pl.BlockSpec(memory_space=pl.ANY))
                operands.append(earlier[k])
    kern = functools.partial(_proj_kernel, channel_major=layered is not None, n_alias=len(aliases))
    return pl.pallas_call(
        kern, grid=(t // ts,), in_specs=in_specs, out_specs=out_specs, out_shape=out_shape,
        input_output_aliases=aliases, compiler_params=_cparams(1), name="proj",
    )(*operands)


def _softmax_step(q, kb, vb, carry, bias=None, ok=None):
    m, l, acc = carry
    s = _nt_dot(q, kb)
    if bias is not None:
        s = s + bias
    if ok is not None:
        s = jnp.where(ok, s, NEG_INF)
    m_new = jnp.maximum(m, jnp.max(s, axis=1, keepdims=True))
    alpha = jnp.exp2(m - m_new)
    p = jnp.exp2(s - m_new)
    l = alpha * l + jnp.sum(p, axis=1, keepdims=True)
    acc = alpha * acc + _dot(p.astype(BF16), vb)
    return m_new, l, acc


def _softmax_init(tq):
    return (jnp.full((tq, 1), NEG_INF, F32), jnp.zeros((tq, 1), F32), jnp.zeros((tq, LANE), F32))


def _pick_lane(block, h):
    col = lax.broadcasted_iota(jnp.int32, block.shape, 1)
    return jnp.sum(jnp.where(col == h, block, 0.0), axis=1, keepdims=True)


def _stack_pair(q_ref):
    return jnp.concatenate([q_ref[0, :, 0:LANE], q_ref[0, :, LANE:2 * LANE]], axis=0)


def _store_pair(o_ref, o, tq):
    o_ref[0, :, 0:LANE] = o[:tq]
    o_ref[0, :, LANE:2 * LANE] = o[tq:]


def _flash_forget_kernel(kmax_ref, decay_ref, q_ref, k_ref, v_ref, fq_ref, fk_ref, o_ref, *, tq, n_blk):
    b = pl.program_id(0)
    pair = pl.program_id(1)
    qi = pl.program_id(2)
    q = _stack_pair(q_ref)
    fq = jnp.concatenate([_pick_lane(fq_ref[0], 2 * pair), _pick_lane(fq_ref[0], 2 * pair + 1)], axis=0) * LOG2E

    def step(j, carry, diag):
        off = pl.multiple_of(j * tq, tq)
        kb = k_ref[0, pl.ds(off, tq), :]
        vb = v_ref[0, pl.ds(off, tq), :]
        fk = jnp.concatenate([jnp.broadcast_to(fk_ref[0, 0, pl.ds(j, 1), :], (tq, tq)),
                              jnp.broadcast_to(fk_ref[0, 1, pl.ds(j, 1), :], (tq, tq))], axis=0)
        ok = None
        if diag:
            row = lax.broadcasted_iota(jnp.int32, (2 * tq, tq), 0) % tq
            col = lax.broadcasted_iota(jnp.int32, (2 * tq, tq), 1)
            ok = col <= row
        return _softmax_step(q, kb, vb, carry, fq - fk * LOG2E, ok)

    carry = step(qi, _softmax_init(2 * tq), True)
    qf = q.astype(F32)
    q_norm = jnp.sqrt(jnp.sum(qf * qf, axis=1, keepdims=True))
    head0 = b * N_HEADS + 2 * pair
    room = q_norm * jnp.where(lax.broadcasted_iota(jnp.int32, q_norm.shape, 0) < tq,
                              kmax_ref[head0], kmax_ref[head0 + 1]) + fq - carry[0]
    slack0 = jnp.max(room[:tq]) - DEAD_LOG2
    slack1 = jnp.max(room[tq:]) - DEAD_LOG2

    def live(state):
        j = jnp.maximum(state[0], 0)
        alive = jnp.logical_or(slack0 + decay_ref[head0 * n_blk + j] >= 0.0,
                               slack1 + decay_ref[(head0 + 1) * n_blk + j] >= 0.0)
        return jnp.logical_and(state[0] >= 0, alive)

    def older(state):
        j = state[0]
        return (j - 1,) + step(j, state[1:], False)

    _, m, l, acc = lax.while_loop(live, older, (qi - 1,) + carry)
    _store_pair(o_ref, acc / l, tq)


def _flash_mla_kernel(q_ref, k_ref, v_ref, o_ref, *, tq, tk):
    h = pl.program_id(1)
    qi = pl.program_id(2)
    q = q_ref[0]
    n_sub = tq // tk

    def step(j, carry, row0=None):
        m, acc = carry
        off = pl.multiple_of(j * tk, tk)
        qs = q if row0 is None else q[row0:]
        s = _nt_dot(qs, k_ref[0, pl.ds(off, tk), :])
        if row0 is not None:
            row = lax.broadcasted_iota(jnp.int32, s.shape, 0)
            col = lax.broadcasted_iota(jnp.int32, s.shape, 1)
            s = jnp.where(col // CHUNK <= row // CHUNK, s, NEG_INF)
        m_new = jnp.maximum(m, jnp.max(s, axis=1, keepdims=True))
        p = jnp.exp2(s - m_new)
        acc = jnp.exp2(m - m_new) * acc + _dot(p.astype(BF16), v_ref[0, pl.ds(off, tk), :])
        return m_new, acc

    carry = (jnp.full((tq, 1), NEG_INF, F32), jnp.zeros((tq, LANE), F32))
    m, acc = lax.fori_loop(0, qi * n_sub, lambda j, c: step(j, c), carry)
    for r in range(n_sub):
        row0 = r * tk
        m_r, acc_r = step(qi * n_sub + r, (m[row0:], acc[row0:]), row0)
        m = m_r if r == 0 else jnp.concatenate([m[:row0], m_r], axis=0)
        acc = acc_r if r == 0 else jnp.concatenate([acc[:row0], acc_r], axis=0)
    denom = jnp.where(h % 2 == 0, acc[:, HEAD_DIM:HEAD_DIM + 1], acc[:, 0:1])
    o_ref[0] = acc / denom


def _flash_stick_kernel(q_ref, k_ref, v_ref, u_ref, o_ref, *, tq, ks):
    qi = pl.program_id(2)
    q = _stack_pair(q_ref)
    n_sub = tq // ks

    def step(jb, carry, diag):
        c, acc = carry
        off = pl.multiple_of(jb * ks, ks)
        kb = k_ref[0, pl.ds(off, ks), :]
        vb = v_ref[0, pl.ds(off, ks), :]
        vis = None
        if diag:
            row = lax.broadcasted_iota(jnp.int32, (2 * tq, ks), 0) % tq + qi * tq
            col = lax.broadcasted_iota(jnp.int32, (2 * tq, ks), 1) + jb * ks
            vis = col < row
        c_new, w = _stick_weights(_nt_dot(q, kb), c, u_ref[...], vis)
        return c_new, acc + _dot(w.astype(BF16), vb)

    carry = (jnp.zeros((2 * tq, 1), F32), jnp.zeros((2 * tq, LANE), F32))
    for r in range(n_sub):
        carry = step(qi * n_sub + (n_sub - 1 - r), carry, True)

    def live(state):
        return jnp.logical_and(state[0] >= 0, state[1] > DEAD_LOG2)

    def older(state):
        c, acc = step(state[0], state[2:], False)
        return state[0] - 1, jnp.max(c), c, acc

    _, _, _, acc = lax.while_loop(live, older, (qi * n_sub - 1, jnp.max(carry[0])) + carry)
    _store_pair(o_ref, acc, tq)


def _stick_weights(z, c, u, vis):
    sp = jnp.log(1.0 + jnp.exp2(-jnp.abs(z))) * LOG2E
    log_rest = jnp.minimum(-z, 0.0) - sp
    log_beta = log_rest + z
    if vis is not None:
        log_rest = jnp.where(vis, log_rest, 0.0)
    hi, lo = _split2(log_rest)
    between = _dot(hi, u) + _dot(lo, u)
    w = jnp.exp2(log_beta + between + c)
    if vis is not None:
        w = jnp.where(vis, w, 0.0)
    return c + jnp.sum(log_rest, axis=1, keepdims=True), w


def _band_kernel(q_ref, k_ref, v_ref, bd_ref, bp_ref, o_ref, *, tq):
    qi = pl.program_id(2)
    q = _stack_pair(q_ref)
    off = pl.multiple_of(qi * tq, tq)
    carry = _softmax_step(q, k_ref[0, pl.ds(off, tq), :], v_ref[0, pl.ds(off, tq), :],
                          _softmax_init(2 * tq), bd_ref[...].reshape(2 * tq, tq))
    offp = pl.multiple_of(jnp.maximum(qi - 1, 0) * tq, tq)
    no_prev = jnp.where(qi == 0, NEG_INF, 0.0)
    m, l, acc = _softmax_step(q, k_ref[0, pl.ds(offp, tq), :], v_ref[0, pl.ds(offp, tq), :],
                              carry, bp_ref[...].reshape(2 * tq, tq) + no_prev)
    _store_pair(o_ref, acc / l, tq)


def _pair_specs(s, tq):
    q_spec = pl.BlockSpec((1, tq, 2 * LANE), lambda b, p, i: (b, i, p))
    kv_spec = pl.BlockSpec((1, s, LANE), lambda b, p, i: (b, 0, p))
    return q_spec, kv_spec


def _pair_call(kern, nb, s, tq, in_specs, operands, name):
    return pl.pallas_call(
        kern, grid=(nb, N_HEADS // 2, s // tq), in_specs=in_specs,
        out_specs=pl.BlockSpec((1, tq, 2 * LANE), lambda b, p, i: (b, i, p)),
        out_shape=jax.ShapeDtypeStruct((nb, s, N_HEADS * LANE), F32),
        compiler_params=_cparams(3), name=name,
    )(*operands)


def _head_call(kern, nb, s, tq, in_specs, operands, name):
    return pl.pallas_call(
        kern, grid=(nb, N_HEADS, s // tq), in_specs=in_specs,
        out_specs=pl.BlockSpec((1, tq, LANE), lambda b, h, i: (b, i, h)),
        out_shape=jax.ShapeDtypeStruct((nb, s, N_HEADS * LANE), F32),
        compiler_params=_cparams(3), name=name,
    )(*operands)


def _strict_upper(n):
    idx = np.arange(n)
    return jnp.asarray((idx[:, None] > idx[None, :]).astype(np.float32), BF16)


def _toeplitz(vec, n_rows, n_cols):
    length = n_rows + n_cols - 1
    assert vec.shape[-1] == length
    lead = vec.shape[:-1]
    rev = jnp.concatenate([vec[..., ::-1], jnp.zeros(lead + (1,), vec.dtype)], axis=-1)
    flat = jnp.tile(rev, (1,) * len(lead) + (n_rows,))[..., :n_rows * length]
    return flat.reshape(lead + (n_rows, length))[..., n_rows - 1: n_rows - 1 + n_cols]


def _rel_bias_tile(rel_bias, n_rows, n_cols, rel00, ok):
    d = np.arange(n_rows + n_cols - 1) - (n_cols - 1) + rel00
    vec = rel_bias.astype(F32)[:, np.clip(d, -REL_CLIP, REL_CLIP) + REL_CLIP] * LOG2E
    return jnp.where(jnp.asarray(ok)[None], _toeplitz(vec, n_rows, n_cols), NEG_INF)


def _band_bias_tiles(rel_bias, tq):
    i = np.arange(tq)[:, None]
    j = np.arange(tq)[None, :]
    own = _rel_bias_tile(rel_bias, tq, tq, 0, (j // CHUNK) <= (i // CHUNK))
    prev = _rel_bias_tile(rel_bias, tq, tq, tq, (j // CHUNK) >= (i // CHUNK) + tq // CHUNK - A_LEFT_CHUNKS)
    return own, prev


def _prompt_attention(pr, misc_t, nb, s, rel_bias, k_gain):
    tq = ATTN_TQ
    assert s % tq == 0 and tq == A_WINDOW
    r3 = lambda a: a.reshape(nb, s, a.shape[-1])
    qa, ka, va, qb, kb, vb, qc, kc, vc, qd, kd, vd = [r3(a) for a in pr[8:20]]
    q_spec, kv_spec = _pair_specs(s, tq)

    bd, bp = _band_bias_tiles(rel_bias, tq)
    b_spec = pl.BlockSpec((2, tq, tq), lambda b, p, i: (p, 0, 0))
    o_a = _pair_call(functools.partial(_band_kernel, tq=tq), nb, s, tq,
                     [q_spec, kv_spec, kv_spec, b_spec, b_spec], (qa, ka, va, bd, bp), "attn_band")

    u = _strict_upper(STICK_KS)
    o_b = _pair_call(functools.partial(_flash_stick_kernel, tq=tq, ks=STICK_KS), nb, s, tq,
                     [q_spec, kv_spec, kv_spec, pl.BlockSpec(u.shape, lambda b, p, i: (0, 0))],
                     (qb, kb, vb, u), "attn_stick")

    n_blk = s // tq
    fk_t = jnp.cumsum(misc_t[:, 0:N_HEADS, :], axis=2)
    c_cum = jnp.swapaxes(fk_t, 1, 2)
    fk = fk_t.reshape(nb, N_HEADS, n_blk, tq)
    kmax = jnp.full((nb * N_HEADS,), HEAD_DIM ** 0.5 * BOUND_MARGIN, F32) * jnp.max(jnp.abs(k_gain))
    decay = lax.cummax(jnp.max(-fk, axis=-1), axis=2) * LOG2E
    smem = pl.BlockSpec(memory_space=pltpu.SMEM)
    fq_spec = pl.BlockSpec((1, tq, N_HEADS), lambda b, p, i: (b, i, 0))
    fk_spec = pl.BlockSpec((1, 2, n_blk, tq), lambda b, p, i: (b, p, 0, 0))
    o_c = _pair_call(functools.partial(_flash_forget_kernel, tq=tq, n_blk=n_blk), nb, s, tq,
                     [smem, smem, q_spec, kv_spec, kv_spec, fq_spec, fk_spec],
                     (kmax, decay.reshape(-1), qc, kc, vc, c_cum, fk), "attn_forget")

    tqd = MLA_TQ if s % MLA_TQ == 0 else tq
    qd_spec = pl.BlockSpec((1, tqd, LANE), lambda b, h, i: (b, i, h))
    kvd_spec = pl.BlockSpec((1, s, LANE), lambda b, h, i: (b, 0, h))
    o_d = _head_call(functools.partial(_flash_mla_kernel, tq=tqd, tk=ATTN_TK), nb, s, tqd,
                     [qd_spec, kvd_spec, kvd_spec], (qd, kd, vd), "attn_mla")
    return o_a, o_b, o_c, o_d


def _two_block_softmax(s1, s2, v1_t, v2):
    m = jnp.maximum(jnp.max(s1, axis=1, keepdims=True), jnp.max(s2, axis=1, keepdims=True))
    p1 = jnp.exp2(s1 - m)
    p2 = jnp.exp2(s2 - m)
    l = jnp.sum(p1, axis=1, keepdims=True) + jnp.sum(p2, axis=1, keepdims=True)
    return (_nt_dot(p1.astype(BF16), v1_t) + _dot(p2.astype(BF16), v2)) / l


def _sample_kernel(qa_ref, ka_ref, va_ref, qb_ref, kb_ref, vb_ref, qc_ref, kc_ref, vc_ref, qd_ref, kd_ref, vd_ref,
                   cak_ref, cav_ref, cbk_ref, cbv_ref, cck_ref, ccv_ref, cckv_ref, ckr_ref,
                   ba_c_ref, ba_n_ref, fq_ref, fkc_ref, fkn_ref,
                   u_ref, un_ref, wukt_ref, wuv_ref, gk_ref,
                   oa_ref, ob_ref, oc_ref, od_ref, *, t, past, ks):
    row = lax.broadcasted_iota(jnp.int32, (t, t), 0)
    col = lax.broadcasted_iota(jnp.int32, (t, t), 1)
    causal_bias = jnp.where(col <= row, 0.0, NEG_INF)
    chunk_bias = jnp.where((past + col) // CHUNK <= (past + row) // CHUNK, 0.0, NEG_INF)
    strict = col < row

    ckv_c = cckv_ref[0].astype(BF16)
    kr_t = ckr_ref[0].astype(BF16)
    pad_t = jnp.zeros((LANE - MLA_NOPE - MLA_ROPE, past), BF16)

    for h in range(N_HEADS):
        hs = slice(LANE * h, LANE * h + LANE)
        ps = slice(LANE * (h // 2), LANE * (h // 2) + LANE)

        q = qa_ref[0][:, hs]
        s1 = _dot(q, cak_ref[0, ps, :].astype(BF16)) + ba_c_ref[h]
        s2 = _nt_dot(q, ka_ref[0][:, ps]) + ba_n_ref[h]
        oa_ref[0, :, hs] = _two_block_softmax(s1, s2, cav_ref[0, ps, :].astype(BF16), va_ref[0][:, ps])

        q = qb_ref[0][:, hs]
        c, w = _stick_weights(_nt_dot(q, kb_ref[0][:, ps]), jnp.zeros((t, 1), F32), un_ref[...], strict)
        acc = _dot(w.astype(BF16), vb_ref[0][:, ps])

        def b_live(state):
            return jnp.logical_and(state[0] >= 0, state[1] > DEAD_LOG2)

        def b_older(state, q=q, ps=ps):
            jb, _, c, acc = state
            off = pl.multiple_of(jb * ks, ks)
            kb_t = cbk_ref[0, ps, pl.ds(off, ks)].astype(BF16)
            vb_t = cbv_ref[0, ps, pl.ds(off, ks)].astype(BF16)
            c, w = _stick_weights(_dot(q, kb_t), c, u_ref[...], None)
            return jb - 1, jnp.max(c), c, acc + _nt_dot(w.astype(BF16), vb_t)

        _, _, _, acc = lax.while_loop(b_live, b_older, (past // ks - 1, jnp.max(c), c, acc))
        ob_ref[0, :, hs] = acc

        q = qc_ref[0][:, hs]
        fq = fq_ref[0][:, h:h + 1] * LOG2E
        s1 = _dot(q, cck_ref[0, ps, :].astype(BF16)) + (fq - fkc_ref[0, h:h + 1, :] * LOG2E)
        s2 = _nt_dot(q, kc_ref[0][:, ps]) + (fq - fkn_ref[0, h:h + 1, :] * LOG2E) + causal_bias
        oc_ref[0, :, hs] = _two_block_softmax(s1, s2, ccv_ref[0, ps, :].astype(BF16), vc_ref[0][:, ps])

        q = qd_ref[0][:, hs]
        kn_t = _nt_dot(wukt_ref[MLA_NOPE * h: MLA_NOPE * (h + 1), :], ckv_c)
        ss = jnp.sum(kn_t * kn_t, axis=0, keepdims=True)
        kn_t = kn_t * lax.rsqrt(ss * (1.0 / MLA_NOPE) + NORM_EPS) * gk_ref[...]
        k_t = jnp.concatenate([kn_t.astype(BF16), kr_t, pad_t], axis=0)
        s1 = _dot(q, k_t)
        s2 = _nt_dot(q, kd_ref[0][:, hs]) + chunk_bias
        v_c = _dot(ckv_c, wuv_ref[:, hs]).astype(BF16)
        m = jnp.maximum(jnp.max(s1, axis=1, keepdims=True), jnp.max(s2, axis=1, keepdims=True))
        p1 = jnp.exp2(s1 - m)
        p2 = jnp.exp2(s2 - m)
        l = jnp.sum(p1, axis=1, keepdims=True) + jnp.sum(p2, axis=1, keepdims=True)
        od_ref[0, :, hs] = (_dot(p1.astype(BF16), v_c) + _dot(p2.astype(BF16), vd_ref[0][:, hs])) / l


def _channel_major(cache):
    depth, nb, rows = cache.shape[:3]
    return jnp.transpose(cache, (0, 1, 3, 4, 2)).reshape(depth, nb, GROUP_W, rows)


def _sample_attention(pr, caches, layer, pw, w_uk, k_gain, rel_bias, nb, t):
    a_k, a_v, b_k, b_v, c_k, c_v, c_lf, d_ckv, d_kr = caches
    wuv = pw[4]
    past = b_k.shape[2]
    win = a_k.shape[2]
    ks = STICK_KS
    assert past % ks == 0
    r3 = lambda a: a.reshape(nb, t, a.shape[-1])
    news = [r3(a) for a in pr[8:20]]

    qpos = past + np.arange(t)
    kpos = past - win + np.arange(win + t)
    qc, kc = qpos // CHUNK, kpos // CHUNK
    ok = (kc[None, :] <= qc[:, None]) & (kc[None, :] >= qc[:, None] - A_LEFT_CHUNKS)
    ba = _rel_bias_tile(rel_bias, t, win + t, win, ok)
    ba_c, ba_n = ba[:, :, :win], ba[:, :, win:]

    clf_new = jnp.swapaxes(r3(pr[7])[..., 0:N_HEADS], 1, 2)
    lf_c = jnp.swapaxes(c_lf[layer], 1, 2).astype(F32)
    fk = jnp.cumsum(jnp.concatenate([lf_c, clf_new], axis=2), axis=2)
    fk_c, fk_n = fk[:, :, :past], fk[:, :, past:]
    fq = jnp.swapaxes(fk_n, 1, 2)

    wukt = w_uk.T.astype(BF16)
    gk = k_gain.reshape(MLA_NOPE, 1)
    u, un = _strict_upper(ks), _strict_upper(t)

    per_b = lambda shape: pl.BlockSpec((1,) + shape, lambda b: (b,) + (0,) * len(shape))
    per_lb = lambda shape: pl.BlockSpec((None, 1) + shape, lambda b: (layer, b) + (0,) * len(shape))
    const = lambda a: pl.BlockSpec(a.shape, lambda b: (0,) * a.ndim)
    cache_ops = [_channel_major(c) for c in (a_k, a_v, b_k, b_v, c_k, c_v)] + [d_ckv, jnp.swapaxes(d_kr, 2, 3)]
    operands = news + cache_ops + [ba_c, ba_n, fq, fk_c, fk_n, u, un, wukt, wuv, gk]
    in_specs = [per_b(a.shape[1:]) for a in news] + [per_lb(a.shape[2:]) for a in cache_ops] + \
               [const(ba_c), const(ba_n)] + [per_b(a.shape[1:]) for a in (fq, fk_c, fk_n)] + \
               [const(a) for a in (u, un, wukt, wuv, gk)]
    out_shape = [jax.ShapeDtypeStruct((nb, t, N_HEADS * LANE), F32)] * 4
    out_specs = [per_b((t, N_HEADS * LANE))] * 4
    return pl.pallas_call(
        functools.partial(_sample_kernel, t=t, past=past, ks=ks), grid=(nb,),
        in_specs=in_specs, out_specs=out_specs, out_shape=out_shape,
        compiler_params=_cparams(1), name="attn_sample",
    )(*operands)


def _out_kernel(oa_ref, ob_ref, oc_ref, od_ref, h_ref, gg_ref, wo_ref, nf_ref,
                wr1_ref, wr2_ref, br_ref, h1_ref, hn_ref, lg_ref):
    rows = h_ref.shape[0]
    low = lax.broadcasted_iota(jnp.int32, (rows, LANE), 1) < HEAD_DIM
    h1 = h_ref[...]
    for g, o_ref in enumerate((oa_ref, ob_ref, oc_ref, od_ref)):
        p0 = jnp.where(low, o_ref[:, 0:LANE], o_ref[:, LANE:2 * LANE])
        p1 = jnp.where(low, o_ref[:, 2 * LANE:3 * LANE], o_ref[:, 3 * LANE:4 * LANE])
        og = jnp.concatenate([p0, p1], axis=1)
        ms = jnp.mean(og * og, axis=-1, keepdims=True)
        y = (og * lax.rsqrt(ms + NORM_EPS) * gg_ref[g:g + 1, :]).astype(BF16)
        h1 = h1 + _dot(y, wo_ref[GROUP_W * g: GROUP_W * (g + 1), :])
    h1_ref[...] = h1
    ms = jnp.mean(h1 * h1, axis=-1, keepdims=True)
    hn = h1 * lax.rsqrt(ms + NORM_EPS) * nf_ref[...]
    hn_ref[...] = hn.astype(BF16)
    a1, a2 = _split2(hn)
    w1, w2 = wr1_ref[...], wr2_ref[...]
    lg_ref[...] = _dot(a1, w1) + _dot(a1, w2) + _dot(a2, w1) + br_ref[...]


def _out_project(outs, h, group_gain, w_out_b, norm_ffn, wr_parts, br_pad):
    t, d = h.shape
    ts = DENSE_ROWS
    assert t % ts == 0
    row = lambda i: (i, 0)
    full = lambda i: (0, 0)
    o_spec = pl.BlockSpec((ts, N_HEADS * LANE), row)
    in_specs = [o_spec] * 4 + [pl.BlockSpec((ts, d), row), pl.BlockSpec(group_gain.shape, full),
                               pl.BlockSpec(w_out_b.shape, full), pl.BlockSpec((1, d), full)] + \
               [pl.BlockSpec((d, LANE), full)] * len(wr_parts) + [pl.BlockSpec((1, LANE), full)]
    out_shape = [jax.ShapeDtypeStruct((t, d), F32), jax.ShapeDtypeStruct((t, d), BF16),
                 jax.ShapeDtypeStruct((t, LANE), F32)]
    out_specs = [pl.BlockSpec((ts, d), row), pl.BlockSpec((ts, d), row), pl.BlockSpec((ts, LANE), row)]
    return pl.pallas_call(
        _out_kernel, grid=(t // ts,), in_specs=in_specs, out_specs=out_specs, out_shape=out_shape,
        compiler_params=_cparams(1), name="out_proj",
    )(*outs, h, group_gain, w_out_b, norm_ffn.reshape(1, d), *wr_parts, br_pad)


def _expert_kernel(be_ref, fe_ref, nu_ref, x_ref, wu_ref, bu_ref, wd_ref, bd_ref, *rest):
    y_ref, wub_ref, wdb_ref = rest[-3:]
    i = pl.program_id(0)
    used = i < nu_ref[0]

    @pl.when(jnp.logical_and(used, fe_ref[i] == 1))
    def _():
        wub_ref[...] = wu_ref[0, 0].astype(BF16)
        wdb_ref[...] = wd_ref[0, 0].astype(BF16)

    @pl.when(used)
    def _():
        u = _dot(x_ref[...], wub_ref[...]) + bu_ref[0, 0]
        glu = jnp.minimum(u[:, :D_FF], SWIGLU_LIMIT)
        lin = jnp.clip(u[:, D_FF:], -SWIGLU_LIMIT, SWIGLU_LIMIT)
        act = glu * jax.nn.sigmoid(SWIGLU_ALPHA * glu) * (lin + 1.0)
        y_ref[...] = (_dot(act.astype(BF16), wdb_ref[...]) + bd_ref[0, 0]).astype(y_ref.dtype)


def _expert_ffn(y_prev, blk0, n_blk_all, x_rows, blk_exp, blk_first, n_used, layer, w_up, b_up, w_down, b_down):
    rows, d = x_rows.shape
    n_blk = rows // MOE_BLOCK
    last = lambda i, nu: jnp.maximum(jnp.minimum(i, nu[0] - 1), 0)
    w_idx = lambda i, be, fe, nu: (layer, be[last(i, nu)], 0, 0)
    in_specs = [pl.BlockSpec((MOE_BLOCK, d), lambda i, be, fe, nu: (last(i, nu), 0)),
                pl.BlockSpec((1, 1, d, 2 * D_FF), w_idx), pl.BlockSpec((1, 1, 1, 2 * D_FF), w_idx),
                pl.BlockSpec((1, 1, D_FF, d), w_idx), pl.BlockSpec((1, 1, 1, d), w_idx)]
    depth = w_up.shape[0]
    operands = [blk_exp, blk_first, n_used, x_rows, w_up, b_up.reshape(depth, N_EXP, 1, 2 * D_FF),
                w_down, b_down.reshape(depth, N_EXP, 1, d)]
    aliases = {}
    if y_prev is not None:
        in_specs.append(pl.BlockSpec(memory_space=pl.ANY))
        aliases = {len(operands): 0}
        operands.append(y_prev)
    grid_spec = pltpu.PrefetchScalarGridSpec(
        num_scalar_prefetch=3, grid=(n_blk,), in_specs=in_specs,
        out_specs=pl.BlockSpec((MOE_BLOCK, d), lambda i, be, fe, nu: (blk0 + last(i, nu), 0)),
        scratch_shapes=[pltpu.VMEM((d, 2 * D_FF), BF16), pltpu.VMEM((D_FF, d), BF16)])
    return pl.pallas_call(
        _expert_kernel, grid_spec=grid_spec,
        out_shape=jax.ShapeDtypeStruct((n_blk_all * MOE_BLOCK, d), BF16), input_output_aliases=aliases,
        compiler_params=_cparams(1), name="expert_ffn",
    )(*operands)


SCATTER_UNROLL = 8


def _row_token_kernel(pad_ref, dest_ref, rt_ref, *, chunk):
    i = pl.program_id(0)

    @pl.when(i == 0)
    def _():
        for e in range(N_EXP + 1):
            def clear(r, carry):
                rt_ref[r] = 0
                return carry
            lax.fori_loop(pad_ref[2 * e], pad_ref[2 * e + 1], clear, 0)

    base = i * chunk

    def place(g, carry):
        k = g * SCATTER_UNROLL
        tok = (base + k) // TOP_K
        for j in range(SCATTER_UNROLL):
            rt_ref[dest_ref[k + j]] = tok + j // TOP_K
        return carry
    lax.fori_loop(0, chunk // SCATTER_UNROLL, place, 0)


def _row_tokens(dest, pad_ranges, n_rows):
    n = dest.shape[0]
    chunk = int(np.gcd(n, SCATTER_CHUNK))
    assert chunk % SCATTER_UNROLL == 0 and SCATTER_UNROLL % TOP_K == 0
    smem = pl.BlockSpec(memory_space=pltpu.SMEM)
    return pl.pallas_call(
        functools.partial(_row_token_kernel, chunk=chunk), grid=(n // chunk,),
        in_specs=[smem, pl.BlockSpec((chunk,), lambda i: (i,), memory_space=pltpu.SMEM)],
        out_specs=smem, out_shape=jax.ShapeDtypeStruct((n_rows,), jnp.int32),
        compiler_params=_cparams(1), name="row_tokens",
    )(pad_ranges, dest)


def _moe(hn, logits, layer, w_up, b_up, w_down, b_down):
    n_tok, d = hn.shape
    top_v, top_i = lax.top_k(logits, TOP_K)
    gates = jax.nn.softmax(top_v, axis=-1)
    n = n_tok * TOP_K
    e_flat = top_i.reshape(-1)
    onehot = (e_flat[:, None] == jnp.arange(N_EXP)[None, :]).astype(jnp.int32)
    rank = jnp.sum((jnp.cumsum(onehot, axis=0) - onehot) * onehot, axis=1)
    counts = jnp.sum(onehot, axis=0)
    padded = (counts + MOE_BLOCK - 1) // MOE_BLOCK * MOE_BLOCK
    p_end = jnp.cumsum(padded)
    p_start = p_end - padded
    dest = p_start[e_flat] + rank
    n_blk = -(-n // MOE_BLOCK) + N_EXP
    rows = n_blk * MOE_BLOCK
    pad_lo = jnp.concatenate([p_start + counts, p_end[-1:]])
    pad_hi = jnp.concatenate([p_end, jnp.full((1,), rows, p_end.dtype)])
    pad_ranges = jnp.stack([pad_lo, pad_hi], axis=1).reshape(-1).astype(jnp.int32)
    row_tok = _row_tokens(dest.astype(jnp.int32), pad_ranges, rows)
    blk_start = jnp.arange(n_blk, dtype=jnp.int32) * MOE_BLOCK
    blk_exp = jnp.sum((p_end[None, :] <= blk_start[:, None]).astype(jnp.int32), axis=1)
    blk_exp = jnp.minimum(blk_exp, N_EXP - 1)
    n_used = (p_end[-1] // MOE_BLOCK).astype(jnp.int32)
    n_grp = MOE_GROUPS if n_blk % MOE_GROUPS == 0 else 1
    nbg = n_blk // n_grp
    y = None
    for g in range(n_grp):
        be = blk_exp[g * nbg:(g + 1) * nbg]
        fe = jnp.concatenate([jnp.ones((1,), jnp.int32), (be[1:] != be[:-1]).astype(jnp.int32)])
        x_rows = hn.at[row_tok[g * nbg * MOE_BLOCK:(g + 1) * nbg * MOE_BLOCK]].get(mode='promise_in_bounds')
        y = _expert_ffn(y, g * nbg, n_blk, x_rows, be, fe, (n_used - g * nbg).reshape(1), layer,
                        w_up, b_up, w_down, b_down)
    picked = y.at[dest.reshape(n_tok, TOP_K).T].get(mode='promise_in_bounds')
    return picked, gates


def _ple_kernel(h_ref, y_ref, g_ref, p_ref, np_ref, wg_ref, wp_ref, o_ref):
    h2 = h_ref[...]
    g = g_ref[...]
    for j in range(TOP_K):
        h2 = h2 + y_ref[j].astype(F32) * g[:, j:j + 1]
    ms = jnp.mean(h2 * h2, axis=-1, keepdims=True)
    hn = (h2 * lax.rsqrt(ms + NORM_EPS) * np_ref[...]).astype(BF16)
    gate = jax.nn.sigmoid(_dot(hn, wg_ref[...]))
    o_ref[...] = h2 + gate * _dot(p_ref[...].astype(BF16), wp_ref[...])


def _ple(h1, picked, gates, row0, p, layer, norm_ple, wg_b, wp_b):
    t, d = h1.shape
    ts = PLE_ROWS
    assert t % ts == 0 and row0 % ts == 0
    off = row0 // ts
    row = lambda i: (i, 0)
    full = lambda i: (0, 0)
    return pl.pallas_call(
        _ple_kernel, grid=(t // ts,),
        in_specs=[pl.BlockSpec((ts, d), row), pl.BlockSpec((TOP_K, ts, d), lambda i: (0, i + off, 0)),
                  pl.BlockSpec((ts, TOP_K), lambda i: (i + off, 0)),
                  pl.BlockSpec((None, ts, p.shape[2]), lambda i: (layer, i, 0)),
                  pl.BlockSpec((1, d), full), pl.BlockSpec(wg_b.shape, full), pl.BlockSpec(wp_b.shape, full)],
        out_specs=pl.BlockSpec((ts, d), row), out_shape=jax.ShapeDtypeStruct((t, d), F32),
        compiler_params=_cparams(1), name="ple",
    )(h1, picked, gates, p, norm_ple.reshape(1, d), wg_b, wp_b)


def kernel(x_prompt, x_sample, p_prompt, p_sample, cache_a_k, cache_a_v, cache_b_k, cache_b_v, cache_c_k, cache_c_v, cache_c_logf, cache_d_ckv, cache_d_krope, norm_mix, w_in, b_forget, qk_gain, rope_gain, kv_gain, w_uk, w_uv, rel_bias, group_gain, w_out, norm_ffn, w_router, b_router, w_up, b_up, w_down, b_down, norm_ple, w_ple_gate, w_ple_proj):
    nb, s, d = x_prompt.shape
    nd, t, _ = x_sample.shape
    depth = w_in.shape[0]
    past = cache_b_k.shape[2]
    assert PROJ_ROWS % t == 0 and s % PROJ_ROWS == 0

    tab_p = _rope_tables(jnp.arange(s))
    tab_s = _rope_tables(past + jnp.arange(PROJ_ROWS) % t)
    hp = x_prompt.reshape(nb * s, d)
    hs = x_sample.reshape(nd * t, d)
    st_p, st_s = [], []
    keep = min(A_WINDOW, s)
    layered_state = None
    for i in range(depth):
        pw = _proj_weights(w_in[i], b_forget[i], qk_gain[i], rope_gain[i], kv_gain[i], w_uk[i], w_uv[i])
        pr_p = _project(hp, norm_mix[i], pw, tab_p, s // PROJ_ROWS, (depth, i, nb, layered_state))
        layered_state = [pr_p[slot] for slot in STATE_SLOTS]
        pr_s = _project(hs, norm_mix[i], pw, tab_s, 1)
        outs_p = _prompt_attention(pr_p, pr_p[7][i], nb, s, rel_bias[i], qk_gain[i, 3])
        caches = (cache_a_k, cache_a_v, cache_b_k, cache_b_v, cache_c_k, cache_c_v,
                  cache_c_logf, cache_d_ckv, cache_d_krope)
        outs_s = _sample_attention(pr_s, caches, i, pw, w_uk[i], qk_gain[i, 5], rel_bias[i], nd, t)

        def heads(a, n, rows):
            return a.reshape(n, rows, N_HEADS, HEAD_DIM)

        p3 = lambda a: a.reshape(nb, s, a.shape[-1])
        s3 = lambda a: a.reshape(nd, t, a.shape[-1])
        st_p.append((heads(p3(pr_p[0])[:, s - keep:], nb, keep), heads(p3(pr_p[1])[:, s - keep:], nb, keep)))
        ka_all = jnp.concatenate([cache_a_k[i], heads(pr_s[0], nd, t)], axis=1)[:, t:]
        va_all = jnp.concatenate([cache_a_v[i], heads(pr_s[1], nd, t)], axis=1)[:, t:]
        st_s.append((ka_all, va_all,
                     heads(pr_s[2], nd, t), heads(pr_s[3], nd, t), heads(pr_s[4], nd, t), heads(pr_s[5], nd, t),
                     s3(pr_s[7])[..., 0:N_HEADS], s3(pr_s[6]), s3(pr_s[7])[..., ROPE_LO:ROPE_LO + MLA_ROPE]))

        w_out_b = w_out[i].astype(BF16)
        wr_pad = jnp.zeros((d, LANE), F32).at[:, 0:N_EXP].set(w_router[i])
        wr_parts = _split2(wr_pad)
        br_pad = jnp.zeros((1, LANE), F32).at[0, 0:N_EXP].set(b_router[i])
        flat4 = lambda o: o.reshape(-1, N_HEADS * LANE)
        h1_p, hn_p, lg_p = _out_project([flat4(o) for o in outs_p], hp, group_gain[i], w_out_b, norm_ffn[i],
                                        wr_parts, br_pad)
        h1_s, hn_s, lg_s = _out_project([flat4(o) for o in outs_s], hs, group_gain[i], w_out_b, norm_ffn[i],
                                        wr_parts, br_pad)

        hn_all = jnp.concatenate([hn_p, hn_s], axis=0)
        lg_all = jnp.concatenate([lg_p, lg_s], axis=0)[:, 0:N_EXP]
        picked, gates = _moe(hn_all, lg_all, i, w_up, b_up, w_down, b_down)

        wg_b = w_ple_gate[i].astype(BF16)
        wp_b = w_ple_proj[i].astype(BF16)
        hp = _ple(h1_p, picked, gates, 0, p_prompt.reshape(depth, nb * s, -1), i, norm_ple[i], wg_b, wp_b)
        hs = _ple(h1_s, picked, gates, nb * s, p_sample.reshape(depth, nd * t, -1), i, norm_ple[i], wg_b, wp_b)

    bk_t, bv_t, ck_t, cv_t, ckv_all, misc_t = layered_state

    def from_channel_major(a):
        return jnp.transpose(a.reshape(depth, nb, N_HEADS, HEAD_DIM, s), (0, 1, 4, 2, 3))

    state_p = [jnp.stack([st[j] for st in st_p]) for j in range(2)] + \
              [from_channel_major(a) for a in (bk_t, bv_t, ck_t, cv_t)] + \
              [jnp.swapaxes(misc_t[:, :, 0:N_HEADS, :], 2, 3), ckv_all.reshape(depth, nb, s, KV_RANK),
               jnp.swapaxes(misc_t[:, :, ROPE_LO:ROPE_LO + MLA_ROPE, :], 2, 3)]
    state_s = [jnp.stack([st[j] for st in st_s]) for j in range(9)]
    return (hp.reshape(nb, s, d), hs.reshape(nd, t, d), *state_p, *state_s)
```

```python
import functools

import numpy as np
import jax
import jax.numpy as jnp
from jax import lax
from jax.experimental import pallas as pl
from jax.experimental.pallas import tpu as pltpu

F32 = jnp.float32
BF16 = jnp.bfloat16

CHUNK = 64
HEAD_DIM = 64
N_HEADS = 4
GROUP_W = 256
A_LEFT_CHUNKS = 8
A_WINDOW = A_LEFT_CHUNKS * CHUNK
REL_CLIP = 128
MLA_NOPE = 64
MLA_ROPE = 32
KV_RANK = 128
ROPE_BASE = 10000.0
N_EXP = 32
TOP_K = 4
D_FF = 1024
SWIGLU_LIMIT = 7.0
SWIGLU_ALPHA = 1.702
MOE_BLOCK = 512
NORM_EPS = 1e-6
NEG_INF = -1e30
LOG2E = 1.4426950408889634

A_Q = 0
C_F = 2304
D_Q = 2308
D_CKV = D_Q + N_HEADS * (MLA_NOPE + MLA_ROPE)
D_KR = D_CKV + KV_RANK

LANE = 128
SEG_ABC = 0
SEG_DQ = 2304
SEG_CKV = 2816
SEG_MISC = 2944
N_COLS = 3072
ROPE_LO = 64
ROPE_HALF = MLA_ROPE // 2

VMEM_LIMIT = 56 * 1024 * 1024

PROJ_ROWS = 512
ATTN_TQ = 512
ATTN_TK = 1024
MLA_TQ = 2048
STICK_KS = 256
DENSE_ROWS = 512
PLE_ROWS = 256
MOE_GROUPS = 4
SCATTER_CHUNK = 8192

DEAD_LOG2 = -160.0
BOUND_MARGIN = 1.01


def _cparams(n_axes):
    return pltpu.CompilerParams(dimension_semantics=("arbitrary",) * n_axes,
                                vmem_limit_bytes=VMEM_LIMIT)


def _nt_dot(a, b):
    return lax.dot_general(a, b, (((1,), (1,)), ((), ())), preferred_element_type=F32)


def _dot(a, b):
    return jnp.dot(a, b, preferred_element_type=F32)


def _split2(x):
    hi = x.astype(BF16)
    lo = (x - hi.astype(F32)).astype(BF16)
    return hi, lo


def _in_col_map():
    cols = np.full((N_COLS,), -1, np.int64)
    cols[0:2304] = np.arange(2304)
    for h in range(N_HEADS):
        base = D_Q + (MLA_NOPE + MLA_ROPE) * h
        cols[SEG_DQ + LANE * h: SEG_DQ + LANE * h + MLA_NOPE + MLA_ROPE] = base + np.arange(MLA_NOPE + MLA_ROPE)
    cols[SEG_CKV:SEG_CKV + KV_RANK] = D_CKV + np.arange(KV_RANK)
    cols[SEG_MISC:SEG_MISC + N_HEADS] = C_F + np.arange(N_HEADS)
    cols[SEG_MISC + ROPE_LO:SEG_MISC + ROPE_LO + MLA_ROPE] = D_KR + np.arange(MLA_ROPE)
    return cols


def _rope_tables(pos):
    inv = ROPE_BASE ** (-jnp.arange(ROPE_HALF, dtype=F32) / ROPE_HALF)
    ang = pos.astype(F32)[:, None] * inv
    cos, sin = jnp.cos(ang), jnp.sin(ang)
    n = pos.shape[0]
    one = jnp.ones((n, ROPE_LO), F32)
    z16 = jnp.zeros((n, ROPE_HALF), F32)
    z64 = jnp.zeros((n, ROPE_LO), F32)
    z32 = jnp.zeros((n, LANE - ROPE_LO - MLA_ROPE), F32)
    cos_t = jnp.concatenate([one, cos, cos, z32 + 1.0], axis=1)
    sin_a = jnp.concatenate([z64, -sin, z16, z32], axis=1)
    sin_b = jnp.concatenate([z64, z16, sin, z32], axis=1)
    return cos_t, sin_a, sin_b


def _rope(y, cos_t, sin_a, sin_b):
    left = pltpu.roll(y, LANE - ROPE_HALF, axis=1)
    right = pltpu.roll(y, ROPE_HALF, axis=1)
    return y * cos_t + left * sin_a + right * sin_b


def _head_norm(x, m_ref, gain):
    hi, lo = _split2(x * x)
    ssq = _dot(hi, m_ref[...]) + _dot(lo, m_ref[...])
    return x * lax.rsqrt(ssq * (1.0 / HEAD_DIM) + NORM_EPS) * gain


def _value_lane0(h):
    return HEAD_DIM * (h % 2)


def _ones_lane(h):
    return HEAD_DIM - _value_lane0(h)


def _store_padded_q(q_ref, q, scale):
    lane = lax.broadcasted_iota(jnp.int32, (q.shape[0], LANE), 1)
    low = lane < HEAD_DIM
    qs = q * scale
    for h in range(N_HEADS):
        pair = qs[:, LANE * (h // 2): LANE * (h // 2) + LANE]
        keep = low if h % 2 == 0 else jnp.logical_not(low)
        q_ref[:, LANE * h: LANE * h + LANE] = jnp.where(keep, pair, 0.0).astype(BF16)


def _proj_kernel(*refs, channel_major, n_alias):
    x_ref, gmix_ref, w_ref, cos_ref, sa_ref, sb_ref, gv_ref, m64_ref, wuk_ref, wuv_ref = refs[:10]
    (ak_ref, av_ref, bk_ref, bv_ref, ck_ref, cv_ref, ckv_ref, misc_ref,
     qa_ref, ka_ref, va_ref, qb_ref, kb_ref, vb_ref, qc_ref, kc_ref, vc_ref,
     qd_ref, kd_ref, vd_ref) = refs[10 + n_alias:]

    def put_state(ref, val):
        ref[...] = val.T if channel_major else val

    x = x_ref[...]
    ms = jnp.mean(x * x, axis=-1, keepdims=True)
    hn = (x * lax.rsqrt(ms + NORM_EPS) * gmix_ref[...]).astype(BF16)
    sm_scale = (HEAD_DIM ** -0.5) * LOG2E
    rows = x.shape[0]
    lane = lax.broadcasted_iota(jnp.int32, (rows, LANE), 1)
    is_nope = lane < MLA_NOPE
    is_rope = jnp.logical_and(lane >= ROPE_LO, lane < ROPE_LO + MLA_ROPE)
    cos_t, sin_a, sin_b = cos_ref[...], sa_ref[...], sb_ref[...]

    z = _dot(hn, w_ref[:, 0:768])
    aq = _head_norm(z[:, 0:256], m64_ref, gv_ref[0:1, :])
    ak = _head_norm(z[:, 256:512], m64_ref, gv_ref[1:2, :])
    av = z[:, 512:768]
    ak_ref[...] = ak
    av_ref[...] = av
    _store_padded_q(qa_ref, aq, sm_scale)
    ka_ref[...] = ak.astype(BF16)
    va_ref[...] = av.astype(BF16)

    z = _dot(hn, w_ref[:, 768:1536])
    put_state(bk_ref, z[:, 256:512])
    put_state(bv_ref, z[:, 512:768])
    _store_padded_q(qb_ref, z[:, 0:256], sm_scale)
    kb_ref[...] = z[:, 256:512].astype(BF16)
    vb_ref[...] = z[:, 512:768].astype(BF16)

    z = _dot(hn, w_ref[:, 1536:2304])
    cq = _head_norm(z[:, 0:256], m64_ref, gv_ref[2:3, :])
    ck = _head_norm(z[:, 256:512], m64_ref, gv_ref[3:4, :])
    cv = z[:, 512:768]
    put_state(ck_ref, ck)
    put_state(cv_ref, cv)
    _store_padded_q(qc_ref, cq, sm_scale)
    kc_ref[...] = ck.astype(BF16)
    vc_ref[...] = cv.astype(BF16)

    z = _dot(hn, w_ref[:, SEG_CKV:N_COLS])
    zc = z[:, 0:KV_RANK]
    ckv = zc * lax.rsqrt(jnp.mean(zc * zc, axis=-1, keepdims=True) + NORM_EPS) * gv_ref[7:8, 0:LANE]
    ckv_ref[...] = ckv
    zm = z[:, KV_RANK:2 * KV_RANK]
    ssr = jnp.sum(jnp.where(is_rope, zm * zm, 0.0), axis=-1, keepdims=True)
    kr = zm * lax.rsqrt(ssr * (1.0 / MLA_ROPE) + NORM_EPS) * gv_ref[6:7, 0:LANE]
    kr = _rope(kr, cos_t, sin_a, sin_b)
    zf = zm + gv_ref[8:9, 0:LANE]
    clf = jnp.minimum(zf, 0.0) - jnp.log1p(jnp.exp(-jnp.abs(zf)))
    put_state(misc_ref, jnp.where(lane < N_HEADS, clf, kr))

    ckv_b = ckv.astype(BF16)
    kn = _dot(ckv_b, wuk_ref[...])
    for h in range(N_HEADS):
        hs = slice(LANE * h, LANE * h + LANE)
        ones = (lane == _ones_lane(h)).astype(F32)
        vd_ref[:, hs] = (_dot(ckv_b, wuv_ref[:, hs]) + ones).astype(BF16)
    for h in range(N_HEADS):
        xh = kn[:, LANE * h: LANE * h + LANE]
        ss = jnp.sum(xh * xh, axis=-1, keepdims=True)
        yh = xh * lax.rsqrt(ss * (1.0 / MLA_NOPE) + NORM_EPS) * gv_ref[5:6, 0:LANE]
        kd_ref[:, LANE * h: LANE * h + LANE] = (yh + kr).astype(BF16)

    z = _dot(hn, w_ref[:, SEG_DQ:SEG_CKV])
    d_scale = ((MLA_NOPE + MLA_ROPE) ** -0.5) * LOG2E
    for h in range(N_HEADS):
        xh = z[:, LANE * h: LANE * h + LANE]
        x2 = xh * xh
        ssn = jnp.sum(jnp.where(is_nope, x2, 0.0), axis=-1, keepdims=True)
        ssr = jnp.sum(jnp.where(is_rope, x2, 0.0), axis=-1, keepdims=True)
        rn = lax.rsqrt(ssn * (1.0 / MLA_NOPE) + NORM_EPS)
        rr = lax.rsqrt(ssr * (1.0 / MLA_ROPE) + NORM_EPS)
        yh = xh * jnp.where(is_nope, rn, rr) * gv_ref[4:5, 0:LANE]
        yh = _rope(yh, cos_t, sin_a, sin_b)
        qd_ref[:, LANE * h: LANE * h + LANE] = (yh * d_scale).astype(BF16)


def _proj_weights(w_in, b_forget, qk_gain, rope_gain, kv_gain, w_uk, w_uv):
    cols = _in_col_map()
    valid = jnp.asarray(cols >= 0)
    w = jnp.where(valid[None, :], w_in[:, np.maximum(cols, 0)], 0.0).astype(BF16)

    def tile4(g):
        return jnp.tile(g, N_HEADS)

    gv = jnp.zeros((16, GROUP_W), F32)
    gv = gv.at[0].set(tile4(qk_gain[0])).at[1].set(tile4(qk_gain[1]))
    gv = gv.at[2].set(tile4(qk_gain[2])).at[3].set(tile4(qk_gain[3]))
    gv = gv.at[4, 0:MLA_NOPE].set(qk_gain[4]).at[4, ROPE_LO:ROPE_LO + MLA_ROPE].set(rope_gain[0])
    gv = gv.at[5, 0:MLA_NOPE].set(qk_gain[5])
    gv = gv.at[6, ROPE_LO:ROPE_LO + MLA_ROPE].set(rope_gain[1])
    gv = gv.at[7, 0:KV_RANK].set(kv_gain)
    gv = gv.at[8, 0:N_HEADS].set(b_forget)
    head = np.arange(GROUP_W) // HEAD_DIM
    m64 = jnp.asarray((head[:, None] == head[None, :]).astype(np.float32), BF16)
    wuk = jnp.zeros((KV_RANK, N_HEADS * LANE), F32)
    for h in range(N_HEADS):
        wuk = wuk.at[:, LANE * h: LANE * h + MLA_NOPE].set(w_uk[:, MLA_NOPE * h: MLA_NOPE * (h + 1)])
    wuv = jnp.zeros((KV_RANK, N_HEADS * LANE), F32)
    for h in range(N_HEADS):
        lo = LANE * h + _value_lane0(h)
        wuv = wuv.at[:, lo: lo + HEAD_DIM].set(w_uv[:, HEAD_DIM * h: HEAD_DIM * (h + 1)])
    return w, gv, m64, wuk.astype(BF16), wuv.astype(BF16)


STATE_SLOTS = (2, 3, 4, 5, 6, 7)


def _project(x, gmix, pw, tables, n_tab_blocks, layered=None):
    w, gv, m64, wuk, wuv = pw
    t, d = x.shape
    ts = PROJ_ROWS
    assert t % ts == 0
    row = lambda i: (i, 0)
    full = lambda i: (0, 0)
    tab = lambda i: (i % n_tab_blocks, 0)
    f32_w = [GROUP_W] * 6 + [KV_RANK, LANE]
    bf_w = [512, 256, 256, 512, 256, 256, 512, 256, 256, 512, 512, 512]
    out_shape = [jax.ShapeDtypeStruct((t, c), F32) for c in f32_w] + \
                [jax.ShapeDtypeStruct((t, c), BF16) for c in bf_w]
    out_specs = [pl.BlockSpec((ts, c), row) for c in f32_w + bf_w]
    in_specs = [pl.BlockSpec((ts, d), row), pl.BlockSpec((1, d), full), pl.BlockSpec((d, N_COLS), full),
                pl.BlockSpec((ts, LANE), tab), pl.BlockSpec((ts, LANE), tab), pl.BlockSpec((ts, LANE), tab),
                pl.BlockSpec(gv.shape, full), pl.BlockSpec(m64.shape, full),
                pl.BlockSpec(wuk.shape, full), pl.BlockSpec(wuv.shape, full)]
    operands = [x, gmix.reshape(1, d), w, *tables, gv, m64, wuk, wuv]
    aliases = {}
    if layered is not None:
        depth, layer, nb, earlier = layered
        s = t // nb
        nt = s // ts
        for slot in STATE_SLOTS:
            c = f32_w[slot]
            if slot == 6:
                out_shape[slot] = jax.ShapeDtypeStruct((depth, t, c), F32)
                out_specs[slot] = pl.BlockSpec((None, ts, c), lambda i: (layer, i, 0))
            else:
                out_shape[slot] = jax.ShapeDtypeStruct((depth, nb, c, s), F32)
                out_specs[slot] = pl.BlockSpec((None, None, c, ts), lambda i: (layer, i // nt, 0, i % nt))
        if earlier is not None:
            for k, slot in enumerate(STATE_SLOTS):
                aliases[len(operands)] = slot
                in_specs.append(pl.BlockSpec(memory_space=pl.ANY))
                operands.append(earlier[k])
    kern = functools.partial(_proj_kernel, channel_major=layered is not None, n_alias=len(aliases))
    return pl.pallas_call(
        kern, grid=(t // ts,), in_specs=in_specs, out_specs=out_specs, out_shape=out_shape,
        input_output_aliases=aliases, compiler_params=_cparams(1), name="proj",
    )(*operands)


def _softmax_step(q, kb, vb, carry, bias=None, ok=None):
    m, l, acc = carry
    s = _nt_dot(q, kb)
    if bias is not None:
        s = s + bias
    if ok is not None:
        s = jnp.where(ok, s, NEG_INF)
    m_new = jnp.maximum(m, jnp.max(s, axis=1, keepdims=True))
    alpha = jnp.exp2(m - m_new)
    p = jnp.exp2(s - m_new)
    l = alpha * l + jnp.sum(p, axis=1, keepdims=True)
    acc = alpha * acc + _dot(p.astype(BF16), vb)
    return m_new, l, acc


def _softmax_init(tq):
    return (jnp.full((tq, 1), NEG_INF, F32), jnp.zeros((tq, 1), F32), jnp.zeros((tq, LANE), F32))


def _pick_lane(block, h):
    col = lax.broadcasted_iota(jnp.int32, block.shape, 1)
    return jnp.sum(jnp.where(col == h, block, 0.0), axis=1, keepdims=True)


def _stack_pair(q_ref):
    return jnp.concatenate([q_ref[0, :, 0:LANE], q_ref[0, :, LANE:2 * LANE]], axis=0)


def _store_pair(o_ref, o, tq):
    o_ref[0, :, 0:LANE] = o[:tq]
    o_ref[0, :, LANE:2 * LANE] = o[tq:]


def _flash_forget_kernel(kmax_ref, decay_ref, q_ref, k_ref, v_ref, fq_ref, fk_ref, o_ref, *, tq, n_blk):
    b = pl.program_id(0)
    pair = pl.program_id(1)
    qi = pl.program_id(2)
    q = _stack_pair(q_ref)
    fq = jnp.concatenate([_pick_lane(fq_ref[0], 2 * pair), _pick_lane(fq_ref[0], 2 * pair + 1)], axis=0) * LOG2E

    def step(j, carry, diag):
        off = pl.multiple_of(j * tq, tq)
        kb = k_ref[0, pl.ds(off, tq), :]
        vb = v_ref[0, pl.ds(off, tq), :]
        fk = jnp.concatenate([jnp.broadcast_to(fk_ref[0, 0, pl.ds(j, 1), :], (tq, tq)),
                              jnp.broadcast_to(fk_ref[0, 1, pl.ds(j, 1), :], (tq, tq))], axis=0)
        ok = None
        if diag:
            row = lax.broadcasted_iota(jnp.int32, (2 * tq, tq), 0) % tq
            col = lax.broadcasted_iota(jnp.int32, (2 * tq, tq), 1)
            ok = col <= row
        return _softmax_step(q, kb, vb, carry, fq - fk * LOG2E, ok)

    carry = step(qi, _softmax_init(2 * tq), True)
    qf = q.astype(F32)
    q_norm = jnp.sqrt(jnp.sum(qf * qf, axis=1, keepdims=True))
    head0 = b * N_HEADS + 2 * pair
    room = q_norm * jnp.where(lax.broadcasted_iota(jnp.int32, q_norm.shape, 0) < tq,
                              kmax_ref[head0], kmax_ref[head0 + 1]) + fq - carry[0]
    slack0 = jnp.max(room[:tq]) - DEAD_LOG2
    slack1 = jnp.max(room[tq:]) - DEAD_LOG2

    def live(state):
        j = jnp.maximum(state[0], 0)
        alive = jnp.logical_or(slack0 + decay_ref[head0 * n_blk + j] >= 0.0,
                               slack1 + decay_ref[(head0 + 1) * n_blk + j] >= 0.0)
        return jnp.logical_and(state[0] >= 0, alive)

    def older(state):
        j = state[0]
        return (j - 1,) + step(j, state[1:], False)

    _, m, l, acc = lax.while_loop(live, older, (qi - 1,) + carry)
    _store_pair(o_ref, acc / l, tq)


def _flash_mla_kernel(q_ref, k_ref, v_ref, o_ref, *, tq, tk):
    qi = pl.program_id(2)
    n_sub = tq // tk

    def step(hh, j, carry, row0=None):
        m, acc = carry
        hs = slice(LANE * hh, LANE * hh + LANE)
        off = pl.multiple_of(j * tk, tk)
        qs = q_ref[0, :, hs] if row0 is None else q_ref[0, row0:, hs]
        s = _nt_dot(qs, k_ref[0, pl.ds(off, tk), hs])
        if row0 is not None:
            row = lax.broadcasted_iota(jnp.int32, s.shape, 0)
            col = lax.broadcasted_iota(jnp.int32, s.shape, 1)
            s = jnp.where(col // CHUNK <= row // CHUNK, s, NEG_INF)
        m_new = jnp.maximum(m, jnp.max(s, axis=1, keepdims=True))
        p = jnp.exp2(s - m_new)
        acc = jnp.exp2(m - m_new) * acc + _dot(p.astype(BF16), v_ref[0, pl.ds(off, tk), hs])
        return m_new, acc

    def both(j, carry):
        return step(0, j, carry[0]), step(1, j, carry[1])

    init = (jnp.full((tq, 1), NEG_INF, F32), jnp.zeros((tq, LANE), F32))
    carry = lax.fori_loop(0, qi * n_sub, both, (init, init))
    for hh in range(2):
        m, acc = carry[hh]
        for r in range(n_sub):
            row0 = r * tk
            m_r, acc_r = step(hh, qi * n_sub + r, (m[row0:], acc[row0:]), row0)
            m = m_r if r == 0 else jnp.concatenate([m[:row0], m_r], axis=0)
            acc = acc_r if r == 0 else jnp.concatenate([acc[:row0], acc_r], axis=0)
        ones_at = _ones_lane(hh)
        o_ref[0, :, LANE * hh: LANE * hh + LANE] = acc / acc[:, ones_at:ones_at + 1]


def _flash_stick_kernel(q_ref, k_ref, v_ref, u_ref, o_ref, *, tq, ks):
    qi = pl.program_id(2)
    q = _stack_pair(q_ref)
    n_sub = tq // ks

    def step(jb, carry, diag):
        c, acc = carry
        off = pl.multiple_of(jb * ks, ks)
        kb = k_ref[0, pl.ds(off, ks), :]
        vb = v_ref[0, pl.ds(off, ks), :]
        vis = None
        if diag:
            row = lax.broadcasted_iota(jnp.int32, (2 * tq, ks), 0) % tq + qi * tq
            col = lax.broadcasted_iota(jnp.int32, (2 * tq, ks), 1) + jb * ks
            vis = col < row
        c_new, w = _stick_weights(_nt_dot(q, kb), c, u_ref[...], vis)
        return c_new, acc + _dot(w.astype(BF16), vb)

    carry = (jnp.zeros((2 * tq, 1), F32), jnp.zeros((2 * tq, LANE), F32))
    for r in range(n_sub):
        carry = step(qi * n_sub + (n_sub - 1 - r), carry, True)

    def live(state):
        return jnp.logical_and(state[0] >= 0, state[1] > DEAD_LOG2)

    def older(state):
        c, acc = step(state[0], state[2:], False)
        return state[0] - 1, jnp.max(c), c, acc

    _, _, _, acc = lax.while_loop(live, older, (qi * n_sub - 1, jnp.max(carry[0])) + carry)
    _store_pair(o_ref, acc, tq)


def _stick_weights(z, c, u, vis):
    sp = jnp.log(1.0 + jnp.exp2(-jnp.abs(z))) * LOG2E
    log_rest = jnp.minimum(-z, 0.0) - sp
    log_beta = log_rest + z
    if vis is not None:
        log_rest = jnp.where(vis, log_rest, 0.0)
    hi, lo = _split2(log_rest)
    between = _dot(hi, u) + _dot(lo, u)
    w = jnp.exp2(log_beta + between + c)
    if vis is not None:
        w = jnp.where(vis, w, 0.0)
    return c + jnp.sum(log_rest, axis=1, keepdims=True), w


def _band_kernel(q_ref, k_ref, v_ref, bd_ref, bp_ref, o_ref, *, tq):
    qi = pl.program_id(2)
    q = _stack_pair(q_ref)
    off = pl.multiple_of(qi * tq, tq)
    carry = _softmax_step(q, k_ref[0, pl.ds(off, tq), :], v_ref[0, pl.ds(off, tq), :],
                          _softmax_init(2 * tq), bd_ref[...].reshape(2 * tq, tq))
    offp = pl.multiple_of(jnp.maximum(qi - 1, 0) * tq, tq)
    no_prev = jnp.where(qi == 0, NEG_INF, 0.0)
    m, l, acc = _softmax_step(q, k_ref[0, pl.ds(offp, tq), :], v_ref[0, pl.ds(offp, tq), :],
                              carry, bp_ref[...].reshape(2 * tq, tq) + no_prev)
    _store_pair(o_ref, acc / l, tq)


def _pair_specs(s, tq):
    q_spec = pl.BlockSpec((1, tq, 2 * LANE), lambda b, p, i: (b, i, p))
    kv_spec = pl.BlockSpec((1, s, LANE), lambda b, p, i: (b, 0, p))
    return q_spec, kv_spec


def _pair_call(kern, nb, s, tq, in_specs, operands, name):
    return pl.pallas_call(
        kern, grid=(nb, N_HEADS // 2, s // tq), in_specs=in_specs,
        out_specs=pl.BlockSpec((1, tq, 2 * LANE), lambda b, p, i: (b, i, p)),
        out_shape=jax.ShapeDtypeStruct((nb, s, N_HEADS * LANE), F32),
        compiler_params=_cparams(3), name=name,
    )(*operands)


def _strict_upper(n):
    idx = np.arange(n)
    return jnp.asarray((idx[:, None] > idx[None, :]).astype(np.float32), BF16)


def _toeplitz(vec, n_rows, n_cols):
    length = n_rows + n_cols - 1
    assert vec.shape[-1] == length
    lead = vec.shape[:-1]
    rev = jnp.concatenate([vec[..., ::-1], jnp.zeros(lead + (1,), vec.dtype)], axis=-1)
    flat = jnp.tile(rev, (1,) * len(lead) + (n_rows,))[..., :n_rows * length]
    return flat.reshape(lead + (n_rows, length))[..., n_rows - 1: n_rows - 1 + n_cols]


def _rel_bias_tile(rel_bias, n_rows, n_cols, rel00, ok):
    d = np.arange(n_rows + n_cols - 1) - (n_cols - 1) + rel00
    vec = rel_bias.astype(F32)[:, np.clip(d, -REL_CLIP, REL_CLIP) + REL_CLIP] * LOG2E
    return jnp.where(jnp.asarray(ok)[None], _toeplitz(vec, n_rows, n_cols), NEG_INF)


def _band_bias_tiles(rel_bias, tq):
    i = np.arange(tq)[:, None]
    j = np.arange(tq)[None, :]
    own = _rel_bias_tile(rel_bias, tq, tq, 0, (j // CHUNK) <= (i // CHUNK))
    prev = _rel_bias_tile(rel_bias, tq, tq, tq, (j // CHUNK) >= (i // CHUNK) + tq // CHUNK - A_LEFT_CHUNKS)
    return own, prev


def _prompt_attention(pr, misc_t, nb, s, rel_bias, k_gain):
    tq = ATTN_TQ
    assert s % tq == 0 and tq == A_WINDOW
    r3 = lambda a: a.reshape(nb, s, a.shape[-1])
    qa, ka, va, qb, kb, vb, qc, kc, vc, qd, kd, vd = [r3(a) for a in pr[8:20]]
    q_spec, kv_spec = _pair_specs(s, tq)

    bd, bp = _band_bias_tiles(rel_bias, tq)
    b_spec = pl.BlockSpec((2, tq, tq), lambda b, p, i: (p, 0, 0))
    o_a = _pair_call(functools.partial(_band_kernel, tq=tq), nb, s, tq,
                     [q_spec, kv_spec, kv_spec, b_spec, b_spec], (qa, ka, va, bd, bp), "attn_band")

    u = _strict_upper(STICK_KS)
    o_b = _pair_call(functools.partial(_flash_stick_kernel, tq=tq, ks=STICK_KS), nb, s, tq,
                     [q_spec, kv_spec, kv_spec, pl.BlockSpec(u.shape, lambda b, p, i: (0, 0))],
                     (qb, kb, vb, u), "attn_stick")

    n_blk = s // tq
    fk_t = jnp.cumsum(misc_t[:, 0:N_HEADS, :], axis=2)
    c_cum = jnp.swapaxes(fk_t, 1, 2)
    fk = fk_t.reshape(nb, N_HEADS, n_blk, tq)
    kmax = jnp.full((nb * N_HEADS,), HEAD_DIM ** 0.5 * BOUND_MARGIN, F32) * jnp.max(jnp.abs(k_gain))
    decay = lax.cummax(jnp.max(-fk, axis=-1), axis=2) * LOG2E
    smem = pl.BlockSpec(memory_space=pltpu.SMEM)
    fq_spec = pl.BlockSpec((1, tq, N_HEADS), lambda b, p, i: (b, i, 0))
    fk_spec = pl.BlockSpec((1, 2, n_blk, tq), lambda b, p, i: (b, p, 0, 0))
    o_c = _pair_call(functools.partial(_flash_forget_kernel, tq=tq, n_blk=n_blk), nb, s, tq,
                     [smem, smem, q_spec, kv_spec, kv_spec, fq_spec, fk_spec],
                     (kmax, decay.reshape(-1), qc, kc, vc, c_cum, fk), "attn_forget")

    tqd = MLA_TQ if s % MLA_TQ == 0 else tq
    qd_spec = pl.BlockSpec((1, tqd, 2 * LANE), lambda b, p, i: (b, i, p))
    kvd_spec = pl.BlockSpec((1, s, 2 * LANE), lambda b, p, i: (b, 0, p), pipeline_mode=pl.Buffered(1))
    o_d = _pair_call(functools.partial(_flash_mla_kernel, tq=tqd, tk=min(ATTN_TK, tqd)), nb, s, tqd,
                     [qd_spec, kvd_spec, kvd_spec], (qd, kd, vd), "attn_mla")
    return o_a, o_b, o_c, o_d


def _two_block_softmax(s1, s2, v1_t, v2):
    m = jnp.maximum(jnp.max(s1, axis=1, keepdims=True), jnp.max(s2, axis=1, keepdims=True))
    p1 = jnp.exp2(s1 - m)
    p2 = jnp.exp2(s2 - m)
    l = jnp.sum(p1, axis=1, keepdims=True) + jnp.sum(p2, axis=1, keepdims=True)
    return (_nt_dot(p1.astype(BF16), v1_t) + _dot(p2.astype(BF16), v2)) / l


def _sample_kernel(qa_ref, ka_ref, va_ref, qb_ref, kb_ref, vb_ref, qc_ref, kc_ref, vc_ref, qd_ref, kd_ref, vd_ref,
                   cak_ref, cav_ref, cbk_ref, cbv_ref, cck_ref, ccv_ref, cckv_ref, ckr_ref,
                   ba_c_ref, ba_n_ref, fq_ref, fkc_ref, fkn_ref,
                   u_ref, un_ref, wukt_ref, wuv_ref, gk_ref,
                   oa_ref, ob_ref, oc_ref, od_ref, *, t, past, ks):
    row = lax.broadcasted_iota(jnp.int32, (t, t), 0)
    col = lax.broadcasted_iota(jnp.int32, (t, t), 1)
    causal_bias = jnp.where(col <= row, 0.0, NEG_INF)
    chunk_bias = jnp.where((past + col) // CHUNK <= (past + row) // CHUNK, 0.0, NEG_INF)
    strict = col < row

    ckv_c = cckv_ref[0].astype(BF16)
    kr_t = ckr_ref[0].astype(BF16)
    pad_t = jnp.zeros((LANE - MLA_NOPE - MLA_ROPE, past), BF16)

    for h in range(N_HEADS):
        hs = slice(LANE * h, LANE * h + LANE)
        ps = slice(LANE * (h // 2), LANE * (h // 2) + LANE)

        q = qa_ref[0][:, hs]
        s1 = _dot(q, cak_ref[0, ps, :].astype(BF16)) + ba_c_ref[h]
        s2 = _nt_dot(q, ka_ref[0][:, ps]) + ba_n_ref[h]
        oa_ref[0, :, hs] = _two_block_softmax(s1, s2, cav_ref[0, ps, :].astype(BF16), va_ref[0][:, ps])

        q = qb_ref[0][:, hs]
        c, w = _stick_weights(_nt_dot(q, kb_ref[0][:, ps]), jnp.zeros((t, 1), F32), un_ref[...], strict)
        acc = _dot(w.astype(BF16), vb_ref[0][:, ps])

        def b_live(state):
            return jnp.logical_and(state[0] >= 0, state[1] > DEAD_LOG2)

        def b_older(state, q=q, ps=ps):
            jb, _, c, acc = state
            off = pl.multiple_of(jb * ks, ks)
            kb_t = cbk_ref[0, ps, pl.ds(off, ks)].astype(BF16)
            vb_t = cbv_ref[0, ps, pl.ds(off, ks)].astype(BF16)
            c, w = _stick_weights(_dot(q, kb_t), c, u_ref[...], None)
            return jb - 1, jnp.max(c), c, acc + _nt_dot(w.astype(BF16), vb_t)

        _, _, _, acc = lax.while_loop(b_live, b_older, (past // ks - 1, jnp.max(c), c, acc))
        ob_ref[0, :, hs] = acc

        q = qc_ref[0][:, hs]
        fq = fq_ref[0][:, h:h + 1] * LOG2E
        s1 = _dot(q, cck_ref[0, ps, :].astype(BF16)) + (fq - fkc_ref[0, h:h + 1, :] * LOG2E)
        s2 = _nt_dot(q, kc_ref[0][:, ps]) + (fq - fkn_ref[0, h:h + 1, :] * LOG2E) + causal_bias
        oc_ref[0, :, hs] = _two_block_softmax(s1, s2, ccv_ref[0, ps, :].astype(BF16), vc_ref[0][:, ps])

        q = qd_ref[0][:, hs]
        kn_t = _nt_dot(wukt_ref[MLA_NOPE * h: MLA_NOPE * (h + 1), :], ckv_c)
        ss = jnp.sum(kn_t * kn_t, axis=0, keepdims=True)
        kn_t = kn_t * lax.rsqrt(ss * (1.0 / MLA_NOPE) + NORM_EPS) * gk_ref[...]
        k_t = jnp.concatenate([kn_t.astype(BF16), kr_t, pad_t], axis=0)
        s1 = _dot(q, k_t)
        s2 = _nt_dot(q, kd_ref[0][:, hs]) + chunk_bias
        v_c = _dot(ckv_c, wuv_ref[:, hs]).astype(BF16)
        m = jnp.maximum(jnp.max(s1, axis=1, keepdims=True), jnp.max(s2, axis=1, keepdims=True))
        p1 = jnp.exp2(s1 - m)
        p2 = jnp.exp2(s2 - m)
        l = jnp.sum(p1, axis=1, keepdims=True) + jnp.sum(p2, axis=1, keepdims=True)
        od_ref[0, :, hs] = (_dot(p1.astype(BF16), v_c) + _dot(p2.astype(BF16), vd_ref[0][:, hs])) / l


def _channel_major(cache):
    depth, nb, rows = cache.shape[:3]
    return jnp.transpose(cache, (0, 1, 3, 4, 2)).reshape(depth, nb, GROUP_W, rows)


def _sample_attention(pr, caches, layer, pw, w_uk, k_gain, rel_bias, nb, t):
    a_k, a_v, b_k, b_v, c_k, c_v, c_lf, d_ckv, d_kr = caches
    wuv = pw[4]
    past = b_k.shape[2]
    win = a_k.shape[2]
    ks = STICK_KS
    assert past % ks == 0
    r3 = lambda a: a.reshape(nb, t, a.shape[-1])
    news = [r3(a) for a in pr[8:20]]

    qpos = past + np.arange(t)
    kpos = past - win + np.arange(win + t)
    qc, kc = qpos // CHUNK, kpos // CHUNK
    ok = (kc[None, :] <= qc[:, None]) & (kc[None, :] >= qc[:, None] - A_LEFT_CHUNKS)
    ba = _rel_bias_tile(rel_bias, t, win + t, win, ok)
    ba_c, ba_n = ba[:, :, :win], ba[:, :, win:]

    clf_new = jnp.swapaxes(r3(pr[7])[..., 0:N_HEADS], 1, 2)
    lf_c = jnp.swapaxes(c_lf[layer], 1, 2).astype(F32)
    fk = jnp.cumsum(jnp.concatenate([lf_c, clf_new], axis=2), axis=2)
    fk_c, fk_n = fk[:, :, :past], fk[:, :, past:]
    fq = jnp.swapaxes(fk_n, 1, 2)

    wukt = w_uk.T.astype(BF16)
    gk = k_gain.reshape(MLA_NOPE, 1)
    u, un = _strict_upper(ks), _strict_upper(t)

    per_b = lambda shape: pl.BlockSpec((1,) + shape, lambda b: (b,) + (0,) * len(shape))
    per_lb = lambda shape: pl.BlockSpec((None, 1) + shape, lambda b: (layer, b) + (0,) * len(shape))
    const = lambda a: pl.BlockSpec(a.shape, lambda b: (0,) * a.ndim)
    cache_ops = [_channel_major(c) for c in (a_k, a_v, b_k, b_v, c_k, c_v)] + [d_ckv, jnp.swapaxes(d_kr, 2, 3)]
    operands = news + cache_ops + [ba_c, ba_n, fq, fk_c, fk_n, u, un, wukt, wuv, gk]
    in_specs = [per_b(a.shape[1:]) for a in news] + [per_lb(a.shape[2:]) for a in cache_ops] + \
               [const(ba_c), const(ba_n)] + [per_b(a.shape[1:]) for a in (fq, fk_c, fk_n)] + \
               [const(a) for a in (u, un, wukt, wuv, gk)]
    out_shape = [jax.ShapeDtypeStruct((nb, t, N_HEADS * LANE), F32)] * 4
    out_specs = [per_b((t, N_HEADS * LANE))] * 4
    return pl.pallas_call(
        functools.partial(_sample_kernel, t=t, past=past, ks=ks), grid=(nb,),
        in_specs=in_specs, out_specs=out_specs, out_shape=out_shape,
        compiler_params=_cparams(1), name="attn_sample",
    )(*operands)


def _out_kernel(oa_ref, ob_ref, oc_ref, od_ref, h_ref, gg_ref, wo_ref, nf_ref,
                wr1_ref, wr2_ref, br_ref, tri_ref, cnt0_ref, h1_ref, hn_ref, route_ref, cnt_ref):
    rows = h_ref.shape[0]
    low = lax.broadcasted_iota(jnp.int32, (rows, LANE), 1) < HEAD_DIM
    h1 = h_ref[...]
    for g, o_ref in enumerate((oa_ref, ob_ref, oc_ref, od_ref)):
        p0 = jnp.where(low, o_ref[:, 0:LANE], o_ref[:, LANE:2 * LANE])
        p1 = jnp.where(low, o_ref[:, 2 * LANE:3 * LANE], o_ref[:, 3 * LANE:4 * LANE])
        og = jnp.concatenate([p0, p1], axis=1)
        ms = jnp.mean(og * og, axis=-1, keepdims=True)
        y = (og * lax.rsqrt(ms + NORM_EPS) * gg_ref[g:g + 1, :]).astype(BF16)
        h1 = h1 + _dot(y, wo_ref[GROUP_W * g: GROUP_W * (g + 1), :])
    h1_ref[...] = h1
    ms = jnp.mean(h1 * h1, axis=-1, keepdims=True)
    hn = h1 * lax.rsqrt(ms + NORM_EPS) * nf_ref[...]
    hn_ref[...] = hn.astype(BF16)
    a1, a2 = _split2(hn)
    w1, w2 = wr1_ref[...], wr2_ref[...]
    logits = _dot(a1, w1) + _dot(a1, w2) + _dot(a2, w1) + br_ref[...]

    @pl.when(pl.program_id(0) == 0)
    def _():
        cnt_ref[...] = cnt0_ref[...]

    lane = lax.broadcasted_iota(jnp.int32, (rows, LANE), 1)
    x = jnp.where(lane < N_EXP, logits, NEG_INF)
    picks, tops, ids = [], [], []
    for j in range(TOP_K):
        top = jnp.max(x, axis=1, keepdims=True)
        idx = jnp.min(jnp.where(x == top, lane, LANE), axis=1, keepdims=True)
        hit = lane == idx
        x = jnp.where(hit, NEG_INF, x)
        picks.append(hit)
        tops.append(top)
        ids.append(idx)
    chosen = functools.reduce(jnp.logical_or, picks)
    before = _dot(tri_ref[...], jnp.where(chosen, 1.0, 0.0).astype(BF16)) + cnt_ref[...]
    exps = [jnp.exp(top - tops[0]) for top in tops]
    total = functools.reduce(jnp.add, exps)
    route = jnp.zeros((rows, LANE), F32)
    for j in range(TOP_K):
        rank = jnp.sum(jnp.where(picks[j], before, 0.0), axis=1, keepdims=True)
        route = jnp.where(lane == j, ids[j].astype(F32), route)
        route = jnp.where(lane == TOP_K + j, exps[j] / total, route)
        route = jnp.where(lane == 2 * TOP_K + j, rank, route)
    route_ref[...] = route
    cnt_ref[...] = cnt_ref[...] + jnp.sum(jnp.where(chosen, 1.0, 0.0), axis=0, keepdims=True)


def _out_project(outs, h, group_gain, w_out_b, norm_ffn, wr_parts, br_pad, counts0):
    t, d = h.shape
    ts = DENSE_ROWS
    assert t % ts == 0
    row = lambda i: (i, 0)
    full = lambda i: (0, 0)
    idx = np.arange(ts)
    tri = jnp.asarray((idx[:, None] > idx[None, :]).astype(np.float32), BF16)
    o_spec = pl.BlockSpec((ts, N_HEADS * LANE), row)
    in_specs = [o_spec] * 4 + [pl.BlockSpec((ts, d), row), pl.BlockSpec(group_gain.shape, full),
                               pl.BlockSpec(w_out_b.shape, full), pl.BlockSpec((1, d), full)] + \
               [pl.BlockSpec((d, LANE), full)] * len(wr_parts) + [pl.BlockSpec((1, LANE), full)] + \
               [pl.BlockSpec((ts, ts), full), pl.BlockSpec((1, LANE), full)]
    out_shape = [jax.ShapeDtypeStruct((t, d), F32), jax.ShapeDtypeStruct((t, d), BF16),
                 jax.ShapeDtypeStruct((t, LANE), F32), jax.ShapeDtypeStruct((1, LANE), F32)]
    out_specs = [pl.BlockSpec((ts, d), row), pl.BlockSpec((ts, d), row), pl.BlockSpec((ts, LANE), row),
                 pl.BlockSpec((1, LANE), full)]
    return pl.pallas_call(
        _out_kernel, grid=(t // ts,), in_specs=in_specs, out_specs=out_specs, out_shape=out_shape,
        compiler_params=_cparams(1), name="out_proj",
    )(*outs, h, group_gain, w_out_b, norm_ffn.reshape(1, d), *wr_parts, br_pad, tri, counts0)


def _expert_kernel(be_ref, fe_ref, nu_ref, x_ref, wu_ref, bu_ref, wd_ref, bd_ref, *rest):
    y_ref, wub_ref, wdb_ref = rest[-3:]
    i = pl.program_id(0)
    used = i < nu_ref[0]

    @pl.when(jnp.logical_and(used, fe_ref[i] == 1))
    def _():
        wub_ref[...] = wu_ref[0, 0].astype(BF16)
        wdb_ref[...] = wd_ref[0, 0].astype(BF16)

    @pl.when(used)
    def _():
        u = _dot(x_ref[...], wub_ref[...]) + bu_ref[0, 0]
        glu = jnp.minimum(u[:, :D_FF], SWIGLU_LIMIT)
        lin = jnp.clip(u[:, D_FF:], -SWIGLU_LIMIT, SWIGLU_LIMIT)
        act = glu * jax.nn.sigmoid(SWIGLU_ALPHA * glu) * (lin + 1.0)
        y_ref[...] = (_dot(act.astype(BF16), wdb_ref[...]) + bd_ref[0, 0]).astype(y_ref.dtype)


def _expert_ffn(y_prev, blk0, n_blk_all, x_rows, blk_exp, blk_first, n_used, layer, w_up, b_up, w_down, b_down):
    rows, d = x_rows.shape
    n_blk = rows // MOE_BLOCK
    last = lambda i, nu: jnp.maximum(jnp.minimum(i, nu[0] - 1), 0)
    w_idx = lambda i, be, fe, nu: (layer, be[last(i, nu)], 0, 0)
    in_specs = [pl.BlockSpec((MOE_BLOCK, d), lambda i, be, fe, nu: (last(i, nu), 0)),
                pl.BlockSpec((1, 1, d, 2 * D_FF), w_idx), pl.BlockSpec((1, 1, 1, 2 * D_FF), w_idx),
                pl.BlockSpec((1, 1, D_FF, d), w_idx), pl.BlockSpec((1, 1, 1, d), w_idx)]
    depth = w_up.shape[0]
    operands = [blk_exp, blk_first, n_used, x_rows, w_up, b_up.reshape(depth, N_EXP, 1, 2 * D_FF),
                w_down, b_down.reshape(depth, N_EXP, 1, d)]
    aliases = {}
    if y_prev is not None:
        in_specs.append(pl.BlockSpec(memory_space=pl.ANY))
        aliases = {len(operands): 0}
        operands.append(y_prev)
    grid_spec = pltpu.PrefetchScalarGridSpec(
        num_scalar_prefetch=3, grid=(n_blk,), in_specs=in_specs,
        out_specs=pl.BlockSpec((MOE_BLOCK, d), lambda i, be, fe, nu: (blk0 + last(i, nu), 0)),
        scratch_shapes=[pltpu.VMEM((d, 2 * D_FF), BF16), pltpu.VMEM((D_FF, d), BF16)])
    return pl.pallas_call(
        _expert_kernel, grid_spec=grid_spec,
        out_shape=jax.ShapeDtypeStruct((n_blk_all * MOE_BLOCK, d), BF16), input_output_aliases=aliases,
        compiler_params=_cparams(1), name="expert_ffn",
    )(*operands)


SCATTER_UNROLL = 8


def _row_token_kernel(pad_ref, dest_ref, rt_ref, *, chunk):
    i = pl.program_id(0)

    @pl.when(i == 0)
    def _():
        for e in range(N_EXP + 1):
            def clear(r, carry):
                rt_ref[r] = 0
                return carry
            lax.fori_loop(pad_ref[2 * e], pad_ref[2 * e + 1], clear, 0)

    def place(g, tok):
        k = g * SCATTER_UNROLL
        for j in range(SCATTER_UNROLL):
            rt_ref[dest_ref[k + j]] = tok + j // TOP_K
        return tok + SCATTER_UNROLL // TOP_K
    lax.fori_loop(0, chunk // SCATTER_UNROLL, place, i * (chunk // TOP_K))


def _row_tokens(dest, pad_ranges, n_rows):
    n = dest.shape[0]
    chunk = int(np.gcd(n, SCATTER_CHUNK))
    assert chunk % SCATTER_UNROLL == 0 and SCATTER_UNROLL % TOP_K == 0
    smem = pl.BlockSpec(memory_space=pltpu.SMEM)
    return pl.pallas_call(
        functools.partial(_row_token_kernel, chunk=chunk), grid=(n // chunk,),
        in_specs=[smem, pl.BlockSpec((chunk,), lambda i: (i,), memory_space=pltpu.SMEM)],
        out_specs=smem, out_shape=jax.ShapeDtypeStruct((n_rows,), jnp.int32),
        compiler_params=_cparams(1), name="row_tokens",
    )(pad_ranges, dest)


def _moe(hn, route, counts, layer, w_up, b_up, w_down, b_down):
    n_tok, d = hn.shape
    top_i = route[:, 0:TOP_K].astype(jnp.int32)
    gates = route[:, TOP_K:2 * TOP_K]
    rank = route[:, 2 * TOP_K:3 * TOP_K].astype(jnp.int32)
    n = n_tok * TOP_K
    padded = (counts + MOE_BLOCK - 1) // MOE_BLOCK * MOE_BLOCK
    p_end = jnp.cumsum(padded)
    p_start = p_end - padded
    dest = (p_start[top_i] + rank).reshape(-1)
    n_blk = -(-n // MOE_BLOCK) + N_EXP
    rows = n_blk * MOE_BLOCK
    pad_lo = jnp.concatenate([p_start + counts, p_end[-1:]])
    pad_hi = jnp.concatenate([p_end, jnp.full((1,), rows, p_end.dtype)])
    pad_ranges = jnp.stack([pad_lo, pad_hi], axis=1).reshape(-1).astype(jnp.int32)
    row_tok = _row_tokens(dest.astype(jnp.int32), pad_ranges, rows)
    blk_start = jnp.arange(n_blk, dtype=jnp.int32) * MOE_BLOCK
    blk_exp = jnp.sum((p_end[None, :] <= blk_start[:, None]).astype(jnp.int32), axis=1)
    blk_exp = jnp.minimum(blk_exp, N_EXP - 1)
    n_used = (p_end[-1] // MOE_BLOCK).astype(jnp.int32)
    n_grp = MOE_GROUPS if n_blk % MOE_GROUPS == 0 else 1
    nbg = n_blk // n_grp
    y = None
    for g in range(n_grp):
        be = blk_exp[g * nbg:(g + 1) * nbg]
        fe = jnp.concatenate([jnp.ones((1,), jnp.int32), (be[1:] != be[:-1]).astype(jnp.int32)])
        x_rows = hn.at[row_tok[g * nbg * MOE_BLOCK:(g + 1) * nbg * MOE_BLOCK]].get(mode='promise_in_bounds')
        y = _expert_ffn(y, g * nbg, n_blk, x_rows, be, fe, (n_used - g * nbg).reshape(1), layer,
                        w_up, b_up, w_down, b_down)
    picked = y.at[dest.reshape(n_tok, TOP_K).T].get(mode='promise_in_bounds')
    return picked, gates


def _ple_kernel(h_ref, y_ref, g_ref, p_ref, np_ref, wg_ref, wp_ref, o_ref):
    h2 = h_ref[...]
    g = g_ref[...]
    for j in range(TOP_K):
        h2 = h2 + y_ref[j].astype(F32) * g[:, j:j + 1]
    ms = jnp.mean(h2 * h2, axis=-1, keepdims=True)
    hn = (h2 * lax.rsqrt(ms + NORM_EPS) * np_ref[...]).astype(BF16)
    gate = jax.nn.sigmoid(_dot(hn, wg_ref[...]))
    o_ref[...] = h2 + gate * _dot(p_ref[...].astype(BF16), wp_ref[...])


def _ple(h1, picked, gates, row0, p, layer, norm_ple, wg_b, wp_b):
    t, d = h1.shape
    ts = PLE_ROWS
    assert t % ts == 0 and row0 % ts == 0
    off = row0 // ts
    row = lambda i: (i, 0)
    full = lambda i: (0, 0)
    return pl.pallas_call(
        _ple_kernel, grid=(t // ts,),
        in_specs=[pl.BlockSpec((ts, d), row), pl.BlockSpec((TOP_K, ts, d), lambda i: (0, i + off, 0)),
                  pl.BlockSpec((ts, TOP_K), lambda i: (i + off, 0)),
                  pl.BlockSpec((None, ts, p.shape[2]), lambda i: (layer, i, 0)),
                  pl.BlockSpec((1, d), full), pl.BlockSpec(wg_b.shape, full), pl.BlockSpec(wp_b.shape, full)],
        out_specs=pl.BlockSpec((ts, d), row), out_shape=jax.ShapeDtypeStruct((t, d), F32),
        compiler_params=_cparams(1), name="ple",
    )(h1, picked, gates, p, norm_ple.reshape(1, d), wg_b, wp_b)


def kernel(x_prompt, x_sample, p_prompt, p_sample, cache_a_k, cache_a_v, cache_b_k, cache_b_v, cache_c_k, cache_c_v, cache_c_logf, cache_d_ckv, cache_d_krope, norm_mix, w_in, b_forget, qk_gain, rope_gain, kv_gain, w_uk, w_uv, rel_bias, group_gain, w_out, norm_ffn, w_router, b_router, w_up, b_up, w_down, b_down, norm_ple, w_ple_gate, w_ple_proj):
    nb, s, d = x_prompt.shape
    nd, t, _ = x_sample.shape
    depth = w_in.shape[0]
    past = cache_b_k.shape[2]
    assert PROJ_ROWS % t == 0 and s % PROJ_ROWS == 0

    tab_p = _rope_tables(jnp.arange(s))
    tab_s = _rope_tables(past + jnp.arange(PROJ_ROWS) % t)
    hp = x_prompt.reshape(nb * s, d)
    hs = x_sample.reshape(nd * t, d)
    st_p, st_s = [], []
    keep = min(A_WINDOW, s)
    layered_state = None
    for i in range(depth):
        pw = _proj_weights(w_in[i], b_forget[i], qk_gain[i], rope_gain[i], kv_gain[i], w_uk[i], w_uv[i])
        pr_p = _project(hp, norm_mix[i], pw, tab_p, s // PROJ_ROWS, (depth, i, nb, layered_state))
        layered_state = [pr_p[slot] for slot in STATE_SLOTS]
        pr_s = _project(hs, norm_mix[i], pw, tab_s, 1)
        outs_p = _prompt_attention(pr_p, pr_p[7][i], nb, s, rel_bias[i], qk_gain[i, 3])
        caches = (cache_a_k, cache_a_v, cache_b_k, cache_b_v, cache_c_k, cache_c_v,
                  cache_c_logf, cache_d_ckv, cache_d_krope)
        outs_s = _sample_attention(pr_s, caches, i, pw, w_uk[i], qk_gain[i, 5], rel_bias[i], nd, t)

        def heads(a, n, rows):
            return a.reshape(n, rows, N_HEADS, HEAD_DIM)

        p3 = lambda a: a.reshape(nb, s, a.shape[-1])
        s3 = lambda a: a.reshape(nd, t, a.shape[-1])
        st_p.append((heads(p3(pr_p[0])[:, s - keep:], nb, keep), heads(p3(pr_p[1])[:, s - keep:], nb, keep)))
        ka_all = jnp.concatenate([cache_a_k[i], heads(pr_s[0], nd, t)], axis=1)[:, t:]
        va_all = jnp.concatenate([cache_a_v[i], heads(pr_s[1], nd, t)], axis=1)[:, t:]
        st_s.append((ka_all, va_all,
                     heads(pr_s[2], nd, t), heads(pr_s[3], nd, t), heads(pr_s[4], nd, t), heads(pr_s[5], nd, t),
                     s3(pr_s[7])[..., 0:N_HEADS], s3(pr_s[6]), s3(pr_s[7])[..., ROPE_LO:ROPE_LO + MLA_ROPE]))

        w_out_b = w_out[i].astype(BF16)
        wr_pad = jnp.zeros((d, LANE), F32).at[:, 0:N_EXP].set(w_router[i])
        wr_parts = _split2(wr_pad)
        br_pad = jnp.zeros((1, LANE), F32).at[0, 0:N_EXP].set(b_router[i])
        flat4 = lambda o: o.reshape(-1, N_HEADS * LANE)
        h1_p, hn_p, rt_p, cnt = _out_project([flat4(o) for o in outs_p], hp, group_gain[i], w_out_b, norm_ffn[i],
                                             wr_parts, br_pad, jnp.zeros((1, LANE), F32))
        h1_s, hn_s, rt_s, cnt = _out_project([flat4(o) for o in outs_s], hs, group_gain[i], w_out_b, norm_ffn[i],
                                             wr_parts, br_pad, cnt)

        hn_all = jnp.concatenate([hn_p, hn_s], axis=0)
        rt_all = jnp.concatenate([rt_p, rt_s], axis=0)
        picked, gates = _moe(hn_all, rt_all, cnt[0, 0:N_EXP].astype(jnp.int32), i, w_up, b_up, w_down, b_down)

        wg_b = w_ple_gate[i].astype(BF16)
        wp_b = w_ple_proj[i].astype(BF16)
        hp = _ple(h1_p, picked, gates, 0, p_prompt.reshape(depth, nb * s, -1), i, norm_ple[i], wg_b, wp_b)
        hs = _ple(h1_s, picked, gates, nb * s, p_sample.reshape(depth, nd * t, -1), i, norm_ple[i], wg_b, wp_b)

    bk_t, bv_t, ck_t, cv_t, ckv_all, misc_t = layered_state

    def from_channel_major(a):
        return jnp.transpose(a.reshape(depth, nb, N_HEADS, HEAD_DIM, s), (0, 1, 4, 2, 3))

    state_p = [jnp.stack([st[j] for st in st_p]) for j in range(2)] + \
              [from_channel_major(a) for a in (bk_t, bv_t, ck_t, cv_t)] + \
              [jnp.swapaxes(misc_t[:, :, 0:N_HEADS, :], 2, 3), ckv_all.reshape(depth, nb, s, KV_RANK),
               jnp.swapaxes(misc_t[:, :, ROPE_LO:ROPE_LO + MLA_ROPE, :], 2, 3)]
    state_s = [jnp.stack([st[j] for st in st_s]) for j in range(9)]
    return (hp.reshape(nb, s, d), hs.reshape(nd, t, d), *state_p, *state_s)
```

```python
import functools

import numpy as np
import jax
import jax.numpy as jnp
from jax import lax
from jax.experimental import pallas as pl
from jax.experimental.pallas import tpu as pltpu

F32 = jnp.float32
BF16 = jnp.bfloat16

CHUNK = 64
HEAD_DIM = 64
N_HEADS = 4
GROUP_W = 256
A_LEFT_CHUNKS = 8
A_WINDOW = A_LEFT_CHUNKS * CHUNK
REL_CLIP = 128
MLA_NOPE = 64
MLA_ROPE = 32
KV_RANK = 128
ROPE_BASE = 10000.0
N_EXP = 32
TOP_K = 4
D_FF = 1024
SWIGLU_LIMIT = 7.0
SWIGLU_ALPHA = 1.702
MOE_BLOCK = 512
NORM_EPS = 1e-6
NEG_INF = -1e30
LOG2E = 1.4426950408889634

A_Q = 0
C_F = 2304
D_Q = 2308
D_CKV = D_Q + N_HEADS * (MLA_NOPE + MLA_ROPE)
D_KR = D_CKV + KV_RANK

LANE = 128
SEG_ABC = 0
SEG_DQ = 2304
SEG_CKV = 2816
SEG_MISC = 2944
N_COLS = 3072
ROPE_LO = 64
ROPE_HALF = MLA_ROPE // 2

VMEM_LIMIT = 56 * 1024 * 1024

PROJ_ROWS = 512
ATTN_TQ = 512
ATTN_TK = 1024
MLA_TQ = 2048
STICK_KS = 256
DENSE_ROWS = 512
PLE_ROWS = 256
MOE_GROUPS = 4
SCATTER_CHUNK = 8192

DEAD_LOG2 = -160.0
BOUND_MARGIN = 1.01


def _cparams(n_axes):
    return pltpu.CompilerParams(dimension_semantics=("arbitrary",) * n_axes,
                                vmem_limit_bytes=VMEM_LIMIT)


def _nt_dot(a, b):
    return lax.dot_general(a, b, (((1,), (1,)), ((), ())), preferred_element_type=F32)


def _dot(a, b):
    return jnp.dot(a, b, preferred_element_type=F32)


def _split2(x):
    hi = x.astype(BF16)
    lo = (x - hi.astype(F32)).astype(BF16)
    return hi, lo


def _in_col_map():
    cols = np.full((N_COLS,), -1, np.int64)
    cols[0:2304] = np.arange(2304)
    for h in range(N_HEADS):
        base = D_Q + (MLA_NOPE + MLA_ROPE) * h
        cols[SEG_DQ + LANE * h: SEG_DQ + LANE * h + MLA_NOPE + MLA_ROPE] = base + np.arange(MLA_NOPE + MLA_ROPE)
    cols[SEG_CKV:SEG_CKV + KV_RANK] = D_CKV + np.arange(KV_RANK)
    cols[SEG_MISC:SEG_MISC + N_HEADS] = C_F + np.arange(N_HEADS)
    cols[SEG_MISC + ROPE_LO:SEG_MISC + ROPE_LO + MLA_ROPE] = D_KR + np.arange(MLA_ROPE)
    return cols


def _rope_tables(pos):
    inv = ROPE_BASE ** (-jnp.arange(ROPE_HALF, dtype=F32) / ROPE_HALF)
    ang = pos.astype(F32)[:, None] * inv
    cos, sin = jnp.cos(ang), jnp.sin(ang)
    n = pos.shape[0]
    one = jnp.ones((n, ROPE_LO), F32)
    z16 = jnp.zeros((n, ROPE_HALF), F32)
    z64 = jnp.zeros((n, ROPE_LO), F32)
    z32 = jnp.zeros((n, LANE - ROPE_LO - MLA_ROPE), F32)
    cos_t = jnp.concatenate([one, cos, cos, z32 + 1.0], axis=1)
    sin_a = jnp.concatenate([z64, -sin, z16, z32], axis=1)
    sin_b = jnp.concatenate([z64, z16, sin, z32], axis=1)
    return cos_t, sin_a, sin_b


def _rope(y, cos_t, sin_a, sin_b):
    left = pltpu.roll(y, LANE - ROPE_HALF, axis=1)
    right = pltpu.roll(y, ROPE_HALF, axis=1)
    return y * cos_t + left * sin_a + right * sin_b


def _head_norm(x, m_ref, gain):
    hi, lo = _split2(x * x)
    ssq = _dot(hi, m_ref[...]) + _dot(lo, m_ref[...])
    return x * lax.rsqrt(ssq * (1.0 / HEAD_DIM) + NORM_EPS) * gain


def _value_lane0(h):
    return HEAD_DIM * (h % 2)


def _ones_lane(h):
    return HEAD_DIM - _value_lane0(h)


def _store_padded_q(q_ref, q, scale):
    lane = lax.broadcasted_iota(jnp.int32, (q.shape[0], LANE), 1)
    low = lane < HEAD_DIM
    qs = q * scale
    for h in range(N_HEADS):
        pair = qs[:, LANE * (h // 2): LANE * (h // 2) + LANE]
        keep = low if h % 2 == 0 else jnp.logical_not(low)
        q_ref[:, LANE * h: LANE * h + LANE] = jnp.where(keep, pair, 0.0).astype(BF16)


def _proj_kernel(*refs, channel_major, n_alias):
    x_ref, gmix_ref, w_ref, cos_ref, sa_ref, sb_ref, gv_ref, m64_ref, wuk_ref, wuv_ref = refs[:10]
    (ak_ref, av_ref, bk_ref, bv_ref, ck_ref, cv_ref, ckv_ref, misc_ref,
     qa_ref, ka_ref, va_ref, qb_ref, kb_ref, vb_ref, qc_ref, kc_ref, vc_ref,
     qd_ref, kd_ref, vd_ref) = refs[10 + n_alias:]

    def put_state(ref, val):
        ref[...] = val.T if channel_major else val

    x = x_ref[...]
    ms = jnp.mean(x * x, axis=-1, keepdims=True)
    hn = (x * lax.rsqrt(ms + NORM_EPS) * gmix_ref[...]).astype(BF16)
    sm_scale = (HEAD_DIM ** -0.5) * LOG2E
    rows = x.shape[0]
    lane = lax.broadcasted_iota(jnp.int32, (rows, LANE), 1)
    is_nope = lane < MLA_NOPE
    is_rope = jnp.logical_and(lane >= ROPE_LO, lane < ROPE_LO + MLA_ROPE)
    cos_t, sin_a, sin_b = cos_ref[...], sa_ref[...], sb_ref[...]

    z = _dot(hn, w_ref[:, 0:768])
    aq = _head_norm(z[:, 0:256], m64_ref, gv_ref[0:1, :])
    ak = _head_norm(z[:, 256:512], m64_ref, gv_ref[1:2, :])
    av = z[:, 512:768]
    ak_ref[...] = ak
    av_ref[...] = av
    _store_padded_q(qa_ref, aq, sm_scale)
    ka_ref[...] = ak.astype(BF16)
    va_ref[...] = av.astype(BF16)

    z = _dot(hn, w_ref[:, 768:1536])
    put_state(bk_ref, z[:, 256:512])
    put_state(bv_ref, z[:, 512:768])
    _store_padded_q(qb_ref, z[:, 0:256], sm_scale)
    kb_ref[...] = z[:, 256:512].astype(BF16)
    vb_ref[...] = z[:, 512:768].astype(BF16)

    z = _dot(hn, w_ref[:, 1536:2304])
    cq = _head_norm(z[:, 0:256], m64_ref, gv_ref[2:3, :])
    ck = _head_norm(z[:, 256:512], m64_ref, gv_ref[3:4, :])
    cv = z[:, 512:768]
    put_state(ck_ref, ck)
    put_state(cv_ref, cv)
    _store_padded_q(qc_ref, cq, sm_scale)
    kc_ref[...] = ck.astype(BF16)
    vc_ref[...] = cv.astype(BF16)

    z = _dot(hn, w_ref[:, SEG_CKV:N_COLS])
    zc = z[:, 0:KV_RANK]
    ckv = zc * lax.rsqrt(jnp.mean(zc * zc, axis=-1, keepdims=True) + NORM_EPS) * gv_ref[7:8, 0:LANE]
    ckv_ref[...] = ckv
    zm = z[:, KV_RANK:2 * KV_RANK]
    ssr = jnp.sum(jnp.where(is_rope, zm * zm, 0.0), axis=-1, keepdims=True)
    kr = zm * lax.rsqrt(ssr * (1.0 / MLA_ROPE) + NORM_EPS) * gv_ref[6:7, 0:LANE]
    kr = _rope(kr, cos_t, sin_a, sin_b)
    zf = zm + gv_ref[8:9, 0:LANE]
    clf = jnp.minimum(zf, 0.0) - jnp.log1p(jnp.exp(-jnp.abs(zf)))
    put_state(misc_ref, jnp.where(lane < N_HEADS, clf, kr))

    ckv_b = ckv.astype(BF16)
    kn = _dot(ckv_b, wuk_ref[...])
    for h in range(N_HEADS):
        hs = slice(LANE * h, LANE * h + LANE)
        ones = (lane == _ones_lane(h)).astype(F32)
        vd_ref[:, hs] = (_dot(ckv_b, wuv_ref[:, hs]) + ones).astype(BF16)
    for h in range(N_HEADS):
        xh = kn[:, LANE * h: LANE * h + LANE]
        ss = jnp.sum(xh * xh, axis=-1, keepdims=True)
        yh = xh * lax.rsqrt(ss * (1.0 / MLA_NOPE) + NORM_EPS) * gv_ref[5:6, 0:LANE]
        kd_ref[:, LANE * h: LANE * h + LANE] = (yh + kr).astype(BF16)

    z = _dot(hn, w_ref[:, SEG_DQ:SEG_CKV])
    d_scale = ((MLA_NOPE + MLA_ROPE) ** -0.5) * LOG2E
    for h in range(N_HEADS):
        xh = z[:, LANE * h: LANE * h + LANE]
        x2 = xh * xh
        ssn = jnp.sum(jnp.where(is_nope, x2, 0.0), axis=-1, keepdims=True)
        ssr = jnp.sum(jnp.where(is_rope, x2, 0.0), axis=-1, keepdims=True)
        rn = lax.rsqrt(ssn * (1.0 / MLA_NOPE) + NORM_EPS)
        rr = lax.rsqrt(ssr * (1.0 / MLA_ROPE) + NORM_EPS)
        yh = xh * jnp.where(is_nope, rn, rr) * gv_ref[4:5, 0:LANE]
        yh = _rope(yh, cos_t, sin_a, sin_b)
        qd_ref[:, LANE * h: LANE * h + LANE] = (yh * d_scale).astype(BF16)


def _proj_weights(w_in, b_forget, qk_gain, rope_gain, kv_gain, w_uk, w_uv):
    cols = _in_col_map()
    valid = jnp.asarray(cols >= 0)
    w = jnp.where(valid[None, :], w_in[:, np.maximum(cols, 0)], 0.0).astype(BF16)

    def tile4(g):
        return jnp.tile(g, N_HEADS)

    gv = jnp.zeros((16, GROUP_W), F32)
    gv = gv.at[0].set(tile4(qk_gain[0])).at[1].set(tile4(qk_gain[1]))
    gv = gv.at[2].set(tile4(qk_gain[2])).at[3].set(tile4(qk_gain[3]))
    gv = gv.at[4, 0:MLA_NOPE].set(qk_gain[4]).at[4, ROPE_LO:ROPE_LO + MLA_ROPE].set(rope_gain[0])
    gv = gv.at[5, 0:MLA_NOPE].set(qk_gain[5])
    gv = gv.at[6, ROPE_LO:ROPE_LO + MLA_ROPE].set(rope_gain[1])
    gv = gv.at[7, 0:KV_RANK].set(kv_gain)
    gv = gv.at[8, 0:N_HEADS].set(b_forget)
    head = np.arange(GROUP_W) // HEAD_DIM
    m64 = jnp.asarray((head[:, None] == head[None, :]).astype(np.float32), BF16)
    wuk = jnp.zeros((KV_RANK, N_HEADS * LANE), F32)
    for h in range(N_HEADS):
        wuk = wuk.at[:, LANE * h: LANE * h + MLA_NOPE].set(w_uk[:, MLA_NOPE * h: MLA_NOPE * (h + 1)])
    wuv = jnp.zeros((KV_RANK, N_HEADS * LANE), F32)
    for h in range(N_HEADS):
        lo = LANE * h + _value_lane0(h)
        wuv = wuv.at[:, lo: lo + HEAD_DIM].set(w_uv[:, HEAD_DIM * h: HEAD_DIM * (h + 1)])
    return w, gv, m64, wuk.astype(BF16), wuv.astype(BF16)


STATE_SLOTS = (2, 3, 4, 5, 6, 7)


def _project(x, gmix, pw, tables, n_tab_blocks, layered=None):
    w, gv, m64, wuk, wuv = pw
    t, d = x.shape
    ts = PROJ_ROWS
    assert t % ts == 0
    row = lambda i: (i, 0)
    full = lambda i: (0, 0)
    tab = lambda i: (i % n_tab_blocks, 0)
    f32_w = [GROUP_W] * 6 + [KV_RANK, LANE]
    bf_w = [512, 256, 256, 512, 256, 256, 512, 256, 256, 512, 512, 512]
    out_shape = [jax.ShapeDtypeStruct((t, c), F32) for c in f32_w] + \
                [jax.ShapeDtypeStruct((t, c), BF16) for c in bf_w]
    out_specs = [pl.BlockSpec((ts, c), row) for c in f32_w + bf_w]
    in_specs = [pl.BlockSpec((ts, d), row), pl.BlockSpec((1, d), full), pl.BlockSpec((d, N_COLS), full),
                pl.BlockSpec((ts, LANE), tab), pl.BlockSpec((ts, LANE), tab), pl.BlockSpec((ts, LANE), tab),
                pl.BlockSpec(gv.shape, full), pl.BlockSpec(m64.shape, full),
                pl.BlockSpec(wuk.shape, full), pl.BlockSpec(wuv.shape, full)]
    operands = [x, gmix.reshape(1, d), w, *tables, gv, m64, wuk, wuv]
    aliases = {}
    if layered is not None:
        depth, layer, nb, earlier = layered
        s = t // nb
        nt = s // ts
        for slot in STATE_SLOTS:
            c = f32_w[slot]
            if slot == 6:
                out_shape[slot] = jax.ShapeDtypeStruct((depth, t, c), F32)
                out_specs[slot] = pl.BlockSpec((None, ts, c), lambda i: (layer, i, 0))
            else:
                out_shape[slot] = jax.ShapeDtypeStruct((depth, nb, c, s), F32)
                out_specs[slot] = pl.BlockSpec((None, None, c, ts), lambda i: (layer, i // nt, 0, i % nt))
        if earlier is not None:
            for k, slot in enumerate(STATE_SLOTS):
                aliases[len(operands)] = slot
                in_specs.append(pl.BlockSpec(memory_space=pl.ANY))
                operands.append(earlier[k])
    kern = functools.partial(_proj_kernel, channel_major=layered is not None, n_alias=len(aliases))
    return pl.pallas_call(
        kern, grid=(t // ts,), in_specs=in_specs, out_specs=out_specs, out_shape=out_shape,
        input_output_aliases=aliases, compiler_params=_cparams(1), name="proj",
    )(*operands)


def _softmax_step(q, kb, vb, carry, bias=None, ok=None):
    m, l, acc = carry
    s = _nt_dot(q, kb)
    if bias is not None:
        s = s + bias
    if ok is not None:
        s = jnp.where(ok, s, NEG_INF)
    m_new = jnp.maximum(m, jnp.max(s, axis=1, keepdims=True))
    alpha = jnp.exp2(m - m_new)
    p = jnp.exp2(s - m_new)
    l = alpha * l + jnp.sum(p, axis=1, keepdims=True)
    acc = alpha * acc + _dot(p.astype(BF16), vb)
    return m_new, l, acc


def _softmax_init(tq):
    return (jnp.full((tq, 1), NEG_INF, F32), jnp.zeros((tq, 1), F32), jnp.zeros((tq, LANE), F32))


def _pick_lane(block, h):
    col = lax.broadcasted_iota(jnp.int32, block.shape, 1)
    return jnp.sum(jnp.where(col == h, block, 0.0), axis=1, keepdims=True)


def _stack_pair(q_ref):
    return jnp.concatenate([q_ref[0, :, 0:LANE], q_ref[0, :, LANE:2 * LANE]], axis=0)


def _store_pair(o_ref, o, tq):
    o_ref[0, :, 0:LANE] = o[:tq].astype(o_ref.dtype)
    o_ref[0, :, LANE:2 * LANE] = o[tq:].astype(o_ref.dtype)


def _flash_forget_kernel(kmax_ref, decay_ref, q_ref, k_ref, v_ref, fq_ref, fk_ref, o_ref, *, tq, n_blk):
    b = pl.program_id(0)
    pair = pl.program_id(1)
    qi = pl.program_id(2)
    q = _stack_pair(q_ref)
    fq = jnp.concatenate([_pick_lane(fq_ref[0], 2 * pair), _pick_lane(fq_ref[0], 2 * pair + 1)], axis=0) * LOG2E

    def step(j, carry, diag):
        off = pl.multiple_of(j * tq, tq)
        kb = k_ref[0, pl.ds(off, tq), :]
        vb = v_ref[0, pl.ds(off, tq), :]
        fk = jnp.concatenate([jnp.broadcast_to(fk_ref[0, 0, pl.ds(j, 1), :], (tq, tq)),
                              jnp.broadcast_to(fk_ref[0, 1, pl.ds(j, 1), :], (tq, tq))], axis=0)
        ok = None
        if diag:
            row = lax.broadcasted_iota(jnp.int32, (2 * tq, tq), 0) % tq
            col = lax.broadcasted_iota(jnp.int32, (2 * tq, tq), 1)
            ok = col <= row
        return _softmax_step(q, kb, vb, carry, fq - fk * LOG2E, ok)

    carry = step(qi, _softmax_init(2 * tq), True)
    qf = q.astype(F32)
    q_norm = jnp.sqrt(jnp.sum(qf * qf, axis=1, keepdims=True))
    head0 = b * N_HEADS + 2 * pair
    room = q_norm * jnp.where(lax.broadcasted_iota(jnp.int32, q_norm.shape, 0) < tq,
                              kmax_ref[head0], kmax_ref[head0 + 1]) + fq - carry[0]
    slack0 = jnp.max(room[:tq]) - DEAD_LOG2
    slack1 = jnp.max(room[tq:]) - DEAD_LOG2

    def live(state):
        j = jnp.maximum(state[0], 0)
        alive = jnp.logical_or(slack0 + decay_ref[head0 * n_blk + j] >= 0.0,
                               slack1 + decay_ref[(head0 + 1) * n_blk + j] >= 0.0)
        return jnp.logical_and(state[0] >= 0, alive)

    def older(state):
        j = state[0]
        return (j - 1,) + step(j, state[1:], False)

    _, m, l, acc = lax.while_loop(live, older, (qi - 1,) + carry)
    _store_pair(o_ref, acc / l, tq)


def _flash_mla_kernel(q_ref, k_ref, v_ref, o_ref, *, tq, tk):
    qi = pl.program_id(2)
    n_sub = tq // tk

    def step(hh, j, carry, row0=None):
        m, acc = carry
        hs = slice(LANE * hh, LANE * hh + LANE)
        off = pl.multiple_of(j * tk, tk)
        qs = q_ref[0, :, hs] if row0 is None else q_ref[0, row0:, hs]
        s = _nt_dot(qs, k_ref[0, pl.ds(off, tk), hs])
        if row0 is not None:
            row = lax.broadcasted_iota(jnp.int32, s.shape, 0)
            col = lax.broadcasted_iota(jnp.int32, s.shape, 1)
            s = jnp.where(col // CHUNK <= row // CHUNK, s, NEG_INF)
        m_new = jnp.maximum(m, jnp.max(s, axis=1, keepdims=True))
        p = jnp.exp2(s - m_new)
        acc = jnp.exp2(m - m_new) * acc + _dot(p.astype(BF16), v_ref[0, pl.ds(off, tk), hs])
        return m_new, acc

    def both(j, carry):
        return step(0, j, carry[0]), step(1, j, carry[1])

    init = (jnp.full((tq, 1), NEG_INF, F32), jnp.zeros((tq, LANE), F32))
    carry = lax.fori_loop(0, qi * n_sub, both, (init, init))
    for hh in range(2):
        m, acc = carry[hh]
        for r in range(n_sub):
            row0 = r * tk
            m_r, acc_r = step(hh, qi * n_sub + r, (m[row0:], acc[row0:]), row0)
            m = m_r if r == 0 else jnp.concatenate([m[:row0], m_r], axis=0)
            acc = acc_r if r == 0 else jnp.concatenate([acc[:row0], acc_r], axis=0)
        ones_at = _ones_lane(hh)
        o_ref[0, :, LANE * hh: LANE * hh + LANE] = (acc / acc[:, ones_at:ones_at + 1]).astype(o_ref.dtype)


def _flash_stick_kernel(q_ref, k_ref, v_ref, u_ref, o_ref, *, tq, ks):
    qi = pl.program_id(2)
    q = _stack_pair(q_ref)
    n_sub = tq // ks

    def step(jb, carry, diag):
        c, acc = carry
        off = pl.multiple_of(jb * ks, ks)
        kb = k_ref[0, pl.ds(off, ks), :]
        vb = v_ref[0, pl.ds(off, ks), :]
        vis = None
        if diag:
            row = lax.broadcasted_iota(jnp.int32, (2 * tq, ks), 0) % tq + qi * tq
            col = lax.broadcasted_iota(jnp.int32, (2 * tq, ks), 1) + jb * ks
            vis = col < row
        c_new, w = _stick_weights(_nt_dot(q, kb), c, u_ref[...], vis)
        return c_new, acc + _dot(w.astype(BF16), vb)

    carry = (jnp.zeros((2 * tq, 1), F32), jnp.zeros((2 * tq, LANE), F32))
    for r in range(n_sub):
        carry = step(qi * n_sub + (n_sub - 1 - r), carry, True)

    def live(state):
        return jnp.logical_and(state[0] >= 0, state[1] > DEAD_LOG2)

    def older(state):
        c, acc = step(state[0], state[2:], False)
        return state[0] - 1, jnp.max(c), c, acc

    _, _, _, acc = lax.while_loop(live, older, (qi * n_sub - 1, jnp.max(carry[0])) + carry)
    _store_pair(o_ref, acc, tq)


def _stick_weights(z, c, u, vis):
    sp = jnp.log(1.0 + jnp.exp2(-jnp.abs(z))) * LOG2E
    log_rest = jnp.minimum(-z, 0.0) - sp
    log_beta = log_rest + z
    if vis is not None:
        log_rest = jnp.where(vis, log_rest, 0.0)
    hi, lo = _split2(log_rest)
    between = _dot(hi, u) + _dot(lo, u)
    w = jnp.exp2(log_beta + between + c)
    if vis is not None:
        w = jnp.where(vis, w, 0.0)
    return c + jnp.sum(log_rest, axis=1, keepdims=True), w


def _band_kernel(q_ref, k_ref, v_ref, bd_ref, bp_ref, o_ref, *, tq):
    qi = pl.program_id(2)
    q = _stack_pair(q_ref)
    off = pl.multiple_of(qi * tq, tq)
    carry = _softmax_step(q, k_ref[0, pl.ds(off, tq), :], v_ref[0, pl.ds(off, tq), :],
                          _softmax_init(2 * tq), bd_ref[...].reshape(2 * tq, tq))
    offp = pl.multiple_of(jnp.maximum(qi - 1, 0) * tq, tq)
    no_prev = jnp.where(qi == 0, NEG_INF, 0.0)
    m, l, acc = _softmax_step(q, k_ref[0, pl.ds(offp, tq), :], v_ref[0, pl.ds(offp, tq), :],
                              carry, bp_ref[...].reshape(2 * tq, tq) + no_prev)
    _store_pair(o_ref, acc / l, tq)


def _pair_specs(s, tq):
    q_spec = pl.BlockSpec((1, tq, 2 * LANE), lambda b, p, i: (b, i, p))
    kv_spec = pl.BlockSpec((1, s, LANE), lambda b, p, i: (b, 0, p))
    return q_spec, kv_spec


def _pair_call(kern, nb, s, tq, in_specs, operands, name):
    return pl.pallas_call(
        kern, grid=(nb, N_HEADS // 2, s // tq), in_specs=in_specs,
        out_specs=pl.BlockSpec((1, tq, 2 * LANE), lambda b, p, i: (b, i, p)),
        out_shape=jax.ShapeDtypeStruct((nb, s, N_HEADS * LANE), BF16),
        compiler_params=_cparams(3), name=name,
    )(*operands)


def _strict_upper(n):
    idx = np.arange(n)
    return jnp.asarray((idx[:, None] > idx[None, :]).astype(np.float32), BF16)


def _toeplitz(vec, n_rows, n_cols):
    length = n_rows + n_cols - 1
    assert vec.shape[-1] == length
    lead = vec.shape[:-1]
    rev = jnp.concatenate([vec[..., ::-1], jnp.zeros(lead + (1,), vec.dtype)], axis=-1)
    flat = jnp.tile(rev, (1,) * len(lead) + (n_rows,))[..., :n_rows * length]
    return flat.reshape(lead + (n_rows, length))[..., n_rows - 1: n_rows - 1 + n_cols]


def _rel_bias_tile(rel_bias, n_rows, n_cols, rel00, ok):
    d = np.arange(n_rows + n_cols - 1) - (n_cols - 1) + rel00
    vec = rel_bias.astype(F32)[:, np.clip(d, -REL_CLIP, REL_CLIP) + REL_CLIP] * LOG2E
    return jnp.where(jnp.asarray(ok)[None], _toeplitz(vec, n_rows, n_cols), NEG_INF)


def _band_bias_tiles(rel_bias, tq):
    i = np.arange(tq)[:, None]
    j = np.arange(tq)[None, :]
    own = _rel_bias_tile(rel_bias, tq, tq, 0, (j // CHUNK) <= (i // CHUNK))
    prev = _rel_bias_tile(rel_bias, tq, tq, tq, (j // CHUNK) >= (i // CHUNK) + tq // CHUNK - A_LEFT_CHUNKS)
    return own, prev


def _prompt_attention(pr, misc_t, nb, s, rel_bias, k_gain):
    tq = ATTN_TQ
    assert s % tq == 0 and tq == A_WINDOW
    r3 = lambda a: a.reshape(nb, s, a.shape[-1])
    qa, ka, va, qb, kb, vb, qc, kc, vc, qd, kd, vd = [r3(a) for a in pr[8:20]]
    q_spec, kv_spec = _pair_specs(s, tq)

    bd, bp = _band_bias_tiles(rel_bias, tq)
    b_spec = pl.BlockSpec((2, tq, tq), lambda b, p, i: (p, 0, 0))
    o_a = _pair_call(functools.partial(_band_kernel, tq=tq), nb, s, tq,
                     [q_spec, kv_spec, kv_spec, b_spec, b_spec], (qa, ka, va, bd, bp), "attn_band")

    u = _strict_upper(STICK_KS)
    o_b = _pair_call(functools.partial(_flash_stick_kernel, tq=tq, ks=STICK_KS), nb, s, tq,
                     [q_spec, kv_spec, kv_spec, pl.BlockSpec(u.shape, lambda b, p, i: (0, 0))],
                     (qb, kb, vb, u), "attn_stick")

    n_blk = s // tq
    fk_t = jnp.cumsum(misc_t[:, 0:N_HEADS, :], axis=2)
    c_cum = jnp.swapaxes(fk_t, 1, 2)
    fk = fk_t.reshape(nb, N_HEADS, n_blk, tq)
    kmax = jnp.full((nb * N_HEADS,), HEAD_DIM ** 0.5 * BOUND_MARGIN, F32) * jnp.max(jnp.abs(k_gain))
    decay = lax.cummax(jnp.max(-fk, axis=-1), axis=2) * LOG2E
    smem = pl.BlockSpec(memory_space=pltpu.SMEM)
    fq_spec = pl.BlockSpec((1, tq, N_HEADS), lambda b, p, i: (b, i, 0))
    fk_spec = pl.BlockSpec((1, 2, n_blk, tq), lambda b, p, i: (b, p, 0, 0))
    o_c = _pair_call(functools.partial(_flash_forget_kernel, tq=tq, n_blk=n_blk), nb, s, tq,
                     [smem, smem, q_spec, kv_spec, kv_spec, fq_spec, fk_spec],
                     (kmax, decay.reshape(-1), qc, kc, vc, c_cum, fk), "attn_forget")

    tqd = MLA_TQ if s % MLA_TQ == 0 else tq
    qd_spec = pl.BlockSpec((1, tqd, 2 * LANE), lambda b, p, i: (b, i, p))
    kvd_spec = pl.BlockSpec((1, s, 2 * LANE), lambda b, p, i: (b, 0, p), pipeline_mode=pl.Buffered(1))
    o_d = _pair_call(functools.partial(_flash_mla_kernel, tq=tqd, tk=min(ATTN_TK, tqd)), nb, s, tqd,
                     [qd_spec, kvd_spec, kvd_spec], (qd, kd, vd), "attn_mla")
    return o_a, o_b, o_c, o_d


def _two_block_softmax(s1, s2, v1_t, v2):
    m = jnp.maximum(jnp.max(s1, axis=1, keepdims=True), jnp.max(s2, axis=1, keepdims=True))
    p1 = jnp.exp2(s1 - m)
    p2 = jnp.exp2(s2 - m)
    l = jnp.sum(p1, axis=1, keepdims=True) + jnp.sum(p2, axis=1, keepdims=True)
    return (_nt_dot(p1.astype(BF16), v1_t) + _dot(p2.astype(BF16), v2)) / l


def _sample_kernel(qa_ref, ka_ref, va_ref, qb_ref, kb_ref, vb_ref, qc_ref, kc_ref, vc_ref, qd_ref, kd_ref, vd_ref,
                   cak_ref, cav_ref, cbk_ref, cbv_ref, cck_ref, ccv_ref, cckv_ref, ckr_ref,
                   ba_c_ref, ba_n_ref, fq_ref, fkc_ref, fkn_ref,
                   u_ref, un_ref, wukt_ref, wuv_ref, gk_ref,
                   oa_ref, ob_ref, oc_ref, od_ref, *, t, past, ks):
    row = lax.broadcasted_iota(jnp.int32, (t, t), 0)
    col = lax.broadcasted_iota(jnp.int32, (t, t), 1)
    causal_bias = jnp.where(col <= row, 0.0, NEG_INF)
    chunk_bias = jnp.where((past + col) // CHUNK <= (past + row) // CHUNK, 0.0, NEG_INF)
    strict = col < row

    ckv_c = cckv_ref[0].astype(BF16)
    kr_t = ckr_ref[0].astype(BF16)
    pad_t = jnp.zeros((LANE - MLA_NOPE - MLA_ROPE, past), BF16)

    for h in range(N_HEADS):
        hs = slice(LANE * h, LANE * h + LANE)
        ps = slice(LANE * (h // 2), LANE * (h // 2) + LANE)

        q = qa_ref[0][:, hs]
        s1 = _dot(q, cak_ref[0, ps, :].astype(BF16)) + ba_c_ref[h]
        s2 = _nt_dot(q, ka_ref[0][:, ps]) + ba_n_ref[h]
        oa_ref[0, :, hs] = _two_block_softmax(s1, s2, cav_ref[0, ps, :].astype(BF16), va_ref[0][:, ps])

        q = qb_ref[0][:, hs]
        c, w = _stick_weights(_nt_dot(q, kb_ref[0][:, ps]), jnp.zeros((t, 1), F32), un_ref[...], strict)
        acc = _dot(w.astype(BF16), vb_ref[0][:, ps])

        def b_live(state):
            return jnp.logical_and(state[0] >= 0, state[1] > DEAD_LOG2)

        def b_older(state, q=q, ps=ps):
            jb, _, c, acc = state
            off = pl.multiple_of(jb * ks, ks)
            kb_t = cbk_ref[0, ps, pl.ds(off, ks)].astype(BF16)
            vb_t = cbv_ref[0, ps, pl.ds(off, ks)].astype(BF16)
            c, w = _stick_weights(_dot(q, kb_t), c, u_ref[...], None)
            return jb - 1, jnp.max(c), c, acc + _nt_dot(w.astype(BF16), vb_t)

        _, _, _, acc = lax.while_loop(b_live, b_older, (past // ks - 1, jnp.max(c), c, acc))
        ob_ref[0, :, hs] = acc

        q = qc_ref[0][:, hs]
        fq = fq_ref[0][:, h:h + 1] * LOG2E
        s1 = _dot(q, cck_ref[0, ps, :].astype(BF16)) + (fq - fkc_ref[0, h:h + 1, :] * LOG2E)
        s2 = _nt_dot(q, kc_ref[0][:, ps]) + (fq - fkn_ref[0, h:h + 1, :] * LOG2E) + causal_bias
        oc_ref[0, :, hs] = _two_block_softmax(s1, s2, ccv_ref[0, ps, :].astype(BF16), vc_ref[0][:, ps])

        q = qd_ref[0][:, hs]
        kn_t = _nt_dot(wukt_ref[MLA_NOPE * h: MLA_NOPE * (h + 1), :], ckv_c)
        ss = jnp.sum(kn_t * kn_t, axis=0, keepdims=True)
        kn_t = kn_t * lax.rsqrt(ss * (1.0 / MLA_NOPE) + NORM_EPS) * gk_ref[...]
        k_t = jnp.concatenate([kn_t.astype(BF16), kr_t, pad_t], axis=0)
        s1 = _dot(q, k_t)
        s2 = _nt_dot(q, kd_ref[0][:, hs]) + chunk_bias
        v_c = _dot(ckv_c, wuv_ref[:, hs]).astype(BF16)
        m = jnp.maximum(jnp.max(s1, axis=1, keepdims=True), jnp.max(s2, axis=1, keepdims=True))
        p1 = jnp.exp2(s1 - m)
        p2 = jnp.exp2(s2 - m)
        l = jnp.sum(p1, axis=1, keepdims=True) + jnp.sum(p2, axis=1, keepdims=True)
        od_ref[0, :, hs] = (_dot(p1.astype(BF16), v_c) + _dot(p2.astype(BF16), vd_ref[0][:, hs])) / l


def _channel_major(cache):
    depth, nb, rows = cache.shape[:3]
    return jnp.transpose(cache, (0, 1, 3, 4, 2)).reshape(depth, nb, GROUP_W, rows)


def _sample_attention(pr, caches, layer, pw, w_uk, k_gain, rel_bias, nb, t):
    a_k, a_v, b_k, b_v, c_k, c_v, c_lf, d_ckv, d_kr = caches
    wuv = pw[4]
    past = b_k.shape[2]
    win = a_k.shape[2]
    ks = STICK_KS
    assert past % ks == 0
    r3 = lambda a: a.reshape(nb, t, a.shape[-1])
    news = [r3(a) for a in pr[8:20]]

    qpos = past + np.arange(t)
    kpos = past - win + np.arange(win + t)
    qc, kc = qpos // CHUNK, kpos // CHUNK
    ok = (kc[None, :] <= qc[:, None]) & (kc[None, :] >= qc[:, None] - A_LEFT_CHUNKS)
    ba = _rel_bias_tile(rel_bias, t, win + t, win, ok)
    ba_c, ba_n = ba[:, :, :win], ba[:, :, win:]

    clf_new = jnp.swapaxes(r3(pr[7])[..., 0:N_HEADS], 1, 2)
    lf_c = jnp.swapaxes(c_lf[layer], 1, 2).astype(F32)
    fk = jnp.cumsum(jnp.concatenate([lf_c, clf_new], axis=2), axis=2)
    fk_c, fk_n = fk[:, :, :past], fk[:, :, past:]
    fq = jnp.swapaxes(fk_n, 1, 2)

    wukt = w_uk.T.astype(BF16)
    gk = k_gain.reshape(MLA_NOPE, 1)
    u, un = _strict_upper(ks), _strict_upper(t)

    per_b = lambda shape: pl.BlockSpec((1,) + shape, lambda b: (b,) + (0,) * len(shape))
    per_lb = lambda shape: pl.BlockSpec((None, 1) + shape, lambda b: (layer, b) + (0,) * len(shape))
    const = lambda a: pl.BlockSpec(a.shape, lambda b: (0,) * a.ndim)
    cache_ops = [_channel_major(c) for c in (a_k, a_v, b_k, b_v, c_k, c_v)] + [d_ckv, jnp.swapaxes(d_kr, 2, 3)]
    operands = news + cache_ops + [ba_c, ba_n, fq, fk_c, fk_n, u, un, wukt, wuv, gk]
    in_specs = [per_b(a.shape[1:]) for a in news] + [per_lb(a.shape[2:]) for a in cache_ops] + \
               [const(ba_c), const(ba_n)] + [per_b(a.shape[1:]) for a in (fq, fk_c, fk_n)] + \
               [const(a) for a in (u, un, wukt, wuv, gk)]
    out_shape = [jax.ShapeDtypeStruct((nb, t, N_HEADS * LANE), F32)] * 4
    out_specs = [per_b((t, N_HEADS * LANE))] * 4
    return pl.pallas_call(
        functools.partial(_sample_kernel, t=t, past=past, ks=ks), grid=(nb,),
        in_specs=in_specs, out_specs=out_specs, out_shape=out_shape,
        compiler_params=_cparams(1), name="attn_sample",
    )(*operands)


def _out_kernel(oa_ref, ob_ref, oc_ref, od_ref, h_ref, gg_ref, wo_ref, nf_ref,
                wr1_ref, wr2_ref, br_ref, tri_ref, cnt0_ref, h1_ref, hn_ref, route_ref, cnt_ref):
    rows = h_ref.shape[0]
    low = lax.broadcasted_iota(jnp.int32, (rows, LANE), 1) < HEAD_DIM
    h1 = h_ref[...]
    for g, o_ref in enumerate((oa_ref, ob_ref, oc_ref, od_ref)):
        p0 = jnp.where(low, o_ref[:, 0:LANE], o_ref[:, LANE:2 * LANE])
        p1 = jnp.where(low, o_ref[:, 2 * LANE:3 * LANE], o_ref[:, 3 * LANE:4 * LANE])
        og = jnp.concatenate([p0, p1], axis=1).astype(F32)
        ms = jnp.mean(og * og, axis=-1, keepdims=True)
        y = (og * lax.rsqrt(ms + NORM_EPS) * gg_ref[g:g + 1, :]).astype(BF16)
        h1 = h1 + _dot(y, wo_ref[GROUP_W * g: GROUP_W * (g + 1), :])
    h1_ref[...] = h1
    ms = jnp.mean(h1 * h1, axis=-1, keepdims=True)
    hn = h1 * lax.rsqrt(ms + NORM_EPS) * nf_ref[...]
    hn_ref[...] = hn.astype(BF16)
    a1, a2 = _split2(hn)
    w1, w2 = wr1_ref[...], wr2_ref[...]
    logits = _dot(a1, w1) + _dot(a1, w2) + _dot(a2, w1) + br_ref[...]

    @pl.when(pl.program_id(0) == 0)
    def _():
        cnt_ref[...] = cnt0_ref[...]

    lane = lax.broadcasted_iota(jnp.int32, (rows, LANE), 1)
    x = jnp.where(lane < N_EXP, logits, NEG_INF)
    picks, tops, ids = [], [], []
    for j in range(TOP_K):
        top = jnp.max(x, axis=1, keepdims=True)
        idx = jnp.min(jnp.where(x == top, lane, LANE), axis=1, keepdims=True)
        hit = lane == idx
        x = jnp.where(hit, NEG_INF, x)
        picks.append(hit)
        tops.append(top)
        ids.append(idx)
    chosen = functools.reduce(jnp.logical_or, picks)
    before = _dot(tri_ref[...], jnp.where(chosen, 1.0, 0.0).astype(BF16)) + cnt_ref[...]
    exps = [jnp.exp(top - tops[0]) for top in tops]
    total = functools.reduce(jnp.add, exps)
    route = jnp.zeros((rows, LANE), F32)
    for j in range(TOP_K):
        rank = jnp.sum(jnp.where(picks[j], before, 0.0), axis=1, keepdims=True)
        route = jnp.where(lane == j, ids[j].astype(F32), route)
        route = jnp.where(lane == TOP_K + j, exps[j] / total, route)
        route = jnp.where(lane == 2 * TOP_K + j, rank, route)
    route_ref[...] = route
    cnt_ref[...] = cnt_ref[...] + jnp.sum(jnp.where(chosen, 1.0, 0.0), axis=0, keepdims=True)


def _out_project(outs, h, group_gain, w_out_b, norm_ffn, wr_parts, br_pad, counts0):
    t, d = h.shape
    ts = DENSE_ROWS
    assert t % ts == 0
    row = lambda i: (i, 0)
    full = lambda i: (0, 0)
    idx = np.arange(ts)
    tri = jnp.asarray((idx[:, None] > idx[None, :]).astype(np.float32), BF16)
    o_spec = pl.BlockSpec((ts, N_HEADS * LANE), row)
    in_specs = [o_spec] * 4 + [pl.BlockSpec((ts, d), row), pl.BlockSpec(group_gain.shape, full),
                               pl.BlockSpec(w_out_b.shape, full), pl.BlockSpec((1, d), full)] + \
               [pl.BlockSpec((d, LANE), full)] * len(wr_parts) + [pl.BlockSpec((1, LANE), full)] + \
               [pl.BlockSpec((ts, ts), full), pl.BlockSpec((1, LANE), full)]
    out_shape = [jax.ShapeDtypeStruct((t, d), F32), jax.ShapeDtypeStruct((t, d), BF16),
                 jax.ShapeDtypeStruct((t, LANE), F32), jax.ShapeDtypeStruct((1, LANE), F32)]
    out_specs = [pl.BlockSpec((ts, d), row), pl.BlockSpec((ts, d), row), pl.BlockSpec((ts, LANE), row),
                 pl.BlockSpec((1, LANE), full)]
    return pl.pallas_call(
        _out_kernel, grid=(t // ts,), in_specs=in_specs, out_specs=out_specs, out_shape=out_shape,
        compiler_params=_cparams(1), name="out_proj",
    )(*outs, h, group_gain, w_out_b, norm_ffn.reshape(1, d), *wr_parts, br_pad, tri, counts0)


def _expert_kernel(be_ref, fe_ref, nu_ref, x_ref, wu_ref, bu_ref, wd_ref, bd_ref, *rest):
    y_ref, wub_ref, wdb_ref = rest[-3:]
    i = pl.program_id(0)
    used = i < nu_ref[0]

    @pl.when(jnp.logical_and(used, fe_ref[i] == 1))
    def _():
        wub_ref[...] = wu_ref[0, 0].astype(BF16)
        wdb_ref[...] = wd_ref[0, 0].astype(BF16)

    @pl.when(used)
    def _():
        u = _dot(x_ref[...], wub_ref[...]) + bu_ref[0, 0]
        glu = jnp.minimum(u[:, :D_FF], SWIGLU_LIMIT)
        lin = jnp.clip(u[:, D_FF:], -SWIGLU_LIMIT, SWIGLU_LIMIT)
        act = glu * jax.nn.sigmoid(SWIGLU_ALPHA * glu) * (lin + 1.0)
        y_ref[...] = (_dot(act.astype(BF16), wdb_ref[...]) + bd_ref[0, 0]).astype(y_ref.dtype)


def _expert_ffn(y_prev, blk0, n_blk_all, x_rows, blk_exp, blk_first, n_used, layer, w_up, b_up, w_down, b_down):
    rows, d = x_rows.shape
    n_blk = rows // MOE_BLOCK
    last = lambda i, nu: jnp.maximum(jnp.minimum(i, nu[0] - 1), 0)
    w_idx = lambda i, be, fe, nu: (layer, be[last(i, nu)], 0, 0)
    in_specs = [pl.BlockSpec((MOE_BLOCK, d), lambda i, be, fe, nu: (last(i, nu), 0)),
                pl.BlockSpec((1, 1, d, 2 * D_FF), w_idx), pl.BlockSpec((1, 1, 1, 2 * D_FF), w_idx),
                pl.BlockSpec((1, 1, D_FF, d), w_idx), pl.BlockSpec((1, 1, 1, d), w_idx)]
    depth = w_up.shape[0]
    operands = [blk_exp, blk_first, n_used, x_rows, w_up, b_up.reshape(depth, N_EXP, 1, 2 * D_FF),
                w_down, b_down.reshape(depth, N_EXP, 1, d)]
    aliases = {}
    if y_prev is not None:
        in_specs.append(pl.BlockSpec(memory_space=pl.ANY))
        aliases = {len(operands): 0}
        operands.append(y_prev)
    grid_spec = pltpu.PrefetchScalarGridSpec(
        num_scalar_prefetch=3, grid=(n_blk,), in_specs=in_specs,
        out_specs=pl.BlockSpec((MOE_BLOCK, d), lambda i, be, fe, nu: (blk0 + last(i, nu), 0)),
        scratch_shapes=[pltpu.VMEM((d, 2 * D_FF), BF16), pltpu.VMEM((D_FF, d), BF16)])
    return pl.pallas_call(
        _expert_kernel, grid_spec=grid_spec,
        out_shape=jax.ShapeDtypeStruct((n_blk_all * MOE_BLOCK, d), BF16), input_output_aliases=aliases,
        compiler_params=_cparams(1), name="expert_ffn",
    )(*operands)


SCATTER_UNROLL = 8


def _row_token_kernel(pad_ref, dest_ref, rt_ref, *, chunk):
    i = pl.program_id(0)

    @pl.when(i == 0)
    def _():
        for e in range(N_EXP + 1):
            def clear(r, carry):
                rt_ref[r] = 0
                return carry
            lax.fori_loop(pad_ref[2 * e], pad_ref[2 * e + 1], clear, 0)

    def place(g, tok):
        k = g * SCATTER_UNROLL
        for j in range(SCATTER_UNROLL):
            rt_ref[dest_ref[k + j]] = tok + j // TOP_K
        return tok + SCATTER_UNROLL // TOP_K
    lax.fori_loop(0, chunk // SCATTER_UNROLL, place, i * (chunk // TOP_K))


def _row_tokens(dest, pad_ranges, n_rows):
    n = dest.shape[0]
    chunk = int(np.gcd(n, SCATTER_CHUNK))
    assert chunk % SCATTER_UNROLL == 0 and SCATTER_UNROLL % TOP_K == 0
    smem = pl.BlockSpec(memory_space=pltpu.SMEM)
    return pl.pallas_call(
        functools.partial(_row_token_kernel, chunk=chunk), grid=(n // chunk,),
        in_specs=[smem, pl.BlockSpec((chunk,), lambda i: (i,), memory_space=pltpu.SMEM)],
        out_specs=smem, out_shape=jax.ShapeDtypeStruct((n_rows,), jnp.int32),
        compiler_params=_cparams(1), name="row_tokens",
    )(pad_ranges, dest)


def _moe(hn, route, counts, layer, w_up, b_up, w_down, b_down):
    n_tok, d = hn.shape
    top_i = route[:, 0:TOP_K].astype(jnp.int32)
    gates = route[:, TOP_K:2 * TOP_K]
    rank = route[:, 2 * TOP_K:3 * TOP_K].astype(jnp.int32)
    n = n_tok * TOP_K
    padded = (counts + MOE_BLOCK - 1) // MOE_BLOCK * MOE_BLOCK
    p_end = jnp.cumsum(padded)
    p_start = p_end - padded
    dest = (p_start[top_i] + rank).reshape(-1)
    n_blk = -(-n // MOE_BLOCK) + N_EXP
    rows = n_blk * MOE_BLOCK
    pad_lo = jnp.concatenate([p_start + counts, p_end[-1:]])
    pad_hi = jnp.concatenate([p_end, jnp.full((1,), rows, p_end.dtype)])
    pad_ranges = jnp.stack([pad_lo, pad_hi], axis=1).reshape(-1).astype(jnp.int32)
    row_tok = _row_tokens(dest.astype(jnp.int32), pad_ranges, rows)
    blk_start = jnp.arange(n_blk, dtype=jnp.int32) * MOE_BLOCK
    blk_exp = jnp.sum((p_end[None, :] <= blk_start[:, None]).astype(jnp.int32), axis=1)
    blk_exp = jnp.minimum(blk_exp, N_EXP - 1)
    n_used = (p_end[-1] // MOE_BLOCK).astype(jnp.int32)
    n_grp = MOE_GROUPS if n_blk % MOE_GROUPS == 0 else 1
    nbg = n_blk // n_grp
    y = None
    for g in range(n_grp):
        be = blk_exp[g * nbg:(g + 1) * nbg]
        fe = jnp.concatenate([jnp.ones((1,), jnp.int32), (be[1:] != be[:-1]).astype(jnp.int32)])
        x_rows = hn.at[row_tok[g * nbg * MOE_BLOCK:(g + 1) * nbg * MOE_BLOCK]].get(mode='promise_in_bounds')
        y = _expert_ffn(y, g * nbg, n_blk, x_rows, be, fe, (n_used - g * nbg).reshape(1), layer,
                        w_up, b_up, w_down, b_down)
    return y, dest.reshape(n_tok, TOP_K), gates


def _pick_rows(y, dest, row0, n_rows):
    return y.at[dest[row0:row0 + n_rows].T].get(mode='promise_in_bounds')


def _ple_kernel(h_ref, y_ref, g_ref, p_ref, np_ref, wg_ref, wp_ref, *rest):
    o_ref = rest[-1]
    h2 = h_ref[...]
    g = g_ref[...]
    for j in range(TOP_K):
        h2 = h2 + y_ref[j].astype(F32) * g[:, j:j + 1]
    ms = jnp.mean(h2 * h2, axis=-1, keepdims=True)
    hn = (h2 * lax.rsqrt(ms + NORM_EPS) * np_ref[...]).astype(BF16)
    gate = jax.nn.sigmoid(_dot(hn, wg_ref[...]))
    o_ref[...] = h2 + gate * _dot(p_ref[...].astype(BF16), wp_ref[...])


def _ple(h1, row0, picked, gates, p, layer, norm_ple, wg_b, wp_b, out_prev=None):
    t, d = h1.shape
    n = gates.shape[0]
    ts = PLE_ROWS
    assert n % ts == 0 and row0 % ts == 0
    off = row0 // ts
    shifted = lambda i: (i + off, 0)
    full = lambda i: (0, 0)
    in_specs = [pl.BlockSpec((ts, d), shifted), pl.BlockSpec((TOP_K, ts, d), lambda i: (0, i, 0)),
                pl.BlockSpec((ts, TOP_K), lambda i: (i, 0)),
                pl.BlockSpec((None, ts, p.shape[2]), lambda i: (layer, i + off, 0)),
                pl.BlockSpec((1, d), full), pl.BlockSpec(wg_b.shape, full), pl.BlockSpec(wp_b.shape, full)]
    operands = [h1, picked, gates, p, norm_ple.reshape(1, d), wg_b, wp_b]
    aliases = {}
    if out_prev is not None:
        aliases = {len(operands): 0}
        in_specs.append(pl.BlockSpec(memory_space=pl.ANY))
        operands.append(out_prev)
    return pl.pallas_call(
        _ple_kernel, grid=(n // ts,), in_specs=in_specs,
        out_specs=pl.BlockSpec((ts, d), shifted), out_shape=jax.ShapeDtypeStruct((t, d), F32),
        input_output_aliases=aliases, compiler_params=_cparams(1), name="ple",
    )(*operands)


def kernel(x_prompt, x_sample, p_prompt, p_sample, cache_a_k, cache_a_v, cache_b_k, cache_b_v, cache_c_k, cache_c_v, cache_c_logf, cache_d_ckv, cache_d_krope, norm_mix, w_in, b_forget, qk_gain, rope_gain, kv_gain, w_uk, w_uv, rel_bias, group_gain, w_out, norm_ffn, w_router, b_router, w_up, b_up, w_down, b_down, norm_ple, w_ple_gate, w_ple_proj):
    nb, s, d = x_prompt.shape
    nd, t, _ = x_sample.shape
    depth = w_in.shape[0]
    past = cache_b_k.shape[2]
    assert PROJ_ROWS % t == 0 and s % PROJ_ROWS == 0

    tab_p = _rope_tables(jnp.arange(s))
    tab_s = _rope_tables(past + jnp.arange(PROJ_ROWS) % t)
    hp = x_prompt.reshape(nb * s, d)
    hs = x_sample.reshape(nd * t, d)
    st_p, st_s = [], []
    keep = min(A_WINDOW, s)
    layered_state = None
    for i in range(depth):
        pw = _proj_weights(w_in[i], b_forget[i], qk_gain[i], rope_gain[i], kv_gain[i], w_uk[i], w_uv[i])
        pr_p = _project(hp, norm_mix[i], pw, tab_p, s // PROJ_ROWS, (depth, i, nb, layered_state))
        layered_state = [pr_p[slot] for slot in STATE_SLOTS]
        pr_s = _project(hs, norm_mix[i], pw, tab_s, 1)
        outs_p = _prompt_attention(pr_p, pr_p[7][i], nb, s, rel_bias[i], qk_gain[i, 3])
        caches = (cache_a_k, cache_a_v, cache_b_k, cache_b_v, cache_c_k, cache_c_v,
                  cache_c_logf, cache_d_ckv, cache_d_krope)
        outs_s = _sample_attention(pr_s, caches, i, pw, w_uk[i], qk_gain[i, 5], rel_bias[i], nd, t)

        def heads(a, n, rows):
            return a.reshape(n, rows, N_HEADS, HEAD_DIM)

        p3 = lambda a: a.reshape(nb, s, a.shape[-1])
        s3 = lambda a: a.reshape(nd, t, a.shape[-1])
        st_p.append((heads(p3(pr_p[0])[:, s - keep:], nb, keep), heads(p3(pr_p[1])[:, s - keep:], nb, keep)))
        ka_all = jnp.concatenate([cache_a_k[i], heads(pr_s[0], nd, t)], axis=1)[:, t:]
        va_all = jnp.concatenate([cache_a_v[i], heads(pr_s[1], nd, t)], axis=1)[:, t:]
        st_s.append((ka_all, va_all,
                     heads(pr_s[2], nd, t), heads(pr_s[3], nd, t), heads(pr_s[4], nd, t), heads(pr_s[5], nd, t),
                     s3(pr_s[7])[..., 0:N_HEADS], s3(pr_s[6]), s3(pr_s[7])[..., ROPE_LO:ROPE_LO + MLA_ROPE]))

        w_out_b = w_out[i].astype(BF16)
        wr_pad = jnp.zeros((d, LANE), F32).at[:, 0:N_EXP].set(w_router[i])
        wr_parts = _split2(wr_pad)
        br_pad = jnp.zeros((1, LANE), F32).at[0, 0:N_EXP].set(b_router[i])
        flat4 = lambda o: o.reshape(-1, N_HEADS * LANE)
        h1_p, hn_p, rt_p, cnt = _out_project([flat4(o) for o in outs_p], hp, group_gain[i], w_out_b, norm_ffn[i],
                                             wr_parts, br_pad, jnp.zeros((1, LANE), F32))
        h1_s, hn_s, rt_s, cnt = _out_project([flat4(o) for o in outs_s], hs, group_gain[i], w_out_b, norm_ffn[i],
                                             wr_parts, br_pad, cnt)

        hn_all = jnp.concatenate([hn_p, hn_s], axis=0)
        rt_all = jnp.concatenate([rt_p, rt_s], axis=0)
        y, dest, gates = _moe(hn_all, rt_all, cnt[0, 0:N_EXP].astype(jnp.int32), i, w_up, b_up, w_down, b_down)

        wg_b = w_ple_gate[i].astype(BF16)
        wp_b = w_ple_proj[i].astype(BF16)
        pp = p_prompt.reshape(depth, nb * s, -1)
        hp = None
        for b in range(nb):
            hp = _ple(h1_p, b * s, _pick_rows(y, dest, b * s, s), gates[b * s:(b + 1) * s], pp, i,
                      norm_ple[i], wg_b, wp_b, hp)
        hs = _ple(h1_s, 0, _pick_rows(y, dest, nb * s, nd * t), gates[nb * s:], p_sample.reshape(depth, nd * t, -1),
                  i, norm_ple[i], wg_b, wp_b)

    bk_t, bv_t, ck_t, cv_t, ckv_all, misc_t = layered_state

    def from_channel_major(a):
        return jnp.transpose(a.reshape(depth, nb, N_HEADS, HEAD_DIM, s), (0, 1, 4, 2, 3))

    state_p = [jnp.stack([st[j] for st in st_p]) for j in range(2)] + \
              [from_channel_major(a) for a in (bk_t, bv_t, ck_t, cv_t)] + \
              [jnp.swapaxes(misc_t[:, :, 0:N_HEADS, :], 2, 3), ckv_all.reshape(depth, nb, s, KV_RANK),
               jnp.swapaxes(misc_t[:, :, ROPE_LO:ROPE_LO + MLA_ROPE, :], 2, 3)]
    state_s = [jnp.stack([st[j] for st in st_s]) for j in range(9)]
    return (hp.reshape(nb, s, d), hs.reshape(nd, t, d), *state_p, *state_s)
```

```python
import functools

import numpy as np
import jax
import jax.numpy as jnp
from jax import lax
from jax.experimental import pallas as pl
from jax.experimental.pallas import tpu as pltpu

F32 = jnp.float32
BF16 = jnp.bfloat16

CHUNK = 64
HEAD_DIM = 64
N_HEADS = 4
GROUP_W = 256
A_LEFT_CHUNKS = 8
A_WINDOW = A_LEFT_CHUNKS * CHUNK
REL_CLIP = 128
MLA_NOPE = 64
MLA_ROPE = 32
KV_RANK = 128
ROPE_BASE = 10000.0
N_EXP = 32
TOP_K = 4
D_FF = 1024
SWIGLU_LIMIT = 7.0
SWIGLU_ALPHA = 1.702
MOE_BLOCK = 512
NORM_EPS = 1e-6
NEG_INF = -1e30
LOG2E = 1.4426950408889634

A_Q = 0
C_F = 2304
D_Q = 2308
D_CKV = D_Q + N_HEADS * (MLA_NOPE + MLA_ROPE)
D_KR = D_CKV + KV_RANK

LANE = 128
SEG_ABC = 0
SEG_DQ = 2304
SEG_CKV = 2816
SEG_MISC = 2944
N_COLS = 3072
ROPE_LO = 64
ROPE_HALF = MLA_ROPE // 2

VMEM_LIMIT = 56 * 1024 * 1024

PROJ_ROWS = 512
ATTN_TQ = 512
ATTN_TK = 1024
MLA_TQ = 2048
STICK_KS = 256
DENSE_ROWS = 512
PLE_ROWS = 256
MOE_GROUPS = 4
SCATTER_CHUNK = 8192

DEAD_LOG2 = -160.0
BOUND_MARGIN = 1.01


def _cparams(n_axes):
    return pltpu.CompilerParams(dimension_semantics=("arbitrary",) * n_axes,
                                vmem_limit_bytes=VMEM_LIMIT)


def _nt_dot(a, b):
    return lax.dot_general(a, b, (((1,), (1,)), ((), ())), preferred_element_type=F32)


def _dot(a, b):
    return jnp.dot(a, b, preferred_element_type=F32)


def _split2(x):
    hi = x.astype(BF16)
    lo = (x - hi.astype(F32)).astype(BF16)
    return hi, lo


def _in_col_map():
    cols = np.full((N_COLS,), -1, np.int64)
    cols[0:2304] = np.arange(2304)
    for h in range(N_HEADS):
        base = D_Q + (MLA_NOPE + MLA_ROPE) * h
        cols[SEG_DQ + LANE * h: SEG_DQ + LANE * h + MLA_NOPE + MLA_ROPE] = base + np.arange(MLA_NOPE + MLA_ROPE)
    cols[SEG_CKV:SEG_CKV + KV_RANK] = D_CKV + np.arange(KV_RANK)
    cols[SEG_MISC:SEG_MISC + N_HEADS] = C_F + np.arange(N_HEADS)
    cols[SEG_MISC + ROPE_LO:SEG_MISC + ROPE_LO + MLA_ROPE] = D_KR + np.arange(MLA_ROPE)
    return cols


def _rope_tables(pos):
    inv = ROPE_BASE ** (-jnp.arange(ROPE_HALF, dtype=F32) / ROPE_HALF)
    ang = pos.astype(F32)[:, None] * inv
    cos, sin = jnp.cos(ang), jnp.sin(ang)
    n = pos.shape[0]
    one = jnp.ones((n, ROPE_LO), F32)
    z16 = jnp.zeros((n, ROPE_HALF), F32)
    z64 = jnp.zeros((n, ROPE_LO), F32)
    z32 = jnp.zeros((n, LANE - ROPE_LO - MLA_ROPE), F32)
    cos_t = jnp.concatenate([one, cos, cos, z32 + 1.0], axis=1)
    sin_a = jnp.concatenate([z64, -sin, z16, z32], axis=1)
    sin_b = jnp.concatenate([z64, z16, sin, z32], axis=1)
    return cos_t, sin_a, sin_b


def _rope(y, cos_t, sin_a, sin_b):
    left = pltpu.roll(y, LANE - ROPE_HALF, axis=1)
    right = pltpu.roll(y, ROPE_HALF, axis=1)
    return y * cos_t + left * sin_a + right * sin_b


def _head_norm(x, m_ref, gain):
    hi, lo = _split2(x * x)
    ssq = _dot(hi, m_ref[...]) + _dot(lo, m_ref[...])
    return x * lax.rsqrt(ssq * (1.0 / HEAD_DIM) + NORM_EPS) * gain


def _value_lane0(h):
    return HEAD_DIM * (h % 2)


def _ones_lane(h):
    return HEAD_DIM - _value_lane0(h)


def _store_padded_q(q_ref, q, scale):
    lane = lax.broadcasted_iota(jnp.int32, (q.shape[0], LANE), 1)
    low = lane < HEAD_DIM
    qs = q * scale
    for h in range(N_HEADS):
        pair = qs[:, LANE * (h // 2): LANE * (h // 2) + LANE]
        keep = low if h % 2 == 0 else jnp.logical_not(low)
        q_ref[:, LANE * h: LANE * h + LANE] = jnp.where(keep, pair, 0.0).astype(BF16)


def _proj_kernel(*refs, channel_major, n_alias):
    x_ref, gmix_ref, w_ref, cos_ref, sa_ref, sb_ref, gv_ref, m64_ref, wuk_ref, wuv_ref = refs[:10]
    (ak_ref, av_ref, bk_ref, bv_ref, ck_ref, cv_ref, ckv_ref, misc_ref,
     qa_ref, ka_ref, va_ref, qb_ref, kb_ref, vb_ref, qc_ref, kc_ref, vc_ref,
     qd_ref, kd_ref, vd_ref) = refs[10 + n_alias:]

    def put_state(ref, val):
        ref[...] = val.T if channel_major else val

    x = x_ref[...]
    ms = jnp.mean(x * x, axis=-1, keepdims=True)
    hn = (x * lax.rsqrt(ms + NORM_EPS) * gmix_ref[...]).astype(BF16)
    sm_scale = (HEAD_DIM ** -0.5) * LOG2E
    rows = x.shape[0]
    lane = lax.broadcasted_iota(jnp.int32, (rows, LANE), 1)
    is_nope = lane < MLA_NOPE
    is_rope = jnp.logical_and(lane >= ROPE_LO, lane < ROPE_LO + MLA_ROPE)
    cos_t, sin_a, sin_b = cos_ref[...], sa_ref[...], sb_ref[...]

    z = _dot(hn, w_ref[:, 0:768])
    aq = _head_norm(z[:, 0:256], m64_ref, gv_ref[0:1, :])
    ak = _head_norm(z[:, 256:512], m64_ref, gv_ref[1:2, :])
    av = z[:, 512:768]
    ak_ref[...] = ak
    av_ref[...] = av
    _store_padded_q(qa_ref, aq, sm_scale)
    ka_ref[...] = ak.astype(BF16)
    va_ref[...] = av.astype(BF16)

    z = _dot(hn, w_ref[:, 768:1536])
    put_state(bk_ref, z[:, 256:512])
    put_state(bv_ref, z[:, 512:768])
    _store_padded_q(qb_ref, z[:, 0:256], sm_scale)
    kb_ref[...] = z[:, 256:512].astype(BF16)
    vb_ref[...] = z[:, 512:768].astype(BF16)

    z = _dot(hn, w_ref[:, 1536:2304])
    cq = _head_norm(z[:, 0:256], m64_ref, gv_ref[2:3, :])
    ck = _head_norm(z[:, 256:512], m64_ref, gv_ref[3:4, :])
    cv = z[:, 512:768]
    put_state(ck_ref, ck)
    put_state(cv_ref, cv)
    _store_padded_q(qc_ref, cq, sm_scale)
    kc_ref[...] = ck.astype(BF16)
    vc_ref[...] = cv.astype(BF16)

    z = _dot(hn, w_ref[:, SEG_CKV:N_COLS])
    zc = z[:, 0:KV_RANK]
    ckv = zc * lax.rsqrt(jnp.mean(zc * zc, axis=-1, keepdims=True) + NORM_EPS) * gv_ref[7:8, 0:LANE]
    ckv_ref[...] = ckv
    zm = z[:, KV_RANK:2 * KV_RANK]
    ssr = jnp.sum(jnp.where(is_rope, zm * zm, 0.0), axis=-1, keepdims=True)
    kr = zm * lax.rsqrt(ssr * (1.0 / MLA_ROPE) + NORM_EPS) * gv_ref[6:7, 0:LANE]
    kr = _rope(kr, cos_t, sin_a, sin_b)
    zf = zm + gv_ref[8:9, 0:LANE]
    clf = jnp.minimum(zf, 0.0) - jnp.log1p(jnp.exp(-jnp.abs(zf)))
    put_state(misc_ref, jnp.where(lane < N_HEADS, clf, kr))

    ckv_b = ckv.astype(BF16)
    kn = _dot(ckv_b, wuk_ref[...])
    for h in range(N_HEADS):
        hs = slice(LANE * h, LANE * h + LANE)
        ones = (lane == _ones_lane(h)).astype(F32)
        vd_ref[:, hs] = (_dot(ckv_b, wuv_ref[:, hs]) + ones).astype(BF16)
    for h in range(N_HEADS):
        xh = kn[:, LANE * h: LANE * h + LANE]
        ss = jnp.sum(xh * xh, axis=-1, keepdims=True)
        yh = xh * lax.rsqrt(ss * (1.0 / MLA_NOPE) + NORM_EPS) * gv_ref[5:6, 0:LANE]
        kd_ref[:, LANE * h: LANE * h + LANE] = (yh + kr).astype(BF16)

    z = _dot(hn, w_ref[:, SEG_DQ:SEG_CKV])
    d_scale = ((MLA_NOPE + MLA_ROPE) ** -0.5) * LOG2E
    for h in range(N_HEADS):
        xh = z[:, LANE * h: LANE * h + LANE]
        x2 = xh * xh
        ssn = jnp.sum(jnp.where(is_nope, x2, 0.0), axis=-1, keepdims=True)
        ssr = jnp.sum(jnp.where(is_rope, x2, 0.0), axis=-1, keepdims=True)
        rn = lax.rsqrt(ssn * (1.0 / MLA_NOPE) + NORM_EPS)
        rr = lax.rsqrt(ssr * (1.0 / MLA_ROPE) + NORM_EPS)
        yh = xh * jnp.where(is_nope, rn, rr) * gv_ref[4:5, 0:LANE]
        yh = _rope(yh, cos_t, sin_a, sin_b)
        qd_ref[:, LANE * h: LANE * h + LANE] = (yh * d_scale).astype(BF16)


def _proj_weights(w_in, b_forget, qk_gain, rope_gain, kv_gain, w_uk, w_uv):
    cols = _in_col_map()
    valid = jnp.asarray(cols >= 0)
    w = jnp.where(valid[None, :], w_in[:, np.maximum(cols, 0)], 0.0).astype(BF16)

    def tile4(g):
        return jnp.tile(g, N_HEADS)

    gv = jnp.zeros((16, GROUP_W), F32)
    gv = gv.at[0].set(tile4(qk_gain[0])).at[1].set(tile4(qk_gain[1]))
    gv = gv.at[2].set(tile4(qk_gain[2])).at[3].set(tile4(qk_gain[3]))
    gv = gv.at[4, 0:MLA_NOPE].set(qk_gain[4]).at[4, ROPE_LO:ROPE_LO + MLA_ROPE].set(rope_gain[0])
    gv = gv.at[5, 0:MLA_NOPE].set(qk_gain[5])
    gv = gv.at[6, ROPE_LO:ROPE_LO + MLA_ROPE].set(rope_gain[1])
    gv = gv.at[7, 0:KV_RANK].set(kv_gain)
    gv = gv.at[8, 0:N_HEADS].set(b_forget)
    head = np.arange(GROUP_W) // HEAD_DIM
    m64 = jnp.asarray((head[:, None] == head[None, :]).astype(np.float32), BF16)
    wuk = jnp.zeros((KV_RANK, N_HEADS * LANE), F32)
    for h in range(N_HEADS):
        wuk = wuk.at[:, LANE * h: LANE * h + MLA_NOPE].set(w_uk[:, MLA_NOPE * h: MLA_NOPE * (h + 1)])
    wuv = jnp.zeros((KV_RANK, N_HEADS * LANE), F32)
    for h in range(N_HEADS):
        lo = LANE * h + _value_lane0(h)
        wuv = wuv.at[:, lo: lo + HEAD_DIM].set(w_uv[:, HEAD_DIM * h: HEAD_DIM * (h + 1)])
    return w, gv, m64, wuk.astype(BF16), wuv.astype(BF16)


STATE_SLOTS = (2, 3, 4, 5, 6, 7)


def _project(x, gmix, pw, tables, n_tab_blocks, layered=None):
    w, gv, m64, wuk, wuv = pw
    t, d = x.shape
    ts = PROJ_ROWS
    assert t % ts == 0
    row = lambda i: (i, 0)
    full = lambda i: (0, 0)
    tab = lambda i: (i % n_tab_blocks, 0)
    f32_w = [GROUP_W] * 6 + [KV_RANK, LANE]
    bf_w = [512, 256, 256, 512, 256, 256, 512, 256, 256, 512, 512, 512]
    out_shape = [jax.ShapeDtypeStruct((t, c), F32) for c in f32_w] + \
                [jax.ShapeDtypeStruct((t, c), BF16) for c in bf_w]
    out_specs = [pl.BlockSpec((ts, c), row) for c in f32_w + bf_w]
    in_specs = [pl.BlockSpec((ts, d), row), pl.BlockSpec((1, d), full), pl.BlockSpec((d, N_COLS), full),
                pl.BlockSpec((ts, LANE), tab), pl.BlockSpec((ts, LANE), tab), pl.BlockSpec((ts, LANE), tab),
                pl.BlockSpec(gv.shape, full), pl.BlockSpec(m64.shape, full),
                pl.BlockSpec(wuk.shape, full), pl.BlockSpec(wuv.shape, full)]
    operands = [x, gmix.reshape(1, d), w, *tables, gv, m64, wuk, wuv]
    aliases = {}
    if layered is not None:
        depth, layer, nb, earlier = layered
        s = t // nb
        nt = s // ts
        for slot in STATE_SLOTS:
            c = f32_w[slot]
            if slot == 6:
                out_shape[slot] = jax.ShapeDtypeStruct((depth, t, c), F32)
                out_specs[slot] = pl.BlockSpec((None, ts, c), lambda i: (layer, i, 0))
            else:
                out_shape[slot] = jax.ShapeDtypeStruct((depth, nb, c, s), F32)
                out_specs[slot] = pl.BlockSpec((None, None, c, ts), lambda i: (layer, i // nt, 0, i % nt))
        if earlier is not None:
            for k, slot in enumerate(STATE_SLOTS):
                aliases[len(operands)] = slot
                in_specs.append(pl.BlockSpec(memory_space=pl.ANY))
                operands.append(earlier[k])
    kern = functools.partial(_proj_kernel, channel_major=layered is not None, n_alias=len(aliases))
    return pl.pallas_call(
        kern, grid=(t // ts,), in_specs=in_specs, out_specs=out_specs, out_shape=out_shape,
        input_output_aliases=aliases, compiler_params=_cparams(1), name="proj",
    )(*operands)


def _softmax_step(q, kb, vb, carry, bias=None, ok=None):
    m, l, acc = carry
    s = _nt_dot(q, kb)
    if bias is not None:
        s = s + bias
    if ok is not None:
        s = jnp.where(ok, s, NEG_INF)
    m_new = jnp.maximum(m, jnp.max(s, axis=1, keepdims=True))
    alpha = jnp.exp2(m - m_new)
    p = jnp.exp2(s - m_new)
    l = alpha * l + jnp.sum(p, axis=1, keepdims=True)
    acc = alpha * acc + _dot(p.astype(BF16), vb)
    return m_new, l, acc


def _softmax_init(tq):
    return (jnp.full((tq, 1), NEG_INF, F32), jnp.zeros((tq, 1), F32), jnp.zeros((tq, LANE), F32))


def _pick_lane(block, h):
    col = lax.broadcasted_iota(jnp.int32, block.shape, 1)
    return jnp.sum(jnp.where(col == h, block, 0.0), axis=1, keepdims=True)


def _stack_pair(q_ref):
    return jnp.concatenate([q_ref[0, :, 0:LANE], q_ref[0, :, LANE:2 * LANE]], axis=0)


def _store_pair(o_ref, o, tq):
    o_ref[0, :, 0:LANE] = o[:tq].astype(o_ref.dtype)
    o_ref[0, :, LANE:2 * LANE] = o[tq:].astype(o_ref.dtype)


def _flash_forget_kernel(kmax_ref, decay_ref, q_ref, k_ref, v_ref, fq_ref, fk_ref, o_ref, *, tq, n_blk):
    b = pl.program_id(0)
    pair = pl.program_id(1)
    qi = pl.program_id(2)
    q = _stack_pair(q_ref)
    fq = jnp.concatenate([_pick_lane(fq_ref[0], 2 * pair), _pick_lane(fq_ref[0], 2 * pair + 1)], axis=0) * LOG2E

    def step(j, carry, diag):
        off = pl.multiple_of(j * tq, tq)
        kb = k_ref[0, pl.ds(off, tq), :]
        vb = v_ref[0, pl.ds(off, tq), :]
        fk = jnp.concatenate([jnp.broadcast_to(fk_ref[0, 0, pl.ds(j, 1), :], (tq, tq)),
                              jnp.broadcast_to(fk_ref[0, 1, pl.ds(j, 1), :], (tq, tq))], axis=0)
        ok = None
        if diag:
            row = lax.broadcasted_iota(jnp.int32, (2 * tq, tq), 0) % tq
            col = lax.broadcasted_iota(jnp.int32, (2 * tq, tq), 1)
            ok = col <= row
        return _softmax_step(q, kb, vb, carry, fq - fk * LOG2E, ok)

    carry = step(qi, _softmax_init(2 * tq), True)
    qf = q.astype(F32)
    q_norm = jnp.sqrt(jnp.sum(qf * qf, axis=1, keepdims=True))
    head0 = b * N_HEADS + 2 * pair
    room = q_norm * jnp.where(lax.broadcasted_iota(jnp.int32, q_norm.shape, 0) < tq,
                              kmax_ref[head0], kmax_ref[head0 + 1]) + fq - carry[0]
    slack0 = jnp.max(room[:tq]) - DEAD_LOG2
    slack1 = jnp.max(room[tq:]) - DEAD_LOG2

    def live(state):
        j = jnp.maximum(state[0], 0)
        alive = jnp.logical_or(slack0 + decay_ref[head0 * n_blk + j] >= 0.0,
                               slack1 + decay_ref[(head0 + 1) * n_blk + j] >= 0.0)
        return jnp.logical_and(state[0] >= 0, alive)

    def older(state):
        j = state[0]
        return (j - 1,) + step(j, state[1:], False)

    _, m, l, acc = lax.while_loop(live, older, (qi - 1,) + carry)
    _store_pair(o_ref, acc / l, tq)


def _flash_mla_kernel(q_ref, k_ref, v_ref, o_ref, *, tq, tk):
    qi = pl.program_id(2)
    n_sub = tq // tk

    def step(hh, j, carry, row0=None):
        m, acc = carry
        hs = slice(LANE * hh, LANE * hh + LANE)
        off = pl.multiple_of(j * tk, tk)
        qs = q_ref[0, :, hs] if row0 is None else q_ref[0, row0:, hs]
        s = _nt_dot(qs, k_ref[0, pl.ds(off, tk), hs])
        if row0 is not None:
            row = lax.broadcasted_iota(jnp.int32, s.shape, 0)
            col = lax.broadcasted_iota(jnp.int32, s.shape, 1)
            s = jnp.where(col // CHUNK <= row // CHUNK, s, NEG_INF)
        m_new = jnp.maximum(m, jnp.max(s, axis=1, keepdims=True))
        p = jnp.exp2(s - m_new)
        acc = jnp.exp2(m - m_new) * acc + _dot(p.astype(BF16), v_ref[0, pl.ds(off, tk), hs])
        return m_new, acc

    def both(j, carry):
        return step(0, j, carry[0]), step(1, j, carry[1])

    init = (jnp.full((tq, 1), NEG_INF, F32), jnp.zeros((tq, LANE), F32))
    carry = lax.fori_loop(0, qi * n_sub, both, (init, init))
    for hh in range(2):
        m, acc = carry[hh]
        for r in range(n_sub):
            row0 = r * tk
            m_r, acc_r = step(hh, qi * n_sub + r, (m[row0:], acc[row0:]), row0)
            m = m_r if r == 0 else jnp.concatenate([m[:row0], m_r], axis=0)
            acc = acc_r if r == 0 else jnp.concatenate([acc[:row0], acc_r], axis=0)
        ones_at = _ones_lane(hh)
        o_ref[0, :, LANE * hh: LANE * hh + LANE] = (acc / acc[:, ones_at:ones_at + 1]).astype(o_ref.dtype)


def _flash_stick_kernel(q_ref, k_ref, v_ref, u_ref, o_ref, *, tq, ks):
    qi = pl.program_id(2)
    q = _stack_pair(q_ref)
    n_sub = tq // ks

    def step(jb, carry, diag):
        c, acc = carry
        off = pl.multiple_of(jb * ks, ks)
        kb = k_ref[0, pl.ds(off, ks), :]
        vb = v_ref[0, pl.ds(off, ks), :]
        vis = None
        if diag:
            row = lax.broadcasted_iota(jnp.int32, (2 * tq, ks), 0) % tq + qi * tq
            col = lax.broadcasted_iota(jnp.int32, (2 * tq, ks), 1) + jb * ks
            vis = col < row
        c_new, w = _stick_weights(_nt_dot(q, kb), c, u_ref[...], vis)
        return c_new, acc + _dot(w.astype(BF16), vb)

    carry = (jnp.zeros((2 * tq, 1), F32), jnp.zeros((2 * tq, LANE), F32))
    for r in range(n_sub):
        carry = step(qi * n_sub + (n_sub - 1 - r), carry, True)

    def live(state):
        return jnp.logical_and(state[0] >= 0, state[1] > DEAD_LOG2)

    def older(state):
        c, acc = step(state[0], state[2:], False)
        return state[0] - 1, jnp.max(c), c, acc

    _, _, _, acc = lax.while_loop(live, older, (qi * n_sub - 1, jnp.max(carry[0])) + carry)
    _store_pair(o_ref, acc, tq)


def _stick_weights(z, c, u, vis):
    sp = jnp.log(1.0 + jnp.exp2(-jnp.abs(z))) * LOG2E
    log_rest = jnp.minimum(-z, 0.0) - sp
    log_beta = log_rest + z
    if vis is not None:
        log_rest = jnp.where(vis, log_rest, 0.0)
    hi, lo = _split2(log_rest)
    between = _dot(hi, u) + _dot(lo, u)
    w = jnp.exp2(log_beta + between + c)
    if vis is not None:
        w = jnp.where(vis, w, 0.0)
    return c + jnp.sum(log_rest, axis=1, keepdims=True), w


def _band_kernel(q_ref, k_ref, v_ref, bd_ref, bp_ref, o_ref, *, tq):
    qi = pl.program_id(2)
    q = _stack_pair(q_ref)
    off = pl.multiple_of(qi * tq, tq)
    carry = _softmax_step(q, k_ref[0, pl.ds(off, tq), :], v_ref[0, pl.ds(off, tq), :],
                          _softmax_init(2 * tq), bd_ref[...].reshape(2 * tq, tq))
    offp = pl.multiple_of(jnp.maximum(qi - 1, 0) * tq, tq)
    no_prev = jnp.where(qi == 0, NEG_INF, 0.0)
    m, l, acc = _softmax_step(q, k_ref[0, pl.ds(offp, tq), :], v_ref[0, pl.ds(offp, tq), :],
                              carry, bp_ref[...].reshape(2 * tq, tq) + no_prev)
    _store_pair(o_ref, acc / l, tq)


def _pair_specs(s, tq):
    q_spec = pl.BlockSpec((1, tq, 2 * LANE), lambda b, p, i: (b, i, p))
    kv_spec = pl.BlockSpec((1, s, LANE), lambda b, p, i: (b, 0, p))
    return q_spec, kv_spec


def _pair_call(kern, nb, s, tq, in_specs, operands, name):
    return pl.pallas_call(
        kern, grid=(nb, N_HEADS // 2, s // tq), in_specs=in_specs,
        out_specs=pl.BlockSpec((1, tq, 2 * LANE), lambda b, p, i: (b, i, p)),
        out_shape=jax.ShapeDtypeStruct((nb, s, N_HEADS * LANE), BF16),
        compiler_params=_cparams(3), name=name,
    )(*operands)


def _strict_upper(n):
    idx = np.arange(n)
    return jnp.asarray((idx[:, None] > idx[None, :]).astype(np.float32), BF16)


def _toeplitz(vec, n_rows, n_cols):
    length = n_rows + n_cols - 1
    assert vec.shape[-1] == length
    lead = vec.shape[:-1]
    rev = jnp.concatenate([vec[..., ::-1], jnp.zeros(lead + (1,), vec.dtype)], axis=-1)
    flat = jnp.tile(rev, (1,) * len(lead) + (n_rows,))[..., :n_rows * length]
    return flat.reshape(lead + (n_rows, length))[..., n_rows - 1: n_rows - 1 + n_cols]


def _rel_bias_tile(rel_bias, n_rows, n_cols, rel00, ok):
    d = np.arange(n_rows + n_cols - 1) - (n_cols - 1) + rel00
    vec = rel_bias.astype(F32)[:, np.clip(d, -REL_CLIP, REL_CLIP) + REL_CLIP] * LOG2E
    return jnp.where(jnp.asarray(ok)[None], _toeplitz(vec, n_rows, n_cols), NEG_INF)


def _band_bias_tiles(rel_bias, tq):
    i = np.arange(tq)[:, None]
    j = np.arange(tq)[None, :]
    own = _rel_bias_tile(rel_bias, tq, tq, 0, (j // CHUNK) <= (i // CHUNK))
    prev = _rel_bias_tile(rel_bias, tq, tq, tq, (j // CHUNK) >= (i // CHUNK) + tq // CHUNK - A_LEFT_CHUNKS)
    return own, prev


def _prompt_attention(pr, misc_t, nb, s, rel_bias, k_gain):
    tq = ATTN_TQ
    assert s % tq == 0 and tq == A_WINDOW
    r3 = lambda a: a.reshape(nb, s, a.shape[-1])
    qa, ka, va, qb, kb, vb, qc, kc, vc, qd, kd, vd = [r3(a) for a in pr[8:20]]
    q_spec, kv_spec = _pair_specs(s, tq)

    bd, bp = _band_bias_tiles(rel_bias, tq)
    b_spec = pl.BlockSpec((2, tq, tq), lambda b, p, i: (p, 0, 0))
    o_a = _pair_call(functools.partial(_band_kernel, tq=tq), nb, s, tq,
                     [q_spec, kv_spec, kv_spec, b_spec, b_spec], (qa, ka, va, bd, bp), "attn_band")

    u = _strict_upper(STICK_KS)
    o_b = _pair_call(functools.partial(_flash_stick_kernel, tq=tq, ks=STICK_KS), nb, s, tq,
                     [q_spec, kv_spec, kv_spec, pl.BlockSpec(u.shape, lambda b, p, i: (0, 0))],
                     (qb, kb, vb, u), "attn_stick")

    n_blk = s // tq
    fk_t = jnp.cumsum(misc_t[:, 0:N_HEADS, :], axis=2)
    c_cum = jnp.swapaxes(fk_t, 1, 2)
    fk = fk_t.reshape(nb, N_HEADS, n_blk, tq)
    kmax = jnp.full((nb * N_HEADS,), HEAD_DIM ** 0.5 * BOUND_MARGIN, F32) * jnp.max(jnp.abs(k_gain))
    decay = lax.cummax(jnp.max(-fk, axis=-1), axis=2) * LOG2E
    smem = pl.BlockSpec(memory_space=pltpu.SMEM)
    fq_spec = pl.BlockSpec((1, tq, N_HEADS), lambda b, p, i: (b, i, 0))
    fk_spec = pl.BlockSpec((1, 2, n_blk, tq), lambda b, p, i: (b, p, 0, 0))
    o_c = _pair_call(functools.partial(_flash_forget_kernel, tq=tq, n_blk=n_blk), nb, s, tq,
                     [smem, smem, q_spec, kv_spec, kv_spec, fq_spec, fk_spec],
                     (kmax, decay.reshape(-1), qc, kc, vc, c_cum, fk), "attn_forget")

    tqd = MLA_TQ if s % MLA_TQ == 0 else tq
    qd_spec = pl.BlockSpec((1, tqd, 2 * LANE), lambda b, p, i: (b, i, p))
    kvd_spec = pl.BlockSpec((1, s, 2 * LANE), lambda b, p, i: (b, 0, p), pipeline_mode=pl.Buffered(1))
    o_d = _pair_call(functools.partial(_flash_mla_kernel, tq=tqd, tk=min(ATTN_TK, tqd)), nb, s, tqd,
                     [qd_spec, kvd_spec, kvd_spec], (qd, kd, vd), "attn_mla")
    return o_a, o_b, o_c, o_d


def _two_block_softmax(s1, s2, v1_t, v2):
    m = jnp.maximum(jnp.max(s1, axis=1, keepdims=True), jnp.max(s2, axis=1, keepdims=True))
    p1 = jnp.exp2(s1 - m)
    p2 = jnp.exp2(s2 - m)
    l = jnp.sum(p1, axis=1, keepdims=True) + jnp.sum(p2, axis=1, keepdims=True)
    return (_nt_dot(p1.astype(BF16), v1_t) + _dot(p2.astype(BF16), v2)) / l


def _sample_kernel(qa_ref, ka_ref, va_ref, qb_ref, kb_ref, vb_ref, qc_ref, kc_ref, vc_ref, qd_ref, kd_ref, vd_ref,
                   cak_ref, cav_ref, cbk_ref, cbv_ref, cck_ref, ccv_ref, cckv_ref, ckr_ref,
                   ba_c_ref, ba_n_ref, fq_ref, fkc_ref, fkn_ref,
                   u_ref, un_ref, wukt_ref, wuv_ref, gk_ref,
                   oa_ref, ob_ref, oc_ref, od_ref, *, t, past, ks):
    row = lax.broadcasted_iota(jnp.int32, (t, t), 0)
    col = lax.broadcasted_iota(jnp.int32, (t, t), 1)
    causal_bias = jnp.where(col <= row, 0.0, NEG_INF)
    chunk_bias = jnp.where((past + col) // CHUNK <= (past + row) // CHUNK, 0.0, NEG_INF)
    strict = col < row

    ckv_c = cckv_ref[0].astype(BF16)
    kr_t = ckr_ref[0].astype(BF16)
    pad_t = jnp.zeros((LANE - MLA_NOPE - MLA_ROPE, past), BF16)

    for h in range(N_HEADS):
        hs = slice(LANE * h, LANE * h + LANE)
        ps = slice(LANE * (h // 2), LANE * (h // 2) + LANE)

        q = qa_ref[0][:, hs]
        s1 = _dot(q, cak_ref[0, ps, :].astype(BF16)) + ba_c_ref[h]
        s2 = _nt_dot(q, ka_ref[0][:, ps]) + ba_n_ref[h]
        oa_ref[0, :, hs] = _two_block_softmax(s1, s2, cav_ref[0, ps, :].astype(BF16), va_ref[0][:, ps])

        q = qb_ref[0][:, hs]
        c, w = _stick_weights(_nt_dot(q, kb_ref[0][:, ps]), jnp.zeros((t, 1), F32), un_ref[...], strict)
        acc = _dot(w.astype(BF16), vb_ref[0][:, ps])

        def b_live(state):
            return jnp.logical_and(state[0] >= 0, state[1] > DEAD_LOG2)

        def b_older(state, q=q, ps=ps):
            jb, _, c, acc = state
            off = pl.multiple_of(jb * ks, ks)
            kb_t = cbk_ref[0, ps, pl.ds(off, ks)].astype(BF16)
            vb_t = cbv_ref[0, ps, pl.ds(off, ks)].astype(BF16)
            c, w = _stick_weights(_dot(q, kb_t), c, u_ref[...], None)
            return jb - 1, jnp.max(c), c, acc + _nt_dot(w.astype(BF16), vb_t)

        _, _, _, acc = lax.while_loop(b_live, b_older, (past // ks - 1, jnp.max(c), c, acc))
        ob_ref[0, :, hs] = acc

        q = qc_ref[0][:, hs]
        fq = fq_ref[0][:, h:h + 1] * LOG2E
        s1 = _dot(q, cck_ref[0, ps, :].astype(BF16)) + (fq - fkc_ref[0, h:h + 1, :] * LOG2E)
        s2 = _nt_dot(q, kc_ref[0][:, ps]) + (fq - fkn_ref[0, h:h + 1, :] * LOG2E) + causal_bias
        oc_ref[0, :, hs] = _two_block_softmax(s1, s2, ccv_ref[0, ps, :].astype(BF16), vc_ref[0][:, ps])

        q = qd_ref[0][:, hs]
        kn_t = _nt_dot(wukt_ref[MLA_NOPE * h: MLA_NOPE * (h + 1), :], ckv_c)
        ss = jnp.sum(kn_t * kn_t, axis=0, keepdims=True)
        kn_t = kn_t * lax.rsqrt(ss * (1.0 / MLA_NOPE) + NORM_EPS) * gk_ref[...]
        k_t = jnp.concatenate([kn_t.astype(BF16), kr_t, pad_t], axis=0)
        s1 = _dot(q, k_t)
        s2 = _nt_dot(q, kd_ref[0][:, hs]) + chunk_bias
        v_c = _dot(ckv_c, wuv_ref[:, hs]).astype(BF16)
        m = jnp.maximum(jnp.max(s1, axis=1, keepdims=True), jnp.max(s2, axis=1, keepdims=True))
        p1 = jnp.exp2(s1 - m)
        p2 = jnp.exp2(s2 - m)
        l = jnp.sum(p1, axis=1, keepdims=True) + jnp.sum(p2, axis=1, keepdims=True)
        od_ref[0, :, hs] = (_dot(p1.astype(BF16), v_c) + _dot(p2.astype(BF16), vd_ref[0][:, hs])) / l


def _channel_major(cache):
    depth, nb, rows = cache.shape[:3]
    return jnp.transpose(cache, (0, 1, 3, 4, 2)).reshape(depth, nb, GROUP_W, rows)


def _sample_attention(pr, caches, layer, pw, w_uk, k_gain, rel_bias, nb, t):
    a_k, a_v, b_k, b_v, c_k, c_v, c_lf, d_ckv, d_kr = caches
    wuv = pw[4]
    past = b_k.shape[2]
    win = a_k.shape[2]
    ks = STICK_KS
    assert past % ks == 0
    r3 = lambda a: a.reshape(nb, t, a.shape[-1])
    news = [r3(a) for a in pr[8:20]]

    qpos = past + np.arange(t)
    kpos = past - win + np.arange(win + t)
    qc, kc = qpos // CHUNK, kpos // CHUNK
    ok = (kc[None, :] <= qc[:, None]) & (kc[None, :] >= qc[:, None] - A_LEFT_CHUNKS)
    ba = _rel_bias_tile(rel_bias, t, win + t, win, ok)
    ba_c, ba_n = ba[:, :, :win], ba[:, :, win:]

    clf_new = jnp.swapaxes(r3(pr[7])[..., 0:N_HEADS], 1, 2)
    lf_c = jnp.swapaxes(c_lf[layer], 1, 2).astype(F32)
    fk = jnp.cumsum(jnp.concatenate([lf_c, clf_new], axis=2), axis=2)
    fk_c, fk_n = fk[:, :, :past], fk[:, :, past:]
    fq = jnp.swapaxes(fk_n, 1, 2)

    wukt = w_uk.T.astype(BF16)
    gk = k_gain.reshape(MLA_NOPE, 1)
    u, un = _strict_upper(ks), _strict_upper(t)

    per_b = lambda shape: pl.BlockSpec((1,) + shape, lambda b: (b,) + (0,) * len(shape))
    per_lb = lambda shape: pl.BlockSpec((None, 1) + shape, lambda b: (layer, b) + (0,) * len(shape))
    const = lambda a: pl.BlockSpec(a.shape, lambda b: (0,) * a.ndim)
    cache_ops = [_channel_major(c) for c in (a_k, a_v, b_k, b_v, c_k, c_v)] + [d_ckv, jnp.swapaxes(d_kr, 2, 3)]
    operands = news + cache_ops + [ba_c, ba_n, fq, fk_c, fk_n, u, un, wukt, wuv, gk]
    in_specs = [per_b(a.shape[1:]) for a in news] + [per_lb(a.shape[2:]) for a in cache_ops] + \
               [const(ba_c), const(ba_n)] + [per_b(a.shape[1:]) for a in (fq, fk_c, fk_n)] + \
               [const(a) for a in (u, un, wukt, wuv, gk)]
    out_shape = [jax.ShapeDtypeStruct((nb, t, N_HEADS * LANE), F32)] * 4
    out_specs = [per_b((t, N_HEADS * LANE))] * 4
    return pl.pallas_call(
        functools.partial(_sample_kernel, t=t, past=past, ks=ks), grid=(nb,),
        in_specs=in_specs, out_specs=out_specs, out_shape=out_shape,
        compiler_params=_cparams(1), name="attn_sample",
    )(*operands)


def _out_kernel(oa_ref, ob_ref, oc_ref, od_ref, h_ref, gg_ref, wo_ref, nf_ref,
                wr1_ref, wr2_ref, br_ref, tri_ref, cnt0_ref, h1_ref, hn_ref, route_ref, cnt_ref):
    rows = h_ref.shape[0]
    low = lax.broadcasted_iota(jnp.int32, (rows, LANE), 1) < HEAD_DIM
    h1 = h_ref[...]
    for g, o_ref in enumerate((oa_ref, ob_ref, oc_ref, od_ref)):
        p0 = jnp.where(low, o_ref[:, 0:LANE], o_ref[:, LANE:2 * LANE])
        p1 = jnp.where(low, o_ref[:, 2 * LANE:3 * LANE], o_ref[:, 3 * LANE:4 * LANE])
        og = jnp.concatenate([p0, p1], axis=1).astype(F32)
        ms = jnp.mean(og * og, axis=-1, keepdims=True)
        y = (og * lax.rsqrt(ms + NORM_EPS) * gg_ref[g:g + 1, :]).astype(BF16)
        h1 = h1 + _dot(y, wo_ref[GROUP_W * g: GROUP_W * (g + 1), :])
    h1_ref[...] = h1
    ms = jnp.mean(h1 * h1, axis=-1, keepdims=True)
    hn = h1 * lax.rsqrt(ms + NORM_EPS) * nf_ref[...]
    hn_ref[...] = hn.astype(BF16)
    a1, a2 = _split2(hn)
    w1, w2 = wr1_ref[...], wr2_ref[...]
    logits = _dot(a1, w1) + _dot(a1, w2) + _dot(a2, w1) + br_ref[...]

    @pl.when(pl.program_id(0) == 0)
    def _():
        cnt_ref[...] = cnt0_ref[...]

    lane = lax.broadcasted_iota(jnp.int32, (rows, LANE), 1)
    x = jnp.where(lane < N_EXP, logits, NEG_INF)
    picks, tops, ids = [], [], []
    for j in range(TOP_K):
        top = jnp.max(x, axis=1, keepdims=True)
        idx = jnp.min(jnp.where(x == top, lane, LANE), axis=1, keepdims=True)
        hit = lane == idx
        x = jnp.where(hit, NEG_INF, x)
        picks.append(hit)
        tops.append(top)
        ids.append(idx)
    chosen = functools.reduce(jnp.logical_or, picks)
    before = _dot(tri_ref[...], jnp.where(chosen, 1.0, 0.0).astype(BF16)) + cnt_ref[...]
    exps = [jnp.exp(top - tops[0]) for top in tops]
    total = functools.reduce(jnp.add, exps)
    route = jnp.zeros((rows, LANE), F32)
    for j in range(TOP_K):
        rank = jnp.sum(jnp.where(picks[j], before, 0.0), axis=1, keepdims=True)
        route = jnp.where(lane == j, ids[j].astype(F32), route)
        route = jnp.where(lane == TOP_K + j, exps[j] / total, route)
        route = jnp.where(lane == 2 * TOP_K + j, rank, route)
    route_ref[...] = route
    cnt_ref[...] = cnt_ref[...] + jnp.sum(jnp.where(chosen, 1.0, 0.0), axis=0, keepdims=True)


def _out_project(outs, h, group_gain, w_out_b, norm_ffn, wr_parts, br_pad, counts0):
    t, d = h.shape
    ts = DENSE_ROWS
    assert t % ts == 0
    row = lambda i: (i, 0)
    full = lambda i: (0, 0)
    idx = np.arange(ts)
    tri = jnp.asarray((idx[:, None] > idx[None, :]).astype(np.float32), BF16)
    o_spec = pl.BlockSpec((ts, N_HEADS * LANE), row)
    in_specs = [o_spec] * 4 + [pl.BlockSpec((ts, d), row), pl.BlockSpec(group_gain.shape, full),
                               pl.BlockSpec(w_out_b.shape, full), pl.BlockSpec((1, d), full)] + \
               [pl.BlockSpec((d, LANE), full)] * len(wr_parts) + [pl.BlockSpec((1, LANE), full)] + \
               [pl.BlockSpec((ts, ts), full), pl.BlockSpec((1, LANE), full)]
    out_shape = [jax.ShapeDtypeStruct((t, d), F32), jax.ShapeDtypeStruct((t, d), BF16),
                 jax.ShapeDtypeStruct((t, LANE), F32), jax.ShapeDtypeStruct((1, LANE), F32)]
    out_specs = [pl.BlockSpec((ts, d), row), pl.BlockSpec((ts, d), row), pl.BlockSpec((ts, LANE), row),
                 pl.BlockSpec((1, LANE), full)]
    return pl.pallas_call(
        _out_kernel, grid=(t // ts,), in_specs=in_specs, out_specs=out_specs, out_shape=out_shape,
        compiler_params=_cparams(1), name="out_proj",
    )(*outs, h, group_gain, w_out_b, norm_ffn.reshape(1, d), *wr_parts, br_pad, tri, counts0)


def _expert_kernel(be_ref, fe_ref, ne_ref, nu_ref, x_ref, wu_hbm, bu_ref, wd_hbm, bd_ref, *rest, layer):
    y_ref, wu_stage, wd_stage, wub_ref, wdb_ref, sems = rest[-6:]
    i = pl.program_id(0)
    used = i < nu_ref[0]

    def weight_copies(expert):
        return (pltpu.make_async_copy(wu_hbm.at[layer, expert], wu_stage, sems.at[0]),
                pltpu.make_async_copy(wd_hbm.at[layer, expert], wd_stage, sems.at[1]))

    @pl.when(jnp.logical_and(used, fe_ref[i] == 1))
    def _():
        @pl.when(i == 0)
        def _():
            for cp in weight_copies(be_ref[0]):
                cp.start()
        for cp in weight_copies(be_ref[i]):
            cp.wait()
        wub_ref[...] = wu_stage[...].astype(BF16)
        wdb_ref[...] = wd_stage[...].astype(BF16)

        @pl.when(ne_ref[i] >= 0)
        def _():
            for cp in weight_copies(ne_ref[i]):
                cp.start()

    @pl.when(used)
    def _():
        u = _dot(x_ref[...], wub_ref[...]) + bu_ref[0, 0]
        glu = jnp.minimum(u[:, :D_FF], SWIGLU_LIMIT)
        lin = jnp.clip(u[:, D_FF:], -SWIGLU_LIMIT, SWIGLU_LIMIT)
        act = glu * jax.nn.sigmoid(SWIGLU_ALPHA * glu) * (lin + 1.0)
        y_ref[...] = (_dot(act.astype(BF16), wdb_ref[...]) + bd_ref[0, 0]).astype(y_ref.dtype)


def _expert_ffn(y_prev, blk0, n_blk_all, x_rows, blk_exp, blk_first, blk_next, n_used, layer,
                w_up, b_up, w_down, b_down):
    rows, d = x_rows.shape
    n_blk = rows // MOE_BLOCK
    last = lambda i, nu: jnp.maximum(jnp.minimum(i, nu[0] - 1), 0)
    b_idx = lambda i, be, fe, ne, nu: (layer, be[last(i, nu)], 0, 0)
    hbm = pl.BlockSpec(memory_space=pl.ANY)
    in_specs = [pl.BlockSpec((MOE_BLOCK, d), lambda i, be, fe, ne, nu: (last(i, nu), 0)),
                hbm, pl.BlockSpec((1, 1, 1, 2 * D_FF), b_idx), hbm, pl.BlockSpec((1, 1, 1, d), b_idx)]
    depth = w_up.shape[0]
    operands = [blk_exp, blk_first, blk_next, n_used, x_rows, w_up, b_up.reshape(depth, N_EXP, 1, 2 * D_FF),
                w_down, b_down.reshape(depth, N_EXP, 1, d)]
    aliases = {}
    if y_prev is not None:
        in_specs.append(hbm)
        aliases = {len(operands): 0}
        operands.append(y_prev)
    grid_spec = pltpu.PrefetchScalarGridSpec(
        num_scalar_prefetch=4, grid=(n_blk,), in_specs=in_specs,
        out_specs=pl.BlockSpec((MOE_BLOCK, d), lambda i, be, fe, ne, nu: (blk0 + last(i, nu), 0)),
        scratch_shapes=[pltpu.VMEM((d, 2 * D_FF), F32), pltpu.VMEM((D_FF, d), F32),
                        pltpu.VMEM((d, 2 * D_FF), BF16), pltpu.VMEM((D_FF, d), BF16),
                        pltpu.SemaphoreType.DMA((2,))])
    return pl.pallas_call(
        functools.partial(_expert_kernel, layer=layer), grid_spec=grid_spec,
        out_shape=jax.ShapeDtypeStruct((n_blk_all * MOE_BLOCK, d), BF16), input_output_aliases=aliases,
        compiler_params=_cparams(1), name="expert_ffn",
    )(*operands)


SCATTER_UNROLL = 8


def _row_token_kernel(pad_ref, dest_ref, rt_ref, *, chunk):
    i = pl.program_id(0)

    @pl.when(i == 0)
    def _():
        for e in range(N_EXP + 1):
            def clear(r, carry):
                rt_ref[r] = 0
                return carry
            lax.fori_loop(pad_ref[2 * e], pad_ref[2 * e + 1], clear, 0)

    def place(g, tok):
        k = g * SCATTER_UNROLL
        for j in range(SCATTER_UNROLL):
            rt_ref[dest_ref[k + j]] = tok + j // TOP_K
        return tok + SCATTER_UNROLL // TOP_K
    lax.fori_loop(0, chunk // SCATTER_UNROLL, place, i * (chunk // TOP_K))


def _row_tokens(dest, pad_ranges, n_rows):
    n = dest.shape[0]
    chunk = int(np.gcd(n, SCATTER_CHUNK))
    assert chunk % SCATTER_UNROLL == 0 and SCATTER_UNROLL % TOP_K == 0
    smem = pl.BlockSpec(memory_space=pltpu.SMEM)
    return pl.pallas_call(
        functools.partial(_row_token_kernel, chunk=chunk), grid=(n // chunk,),
        in_specs=[smem, pl.BlockSpec((chunk,), lambda i: (i,), memory_space=pltpu.SMEM)],
        out_specs=smem, out_shape=jax.ShapeDtypeStruct((n_rows,), jnp.int32),
        compiler_params=_cparams(1), name="row_tokens",
    )(pad_ranges, dest)


def _moe(hn, route, counts, layer, w_up, b_up, w_down, b_down):
    n_tok, d = hn.shape
    top_i = route[:, 0:TOP_K].astype(jnp.int32)
    gates = route[:, TOP_K:2 * TOP_K]
    rank = route[:, 2 * TOP_K:3 * TOP_K].astype(jnp.int32)
    n = n_tok * TOP_K
    padded = (counts + MOE_BLOCK - 1) // MOE_BLOCK * MOE_BLOCK
    p_end = jnp.cumsum(padded)
    p_start = p_end - padded
    dest = (p_start[top_i] + rank).reshape(-1)
    n_blk = -(-n // MOE_BLOCK) + N_EXP
    rows = n_blk * MOE_BLOCK
    pad_lo = jnp.concatenate([p_start + counts, p_end[-1:]])
    pad_hi = jnp.concatenate([p_end, jnp.full((1,), rows, p_end.dtype)])
    pad_ranges = jnp.stack([pad_lo, pad_hi], axis=1).reshape(-1).astype(jnp.int32)
    row_tok = _row_tokens(dest.astype(jnp.int32), pad_ranges, rows)
    blk_start = jnp.arange(n_blk, dtype=jnp.int32) * MOE_BLOCK
    blk_exp = jnp.sum((p_end[None, :] <= blk_start[:, None]).astype(jnp.int32), axis=1)
    blk_exp = jnp.minimum(blk_exp, N_EXP - 1)
    n_used = (p_end[-1] // MOE_BLOCK).astype(jnp.int32)
    n_grp = MOE_GROUPS if n_blk % MOE_GROUPS == 0 else 1
    nbg = n_blk // n_grp
    y = None
    for g in range(n_grp):
        be = blk_exp[g * nbg:(g + 1) * nbg]
        fe = jnp.concatenate([jnp.ones((1,), jnp.int32), (be[1:] != be[:-1]).astype(jnp.int32)])
        nu = n_used - g * nbg
        nxt = jnp.sum((be[None, :] <= be[:, None]).astype(jnp.int32), axis=1)
        ne = jnp.where(nxt < jnp.minimum(nu, nbg), be[jnp.minimum(nxt, nbg - 1)], -1).astype(jnp.int32)
        x_rows = hn.at[row_tok[g * nbg * MOE_BLOCK:(g + 1) * nbg * MOE_BLOCK]].get(mode='promise_in_bounds')
        y = _expert_ffn(y, g * nbg, n_blk, x_rows, be, fe, ne, nu.reshape(1), layer, w_up, b_up, w_down, b_down)
    return y, dest.reshape(n_tok, TOP_K), gates


def _pick_rows(y, dest, row0, n_rows):
    return y.at[dest[row0:row0 + n_rows].T].get(mode='promise_in_bounds')


def _ple_kernel(h_ref, y_ref, g_ref, p_ref, np_ref, wg_ref, wp_ref, *rest):
    o_ref = rest[-1]
    h2 = h_ref[...]
    g = g_ref[...]
    for j in range(TOP_K):
        h2 = h2 + y_ref[j].astype(F32) * g[:, j:j + 1]
    ms = jnp.mean(h2 * h2, axis=-1, keepdims=True)
    hn = (h2 * lax.rsqrt(ms + NORM_EPS) * np_ref[...]).astype(BF16)
    gate = jax.nn.sigmoid(_dot(hn, wg_ref[...]))
    o_ref[...] = h2 + gate * _dot(p_ref[...].astype(BF16), wp_ref[...])


def _ple(h1, row0, picked, gates, p, layer, norm_ple, wg_b, wp_b, out_prev=None):
    t, d = h1.shape
    n = gates.shape[0]
    ts = PLE_ROWS
    assert n % ts == 0 and row0 % ts == 0
    off = row0 // ts
    shifted = lambda i: (i + off, 0)
    full = lambda i: (0, 0)
    in_specs = [pl.BlockSpec((ts, d), shifted), pl.BlockSpec((TOP_K, ts, d), lambda i: (0, i, 0)),
                pl.BlockSpec((ts, TOP_K), lambda i: (i, 0)),
                pl.BlockSpec((None, ts, p.shape[2]), lambda i: (layer, i + off, 0)),
                pl.BlockSpec((1, d), full), pl.BlockSpec(wg_b.shape, full), pl.BlockSpec(wp_b.shape, full)]
    operands = [h1, picked, gates, p, norm_ple.reshape(1, d), wg_b, wp_b]
    aliases = {}
    if out_prev is not None:
        aliases = {len(operands): 0}
        in_specs.append(pl.BlockSpec(memory_space=pl.ANY))
        operands.append(out_prev)
    return pl.pallas_call(
        _ple_kernel, grid=(n // ts,), in_specs=in_specs,
        out_specs=pl.BlockSpec((ts, d), shifted), out_shape=jax.ShapeDtypeStruct((t, d), F32),
        input_output_aliases=aliases, compiler_params=_cparams(1), name="ple",
    )(*operands)


def kernel(x_prompt, x_sample, p_prompt, p_sample, cache_a_k, cache_a_v, cache_b_k, cache_b_v, cache_c_k, cache_c_v, cache_c_logf, cache_d_ckv, cache_d_krope, norm_mix, w_in, b_forget, qk_gain, rope_gain, kv_gain, w_uk, w_uv, rel_bias, group_gain, w_out, norm_ffn, w_router, b_router, w_up, b_up, w_down, b_down, norm_ple, w_ple_gate, w_ple_proj):
    nb, s, d = x_prompt.shape
    nd, t, _ = x_sample.shape
    depth = w_in.shape[0]
    past = cache_b_k.shape[2]
    assert PROJ_ROWS % t == 0 and s % PROJ_ROWS == 0

    tab_p = _rope_tables(jnp.arange(s))
    tab_s = _rope_tables(past + jnp.arange(PROJ_ROWS) % t)
    hp = x_prompt.reshape(nb * s, d)
    hs = x_sample.reshape(nd * t, d)
    st_p, st_s = [], []
    keep = min(A_WINDOW, s)
    layered_state = None
    for i in range(depth):
        pw = _proj_weights(w_in[i], b_forget[i], qk_gain[i], rope_gain[i], kv_gain[i], w_uk[i], w_uv[i])
        pr_p = _project(hp, norm_mix[i], pw, tab_p, s // PROJ_ROWS, (depth, i, nb, layered_state))
        layered_state = [pr_p[slot] for slot in STATE_SLOTS]
        pr_s = _project(hs, norm_mix[i], pw, tab_s, 1)
        outs_p = _prompt_attention(pr_p, pr_p[7][i], nb, s, rel_bias[i], qk_gain[i, 3])
        caches = (cache_a_k, cache_a_v, cache_b_k, cache_b_v, cache_c_k, cache_c_v,
                  cache_c_logf, cache_d_ckv, cache_d_krope)
        outs_s = _sample_attention(pr_s, caches, i, pw, w_uk[i], qk_gain[i, 5], rel_bias[i], nd, t)

        def heads(a, n, rows):
            return a.reshape(n, rows, N_HEADS, HEAD_DIM)

        p3 = lambda a: a.reshape(nb, s, a.shape[-1])
        s3 = lambda a: a.reshape(nd, t, a.shape[-1])
        st_p.append((heads(p3(pr_p[0])[:, s - keep:], nb, keep), heads(p3(pr_p[1])[:, s - keep:], nb, keep)))
        ka_all = jnp.concatenate([cache_a_k[i], heads(pr_s[0], nd, t)], axis=1)[:, t:]
        va_all = jnp.concatenate([cache_a_v[i], heads(pr_s[1], nd, t)], axis=1)[:, t:]
        st_s.append((ka_all, va_all,
                     heads(pr_s[2], nd, t), heads(pr_s[3], nd, t), heads(pr_s[4], nd, t), heads(pr_s[5], nd, t),
                     s3(pr_s[7])[..., 0:N_HEADS], s3(pr_s[6]), s3(pr_s[7])[..., ROPE_LO:ROPE_LO + MLA_ROPE]))

        w_out_b = w_out[i].astype(BF16)
        wr_pad = jnp.zeros((d, LANE), F32).at[:, 0:N_EXP].set(w_router[i])
        wr_parts = _split2(wr_pad)
        br_pad = jnp.zeros((1, LANE), F32).at[0, 0:N_EXP].set(b_router[i])
        flat4 = lambda o: o.reshape(-1, N_HEADS * LANE)
        h1_p, hn_p, rt_p, cnt = _out_project([flat4(o) for o in outs_p], hp, group_gain[i], w_out_b, norm_ffn[i],
                                             wr_parts, br_pad, jnp.zeros((1, LANE), F32))
        h1_s, hn_s, rt_s, cnt = _out_project([flat4(o) for o in outs_s], hs, group_gain[i], w_out_b, norm_ffn[i],
                                             wr_parts, br_pad, cnt)

        hn_all = jnp.concatenate([hn_p, hn_s], axis=0)
        rt_all = jnp.concatenate([rt_p, rt_s], axis=0)
        y, dest, gates = _moe(hn_all, rt_all, cnt[0, 0:N_EXP].astype(jnp.int32), i, w_up, b_up, w_down, b_down)

        wg_b = w_ple_gate[i].astype(BF16)
        wp_b = w_ple_proj[i].astype(BF16)
        pp = p_prompt.reshape(depth, nb * s, -1)
        hp = None
        for b in range(nb):
            hp = _ple(h1_p, b * s, _pick_rows(y, dest, b * s, s), gates[b * s:(b + 1) * s], pp, i,
                      norm_ple[i], wg_b, wp_b, hp)
        hs = _ple(h1_s, 0, _pick_rows(y, dest, nb * s, nd * t), gates[nb * s:], p_sample.reshape(depth, nd * t, -1),
                  i, norm_ple[i], wg_b, wp_b)

    bk_t, bv_t, ck_t, cv_t, ckv_all, misc_t = layered_state

    def from_channel_major(a):
        return jnp.transpose(a.reshape(depth, nb, N_HEADS, HEAD_DIM, s), (0, 1, 4, 2, 3))

    state_p = [jnp.stack([st[j] for st in st_p]) for j in range(2)] + \
              [from_channel_major(a) for a in (bk_t, bv_t, ck_t, cv_t)] + \
              [jnp.swapaxes(misc_t[:, :, 0:N_HEADS, :], 2, 3), ckv_all.reshape(depth, nb, s, KV_RANK),
               jnp.swapaxes(misc_t[:, :, ROPE_LO:ROPE_LO + MLA_ROPE, :], 2, 3)]
    state_s = [jnp.stack([st[j] for st in st_s]) for j in range(9)]
    return (hp.reshape(nb, s, d), hs.reshape(nd, t, d), *state_p, *state_s)
```

```python
import functools

import numpy as np
import jax
import jax.numpy as jnp
from jax import lax
from jax.experimental import pallas as pl
from jax.experimental.pallas import tpu as pltpu

F32 = jnp.float32
BF16 = jnp.bfloat16

CHUNK = 64
HEAD_DIM = 64
N_HEADS = 4
GROUP_W = 256
A_LEFT_CHUNKS = 8
A_WINDOW = A_LEFT_CHUNKS * CHUNK
REL_CLIP = 128
MLA_NOPE = 64
MLA_ROPE = 32
KV_RANK = 128
ROPE_BASE = 10000.0
N_EXP = 32
TOP_K = 4
D_FF = 1024
SWIGLU_LIMIT = 7.0
SWIGLU_ALPHA = 1.702
MOE_BLOCK = 512
NORM_EPS = 1e-6
NEG_INF = -1e30
LOG2E = 1.4426950408889634

A_Q = 0
C_F = 2304
D_Q = 2308
D_CKV = D_Q + N_HEADS * (MLA_NOPE + MLA_ROPE)
D_KR = D_CKV + KV_RANK

LANE = 128
SEG_ABC = 0
SEG_DQ = 2304
SEG_CKV = 2816
SEG_MISC = 2944
N_COLS = 3072
ROPE_LO = 64
ROPE_HALF = MLA_ROPE // 2

VMEM_LIMIT = 56 * 1024 * 1024

PROJ_ROWS = 512
ATTN_TQ = 512
ATTN_TK = 1024
MLA_TQ = 2048
STICK_KS = 256
DENSE_ROWS = 512
PLE_ROWS = 512
MOE_GROUPS = 4
SCATTER_CHUNK = 8192

DEAD_LOG2 = -160.0
BOUND_MARGIN = 1.01


def _cparams(n_axes):
    return pltpu.CompilerParams(dimension_semantics=("arbitrary",) * n_axes,
                                vmem_limit_bytes=VMEM_LIMIT)


def _nt_dot(a, b):
    return lax.dot_general(a, b, (((1,), (1,)), ((), ())), preferred_element_type=F32)


def _dot(a, b):
    return jnp.dot(a, b, preferred_element_type=F32)


def _split2(x):
    hi = x.astype(BF16)
    lo = (x - hi.astype(F32)).astype(BF16)
    return hi, lo


def _in_col_map():
    cols = np.full((N_COLS,), -1, np.int64)
    cols[0:2304] = np.arange(2304)
    for h in range(N_HEADS):
        base = D_Q + (MLA_NOPE + MLA_ROPE) * h
        cols[SEG_DQ + LANE * h: SEG_DQ + LANE * h + MLA_NOPE + MLA_ROPE] = base + np.arange(MLA_NOPE + MLA_ROPE)
    cols[SEG_CKV:SEG_CKV + KV_RANK] = D_CKV + np.arange(KV_RANK)
    cols[SEG_MISC:SEG_MISC + N_HEADS] = C_F + np.arange(N_HEADS)
    cols[SEG_MISC + ROPE_LO:SEG_MISC + ROPE_LO + MLA_ROPE] = D_KR + np.arange(MLA_ROPE)
    return cols


def _rope_tables(pos):
    inv = ROPE_BASE ** (-jnp.arange(ROPE_HALF, dtype=F32) / ROPE_HALF)
    ang = pos.astype(F32)[:, None] * inv
    cos, sin = jnp.cos(ang), jnp.sin(ang)
    n = pos.shape[0]
    one = jnp.ones((n, ROPE_LO), F32)
    z16 = jnp.zeros((n, ROPE_HALF), F32)
    z64 = jnp.zeros((n, ROPE_LO), F32)
    z32 = jnp.zeros((n, LANE - ROPE_LO - MLA_ROPE), F32)
    cos_t = jnp.concatenate([one, cos, cos, z32 + 1.0], axis=1)
    sin_a = jnp.concatenate([z64, -sin, z16, z32], axis=1)
    sin_b = jnp.concatenate([z64, z16, sin, z32], axis=1)
    return cos_t, sin_a, sin_b


def _rope(y, cos_t, sin_a, sin_b):
    left = pltpu.roll(y, LANE - ROPE_HALF, axis=1)
    right = pltpu.roll(y, ROPE_HALF, axis=1)
    return y * cos_t + left * sin_a + right * sin_b


def _head_norm(x, m_ref, gain):
    hi, lo = _split2(x * x)
    ssq = _dot(hi, m_ref[...]) + _dot(lo, m_ref[...])
    return x * lax.rsqrt(ssq * (1.0 / HEAD_DIM) + NORM_EPS) * gain


def _value_lane0(h):
    return HEAD_DIM * (h % 2)


def _ones_lane(h):
    return HEAD_DIM - _value_lane0(h)


def _store_padded_q(q_ref, q, scale):
    lane = lax.broadcasted_iota(jnp.int32, (q.shape[0], LANE), 1)
    low = lane < HEAD_DIM
    qs = q * scale
    for h in range(N_HEADS):
        pair = qs[:, LANE * (h // 2): LANE * (h // 2) + LANE]
        keep = low if h % 2 == 0 else jnp.logical_not(low)
        q_ref[:, LANE * h: LANE * h + LANE] = jnp.where(keep, pair, 0.0).astype(BF16)


def _proj_kernel(*refs, channel_major, n_alias):
    x_ref, gmix_ref, w_ref, cos_ref, sa_ref, sb_ref, gv_ref, m64_ref, wuk_ref, wuv_ref = refs[:10]
    (ak_ref, av_ref, bk_ref, bv_ref, ck_ref, cv_ref, ckv_ref, misc_ref,
     qa_ref, ka_ref, va_ref, qb_ref, kb_ref, vb_ref, qc_ref, kc_ref, vc_ref,
     qd_ref, kd_ref, vd_ref) = refs[10 + n_alias:]

    def put_state(ref, val):
        ref[...] = val.T if channel_major else val

    x = x_ref[...]
    ms = jnp.mean(x * x, axis=-1, keepdims=True)
    hn = (x * lax.rsqrt(ms + NORM_EPS) * gmix_ref[...]).astype(BF16)
    sm_scale = (HEAD_DIM ** -0.5) * LOG2E
    rows = x.shape[0]
    lane = lax.broadcasted_iota(jnp.int32, (rows, LANE), 1)
    is_nope = lane < MLA_NOPE
    is_rope = jnp.logical_and(lane >= ROPE_LO, lane < ROPE_LO + MLA_ROPE)
    cos_t, sin_a, sin_b = cos_ref[...], sa_ref[...], sb_ref[...]

    z_a = _dot(hn, w_ref[:, 0:768])
    z_b = _dot(hn, w_ref[:, 768:1536])
    z_c = _dot(hn, w_ref[:, 1536:2304])
    z_l = _dot(hn, w_ref[:, SEG_CKV:N_COLS])
    z_q = _dot(hn, w_ref[:, SEG_DQ:SEG_CKV])

    z = z_a
    aq = _head_norm(z[:, 0:256], m64_ref, gv_ref[0:1, :])
    ak = _head_norm(z[:, 256:512], m64_ref, gv_ref[1:2, :])
    av = z[:, 512:768]
    ak_ref[...] = ak
    av_ref[...] = av
    _store_padded_q(qa_ref, aq, sm_scale)
    ka_ref[...] = ak.astype(BF16)
    va_ref[...] = av.astype(BF16)

    z = z_b
    put_state(bk_ref, z[:, 256:512])
    put_state(bv_ref, z[:, 512:768])
    _store_padded_q(qb_ref, z[:, 0:256], sm_scale)
    kb_ref[...] = z[:, 256:512].astype(BF16)
    vb_ref[...] = z[:, 512:768].astype(BF16)

    z = z_c
    cq = _head_norm(z[:, 0:256], m64_ref, gv_ref[2:3, :])
    ck = _head_norm(z[:, 256:512], m64_ref, gv_ref[3:4, :])
    cv = z[:, 512:768]
    put_state(ck_ref, ck)
    put_state(cv_ref, cv)
    _store_padded_q(qc_ref, cq, sm_scale)
    kc_ref[...] = ck.astype(BF16)
    vc_ref[...] = cv.astype(BF16)

    z = z_l
    zc = z[:, 0:KV_RANK]
    ckv = zc * lax.rsqrt(jnp.mean(zc * zc, axis=-1, keepdims=True) + NORM_EPS) * gv_ref[7:8, 0:LANE]
    ckv_ref[...] = ckv
    zm = z[:, KV_RANK:2 * KV_RANK]
    ssr = jnp.sum(jnp.where(is_rope, zm * zm, 0.0), axis=-1, keepdims=True)
    kr = zm * lax.rsqrt(ssr * (1.0 / MLA_ROPE) + NORM_EPS) * gv_ref[6:7, 0:LANE]
    kr = _rope(kr, cos_t, sin_a, sin_b)
    zf = zm + gv_ref[8:9, 0:LANE]
    clf = jnp.minimum(zf, 0.0) - jnp.log1p(jnp.exp(-jnp.abs(zf)))
    put_state(misc_ref, jnp.where(lane < N_HEADS, clf, kr))

    ckv_b = ckv.astype(BF16)
    kn = _dot(ckv_b, wuk_ref[...])
    for h in range(N_HEADS):
        hs = slice(LANE * h, LANE * h + LANE)
        ones = (lane == _ones_lane(h)).astype(F32)
        vd_ref[:, hs] = (_dot(ckv_b, wuv_ref[:, hs]) + ones).astype(BF16)
    for h in range(N_HEADS):
        xh = kn[:, LANE * h: LANE * h + LANE]
        ss = jnp.sum(xh * xh, axis=-1, keepdims=True)
        yh = xh * lax.rsqrt(ss * (1.0 / MLA_NOPE) + NORM_EPS) * gv_ref[5:6, 0:LANE]
        kd_ref[:, LANE * h: LANE * h + LANE] = (yh + kr).astype(BF16)

    z = z_q
    d_scale = ((MLA_NOPE + MLA_ROPE) ** -0.5) * LOG2E
    for h in range(N_HEADS):
        xh = z[:, LANE * h: LANE * h + LANE]
        x2 = xh * xh
        ssn = jnp.sum(jnp.where(is_nope, x2, 0.0), axis=-1, keepdims=True)
        ssr = jnp.sum(jnp.where(is_rope, x2, 0.0), axis=-1, keepdims=True)
        rn = lax.rsqrt(ssn * (1.0 / MLA_NOPE) + NORM_EPS)
        rr = lax.rsqrt(ssr * (1.0 / MLA_ROPE) + NORM_EPS)
        yh = xh * jnp.where(is_nope, rn, rr) * gv_ref[4:5, 0:LANE]
        yh = _rope(yh, cos_t, sin_a, sin_b)
        qd_ref[:, LANE * h: LANE * h + LANE] = (yh * d_scale).astype(BF16)


def _proj_weights(w_in, b_forget, qk_gain, rope_gain, kv_gain, w_uk, w_uv):
    cols = _in_col_map()
    valid = jnp.asarray(cols >= 0)
    w = jnp.where(valid[None, :], w_in[:, np.maximum(cols, 0)], 0.0).astype(BF16)

    def tile4(g):
        return jnp.tile(g, N_HEADS)

    gv = jnp.zeros((16, GROUP_W), F32)
    gv = gv.at[0].set(tile4(qk_gain[0])).at[1].set(tile4(qk_gain[1]))
    gv = gv.at[2].set(tile4(qk_gain[2])).at[3].set(tile4(qk_gain[3]))
    gv = gv.at[4, 0:MLA_NOPE].set(qk_gain[4]).at[4, ROPE_LO:ROPE_LO + MLA_ROPE].set(rope_gain[0])
    gv = gv.at[5, 0:MLA_NOPE].set(qk_gain[5])
    gv = gv.at[6, ROPE_LO:ROPE_LO + MLA_ROPE].set(rope_gain[1])
    gv = gv.at[7, 0:KV_RANK].set(kv_gain)
    gv = gv.at[8, 0:N_HEADS].set(b_forget)
    head = np.arange(GROUP_W) // HEAD_DIM
    m64 = jnp.asarray((head[:, None] == head[None, :]).astype(np.float32), BF16)
    wuk = jnp.zeros((KV_RANK, N_HEADS * LANE), F32)
    for h in range(N_HEADS):
        wuk = wuk.at[:, LANE * h: LANE * h + MLA_NOPE].set(w_uk[:, MLA_NOPE * h: MLA_NOPE * (h + 1)])
    wuv = jnp.zeros((KV_RANK, N_HEADS * LANE), F32)
    for h in range(N_HEADS):
        lo = LANE * h + _value_lane0(h)
        wuv = wuv.at[:, lo: lo + HEAD_DIM].set(w_uv[:, HEAD_DIM * h: HEAD_DIM * (h + 1)])
    return w, gv, m64, wuk.astype(BF16), wuv.astype(BF16)


STATE_SLOTS = (2, 3, 4, 5, 6, 7)


def _project(x, gmix, pw, tables, n_tab_blocks, layered=None):
    w, gv, m64, wuk, wuv = pw
    t, d = x.shape
    ts = PROJ_ROWS
    assert t % ts == 0
    row = lambda i: (i, 0)
    full = lambda i: (0, 0)
    tab = lambda i: (i % n_tab_blocks, 0)
    f32_w = [GROUP_W] * 6 + [KV_RANK, LANE]
    bf_w = [512, 256, 256, 512, 256, 256, 512, 256, 256, 512, 512, 512]
    out_shape = [jax.ShapeDtypeStruct((t, c), F32) for c in f32_w] + \
                [jax.ShapeDtypeStruct((t, c), BF16) for c in bf_w]
    out_specs = [pl.BlockSpec((ts, c), row) for c in f32_w + bf_w]
    in_specs = [pl.BlockSpec((ts, d), row), pl.BlockSpec((1, d), full), pl.BlockSpec((d, N_COLS), full),
                pl.BlockSpec((ts, LANE), tab), pl.BlockSpec((ts, LANE), tab), pl.BlockSpec((ts, LANE), tab),
                pl.BlockSpec(gv.shape, full), pl.BlockSpec(m64.shape, full),
                pl.BlockSpec(wuk.shape, full), pl.BlockSpec(wuv.shape, full)]
    operands = [x, gmix.reshape(1, d), w, *tables, gv, m64, wuk, wuv]
    aliases = {}
    if layered is not None:
        depth, layer, nb, earlier = layered
        s = t // nb
        nt = s // ts
        for slot in STATE_SLOTS:
            c = f32_w[slot]
            if slot == 6:
                out_shape[slot] = jax.ShapeDtypeStruct((depth, t, c), F32)
                out_specs[slot] = pl.BlockSpec((None, ts, c), lambda i: (layer, i, 0))
            else:
                out_shape[slot] = jax.ShapeDtypeStruct((depth, nb, c, s), F32)
                out_specs[slot] = pl.BlockSpec((None, None, c, ts), lambda i: (layer, i // nt, 0, i % nt))
        if earlier is not None:
            for k, slot in enumerate(STATE_SLOTS):
                aliases[len(operands)] = slot
                in_specs.append(pl.BlockSpec(memory_space=pl.ANY))
                operands.append(earlier[k])
    kern = functools.partial(_proj_kernel, channel_major=layered is not None, n_alias=len(aliases))
    return pl.pallas_call(
        kern, grid=(t // ts,), in_specs=in_specs, out_specs=out_specs, out_shape=out_shape,
        input_output_aliases=aliases, compiler_params=_cparams(1), name="proj",
    )(*operands)


def _softmax_step(q, kb, vb, carry, bias=None, ok=None):
    m, l, acc = carry
    s = _nt_dot(q, kb)
    if bias is not None:
        s = s + bias
    if ok is not None:
        s = jnp.where(ok, s, NEG_INF)
    m_new = jnp.maximum(m, jnp.max(s, axis=1, keepdims=True))
    alpha = jnp.exp2(m - m_new)
    p = jnp.exp2(s - m_new)
    l = alpha * l + jnp.sum(p, axis=1, keepdims=True)
    acc = alpha * acc + _dot(p.astype(BF16), vb)
    return m_new, l, acc


def _softmax_init(tq):
    return (jnp.full((tq, 1), NEG_INF, F32), jnp.zeros((tq, 1), F32), jnp.zeros((tq, LANE), F32))


def _pick_lane(block, h):
    col = lax.broadcasted_iota(jnp.int32, block.shape, 1)
    return jnp.sum(jnp.where(col == h, block, 0.0), axis=1, keepdims=True)


def _stack_pair(q_ref):
    return jnp.concatenate([q_ref[0, :, 0:LANE], q_ref[0, :, LANE:2 * LANE]], axis=0)


def _store_pair(o_ref, o, tq):
    o_ref[0, :, 0:LANE] = o[:tq].astype(o_ref.dtype)
    o_ref[0, :, LANE:2 * LANE] = o[tq:].astype(o_ref.dtype)


def _flash_forget_kernel(kmax_ref, decay_ref, q_ref, k_ref, v_ref, fq_ref, fk_ref, o_ref, *, tq, n_blk):
    b = pl.program_id(0)
    pair = pl.program_id(1)
    qi = pl.program_id(2)
    q = _stack_pair(q_ref)
    fq = jnp.concatenate([_pick_lane(fq_ref[0], 2 * pair), _pick_lane(fq_ref[0], 2 * pair + 1)], axis=0) * LOG2E

    def step(j, carry, diag):
        off = pl.multiple_of(j * tq, tq)
        kb = k_ref[0, pl.ds(off, tq), :]
        vb = v_ref[0, pl.ds(off, tq), :]
        fk = jnp.concatenate([jnp.broadcast_to(fk_ref[0, 0, pl.ds(j, 1), :], (tq, tq)),
                              jnp.broadcast_to(fk_ref[0, 1, pl.ds(j, 1), :], (tq, tq))], axis=0)
        ok = None
        if diag:
            row = lax.broadcasted_iota(jnp.int32, (2 * tq, tq), 0) % tq
            col = lax.broadcasted_iota(jnp.int32, (2 * tq, tq), 1)
            ok = col <= row
        return _softmax_step(q, kb, vb, carry, fq - fk * LOG2E, ok)

    carry = step(qi, _softmax_init(2 * tq), True)
    qf = q.astype(F32)
    q_norm = jnp.sqrt(jnp.sum(qf * qf, axis=1, keepdims=True))
    head0 = b * N_HEADS + 2 * pair
    room = q_norm * jnp.where(lax.broadcasted_iota(jnp.int32, q_norm.shape, 0) < tq,
                              kmax_ref[head0], kmax_ref[head0 + 1]) + fq - carry[0]
    slack0 = jnp.max(room[:tq]) - DEAD_LOG2
    slack1 = jnp.max(room[tq:]) - DEAD_LOG2

    def live(state):
        j = jnp.maximum(state[0], 0)
        alive = jnp.logical_or(slack0 + decay_ref[head0 * n_blk + j] >= 0.0,
                               slack1 + decay_ref[(head0 + 1) * n_blk + j] >= 0.0)
        return jnp.logical_and(state[0] >= 0, alive)

    def older(state):
        j = state[0]
        return (j - 1,) + step(j, state[1:], False)

    _, m, l, acc = lax.while_loop(live, older, (qi - 1,) + carry)
    _store_pair(o_ref, acc / l, tq)


def _flash_mla_kernel(q_ref, k_ref, v_ref, o_ref, *, tq, tk):
    qi = pl.program_id(2)
    n_sub = tq // tk

    def step(hh, j, carry, row0=None):
        m, acc = carry
        hs = slice(LANE * hh, LANE * hh + LANE)
        off = pl.multiple_of(j * tk, tk)
        qs = q_ref[0, :, hs] if row0 is None else q_ref[0, row0:, hs]
        s = _nt_dot(qs, k_ref[0, pl.ds(off, tk), hs])
        if row0 is not None:
            row = lax.broadcasted_iota(jnp.int32, s.shape, 0)
            col = lax.broadcasted_iota(jnp.int32, s.shape, 1)
            s = jnp.where(col // CHUNK <= row // CHUNK, s, NEG_INF)
        m_new = jnp.maximum(m, jnp.max(s, axis=1, keepdims=True))
        p = jnp.exp2(s - m_new)
        acc = jnp.exp2(m - m_new) * acc + _dot(p.astype(BF16), v_ref[0, pl.ds(off, tk), hs])
        return m_new, acc

    def both(g, carry):
        c0, c1 = carry
        for r in range(n_sub):
            c0, c1 = step(0, g * n_sub + r, c0), step(1, g * n_sub + r, c1)
        return c0, c1

    init = (jnp.full((tq, 1), NEG_INF, F32), jnp.zeros((tq, LANE), F32))
    carry = lax.fori_loop(0, qi, both, (init, init))
    for hh in range(2):
        m, acc = carry[hh]
        for r in range(n_sub):
            row0 = r * tk
            m_r, acc_r = step(hh, qi * n_sub + r, (m[row0:], acc[row0:]), row0)
            m = m_r if r == 0 else jnp.concatenate([m[:row0], m_r], axis=0)
            acc = acc_r if r == 0 else jnp.concatenate([acc[:row0], acc_r], axis=0)
        ones_at = _ones_lane(hh)
        o_ref[0, :, LANE * hh: LANE * hh + LANE] = (acc / acc[:, ones_at:ones_at + 1]).astype(o_ref.dtype)


def _flash_stick_kernel(q_ref, k_ref, v_ref, u_ref, o_ref, *, tq, ks):
    qi = pl.program_id(2)
    q = _stack_pair(q_ref)
    n_sub = tq // ks

    def step(jb, carry, diag):
        c, acc = carry
        off = pl.multiple_of(jb * ks, ks)
        kb = k_ref[0, pl.ds(off, ks), :]
        vb = v_ref[0, pl.ds(off, ks), :]
        vis = None
        if diag:
            row = lax.broadcasted_iota(jnp.int32, (2 * tq, ks), 0) % tq + qi * tq
            col = lax.broadcasted_iota(jnp.int32, (2 * tq, ks), 1) + jb * ks
            vis = col < row
        c_new, w = _stick_weights(_nt_dot(q, kb), c, u_ref[...], vis)
        return c_new, acc + _dot(w.astype(BF16), vb)

    carry = (jnp.zeros((2 * tq, 1), F32), jnp.zeros((2 * tq, LANE), F32))
    for r in range(n_sub):
        carry = step(qi * n_sub + (n_sub - 1 - r), carry, True)

    def live(state):
        return jnp.logical_and(state[0] >= 0, state[1] > DEAD_LOG2)

    def older(state):
        c, acc = step(state[0], state[2:], False)
        return state[0] - 1, jnp.max(c), c, acc

    _, _, _, acc = lax.while_loop(live, older, (qi * n_sub - 1, jnp.max(carry[0])) + carry)
    _store_pair(o_ref, acc, tq)


def _stick_weights(z, c, u, vis):
    sp = jnp.log(1.0 + jnp.exp2(-jnp.abs(z))) * LOG2E
    log_rest = jnp.minimum(-z, 0.0) - sp
    log_beta = log_rest + z
    if vis is not None:
        log_rest = jnp.where(vis, log_rest, 0.0)
    hi, lo = _split2(log_rest)
    between = _dot(hi, u) + _dot(lo, u)
    w = jnp.exp2(log_beta + between + c)
    if vis is not None:
        w = jnp.where(vis, w, 0.0)
    return c + jnp.sum(log_rest, axis=1, keepdims=True), w


def _band_kernel(q_ref, k_ref, v_ref, bd_ref, bp_ref, o_ref, *, tq):
    qi = pl.program_id(2)
    q = _stack_pair(q_ref)
    off = pl.multiple_of(qi * tq, tq)
    carry = _softmax_step(q, k_ref[0, pl.ds(off, tq), :], v_ref[0, pl.ds(off, tq), :],
                          _softmax_init(2 * tq), bd_ref[...].reshape(2 * tq, tq))
    offp = pl.multiple_of(jnp.maximum(qi - 1, 0) * tq, tq)
    no_prev = jnp.where(qi == 0, NEG_INF, 0.0)
    m, l, acc = _softmax_step(q, k_ref[0, pl.ds(offp, tq), :], v_ref[0, pl.ds(offp, tq), :],
                              carry, bp_ref[...].reshape(2 * tq, tq) + no_prev)
    _store_pair(o_ref, acc / l, tq)


def _pair_specs(s, tq):
    q_spec = pl.BlockSpec((1, tq, 2 * LANE), lambda b, p, i: (b, i, p))
    kv_spec = pl.BlockSpec((1, s, LANE), lambda b, p, i: (b, 0, p))
    return q_spec, kv_spec


def _pair_call(kern, nb, s, tq, in_specs, operands, name):
    return pl.pallas_call(
        kern, grid=(nb, N_HEADS // 2, s // tq), in_specs=in_specs,
        out_specs=pl.BlockSpec((1, tq, 2 * LANE), lambda b, p, i: (b, i, p)),
        out_shape=jax.ShapeDtypeStruct((nb, s, N_HEADS * LANE), BF16),
        compiler_params=_cparams(3), name=name,
    )(*operands)


def _strict_upper(n):
    idx = np.arange(n)
    return jnp.asarray((idx[:, None] > idx[None, :]).astype(np.float32), BF16)


def _toeplitz(vec, n_rows, n_cols):
    length = n_rows + n_cols - 1
    assert vec.shape[-1] == length
    lead = vec.shape[:-1]
    rev = jnp.concatenate([vec[..., ::-1], jnp.zeros(lead + (1,), vec.dtype)], axis=-1)
    flat = jnp.tile(rev, (1,) * len(lead) + (n_rows,))[..., :n_rows * length]
    return flat.reshape(lead + (n_rows, length))[..., n_rows - 1: n_rows - 1 + n_cols]


def _rel_bias_tile(rel_bias, n_rows, n_cols, rel00, ok):
    d = np.arange(n_rows + n_cols - 1) - (n_cols - 1) + rel00
    vec = rel_bias.astype(F32)[:, np.clip(d, -REL_CLIP, REL_CLIP) + REL_CLIP] * LOG2E
    return jnp.where(jnp.asarray(ok)[None], _toeplitz(vec, n_rows, n_cols), NEG_INF)


def _band_bias_tiles(rel_bias, tq):
    i = np.arange(tq)[:, None]
    j = np.arange(tq)[None, :]
    own = _rel_bias_tile(rel_bias, tq, tq, 0, (j // CHUNK) <= (i // CHUNK))
    prev = _rel_bias_tile(rel_bias, tq, tq, tq, (j // CHUNK) >= (i // CHUNK) + tq // CHUNK - A_LEFT_CHUNKS)
    return own, prev


def _prompt_attention(pr, misc_t, nb, s, rel_bias, k_gain):
    tq = ATTN_TQ
    assert s % tq == 0 and tq == A_WINDOW
    r3 = lambda a: a.reshape(nb, s, a.shape[-1])
    qa, ka, va, qb, kb, vb, qc, kc, vc, qd, kd, vd = [r3(a) for a in pr[8:20]]
    q_spec, kv_spec = _pair_specs(s, tq)

    bd, bp = _band_bias_tiles(rel_bias, tq)
    b_spec = pl.BlockSpec((2, tq, tq), lambda b, p, i: (p, 0, 0))
    o_a = _pair_call(functools.partial(_band_kernel, tq=tq), nb, s, tq,
                     [q_spec, kv_spec, kv_spec, b_spec, b_spec], (qa, ka, va, bd, bp), "attn_band")

    u = _strict_upper(STICK_KS)
    o_b = _pair_call(functools.partial(_flash_stick_kernel, tq=tq, ks=STICK_KS), nb, s, tq,
                     [q_spec, kv_spec, kv_spec, pl.BlockSpec(u.shape, lambda b, p, i: (0, 0))],
                     (qb, kb, vb, u), "attn_stick")

    n_blk = s // tq
    fk_t = jnp.cumsum(misc_t[:, 0:N_HEADS, :], axis=2)
    c_cum = jnp.swapaxes(fk_t, 1, 2)
    fk = fk_t.reshape(nb, N_HEADS, n_blk, tq)
    kmax = jnp.full((nb * N_HEADS,), HEAD_DIM ** 0.5 * BOUND_MARGIN, F32) * jnp.max(jnp.abs(k_gain))
    decay = lax.cummax(jnp.max(-fk, axis=-1), axis=2) * LOG2E
    smem = pl.BlockSpec(memory_space=pltpu.SMEM)
    fq_spec = pl.BlockSpec((1, tq, N_HEADS), lambda b, p, i: (b, i, 0))
    fk_spec = pl.BlockSpec((1, 2, n_blk, tq), lambda b, p, i: (b, p, 0, 0))
    o_c = _pair_call(functools.partial(_flash_forget_kernel, tq=tq, n_blk=n_blk), nb, s, tq,
                     [smem, smem, q_spec, kv_spec, kv_spec, fq_spec, fk_spec],
                     (kmax, decay.reshape(-1), qc, kc, vc, c_cum, fk), "attn_forget")

    tqd = MLA_TQ if s % MLA_TQ == 0 else tq
    qd_spec = pl.BlockSpec((1, tqd, 2 * LANE), lambda b, p, i: (b, i, p))
    kvd_spec = pl.BlockSpec((1, s, 2 * LANE), lambda b, p, i: (b, 0, p), pipeline_mode=pl.Buffered(1))
    o_d = _pair_call(functools.partial(_flash_mla_kernel, tq=tqd, tk=min(ATTN_TK, tqd)), nb, s, tqd,
                     [qd_spec, kvd_spec, kvd_spec], (qd, kd, vd), "attn_mla")
    return o_a, o_b, o_c, o_d


def _two_block_softmax(s1, s2, v1_t, v2):
    m = jnp.maximum(jnp.max(s1, axis=1, keepdims=True), jnp.max(s2, axis=1, keepdims=True))
    p1 = jnp.exp2(s1 - m)
    p2 = jnp.exp2(s2 - m)
    l = jnp.sum(p1, axis=1, keepdims=True) + jnp.sum(p2, axis=1, keepdims=True)
    return (_nt_dot(p1.astype(BF16), v1_t) + _dot(p2.astype(BF16), v2)) / l


def _sample_kernel(qa_ref, ka_ref, va_ref, qb_ref, kb_ref, vb_ref, qc_ref, kc_ref, vc_ref, qd_ref, kd_ref, vd_ref,
                   cak_ref, cav_ref, cbk_ref, cbv_ref, cck_ref, ccv_ref, cckv_ref, ckr_ref,
                   ba_c_ref, ba_n_ref, fq_ref, fkc_ref, fkn_ref,
                   u_ref, un_ref, wukt_ref, wuv_ref, gk_ref,
                   oa_ref, ob_ref, oc_ref, od_ref, *, t, past, ks):
    row = lax.broadcasted_iota(jnp.int32, (t, t), 0)
    col = lax.broadcasted_iota(jnp.int32, (t, t), 1)
    causal_bias = jnp.where(col <= row, 0.0, NEG_INF)
    chunk_bias = jnp.where((past + col) // CHUNK <= (past + row) // CHUNK, 0.0, NEG_INF)
    strict = col < row

    ckv_c = cckv_ref[0].astype(BF16)
    kr_t = ckr_ref[0].astype(BF16)
    pad_t = jnp.zeros((LANE - MLA_NOPE - MLA_ROPE, past), BF16)

    for h in range(N_HEADS):
        hs = slice(LANE * h, LANE * h + LANE)
        ps = slice(LANE * (h // 2), LANE * (h // 2) + LANE)

        q = qa_ref[0][:, hs]
        s1 = _dot(q, cak_ref[0, ps, :].astype(BF16)) + ba_c_ref[h]
        s2 = _nt_dot(q, ka_ref[0][:, ps]) + ba_n_ref[h]
        oa_ref[0, :, hs] = _two_block_softmax(s1, s2, cav_ref[0, ps, :].astype(BF16), va_ref[0][:, ps])

        q = qb_ref[0][:, hs]
        c, w = _stick_weights(_nt_dot(q, kb_ref[0][:, ps]), jnp.zeros((t, 1), F32), un_ref[...], strict)
        acc = _dot(w.astype(BF16), vb_ref[0][:, ps])

        def b_live(state):
            return jnp.logical_and(state[0] >= 0, state[1] > DEAD_LOG2)

        def b_older(state, q=q, ps=ps):
            jb, _, c, acc = state
            off = pl.multiple_of(jb * ks, ks)
            kb_t = cbk_ref[0, ps, pl.ds(off, ks)].astype(BF16)
            vb_t = cbv_ref[0, ps, pl.ds(off, ks)].astype(BF16)
            c, w = _stick_weights(_dot(q, kb_t), c, u_ref[...], None)
            return jb - 1, jnp.max(c), c, acc + _nt_dot(w.astype(BF16), vb_t)

        _, _, _, acc = lax.while_loop(b_live, b_older, (past // ks - 1, jnp.max(c), c, acc))
        ob_ref[0, :, hs] = acc

        q = qc_ref[0][:, hs]
        fq = fq_ref[0][:, h:h + 1] * LOG2E
        s1 = _dot(q, cck_ref[0, ps, :].astype(BF16)) + (fq - fkc_ref[0, h:h + 1, :] * LOG2E)
        s2 = _nt_dot(q, kc_ref[0][:, ps]) + (fq - fkn_ref[0, h:h + 1, :] * LOG2E) + causal_bias
        oc_ref[0, :, hs] = _two_block_softmax(s1, s2, ccv_ref[0, ps, :].astype(BF16), vc_ref[0][:, ps])

        q = qd_ref[0][:, hs]
        kn_t = _nt_dot(wukt_ref[MLA_NOPE * h: MLA_NOPE * (h + 1), :], ckv_c)
        ss = jnp.sum(kn_t * kn_t, axis=0, keepdims=True)
        kn_t = kn_t * lax.rsqrt(ss * (1.0 / MLA_NOPE) + NORM_EPS) * gk_ref[...]
        k_t = jnp.concatenate([kn_t.astype(BF16), kr_t, pad_t], axis=0)
        s1 = _dot(q, k_t)
        s2 = _nt_dot(q, kd_ref[0][:, hs]) + chunk_bias
        v_c = _dot(ckv_c, wuv_ref[:, hs]).astype(BF16)
        m = jnp.maximum(jnp.max(s1, axis=1, keepdims=True), jnp.max(s2, axis=1, keepdims=True))
        p1 = jnp.exp2(s1 - m)
        p2 = jnp.exp2(s2 - m)
        l = jnp.sum(p1, axis=1, keepdims=True) + jnp.sum(p2, axis=1, keepdims=True)
        od_ref[0, :, hs] = (_dot(p1.astype(BF16), v_c) + _dot(p2.astype(BF16), vd_ref[0][:, hs])) / l


def _channel_major(cache):
    depth, nb, rows = cache.shape[:3]
    return jnp.transpose(cache, (0, 1, 3, 4, 2)).reshape(depth, nb, GROUP_W, rows)


def _sample_attention(pr, caches, layer, pw, w_uk, k_gain, rel_bias, nb, t):
    a_k, a_v, b_k, b_v, c_k, c_v, c_lf, d_ckv, d_kr = caches
    wuv = pw[4]
    past = b_k.shape[2]
    win = a_k.shape[2]
    ks = STICK_KS
    assert past % ks == 0
    r3 = lambda a: a.reshape(nb, t, a.shape[-1])
    news = [r3(a) for a in pr[8:20]]

    qpos = past + np.arange(t)
    kpos = past - win + np.arange(win + t)
    qc, kc = qpos // CHUNK, kpos // CHUNK
    ok = (kc[None, :] <= qc[:, None]) & (kc[None, :] >= qc[:, None] - A_LEFT_CHUNKS)
    ba = _rel_bias_tile(rel_bias, t, win + t, win, ok)
    ba_c, ba_n = ba[:, :, :win], ba[:, :, win:]

    clf_new = jnp.swapaxes(r3(pr[7])[..., 0:N_HEADS], 1, 2)
    lf_c = jnp.swapaxes(c_lf[layer], 1, 2).astype(F32)
    fk = jnp.cumsum(jnp.concatenate([lf_c, clf_new], axis=2), axis=2)
    fk_c, fk_n = fk[:, :, :past], fk[:, :, past:]
    fq = jnp.swapaxes(fk_n, 1, 2)

    wukt = w_uk.T.astype(BF16)
    gk = k_gain.reshape(MLA_NOPE, 1)
    u, un = _strict_upper(ks), _strict_upper(t)

    per_b = lambda shape: pl.BlockSpec((1,) + shape, lambda b: (b,) + (0,) * len(shape))
    per_lb = lambda shape: pl.BlockSpec((None, 1) + shape, lambda b: (layer, b) + (0,) * len(shape))
    const = lambda a: pl.BlockSpec(a.shape, lambda b: (0,) * a.ndim)
    cache_ops = [_channel_major(c) for c in (a_k, a_v, b_k, b_v, c_k, c_v)] + [d_ckv, jnp.swapaxes(d_kr, 2, 3)]
    operands = news + cache_ops + [ba_c, ba_n, fq, fk_c, fk_n, u, un, wukt, wuv, gk]
    in_specs = [per_b(a.shape[1:]) for a in news] + [per_lb(a.shape[2:]) for a in cache_ops] + \
               [const(ba_c), const(ba_n)] + [per_b(a.shape[1:]) for a in (fq, fk_c, fk_n)] + \
               [const(a) for a in (u, un, wukt, wuv, gk)]
    out_shape = [jax.ShapeDtypeStruct((nb, t, N_HEADS * LANE), F32)] * 4
    out_specs = [per_b((t, N_HEADS * LANE))] * 4
    return pl.pallas_call(
        functools.partial(_sample_kernel, t=t, past=past, ks=ks), grid=(nb,),
        in_specs=in_specs, out_specs=out_specs, out_shape=out_shape,
        compiler_params=_cparams(1), name="attn_sample",
    )(*operands)


def _out_kernel(oa_ref, ob_ref, oc_ref, od_ref, h_ref, gg_ref, wo_ref, nf_ref,
                wr1_ref, wr2_ref, br_ref, tri_ref, cnt0_ref, h1_ref, hn_ref, route_ref, cnt_ref):
    rows = h_ref.shape[0]
    low = lax.broadcasted_iota(jnp.int32, (rows, LANE), 1) < HEAD_DIM
    h1 = h_ref[...]
    for g, o_ref in enumerate((oa_ref, ob_ref, oc_ref, od_ref)):
        p0 = jnp.where(low, o_ref[:, 0:LANE], o_ref[:, LANE:2 * LANE])
        p1 = jnp.where(low, o_ref[:, 2 * LANE:3 * LANE], o_ref[:, 3 * LANE:4 * LANE])
        og = jnp.concatenate([p0, p1], axis=1).astype(F32)
        ms = jnp.mean(og * og, axis=-1, keepdims=True)
        y = (og * lax.rsqrt(ms + NORM_EPS) * gg_ref[g:g + 1, :]).astype(BF16)
        h1 = h1 + _dot(y, wo_ref[GROUP_W * g: GROUP_W * (g + 1), :])
    h1_ref[...] = h1
    ms = jnp.mean(h1 * h1, axis=-1, keepdims=True)
    hn = h1 * lax.rsqrt(ms + NORM_EPS) * nf_ref[...]
    hn_ref[...] = hn.astype(BF16)
    a1, a2 = _split2(hn)
    w1, w2 = wr1_ref[...], wr2_ref[...]
    logits = _dot(a1, w1) + _dot(a1, w2) + _dot(a2, w1) + br_ref[...]

    @pl.when(pl.program_id(0) == 0)
    def _():
        cnt_ref[...] = cnt0_ref[...]

    lane = lax.broadcasted_iota(jnp.int32, (rows, LANE), 1)
    x = jnp.where(lane < N_EXP, logits, NEG_INF)
    picks, tops, ids = [], [], []
    for j in range(TOP_K):
        top = jnp.max(x, axis=1, keepdims=True)
        idx = jnp.min(jnp.where(x == top, lane, LANE), axis=1, keepdims=True)
        hit = lane == idx
        x = jnp.where(hit, NEG_INF, x)
        picks.append(hit)
        tops.append(top)
        ids.append(idx)
    chosen = functools.reduce(jnp.logical_or, picks)
    before = _dot(tri_ref[...], jnp.where(chosen, 1.0, 0.0).astype(BF16)) + cnt_ref[...]
    exps = [jnp.exp(top - tops[0]) for top in tops]
    total = functools.reduce(jnp.add, exps)
    route = jnp.zeros((rows, LANE), F32)
    for j in range(TOP_K):
        rank = jnp.sum(jnp.where(picks[j], before, 0.0), axis=1, keepdims=True)
        route = jnp.where(lane == j, ids[j].astype(F32), route)
        route = jnp.where(lane == TOP_K + j, exps[j] / total, route)
        route = jnp.where(lane == 2 * TOP_K + j, rank, route)
    route_ref[...] = route
    cnt_ref[...] = cnt_ref[...] + jnp.sum(jnp.where(chosen, 1.0, 0.0), axis=0, keepdims=True)


def _out_project(outs, h, group_gain, w_out_b, norm_ffn, wr_parts, br_pad, counts0):
    t, d = h.shape
    ts = DENSE_ROWS
    assert t % ts == 0
    row = lambda i: (i, 0)
    full = lambda i: (0, 0)
    idx = np.arange(ts)
    tri = jnp.asarray((idx[:, None] > idx[None, :]).astype(np.float32), BF16)
    o_spec = pl.BlockSpec((ts, N_HEADS * LANE), row)
    in_specs = [o_spec] * 4 + [pl.BlockSpec((ts, d), row), pl.BlockSpec(group_gain.shape, full),
                               pl.BlockSpec(w_out_b.shape, full), pl.BlockSpec((1, d), full)] + \
               [pl.BlockSpec((d, LANE), full)] * len(wr_parts) + [pl.BlockSpec((1, LANE), full)] + \
               [pl.BlockSpec((ts, ts), full), pl.BlockSpec((1, LANE), full)]
    out_shape = [jax.ShapeDtypeStruct((t, d), F32), jax.ShapeDtypeStruct((t, d), BF16),
                 jax.ShapeDtypeStruct((t, LANE), F32), jax.ShapeDtypeStruct((1, LANE), F32)]
    out_specs = [pl.BlockSpec((ts, d), row), pl.BlockSpec((ts, d), row), pl.BlockSpec((ts, LANE), row),
                 pl.BlockSpec((1, LANE), full)]
    return pl.pallas_call(
        _out_kernel, grid=(t // ts,), in_specs=in_specs, out_specs=out_specs, out_shape=out_shape,
        compiler_params=_cparams(1), name="out_proj",
    )(*outs, h, group_gain, w_out_b, norm_ffn.reshape(1, d), *wr_parts, br_pad, tri, counts0)


def _expert_kernel(be_ref, fe_ref, ne_ref, nu_ref, x_ref, wu_hbm, bu_ref, wd_hbm, bd_ref, *rest, layer):
    y_ref, wu_stage, wd_stage, wub_ref, wdb_ref, sems = rest[-6:]
    i = pl.program_id(0)
    used = i < nu_ref[0]

    def weight_copies(expert):
        return (pltpu.make_async_copy(wu_hbm.at[layer, expert], wu_stage, sems.at[0]),
                pltpu.make_async_copy(wd_hbm.at[layer, expert], wd_stage, sems.at[1]))

    @pl.when(jnp.logical_and(used, fe_ref[i] == 1))
    def _():
        @pl.when(i == 0)
        def _():
            for cp in weight_copies(be_ref[0]):
                cp.start()
        for cp in weight_copies(be_ref[i]):
            cp.wait()
        wub_ref[...] = wu_stage[...].astype(BF16)
        wdb_ref[...] = wd_stage[...].astype(BF16)

        @pl.when(ne_ref[i] >= 0)
        def _():
            for cp in weight_copies(ne_ref[i]):
                cp.start()

    @pl.when(used)
    def _():
        u = _dot(x_ref[...], wub_ref[...]) + bu_ref[0, 0]
        glu = jnp.minimum(u[:, :D_FF], SWIGLU_LIMIT)
        lin = jnp.clip(u[:, D_FF:], -SWIGLU_LIMIT, SWIGLU_LIMIT)
        act = glu * jax.nn.sigmoid(SWIGLU_ALPHA * glu) * (lin + 1.0)
        y_ref[...] = (_dot(act.astype(BF16), wdb_ref[...]) + bd_ref[0, 0]).astype(y_ref.dtype)


def _expert_ffn(y_prev, blk0, n_blk_all, x_rows, blk_exp, blk_first, blk_next, n_used, layer,
                w_up, b_up, w_down, b_down):
    rows, d = x_rows.shape
    n_blk = rows // MOE_BLOCK
    last = lambda i, nu: jnp.maximum(jnp.minimum(i, nu[0] - 1), 0)
    b_idx = lambda i, be, fe, ne, nu: (layer, be[last(i, nu)], 0, 0)
    hbm = pl.BlockSpec(memory_space=pl.ANY)
    in_specs = [pl.BlockSpec((MOE_BLOCK, d), lambda i, be, fe, ne, nu: (last(i, nu), 0)),
                hbm, pl.BlockSpec((1, 1, 1, 2 * D_FF), b_idx), hbm, pl.BlockSpec((1, 1, 1, d), b_idx)]
    depth = w_up.shape[0]
    operands = [blk_exp, blk_first, blk_next, n_used, x_rows, w_up, b_up.reshape(depth, N_EXP, 1, 2 * D_FF),
                w_down, b_down.reshape(depth, N_EXP, 1, d)]
    aliases = {}
    if y_prev is not None:
        in_specs.append(hbm)
        aliases = {len(operands): 0}
        operands.append(y_prev)
    grid_spec = pltpu.PrefetchScalarGridSpec(
        num_scalar_prefetch=4, grid=(n_blk,), in_specs=in_specs,
        out_specs=pl.BlockSpec((MOE_BLOCK, d), lambda i, be, fe, ne, nu: (blk0 + last(i, nu), 0)),
        scratch_shapes=[pltpu.VMEM((d, 2 * D_FF), F32), pltpu.VMEM((D_FF, d), F32),
                        pltpu.VMEM((d, 2 * D_FF), BF16), pltpu.VMEM((D_FF, d), BF16),
                        pltpu.SemaphoreType.DMA((2,))])
    return pl.pallas_call(
        functools.partial(_expert_kernel, layer=layer), grid_spec=grid_spec,
        out_shape=jax.ShapeDtypeStruct((n_blk_all * MOE_BLOCK, d), BF16), input_output_aliases=aliases,
        compiler_params=_cparams(1), name="expert_ffn",
    )(*operands)


SCATTER_UNROLL = 8


def _row_token_kernel(pad_ref, dest_ref, rt_ref, *, chunk):
    i = pl.program_id(0)

    @pl.when(i == 0)
    def _():
        for e in range(N_EXP + 1):
            def clear(r, carry):
                rt_ref[r] = 0
                return carry
            lax.fori_loop(pad_ref[2 * e], pad_ref[2 * e + 1], clear, 0)

    def place(g, tok):
        k = g * SCATTER_UNROLL
        for j in range(SCATTER_UNROLL):
            rt_ref[dest_ref[k + j]] = tok + j // TOP_K
        return tok + SCATTER_UNROLL // TOP_K
    lax.fori_loop(0, chunk // SCATTER_UNROLL, place, i * (chunk // TOP_K))


def _row_tokens(dest, pad_ranges, n_rows):
    n = dest.shape[0]
    chunk = int(np.gcd(n, SCATTER_CHUNK))
    assert chunk % SCATTER_UNROLL == 0 and SCATTER_UNROLL % TOP_K == 0
    smem = pl.BlockSpec(memory_space=pltpu.SMEM)
    return pl.pallas_call(
        functools.partial(_row_token_kernel, chunk=chunk), grid=(n // chunk,),
        in_specs=[smem, pl.BlockSpec((chunk,), lambda i: (i,), memory_space=pltpu.SMEM)],
        out_specs=smem, out_shape=jax.ShapeDtypeStruct((n_rows,), jnp.int32),
        compiler_params=_cparams(1), name="row_tokens",
    )(pad_ranges, dest)


def _moe(hn, route, counts, layer, w_up, b_up, w_down, b_down):
    n_tok, d = hn.shape
    top_i = route[:, 0:TOP_K].astype(jnp.int32)
    gates = route[:, TOP_K:2 * TOP_K]
    rank = route[:, 2 * TOP_K:3 * TOP_K].astype(jnp.int32)
    n = n_tok * TOP_K
    padded = (counts + MOE_BLOCK - 1) // MOE_BLOCK * MOE_BLOCK
    p_end = jnp.cumsum(padded)
    p_start = p_end - padded
    dest = (p_start[top_i] + rank).reshape(-1)
    n_blk = -(-n // MOE_BLOCK) + N_EXP
    rows = n_blk * MOE_BLOCK
    pad_lo = jnp.concatenate([p_start + counts, p_end[-1:]])
    pad_hi = jnp.concatenate([p_end, jnp.full((1,), rows, p_end.dtype)])
    pad_ranges = jnp.stack([pad_lo, pad_hi], axis=1).reshape(-1).astype(jnp.int32)
    row_tok = _row_tokens(dest.astype(jnp.int32), pad_ranges, rows)
    blk_start = jnp.arange(n_blk, dtype=jnp.int32) * MOE_BLOCK
    blk_exp = jnp.sum((p_end[None, :] <= blk_start[:, None]).astype(jnp.int32), axis=1)
    blk_exp = jnp.minimum(blk_exp, N_EXP - 1)
    n_used = (p_end[-1] // MOE_BLOCK).astype(jnp.int32)
    n_grp = MOE_GROUPS if n_blk % MOE_GROUPS == 0 else 1
    nbg = n_blk // n_grp
    y = None
    for g in range(n_grp):
        be = blk_exp[g * nbg:(g + 1) * nbg]
        fe = jnp.concatenate([jnp.ones((1,), jnp.int32), (be[1:] != be[:-1]).astype(jnp.int32)])
        nu = n_used - g * nbg
        nxt = jnp.sum((be[None, :] <= be[:, None]).astype(jnp.int32), axis=1)
        ne = jnp.where(nxt < jnp.minimum(nu, nbg), be[jnp.minimum(nxt, nbg - 1)], -1).astype(jnp.int32)
        x_rows = hn.at[row_tok[g * nbg * MOE_BLOCK:(g + 1) * nbg * MOE_BLOCK]].get(mode='promise_in_bounds')
        y = _expert_ffn(y, g * nbg, n_blk, x_rows, be, fe, ne, nu.reshape(1), layer, w_up, b_up, w_down, b_down)
    return y, dest.reshape(n_tok, TOP_K), gates


def _pick_rows(y, dest, row0, n_rows):
    return y.at[dest[row0:row0 + n_rows].T].get(mode='promise_in_bounds')


def _ple_kernel(h_ref, y_ref, g_ref, p_ref, np_ref, wg_ref, wp_ref, *rest):
    o_ref = rest[-1]
    h2 = h_ref[...]
    g = g_ref[...]
    for j in range(TOP_K):
        h2 = h2 + y_ref[j].astype(F32) * g[:, j:j + 1]
    ms = jnp.mean(h2 * h2, axis=-1, keepdims=True)
    hn = (h2 * lax.rsqrt(ms + NORM_EPS) * np_ref[...]).astype(BF16)
    gate = jax.nn.sigmoid(_dot(hn, wg_ref[...]))
    o_ref[...] = h2 + gate * _dot(p_ref[...].astype(BF16), wp_ref[...])


def _ple(h1, row0, picked, gates, p, layer, norm_ple, wg_b, wp_b, out_prev=None):
    t, d = h1.shape
    n = gates.shape[0]
    ts = PLE_ROWS
    assert n % ts == 0 and row0 % ts == 0
    off = row0 // ts
    shifted = lambda i: (i + off, 0)
    full = lambda i: (0, 0)
    in_specs = [pl.BlockSpec((ts, d), shifted), pl.BlockSpec((TOP_K, ts, d), lambda i: (0, i, 0)),
                pl.BlockSpec((ts, TOP_K), lambda i: (i, 0)),
                pl.BlockSpec((None, ts, p.shape[2]), lambda i: (layer, i + off, 0)),
                pl.BlockSpec((1, d), full), pl.BlockSpec(wg_b.shape, full), pl.BlockSpec(wp_b.shape, full)]
    operands = [h1, picked, gates, p, norm_ple.reshape(1, d), wg_b, wp_b]
    aliases = {}
    if out_prev is not None:
        aliases = {len(operands): 0}
        in_specs.append(pl.BlockSpec(memory_space=pl.ANY))
        operands.append(out_prev)
    return pl.pallas_call(
        _ple_kernel, grid=(n // ts,), in_specs=in_specs,
        out_specs=pl.BlockSpec((ts, d), shifted), out_shape=jax.ShapeDtypeStruct((t, d), F32),
        input_output_aliases=aliases, compiler_params=_cparams(1), name="ple",
    )(*operands)


def kernel(x_prompt, x_sample, p_prompt, p_sample, cache_a_k, cache_a_v, cache_b_k, cache_b_v, cache_c_k, cache_c_v, cache_c_logf, cache_d_ckv, cache_d_krope, norm_mix, w_in, b_forget, qk_gain, rope_gain, kv_gain, w_uk, w_uv, rel_bias, group_gain, w_out, norm_ffn, w_router, b_router, w_up, b_up, w_down, b_down, norm_ple, w_ple_gate, w_ple_proj):
    nb, s, d = x_prompt.shape
    nd, t, _ = x_sample.shape
    depth = w_in.shape[0]
    past = cache_b_k.shape[2]
    assert PROJ_ROWS % t == 0 and s % PROJ_ROWS == 0

    tab_p = _rope_tables(jnp.arange(s))
    tab_s = _rope_tables(past + jnp.arange(PROJ_ROWS) % t)
    hp = x_prompt.reshape(nb * s, d)
    hs = x_sample.reshape(nd * t, d)
    st_p, st_s = [], []
    keep = min(A_WINDOW, s)
    layered_state = None
    for i in range(depth):
        pw = _proj_weights(w_in[i], b_forget[i], qk_gain[i], rope_gain[i], kv_gain[i], w_uk[i], w_uv[i])
        pr_p = _project(hp, norm_mix[i], pw, tab_p, s // PROJ_ROWS, (depth, i, nb, layered_state))
        layered_state = [pr_p[slot] for slot in STATE_SLOTS]
        pr_s = _project(hs, norm_mix[i], pw, tab_s, 1)
        outs_p = _prompt_attention(pr_p, pr_p[7][i], nb, s, rel_bias[i], qk_gain[i, 3])
        caches = (cache_a_k, cache_a_v, cache_b_k, cache_b_v, cache_c_k, cache_c_v,
                  cache_c_logf, cache_d_ckv, cache_d_krope)
        outs_s = _sample_attention(pr_s, caches, i, pw, w_uk[i], qk_gain[i, 5], rel_bias[i], nd, t)

        def heads(a, n, rows):
            return a.reshape(n, rows, N_HEADS, HEAD_DIM)

        p3 = lambda a: a.reshape(nb, s, a.shape[-1])
        s3 = lambda a: a.reshape(nd, t, a.shape[-1])
        st_p.append((heads(p3(pr_p[0])[:, s - keep:], nb, keep), heads(p3(pr_p[1])[:, s - keep:], nb, keep)))
        ka_all = jnp.concatenate([cache_a_k[i], heads(pr_s[0], nd, t)], axis=1)[:, t:]
        va_all = jnp.concatenate([cache_a_v[i], heads(pr_s[1], nd, t)], axis=1)[:, t:]
        st_s.append((ka_all, va_all,
                     heads(pr_s[2], nd, t), heads(pr_s[3], nd, t), heads(pr_s[4], nd, t), heads(pr_s[5], nd, t),
                     s3(pr_s[7])[..., 0:N_HEADS], s3(pr_s[6]), s3(pr_s[7])[..., ROPE_LO:ROPE_LO + MLA_ROPE]))

        w_out_b = w_out[i].astype(BF16)
        wr_pad = jnp.zeros((d, LANE), F32).at[:, 0:N_EXP].set(w_router[i])
        wr_parts = _split2(wr_pad)
        br_pad = jnp.zeros((1, LANE), F32).at[0, 0:N_EXP].set(b_router[i])
        flat4 = lambda o: o.reshape(-1, N_HEADS * LANE)
        h1_p, hn_p, rt_p, cnt = _out_project([flat4(o) for o in outs_p], hp, group_gain[i], w_out_b, norm_ffn[i],
                                             wr_parts, br_pad, jnp.zeros((1, LANE), F32))
        h1_s, hn_s, rt_s, cnt = _out_project([flat4(o) for o in outs_s], hs, group_gain[i], w_out_b, norm_ffn[i],
                                             wr_parts, br_pad, cnt)

        hn_all = jnp.concatenate([hn_p, hn_s], axis=0)
        rt_all = jnp.concatenate([rt_p, rt_s], axis=0)
        y, dest, gates = _moe(hn_all, rt_all, cnt[0, 0:N_EXP].astype(jnp.int32), i, w_up, b_up, w_down, b_down)

        wg_b = w_ple_gate[i].astype(BF16)
        wp_b = w_ple_proj[i].astype(BF16)
        pp = p_prompt.reshape(depth, nb * s, -1)
        hp = None
        for b in range(nb):
            hp = _ple(h1_p, b * s, _pick_rows(y, dest, b * s, s), gates[b * s:(b + 1) * s], pp, i,
                      norm_ple[i], wg_b, wp_b, hp)
        hs = _ple(h1_s, 0, _pick_rows(y, dest, nb * s, nd * t), gates[nb * s:], p_sample.reshape(depth, nd * t, -1),
                  i, norm_ple[i], wg_b, wp_b)

    bk_t, bv_t, ck_t, cv_t, ckv_all, misc_t = layered_state

    def from_channel_major(a):
        return jnp.transpose(a.reshape(depth, nb, N_HEADS, HEAD_DIM, s), (0, 1, 4, 2, 3))

    state_p = [jnp.stack([st[j] for st in st_p]) for j in range(2)] + \
              [from_channel_major(a) for a in (bk_t, bv_t, ck_t, cv_t)] + \
              [jnp.swapaxes(misc_t[:, :, 0:N_HEADS, :], 2, 3), ckv_all.reshape(depth, nb, s, KV_RANK),
               jnp.swapaxes(misc_t[:, :, ROPE_LO:ROPE_LO + MLA_ROPE, :], 2, 3)]
    state_s = [jnp.stack([st[j] for st in st_s]) for j in range(9)]
    return (hp.reshape(nb, s, d), hs.reshape(nd, t, d), *state_p, *state_s)
```

```python
import functools

import numpy as np
import jax
import jax.numpy as jnp
from jax import lax
from jax.experimental import pallas as pl
from jax.experimental.pallas import tpu as pltpu

F32 = jnp.float32
BF16 = jnp.bfloat16

CHUNK = 64
HEAD_DIM = 64
N_HEADS = 4
GROUP_W = 256
A_LEFT_CHUNKS = 8
A_WINDOW = A_LEFT_CHUNKS * CHUNK
REL_CLIP = 128
MLA_NOPE = 64
MLA_ROPE = 32
KV_RANK = 128
ROPE_BASE = 10000.0
N_EXP = 32
TOP_K = 4
D_FF = 1024
SWIGLU_LIMIT = 7.0
SWIGLU_ALPHA = 1.702
MOE_BLOCK = 512
NORM_EPS = 1e-6
NEG_INF = -1e30
LOG2E = 1.4426950408889634

A_Q = 0
C_F = 2304
D_Q = 2308
D_CKV = D_Q + N_HEADS * (MLA_NOPE + MLA_ROPE)
D_KR = D_CKV + KV_RANK

LANE = 128
SEG_ABC = 0
SEG_DQ = 2304
SEG_CKV = 2816
SEG_MISC = 2944
N_COLS = 3072
ROPE_LO = 64
ROPE_HALF = MLA_ROPE // 2

VMEM_LIMIT = 56 * 1024 * 1024

PROJ_ROWS = 512
ATTN_TQ = 512
ATTN_TK = 1024
MLA_TQ = 2048
STICK_KS = 256
DENSE_ROWS = 512
PLE_ROWS = 512
MOE_GROUPS = 4
COMBINE_SPLIT = 2
SCATTER_CHUNK = 8192

DEAD_LOG2 = -160.0
BOUND_MARGIN = 1.01


def _cparams(n_axes):
    return pltpu.CompilerParams(dimension_semantics=("arbitrary",) * n_axes,
                                vmem_limit_bytes=VMEM_LIMIT)


def _nt_dot(a, b):
    return lax.dot_general(a, b, (((1,), (1,)), ((), ())), preferred_element_type=F32)


def _dot(a, b):
    return jnp.dot(a, b, preferred_element_type=F32)


def _split2(x):
    hi = x.astype(BF16)
    lo = (x - hi.astype(F32)).astype(BF16)
    return hi, lo


def _in_col_map():
    cols = np.full((N_COLS,), -1, np.int64)
    cols[0:2304] = np.arange(2304)
    for h in range(N_HEADS):
        base = D_Q + (MLA_NOPE + MLA_ROPE) * h
        cols[SEG_DQ + LANE * h: SEG_DQ + LANE * h + MLA_NOPE + MLA_ROPE] = base + np.arange(MLA_NOPE + MLA_ROPE)
    cols[SEG_CKV:SEG_CKV + KV_RANK] = D_CKV + np.arange(KV_RANK)
    cols[SEG_MISC:SEG_MISC + N_HEADS] = C_F + np.arange(N_HEADS)
    cols[SEG_MISC + ROPE_LO:SEG_MISC + ROPE_LO + MLA_ROPE] = D_KR + np.arange(MLA_ROPE)
    return cols


def _rope_tables(pos):
    inv = ROPE_BASE ** (-jnp.arange(ROPE_HALF, dtype=F32) / ROPE_HALF)
    ang = pos.astype(F32)[:, None] * inv
    cos, sin = jnp.cos(ang), jnp.sin(ang)
    n = pos.shape[0]
    one = jnp.ones((n, ROPE_LO), F32)
    z16 = jnp.zeros((n, ROPE_HALF), F32)
    z64 = jnp.zeros((n, ROPE_LO), F32)
    z32 = jnp.zeros((n, LANE - ROPE_LO - MLA_ROPE), F32)
    cos_t = jnp.concatenate([one, cos, cos, z32 + 1.0], axis=1)
    sin_a = jnp.concatenate([z64, -sin, z16, z32], axis=1)
    sin_b = jnp.concatenate([z64, z16, sin, z32], axis=1)
    return cos_t, sin_a, sin_b


def _rope(y, cos_t, sin_a, sin_b):
    left = pltpu.roll(y, LANE - ROPE_HALF, axis=1)
    right = pltpu.roll(y, ROPE_HALF, axis=1)
    return y * cos_t + left * sin_a + right * sin_b


def _head_norm(x, m_ref, gain):
    hi, lo = _split2(x * x)
    ssq = _dot(hi, m_ref[...]) + _dot(lo, m_ref[...])
    return x * lax.rsqrt(ssq * (1.0 / HEAD_DIM) + NORM_EPS) * gain


def _value_lane0(h):
    return HEAD_DIM * (h % 2)


def _ones_lane(h):
    return HEAD_DIM - _value_lane0(h)


def _store_padded_q(q_ref, q, scale):
    lane = lax.broadcasted_iota(jnp.int32, (q.shape[0], LANE), 1)
    low = lane < HEAD_DIM
    qs = q * scale
    for h in range(N_HEADS):
        pair = qs[:, LANE * (h // 2): LANE * (h // 2) + LANE]
        keep = low if h % 2 == 0 else jnp.logical_not(low)
        q_ref[:, LANE * h: LANE * h + LANE] = jnp.where(keep, pair, 0.0).astype(BF16)


def _proj_kernel(*refs, channel_major, n_alias):
    x_ref, gmix_ref, w_ref, cos_ref, sa_ref, sb_ref, gv_ref, m64_ref, wuk_ref, wuv_ref = refs[:10]
    (ak_ref, av_ref, bk_ref, bv_ref, ck_ref, cv_ref, ckv_ref, misc_ref,
     qa_ref, ka_ref, va_ref, qb_ref, kb_ref, vb_ref, qc_ref, kc_ref, vc_ref,
     qd_ref, kd_ref, vd_ref) = refs[10 + n_alias:]

    def put_state(ref, val):
        ref[...] = val.T if channel_major else val

    x = x_ref[...]
    ms = jnp.mean(x * x, axis=-1, keepdims=True)
    hn = (x * lax.rsqrt(ms + NORM_EPS) * gmix_ref[...]).astype(BF16)
    sm_scale = (HEAD_DIM ** -0.5) * LOG2E
    rows = x.shape[0]
    lane = lax.broadcasted_iota(jnp.int32, (rows, LANE), 1)
    is_nope = lane < MLA_NOPE
    is_rope = jnp.logical_and(lane >= ROPE_LO, lane < ROPE_LO + MLA_ROPE)
    cos_t, sin_a, sin_b = cos_ref[...], sa_ref[...], sb_ref[...]

    z_a = _dot(hn, w_ref[:, 0:768])
    z_b = _dot(hn, w_ref[:, 768:1536])
    z_c = _dot(hn, w_ref[:, 1536:2304])
    z_l = _dot(hn, w_ref[:, SEG_CKV:N_COLS])
    z_q = _dot(hn, w_ref[:, SEG_DQ:SEG_CKV])

    z = z_a
    aq = _head_norm(z[:, 0:256], m64_ref, gv_ref[0:1, :])
    ak = _head_norm(z[:, 256:512], m64_ref, gv_ref[1:2, :])
    av = z[:, 512:768]
    ak_ref[...] = ak
    av_ref[...] = av
    _store_padded_q(qa_ref, aq, sm_scale)
    ka_ref[...] = ak.astype(BF16)
    va_ref[...] = av.astype(BF16)

    z = z_b
    put_state(bk_ref, z[:, 256:512])
    put_state(bv_ref, z[:, 512:768])
    _store_padded_q(qb_ref, z[:, 0:256], sm_scale)
    kb_ref[...] = z[:, 256:512].astype(BF16)
    vb_ref[...] = z[:, 512:768].astype(BF16)

    z = z_c
    cq = _head_norm(z[:, 0:256], m64_ref, gv_ref[2:3, :])
    ck = _head_norm(z[:, 256:512], m64_ref, gv_ref[3:4, :])
    cv = z[:, 512:768]
    put_state(ck_ref, ck)
    put_state(cv_ref, cv)
    _store_padded_q(qc_ref, cq, sm_scale)
    kc_ref[...] = ck.astype(BF16)
    vc_ref[...] = cv.astype(BF16)

    z = z_l
    zc = z[:, 0:KV_RANK]
    ckv = zc * lax.rsqrt(jnp.mean(zc * zc, axis=-1, keepdims=True) + NORM_EPS) * gv_ref[7:8, 0:LANE]
    ckv_ref[...] = ckv
    zm = z[:, KV_RANK:2 * KV_RANK]
    ssr = jnp.sum(jnp.where(is_rope, zm * zm, 0.0), axis=-1, keepdims=True)
    kr = zm * lax.rsqrt(ssr * (1.0 / MLA_ROPE) + NORM_EPS) * gv_ref[6:7, 0:LANE]
    kr = _rope(kr, cos_t, sin_a, sin_b)
    zf = zm + gv_ref[8:9, 0:LANE]
    clf = jnp.minimum(zf, 0.0) - jnp.log1p(jnp.exp(-jnp.abs(zf)))
    put_state(misc_ref, jnp.where(lane < N_HEADS, clf, kr))

    ckv_b = ckv.astype(BF16)
    kn = _dot(ckv_b, wuk_ref[...])
    for h in range(N_HEADS):
        hs = slice(LANE * h, LANE * h + LANE)
        ones = (lane == _ones_lane(h)).astype(F32)
        vd_ref[:, hs] = (_dot(ckv_b, wuv_ref[:, hs]) + ones).astype(BF16)
    for h in range(N_HEADS):
        xh = kn[:, LANE * h: LANE * h + LANE]
        ss = jnp.sum(xh * xh, axis=-1, keepdims=True)
        yh = xh * lax.rsqrt(ss * (1.0 / MLA_NOPE) + NORM_EPS) * gv_ref[5:6, 0:LANE]
        kd_ref[:, LANE * h: LANE * h + LANE] = (yh + kr).astype(BF16)

    z = z_q
    d_scale = ((MLA_NOPE + MLA_ROPE) ** -0.5) * LOG2E
    for h in range(N_HEADS):
        xh = z[:, LANE * h: LANE * h + LANE]
        x2 = xh * xh
        ssn = jnp.sum(jnp.where(is_nope, x2, 0.0), axis=-1, keepdims=True)
        ssr = jnp.sum(jnp.where(is_rope, x2, 0.0), axis=-1, keepdims=True)
        rn = lax.rsqrt(ssn * (1.0 / MLA_NOPE) + NORM_EPS)
        rr = lax.rsqrt(ssr * (1.0 / MLA_ROPE) + NORM_EPS)
        yh = xh * jnp.where(is_nope, rn, rr) * gv_ref[4:5, 0:LANE]
        yh = _rope(yh, cos_t, sin_a, sin_b)
        qd_ref[:, LANE * h: LANE * h + LANE] = (yh * d_scale).astype(BF16)


def _proj_weights(w_in, b_forget, qk_gain, rope_gain, kv_gain, w_uk, w_uv):
    cols = _in_col_map()
    valid = jnp.asarray(cols >= 0)
    w = jnp.where(valid[None, :], w_in[:, np.maximum(cols, 0)], 0.0).astype(BF16)

    def tile4(g):
        return jnp.tile(g, N_HEADS)

    gv = jnp.zeros((16, GROUP_W), F32)
    gv = gv.at[0].set(tile4(qk_gain[0])).at[1].set(tile4(qk_gain[1]))
    gv = gv.at[2].set(tile4(qk_gain[2])).at[3].set(tile4(qk_gain[3]))
    gv = gv.at[4, 0:MLA_NOPE].set(qk_gain[4]).at[4, ROPE_LO:ROPE_LO + MLA_ROPE].set(rope_gain[0])
    gv = gv.at[5, 0:MLA_NOPE].set(qk_gain[5])
    gv = gv.at[6, ROPE_LO:ROPE_LO + MLA_ROPE].set(rope_gain[1])
    gv = gv.at[7, 0:KV_RANK].set(kv_gain)
    gv = gv.at[8, 0:N_HEADS].set(b_forget)
    head = np.arange(GROUP_W) // HEAD_DIM
    m64 = jnp.asarray((head[:, None] == head[None, :]).astype(np.float32), BF16)
    wuk = jnp.zeros((KV_RANK, N_HEADS * LANE), F32)
    for h in range(N_HEADS):
        wuk = wuk.at[:, LANE * h: LANE * h + MLA_NOPE].set(w_uk[:, MLA_NOPE * h: MLA_NOPE * (h + 1)])
    wuv = jnp.zeros((KV_RANK, N_HEADS * LANE), F32)
    for h in range(N_HEADS):
        lo = LANE * h + _value_lane0(h)
        wuv = wuv.at[:, lo: lo + HEAD_DIM].set(w_uv[:, HEAD_DIM * h: HEAD_DIM * (h + 1)])
    return w, gv, m64, wuk.astype(BF16), wuv.astype(BF16)


STATE_SLOTS = (2, 3, 4, 5, 6, 7)


def _project(x, gmix, pw, tables, n_tab_blocks, layered=None):
    w, gv, m64, wuk, wuv = pw
    t, d = x.shape
    ts = PROJ_ROWS
    assert t % ts == 0
    row = lambda i: (i, 0)
    full = lambda i: (0, 0)
    tab = lambda i: (i % n_tab_blocks, 0)
    f32_w = [GROUP_W] * 6 + [KV_RANK, LANE]
    bf_w = [512, 256, 256, 512, 256, 256, 512, 256, 256, 512, 512, 512]
    out_shape = [jax.ShapeDtypeStruct((t, c), F32) for c in f32_w] + \
                [jax.ShapeDtypeStruct((t, c), BF16) for c in bf_w]
    out_specs = [pl.BlockSpec((ts, c), row) for c in f32_w + bf_w]
    in_specs = [pl.BlockSpec((ts, d), row), pl.BlockSpec((1, d), full), pl.BlockSpec((d, N_COLS), full),
                pl.BlockSpec((ts, LANE), tab), pl.BlockSpec((ts, LANE), tab), pl.BlockSpec((ts, LANE), tab),
                pl.BlockSpec(gv.shape, full), pl.BlockSpec(m64.shape, full),
                pl.BlockSpec(wuk.shape, full), pl.BlockSpec(wuv.shape, full)]
    operands = [x, gmix.reshape(1, d), w, *tables, gv, m64, wuk, wuv]
    aliases = {}
    if layered is not None:
        depth, layer, nb, earlier = layered
        s = t // nb
        nt = s // ts
        for slot in STATE_SLOTS:
            c = f32_w[slot]
            if slot == 6:
                out_shape[slot] = jax.ShapeDtypeStruct((depth, t, c), F32)
                out_specs[slot] = pl.BlockSpec((None, ts, c), lambda i: (layer, i, 0))
            else:
                out_shape[slot] = jax.ShapeDtypeStruct((depth, nb, c, s), F32)
                out_specs[slot] = pl.BlockSpec((None, None, c, ts), lambda i: (layer, i // nt, 0, i % nt))
        if earlier is not None:
            for k, slot in enumerate(STATE_SLOTS):
                aliases[len(operands)] = slot
                in_specs.append(pl.BlockSpec(memory_space=pl.ANY))
                operands.append(earlier[k])
    kern = functools.partial(_proj_kernel, channel_major=layered is not None, n_alias=len(aliases))
    return pl.pallas_call(
        kern, grid=(t // ts,), in_specs=in_specs, out_specs=out_specs, out_shape=out_shape,
        input_output_aliases=aliases, compiler_params=_cparams(1), name="proj",
    )(*operands)


def _softmax_step(q, kb, vb, carry, bias=None, ok=None):
    m, l, acc = carry
    s = _nt_dot(q, kb)
    if bias is not None:
        s = s + bias
    if ok is not None:
        s = jnp.where(ok, s, NEG_INF)
    m_new = jnp.maximum(m, jnp.max(s, axis=1, keepdims=True))
    alpha = jnp.exp2(m - m_new)
    p = jnp.exp2(s - m_new)
    l = alpha * l + jnp.sum(p, axis=1, keepdims=True)
    acc = alpha * acc + _dot(p.astype(BF16), vb)
    return m_new, l, acc


def _softmax_init(tq):
    return (jnp.full((tq, 1), NEG_INF, F32), jnp.zeros((tq, 1), F32), jnp.zeros((tq, LANE), F32))


def _pick_lane(block, h):
    col = lax.broadcasted_iota(jnp.int32, block.shape, 1)
    return jnp.sum(jnp.where(col == h, block, 0.0), axis=1, keepdims=True)


def _stack_pair(q_ref):
    return jnp.concatenate([q_ref[0, :, 0:LANE], q_ref[0, :, LANE:2 * LANE]], axis=0)


def _store_pair(o_ref, o, tq):
    o_ref[0, :, 0:LANE] = o[:tq].astype(o_ref.dtype)
    o_ref[0, :, LANE:2 * LANE] = o[tq:].astype(o_ref.dtype)


def _flash_forget_kernel(kmax_ref, decay_ref, q_ref, k_ref, v_ref, fq_ref, fk_ref, o_ref, *, tq, n_blk):
    b = pl.program_id(0)
    pair = pl.program_id(1)
    qi = pl.program_id(2)
    q = _stack_pair(q_ref)
    fq = jnp.concatenate([_pick_lane(fq_ref[0], 2 * pair), _pick_lane(fq_ref[0], 2 * pair + 1)], axis=0) * LOG2E

    def step(j, carry, diag):
        off = pl.multiple_of(j * tq, tq)
        kb = k_ref[0, pl.ds(off, tq), :]
        vb = v_ref[0, pl.ds(off, tq), :]
        fk = jnp.concatenate([jnp.broadcast_to(fk_ref[0, 0, pl.ds(j, 1), :], (tq, tq)),
                              jnp.broadcast_to(fk_ref[0, 1, pl.ds(j, 1), :], (tq, tq))], axis=0)
        ok = None
        if diag:
            row = lax.broadcasted_iota(jnp.int32, (2 * tq, tq), 0) % tq
            col = lax.broadcasted_iota(jnp.int32, (2 * tq, tq), 1)
            ok = col <= row
        return _softmax_step(q, kb, vb, carry, fq - fk * LOG2E, ok)

    carry = step(qi, _softmax_init(2 * tq), True)
    qf = q.astype(F32)
    q_norm = jnp.sqrt(jnp.sum(qf * qf, axis=1, keepdims=True))
    head0 = b * N_HEADS + 2 * pair
    room = q_norm * jnp.where(lax.broadcasted_iota(jnp.int32, q_norm.shape, 0) < tq,
                              kmax_ref[head0], kmax_ref[head0 + 1]) + fq - carry[0]
    slack0 = jnp.max(room[:tq]) - DEAD_LOG2
    slack1 = jnp.max(room[tq:]) - DEAD_LOG2

    def live(state):
        j = jnp.maximum(state[0], 0)
        alive = jnp.logical_or(slack0 + decay_ref[head0 * n_blk + j] >= 0.0,
                               slack1 + decay_ref[(head0 + 1) * n_blk + j] >= 0.0)
        return jnp.logical_and(state[0] >= 0, alive)

    def older(state):
        j = state[0]
        return (j - 1,) + step(j, state[1:], False)

    _, m, l, acc = lax.while_loop(live, older, (qi - 1,) + carry)
    _store_pair(o_ref, acc / l, tq)


def _flash_mla_kernel(q_ref, k_ref, v_ref, o_ref, *, tq, tk):
    qi = pl.program_id(2)
    n_sub = tq // tk

    def step(hh, j, carry, row0=None):
        m, acc = carry
        hs = slice(LANE * hh, LANE * hh + LANE)
        off = pl.multiple_of(j * tk, tk)
        qs = q_ref[0, :, hs] if row0 is None else q_ref[0, row0:, hs]
        s = _nt_dot(qs, k_ref[0, pl.ds(off, tk), hs])
        if row0 is not None:
            row = lax.broadcasted_iota(jnp.int32, s.shape, 0)
            col = lax.broadcasted_iota(jnp.int32, s.shape, 1)
            s = jnp.where(col // CHUNK <= row // CHUNK, s, NEG_INF)
        m_new = jnp.maximum(m, jnp.max(s, axis=1, keepdims=True))
        p = jnp.exp2(s - m_new)
        acc = jnp.exp2(m - m_new) * acc + _dot(p.astype(BF16), v_ref[0, pl.ds(off, tk), hs])
        return m_new, acc

    def both(g, carry):
        c0, c1 = carry
        for r in range(n_sub):
            c0, c1 = step(0, g * n_sub + r, c0), step(1, g * n_sub + r, c1)
        return c0, c1

    init = (jnp.full((tq, 1), NEG_INF, F32), jnp.zeros((tq, LANE), F32))
    carry = lax.fori_loop(0, qi, both, (init, init))
    for hh in range(2):
        m, acc = carry[hh]
        for r in range(n_sub):
            row0 = r * tk
            m_r, acc_r = step(hh, qi * n_sub + r, (m[row0:], acc[row0:]), row0)
            m = m_r if r == 0 else jnp.concatenate([m[:row0], m_r], axis=0)
            acc = acc_r if r == 0 else jnp.concatenate([acc[:row0], acc_r], axis=0)
        ones_at = _ones_lane(hh)
        o_ref[0, :, LANE * hh: LANE * hh + LANE] = (acc / acc[:, ones_at:ones_at + 1]).astype(o_ref.dtype)


def _flash_stick_kernel(q_ref, k_ref, v_ref, u_ref, o_ref, *, tq, ks):
    qi = pl.program_id(2)
    q = _stack_pair(q_ref)
    n_sub = tq // ks

    def step(jb, carry, diag):
        c, acc = carry
        off = pl.multiple_of(jb * ks, ks)
        kb = k_ref[0, pl.ds(off, ks), :]
        vb = v_ref[0, pl.ds(off, ks), :]
        vis = None
        if diag:
            row = lax.broadcasted_iota(jnp.int32, (2 * tq, ks), 0) % tq + qi * tq
            col = lax.broadcasted_iota(jnp.int32, (2 * tq, ks), 1) + jb * ks
            vis = col < row
        c_new, w = _stick_weights(_nt_dot(q, kb), c, u_ref[...], vis)
        return c_new, acc + _dot(w.astype(BF16), vb)

    carry = (jnp.zeros((2 * tq, 1), F32), jnp.zeros((2 * tq, LANE), F32))
    for r in range(n_sub):
        carry = step(qi * n_sub + (n_sub - 1 - r), carry, True)

    def live(state):
        return jnp.logical_and(state[0] >= 0, state[1] > DEAD_LOG2)

    def older(state):
        c, acc = step(state[0], state[2:], False)
        return state[0] - 1, jnp.max(c), c, acc

    _, _, _, acc = lax.while_loop(live, older, (qi * n_sub - 1, jnp.max(carry[0])) + carry)
    _store_pair(o_ref, acc, tq)


def _stick_weights(z, c, u, vis):
    sp = jnp.log(1.0 + jnp.exp2(-jnp.abs(z))) * LOG2E
    log_rest = jnp.minimum(-z, 0.0) - sp
    log_beta = log_rest + z
    if vis is not None:
        log_rest = jnp.where(vis, log_rest, 0.0)
    hi, lo = _split2(log_rest)
    between = _dot(hi, u) + _dot(lo, u)
    w = jnp.exp2(log_beta + between + c)
    if vis is not None:
        w = jnp.where(vis, w, 0.0)
    return c + jnp.sum(log_rest, axis=1, keepdims=True), w


def _band_kernel(q_ref, k_ref, v_ref, bd_ref, bp_ref, o_ref, *, tq):
    qi = pl.program_id(2)
    q = _stack_pair(q_ref)
    off = pl.multiple_of(qi * tq, tq)
    carry = _softmax_step(q, k_ref[0, pl.ds(off, tq), :], v_ref[0, pl.ds(off, tq), :],
                          _softmax_init(2 * tq), bd_ref[...].reshape(2 * tq, tq))
    offp = pl.multiple_of(jnp.maximum(qi - 1, 0) * tq, tq)
    no_prev = jnp.where(qi == 0, NEG_INF, 0.0)
    m, l, acc = _softmax_step(q, k_ref[0, pl.ds(offp, tq), :], v_ref[0, pl.ds(offp, tq), :],
                              carry, bp_ref[...].reshape(2 * tq, tq) + no_prev)
    _store_pair(o_ref, acc / l, tq)


def _pair_specs(s, tq):
    q_spec = pl.BlockSpec((1, tq, 2 * LANE), lambda b, p, i: (b, i, p))
    kv_spec = pl.BlockSpec((1, s, LANE), lambda b, p, i: (b, 0, p))
    return q_spec, kv_spec


def _pair_call(kern, nb, s, tq, in_specs, operands, name):
    return pl.pallas_call(
        kern, grid=(nb, N_HEADS // 2, s // tq), in_specs=in_specs,
        out_specs=pl.BlockSpec((1, tq, 2 * LANE), lambda b, p, i: (b, i, p)),
        out_shape=jax.ShapeDtypeStruct((nb, s, N_HEADS * LANE), BF16),
        compiler_params=_cparams(3), name=name,
    )(*operands)


def _strict_upper(n):
    idx = np.arange(n)
    return jnp.asarray((idx[:, None] > idx[None, :]).astype(np.float32), BF16)


def _toeplitz(vec, n_rows, n_cols):
    length = n_rows + n_cols - 1
    assert vec.shape[-1] == length
    lead = vec.shape[:-1]
    rev = jnp.concatenate([vec[..., ::-1], jnp.zeros(lead + (1,), vec.dtype)], axis=-1)
    flat = jnp.tile(rev, (1,) * len(lead) + (n_rows,))[..., :n_rows * length]
    return flat.reshape(lead + (n_rows, length))[..., n_rows - 1: n_rows - 1 + n_cols]


def _rel_bias_tile(rel_bias, n_rows, n_cols, rel00, ok):
    d = np.arange(n_rows + n_cols - 1) - (n_cols - 1) + rel00
    vec = rel_bias.astype(F32)[:, np.clip(d, -REL_CLIP, REL_CLIP) + REL_CLIP] * LOG2E
    return jnp.where(jnp.asarray(ok)[None], _toeplitz(vec, n_rows, n_cols), NEG_INF)


def _band_bias_tiles(rel_bias, tq):
    i = np.arange(tq)[:, None]
    j = np.arange(tq)[None, :]
    own = _rel_bias_tile(rel_bias, tq, tq, 0, (j // CHUNK) <= (i // CHUNK))
    prev = _rel_bias_tile(rel_bias, tq, tq, tq, (j // CHUNK) >= (i // CHUNK) + tq // CHUNK - A_LEFT_CHUNKS)
    return own, prev


def _prompt_attention(pr, misc_t, nb, s, rel_bias, k_gain):
    tq = ATTN_TQ
    assert s % tq == 0 and tq == A_WINDOW
    r3 = lambda a: a.reshape(nb, s, a.shape[-1])
    qa, ka, va, qb, kb, vb, qc, kc, vc, qd, kd, vd = [r3(a) for a in pr[8:20]]
    q_spec, kv_spec = _pair_specs(s, tq)

    bd, bp = _band_bias_tiles(rel_bias, tq)
    b_spec = pl.BlockSpec((2, tq, tq), lambda b, p, i: (p, 0, 0))
    o_a = _pair_call(functools.partial(_band_kernel, tq=tq), nb, s, tq,
                     [q_spec, kv_spec, kv_spec, b_spec, b_spec], (qa, ka, va, bd, bp), "attn_band")

    u = _strict_upper(STICK_KS)
    o_b = _pair_call(functools.partial(_flash_stick_kernel, tq=tq, ks=STICK_KS), nb, s, tq,
                     [q_spec, kv_spec, kv_spec, pl.BlockSpec(u.shape, lambda b, p, i: (0, 0))],
                     (qb, kb, vb, u), "attn_stick")

    n_blk = s // tq
    fk_t = jnp.cumsum(misc_t[:, 0:N_HEADS, :], axis=2)
    c_cum = jnp.swapaxes(fk_t, 1, 2)
    fk = fk_t.reshape(nb, N_HEADS, n_blk, tq)
    kmax = jnp.full((nb * N_HEADS,), HEAD_DIM ** 0.5 * BOUND_MARGIN, F32) * jnp.max(jnp.abs(k_gain))
    decay = lax.cummax(jnp.max(-fk, axis=-1), axis=2) * LOG2E
    smem = pl.BlockSpec(memory_space=pltpu.SMEM)
    fq_spec = pl.BlockSpec((1, tq, N_HEADS), lambda b, p, i: (b, i, 0))
    fk_spec = pl.BlockSpec((1, 2, n_blk, tq), lambda b, p, i: (b, p, 0, 0))
    o_c = _pair_call(functools.partial(_flash_forget_kernel, tq=tq, n_blk=n_blk), nb, s, tq,
                     [smem, smem, q_spec, kv_spec, kv_spec, fq_spec, fk_spec],
                     (kmax, decay.reshape(-1), qc, kc, vc, c_cum, fk), "attn_forget")

    tqd = MLA_TQ if s % MLA_TQ == 0 else tq
    qd_spec = pl.BlockSpec((1, tqd, 2 * LANE), lambda b, p, i: (b, i, p))
    kvd_spec = pl.BlockSpec((1, s, 2 * LANE), lambda b, p, i: (b, 0, p), pipeline_mode=pl.Buffered(1))
    o_d = _pair_call(functools.partial(_flash_mla_kernel, tq=tqd, tk=min(ATTN_TK, tqd)), nb, s, tqd,
                     [qd_spec, kvd_spec, kvd_spec], (qd, kd, vd), "attn_mla")
    return o_a, o_b, o_c, o_d


def _two_block_softmax(s1, s2, v1_t, v2):
    m = jnp.maximum(jnp.max(s1, axis=1, keepdims=True), jnp.max(s2, axis=1, keepdims=True))
    p1 = jnp.exp2(s1 - m)
    p2 = jnp.exp2(s2 - m)
    l = jnp.sum(p1, axis=1, keepdims=True) + jnp.sum(p2, axis=1, keepdims=True)
    return (_nt_dot(p1.astype(BF16), v1_t) + _dot(p2.astype(BF16), v2)) / l


def _sample_kernel(qa_ref, ka_ref, va_ref, qb_ref, kb_ref, vb_ref, qc_ref, kc_ref, vc_ref, qd_ref, kd_ref, vd_ref,
                   cak_ref, cav_ref, cbk_ref, cbv_ref, cck_ref, ccv_ref, cckv_ref, ckr_ref,
                   ba_c_ref, ba_n_ref, fq_ref, fkc_ref, fkn_ref,
                   u_ref, un_ref, wukt_ref, wuv_ref, gk_ref,
                   oa_ref, ob_ref, oc_ref, od_ref, *, t, past, ks):
    row = lax.broadcasted_iota(jnp.int32, (t, t), 0)
    col = lax.broadcasted_iota(jnp.int32, (t, t), 1)
    causal_bias = jnp.where(col <= row, 0.0, NEG_INF)
    chunk_bias = jnp.where((past + col) // CHUNK <= (past + row) // CHUNK, 0.0, NEG_INF)
    strict = col < row

    ckv_c = cckv_ref[0].astype(BF16)
    kr_t = ckr_ref[0].astype(BF16)
    pad_t = jnp.zeros((LANE - MLA_NOPE - MLA_ROPE, past), BF16)

    for h in range(N_HEADS):
        hs = slice(LANE * h, LANE * h + LANE)
        ps = slice(LANE * (h // 2), LANE * (h // 2) + LANE)

        q = qa_ref[0][:, hs]
        s1 = _dot(q, cak_ref[0, ps, :].astype(BF16)) + ba_c_ref[h]
        s2 = _nt_dot(q, ka_ref[0][:, ps]) + ba_n_ref[h]
        oa_ref[0, :, hs] = _two_block_softmax(s1, s2, cav_ref[0, ps, :].astype(BF16), va_ref[0][:, ps])

        q = qb_ref[0][:, hs]
        c, w = _stick_weights(_nt_dot(q, kb_ref[0][:, ps]), jnp.zeros((t, 1), F32), un_ref[...], strict)
        acc = _dot(w.astype(BF16), vb_ref[0][:, ps])

        def b_live(state):
            return jnp.logical_and(state[0] >= 0, state[1] > DEAD_LOG2)

        def b_older(state, q=q, ps=ps):
            jb, _, c, acc = state
            off = pl.multiple_of(jb * ks, ks)
            kb_t = cbk_ref[0, ps, pl.ds(off, ks)].astype(BF16)
            vb_t = cbv_ref[0, ps, pl.ds(off, ks)].astype(BF16)
            c, w = _stick_weights(_dot(q, kb_t), c, u_ref[...], None)
            return jb - 1, jnp.max(c), c, acc + _nt_dot(w.astype(BF16), vb_t)

        _, _, _, acc = lax.while_loop(b_live, b_older, (past // ks - 1, jnp.max(c), c, acc))
        ob_ref[0, :, hs] = acc

        q = qc_ref[0][:, hs]
        fq = fq_ref[0][:, h:h + 1] * LOG2E
        s1 = _dot(q, cck_ref[0, ps, :].astype(BF16)) + (fq - fkc_ref[0, h:h + 1, :] * LOG2E)
        s2 = _nt_dot(q, kc_ref[0][:, ps]) + (fq - fkn_ref[0, h:h + 1, :] * LOG2E) + causal_bias
        oc_ref[0, :, hs] = _two_block_softmax(s1, s2, ccv_ref[0, ps, :].astype(BF16), vc_ref[0][:, ps])

        q = qd_ref[0][:, hs]
        kn_t = _nt_dot(wukt_ref[MLA_NOPE * h: MLA_NOPE * (h + 1), :], ckv_c)
        ss = jnp.sum(kn_t * kn_t, axis=0, keepdims=True)
        kn_t = kn_t * lax.rsqrt(ss * (1.0 / MLA_NOPE) + NORM_EPS) * gk_ref[...]
        k_t = jnp.concatenate([kn_t.astype(BF16), kr_t, pad_t], axis=0)
        s1 = _dot(q, k_t)
        s2 = _nt_dot(q, kd_ref[0][:, hs]) + chunk_bias
        v_c = _dot(ckv_c, wuv_ref[:, hs]).astype(BF16)
        m = jnp.maximum(jnp.max(s1, axis=1, keepdims=True), jnp.max(s2, axis=1, keepdims=True))
        p1 = jnp.exp2(s1 - m)
        p2 = jnp.exp2(s2 - m)
        l = jnp.sum(p1, axis=1, keepdims=True) + jnp.sum(p2, axis=1, keepdims=True)
        od_ref[0, :, hs] = (_dot(p1.astype(BF16), v_c) + _dot(p2.astype(BF16), vd_ref[0][:, hs])) / l


def _channel_major(cache):
    depth, nb, rows = cache.shape[:3]
    return jnp.transpose(cache, (0, 1, 3, 4, 2)).reshape(depth, nb, GROUP_W, rows)


def _sample_attention(pr, caches, layer, pw, w_uk, k_gain, rel_bias, nb, t):
    a_k, a_v, b_k, b_v, c_k, c_v, c_lf, d_ckv, d_kr = caches
    wuv = pw[4]
    past = b_k.shape[2]
    win = a_k.shape[2]
    ks = STICK_KS
    assert past % ks == 0
    r3 = lambda a: a.reshape(nb, t, a.shape[-1])
    news = [r3(a) for a in pr[8:20]]

    qpos = past + np.arange(t)
    kpos = past - win + np.arange(win + t)
    qc, kc = qpos // CHUNK, kpos // CHUNK
    ok = (kc[None, :] <= qc[:, None]) & (kc[None, :] >= qc[:, None] - A_LEFT_CHUNKS)
    ba = _rel_bias_tile(rel_bias, t, win + t, win, ok)
    ba_c, ba_n = ba[:, :, :win], ba[:, :, win:]

    clf_new = jnp.swapaxes(r3(pr[7])[..., 0:N_HEADS], 1, 2)
    lf_c = jnp.swapaxes(c_lf[layer], 1, 2).astype(F32)
    fk = jnp.cumsum(jnp.concatenate([lf_c, clf_new], axis=2), axis=2)
    fk_c, fk_n = fk[:, :, :past], fk[:, :, past:]
    fq = jnp.swapaxes(fk_n, 1, 2)

    wukt = w_uk.T.astype(BF16)
    gk = k_gain.reshape(MLA_NOPE, 1)
    u, un = _strict_upper(ks), _strict_upper(t)

    per_b = lambda shape: pl.BlockSpec((1,) + shape, lambda b: (b,) + (0,) * len(shape))
    per_lb = lambda shape: pl.BlockSpec((None, 1) + shape, lambda b: (layer, b) + (0,) * len(shape))
    const = lambda a: pl.BlockSpec(a.shape, lambda b: (0,) * a.ndim)
    cache_ops = [_channel_major(c) for c in (a_k, a_v, b_k, b_v, c_k, c_v)] + [d_ckv, jnp.swapaxes(d_kr, 2, 3)]
    operands = news + cache_ops + [ba_c, ba_n, fq, fk_c, fk_n, u, un, wukt, wuv, gk]
    in_specs = [per_b(a.shape[1:]) for a in news] + [per_lb(a.shape[2:]) for a in cache_ops] + \
               [const(ba_c), const(ba_n)] + [per_b(a.shape[1:]) for a in (fq, fk_c, fk_n)] + \
               [const(a) for a in (u, un, wukt, wuv, gk)]
    out_shape = [jax.ShapeDtypeStruct((nb, t, N_HEADS * LANE), F32)] * 4
    out_specs = [per_b((t, N_HEADS * LANE))] * 4
    return pl.pallas_call(
        functools.partial(_sample_kernel, t=t, past=past, ks=ks), grid=(nb,),
        in_specs=in_specs, out_specs=out_specs, out_shape=out_shape,
        compiler_params=_cparams(1), name="attn_sample",
    )(*operands)


def _out_kernel(oa_ref, ob_ref, oc_ref, od_ref, h_ref, gg_ref, wo_ref, nf_ref,
                wr1_ref, wr2_ref, br_ref, tri_ref, cnt0_ref, h1_ref, hn_ref, route_ref, cnt_ref):
    rows = h_ref.shape[0]
    low = lax.broadcasted_iota(jnp.int32, (rows, LANE), 1) < HEAD_DIM
    h1 = h_ref[...]
    for g, o_ref in enumerate((oa_ref, ob_ref, oc_ref, od_ref)):
        p0 = jnp.where(low, o_ref[:, 0:LANE], o_ref[:, LANE:2 * LANE])
        p1 = jnp.where(low, o_ref[:, 2 * LANE:3 * LANE], o_ref[:, 3 * LANE:4 * LANE])
        og = jnp.concatenate([p0, p1], axis=1).astype(F32)
        ms = jnp.mean(og * og, axis=-1, keepdims=True)
        y = (og * lax.rsqrt(ms + NORM_EPS) * gg_ref[g:g + 1, :]).astype(BF16)
        h1 = h1 + _dot(y, wo_ref[GROUP_W * g: GROUP_W * (g + 1), :])
    h1_ref[...] = h1
    ms = jnp.mean(h1 * h1, axis=-1, keepdims=True)
    hn = h1 * lax.rsqrt(ms + NORM_EPS) * nf_ref[...]
    hn_ref[...] = hn.astype(BF16)
    a1, a2 = _split2(hn)
    w1, w2 = wr1_ref[...], wr2_ref[...]
    logits = _dot(a1, w1) + _dot(a1, w2) + _dot(a2, w1) + br_ref[...]

    @pl.when(pl.program_id(0) == 0)
    def _():
        cnt_ref[...] = cnt0_ref[...]

    lane = lax.broadcasted_iota(jnp.int32, (rows, LANE), 1)
    x = jnp.where(lane < N_EXP, logits, NEG_INF)
    picks, tops, ids = [], [], []
    for j in range(TOP_K):
        top = jnp.max(x, axis=1, keepdims=True)
        idx = jnp.min(jnp.where(x == top, lane, LANE), axis=1, keepdims=True)
        hit = lane == idx
        x = jnp.where(hit, NEG_INF, x)
        picks.append(hit)
        tops.append(top)
        ids.append(idx)
    chosen = functools.reduce(jnp.logical_or, picks)
    before = _dot(tri_ref[...], jnp.where(chosen, 1.0, 0.0).astype(BF16)) + cnt_ref[...]
    exps = [jnp.exp(top - tops[0]) for top in tops]
    total = functools.reduce(jnp.add, exps)
    route = jnp.zeros((rows, LANE), F32)
    for j in range(TOP_K):
        rank = jnp.sum(jnp.where(picks[j], before, 0.0), axis=1, keepdims=True)
        route = jnp.where(lane == j, ids[j].astype(F32), route)
        route = jnp.where(lane == TOP_K + j, exps[j] / total, route)
        route = jnp.where(lane == 2 * TOP_K + j, rank, route)
    route_ref[...] = route
    cnt_ref[...] = cnt_ref[...] + jnp.sum(jnp.where(chosen, 1.0, 0.0), axis=0, keepdims=True)


def _out_project(outs, h, group_gain, w_out_b, norm_ffn, wr_parts, br_pad, counts0):
    t, d = h.shape
    ts = DENSE_ROWS
    assert t % ts == 0
    row = lambda i: (i, 0)
    full = lambda i: (0, 0)
    idx = np.arange(ts)
    tri = jnp.asarray((idx[:, None] > idx[None, :]).astype(np.float32), BF16)
    o_spec = pl.BlockSpec((ts, N_HEADS * LANE), row)
    in_specs = [o_spec] * 4 + [pl.BlockSpec((ts, d), row), pl.BlockSpec(group_gain.shape, full),
                               pl.BlockSpec(w_out_b.shape, full), pl.BlockSpec((1, d), full)] + \
               [pl.BlockSpec((d, LANE), full)] * len(wr_parts) + [pl.BlockSpec((1, LANE), full)] + \
               [pl.BlockSpec((ts, ts), full), pl.BlockSpec((1, LANE), full)]
    out_shape = [jax.ShapeDtypeStruct((t, d), F32), jax.ShapeDtypeStruct((t, d), BF16),
                 jax.ShapeDtypeStruct((t, LANE), F32), jax.ShapeDtypeStruct((1, LANE), F32)]
    out_specs = [pl.BlockSpec((ts, d), row), pl.BlockSpec((ts, d), row), pl.BlockSpec((ts, LANE), row),
                 pl.BlockSpec((1, LANE), full)]
    return pl.pallas_call(
        _out_kernel, grid=(t // ts,), in_specs=in_specs, out_specs=out_specs, out_shape=out_shape,
        compiler_params=_cparams(1), name="out_proj",
    )(*outs, h, group_gain, w_out_b, norm_ffn.reshape(1, d), *wr_parts, br_pad, tri, counts0)


def _expert_kernel(be_ref, fe_ref, ne_ref, nu_ref, x_ref, wu_hbm, bu_ref, wd_hbm, bd_ref, *rest, layer):
    y_ref, wu_stage, wd_stage, wub_ref, wdb_ref, sems = rest[-6:]
    i = pl.program_id(0)
    used = i < nu_ref[0]

    def weight_copies(expert):
        return (pltpu.make_async_copy(wu_hbm.at[layer, expert], wu_stage, sems.at[0]),
                pltpu.make_async_copy(wd_hbm.at[layer, expert], wd_stage, sems.at[1]))

    @pl.when(jnp.logical_and(used, fe_ref[i] == 1))
    def _():
        @pl.when(i == 0)
        def _():
            for cp in weight_copies(be_ref[0]):
                cp.start()
        for cp in weight_copies(be_ref[i]):
            cp.wait()
        wub_ref[...] = wu_stage[...].astype(BF16)
        wdb_ref[...] = wd_stage[...].astype(BF16)

        @pl.when(ne_ref[i] >= 0)
        def _():
            for cp in weight_copies(ne_ref[i]):
                cp.start()

    @pl.when(used)
    def _():
        u = _dot(x_ref[...], wub_ref[...]) + bu_ref[0, 0]
        glu = jnp.minimum(u[:, :D_FF], SWIGLU_LIMIT)
        lin = jnp.clip(u[:, D_FF:], -SWIGLU_LIMIT, SWIGLU_LIMIT)
        act = glu * jax.nn.sigmoid(SWIGLU_ALPHA * glu) * (lin + 1.0)
        y_ref[...] = (_dot(act.astype(BF16), wdb_ref[...]) + bd_ref[0, 0]).astype(y_ref.dtype)


def _expert_ffn(y_prev, blk0, n_blk_all, x_rows, blk_exp, blk_first, blk_next, n_used, layer,
                w_up, b_up, w_down, b_down):
    rows, d = x_rows.shape
    n_blk = rows // MOE_BLOCK
    last = lambda i, nu: jnp.maximum(jnp.minimum(i, nu[0] - 1), 0)
    b_idx = lambda i, be, fe, ne, nu: (layer, be[last(i, nu)], 0, 0)
    hbm = pl.BlockSpec(memory_space=pl.ANY)
    in_specs = [pl.BlockSpec((MOE_BLOCK, d), lambda i, be, fe, ne, nu: (last(i, nu), 0)),
                hbm, pl.BlockSpec((1, 1, 1, 2 * D_FF), b_idx), hbm, pl.BlockSpec((1, 1, 1, d), b_idx)]
    depth = w_up.shape[0]
    operands = [blk_exp, blk_first, blk_next, n_used, x_rows, w_up, b_up.reshape(depth, N_EXP, 1, 2 * D_FF),
                w_down, b_down.reshape(depth, N_EXP, 1, d)]
    aliases = {}
    if y_prev is not None:
        in_specs.append(hbm)
        aliases = {len(operands): 0}
        operands.append(y_prev)
    grid_spec = pltpu.PrefetchScalarGridSpec(
        num_scalar_prefetch=4, grid=(n_blk,), in_specs=in_specs,
        out_specs=pl.BlockSpec((MOE_BLOCK, d), lambda i, be, fe, ne, nu: (blk0 + last(i, nu), 0)),
        scratch_shapes=[pltpu.VMEM((d, 2 * D_FF), F32), pltpu.VMEM((D_FF, d), F32),
                        pltpu.VMEM((d, 2 * D_FF), BF16), pltpu.VMEM((D_FF, d), BF16),
                        pltpu.SemaphoreType.DMA((2,))])
    return pl.pallas_call(
        functools.partial(_expert_kernel, layer=layer), grid_spec=grid_spec,
        out_shape=jax.ShapeDtypeStruct((n_blk_all * MOE_BLOCK, d), BF16), input_output_aliases=aliases,
        compiler_params=_cparams(1), name="expert_ffn",
    )(*operands)


SCATTER_UNROLL = 8


def _row_token_kernel(pad_ref, dest_ref, rt_ref, *, chunk):
    i = pl.program_id(0)

    @pl.when(i == 0)
    def _():
        for e in range(N_EXP + 1):
            def clear(r, carry):
                rt_ref[r] = 0
                return carry
            lax.fori_loop(pad_ref[2 * e], pad_ref[2 * e + 1], clear, 0)

    def place(g, tok):
        k = g * SCATTER_UNROLL
        for j in range(SCATTER_UNROLL):
            rt_ref[dest_ref[k + j]] = tok + j // TOP_K
        return tok + SCATTER_UNROLL // TOP_K
    lax.fori_loop(0, chunk // SCATTER_UNROLL, place, i * (chunk // TOP_K))


def _row_tokens(dest, pad_ranges, n_rows):
    n = dest.shape[0]
    chunk = int(np.gcd(n, SCATTER_CHUNK))
    assert chunk % SCATTER_UNROLL == 0 and SCATTER_UNROLL % TOP_K == 0
    smem = pl.BlockSpec(memory_space=pltpu.SMEM)
    return pl.pallas_call(
        functools.partial(_row_token_kernel, chunk=chunk), grid=(n // chunk,),
        in_specs=[smem, pl.BlockSpec((chunk,), lambda i: (i,), memory_space=pltpu.SMEM)],
        out_specs=smem, out_shape=jax.ShapeDtypeStruct((n_rows,), jnp.int32),
        compiler_params=_cparams(1), name="row_tokens",
    )(pad_ranges, dest)


def _moe(hn, route, counts, layer, w_up, b_up, w_down, b_down):
    n_tok, d = hn.shape
    top_i = route[:, 0:TOP_K].astype(jnp.int32)
    gates = route[:, TOP_K:2 * TOP_K]
    rank = route[:, 2 * TOP_K:3 * TOP_K].astype(jnp.int32)
    n = n_tok * TOP_K
    padded = (counts + MOE_BLOCK - 1) // MOE_BLOCK * MOE_BLOCK
    p_end = jnp.cumsum(padded)
    p_start = p_end - padded
    dest = (p_start[top_i] + rank).reshape(-1)
    n_blk = -(-n // MOE_BLOCK) + N_EXP
    rows = n_blk * MOE_BLOCK
    pad_lo = jnp.concatenate([p_start + counts, p_end[-1:]])
    pad_hi = jnp.concatenate([p_end, jnp.full((1,), rows, p_end.dtype)])
    pad_ranges = jnp.stack([pad_lo, pad_hi], axis=1).reshape(-1).astype(jnp.int32)
    row_tok = _row_tokens(dest.astype(jnp.int32), pad_ranges, rows)
    blk_start = jnp.arange(n_blk, dtype=jnp.int32) * MOE_BLOCK
    blk_exp = jnp.sum((p_end[None, :] <= blk_start[:, None]).astype(jnp.int32), axis=1)
    blk_exp = jnp.minimum(blk_exp, N_EXP - 1)
    n_used = (p_end[-1] // MOE_BLOCK).astype(jnp.int32)
    n_grp = MOE_GROUPS if n_blk % MOE_GROUPS == 0 else 1
    nbg = n_blk // n_grp
    y = None
    for g in range(n_grp):
        be = blk_exp[g * nbg:(g + 1) * nbg]
        fe = jnp.concatenate([jnp.ones((1,), jnp.int32), (be[1:] != be[:-1]).astype(jnp.int32)])
        nu = n_used - g * nbg
        nxt = jnp.sum((be[None, :] <= be[:, None]).astype(jnp.int32), axis=1)
        ne = jnp.where(nxt < jnp.minimum(nu, nbg), be[jnp.minimum(nxt, nbg - 1)], -1).astype(jnp.int32)
        x_rows = hn.at[row_tok[g * nbg * MOE_BLOCK:(g + 1) * nbg * MOE_BLOCK]].get(mode='promise_in_bounds')
        y = _expert_ffn(y, g * nbg, n_blk, x_rows, be, fe, ne, nu.reshape(1), layer, w_up, b_up, w_down, b_down)
    return y, dest.reshape(n_tok, TOP_K), gates


def _pick_rows(y, dest, row0, n_rows):
    return y.at[dest[row0:row0 + n_rows].T].get(mode='promise_in_bounds')


def _ple_kernel(h_ref, y_ref, g_ref, p_ref, np_ref, wg_ref, wp_ref, *rest):
    o_ref = rest[-1]
    h2 = h_ref[...]
    g = g_ref[...]
    for j in range(TOP_K):
        h2 = h2 + y_ref[j].astype(F32) * g[:, j:j + 1]
    ms = jnp.mean(h2 * h2, axis=-1, keepdims=True)
    hn = (h2 * lax.rsqrt(ms + NORM_EPS) * np_ref[...]).astype(BF16)
    gate = jax.nn.sigmoid(_dot(hn, wg_ref[...]))
    o_ref[...] = h2 + gate * _dot(p_ref[...].astype(BF16), wp_ref[...])


def _ple(h1, row0, picked, gates, p, layer, norm_ple, wg_b, wp_b, out_prev=None):
    t, d = h1.shape
    n = gates.shape[0]
    ts = PLE_ROWS
    assert n % ts == 0 and row0 % ts == 0
    off = row0 // ts
    shifted = lambda i: (i + off, 0)
    full = lambda i: (0, 0)
    in_specs = [pl.BlockSpec((ts, d), shifted), pl.BlockSpec((TOP_K, ts, d), lambda i: (0, i, 0)),
                pl.BlockSpec((ts, TOP_K), lambda i: (i, 0)),
                pl.BlockSpec((None, ts, p.shape[2]), lambda i: (layer, i + off, 0)),
                pl.BlockSpec((1, d), full), pl.BlockSpec(wg_b.shape, full), pl.BlockSpec(wp_b.shape, full)]
    operands = [h1, picked, gates, p, norm_ple.reshape(1, d), wg_b, wp_b]
    aliases = {}
    if out_prev is not None:
        aliases = {len(operands): 0}
        in_specs.append(pl.BlockSpec(memory_space=pl.ANY))
        operands.append(out_prev)
    return pl.pallas_call(
        _ple_kernel, grid=(n // ts,), in_specs=in_specs,
        out_specs=pl.BlockSpec((ts, d), shifted), out_shape=jax.ShapeDtypeStruct((t, d), F32),
        input_output_aliases=aliases, compiler_params=_cparams(1), name="ple",
    )(*operands)


def kernel(x_prompt, x_sample, p_prompt, p_sample, cache_a_k, cache_a_v, cache_b_k, cache_b_v, cache_c_k, cache_c_v, cache_c_logf, cache_d_ckv, cache_d_krope, norm_mix, w_in, b_forget, qk_gain, rope_gain, kv_gain, w_uk, w_uv, rel_bias, group_gain, w_out, norm_ffn, w_router, b_router, w_up, b_up, w_down, b_down, norm_ple, w_ple_gate, w_ple_proj):
    nb, s, d = x_prompt.shape
    nd, t, _ = x_sample.shape
    depth = w_in.shape[0]
    past = cache_b_k.shape[2]
    assert PROJ_ROWS % t == 0 and s % PROJ_ROWS == 0

    tab_p = _rope_tables(jnp.arange(s))
    tab_s = _rope_tables(past + jnp.arange(PROJ_ROWS) % t)
    hp = x_prompt.reshape(nb * s, d)
    hs = x_sample.reshape(nd * t, d)
    st_p, st_s = [], []
    keep = min(A_WINDOW, s)
    layered_state = None
    for i in range(depth):
        pw = _proj_weights(w_in[i], b_forget[i], qk_gain[i], rope_gain[i], kv_gain[i], w_uk[i], w_uv[i])
        pr_p = _project(hp, norm_mix[i], pw, tab_p, s // PROJ_ROWS, (depth, i, nb, layered_state))
        layered_state = [pr_p[slot] for slot in STATE_SLOTS]
        pr_s = _project(hs, norm_mix[i], pw, tab_s, 1)
        outs_p = _prompt_attention(pr_p, pr_p[7][i], nb, s, rel_bias[i], qk_gain[i, 3])
        caches = (cache_a_k, cache_a_v, cache_b_k, cache_b_v, cache_c_k, cache_c_v,
                  cache_c_logf, cache_d_ckv, cache_d_krope)
        outs_s = _sample_attention(pr_s, caches, i, pw, w_uk[i], qk_gain[i, 5], rel_bias[i], nd, t)

        def heads(a, n, rows):
            return a.reshape(n, rows, N_HEADS, HEAD_DIM)

        p3 = lambda a: a.reshape(nb, s, a.shape[-1])
        s3 = lambda a: a.reshape(nd, t, a.shape[-1])
        st_p.append((heads(p3(pr_p[0])[:, s - keep:], nb, keep), heads(p3(pr_p[1])[:, s - keep:], nb, keep)))
        ka_all = jnp.concatenate([cache_a_k[i], heads(pr_s[0], nd, t)], axis=1)[:, t:]
        va_all = jnp.concatenate([cache_a_v[i], heads(pr_s[1], nd, t)], axis=1)[:, t:]
        st_s.append((ka_all, va_all,
                     heads(pr_s[2], nd, t), heads(pr_s[3], nd, t), heads(pr_s[4], nd, t), heads(pr_s[5], nd, t),
                     s3(pr_s[7])[..., 0:N_HEADS], s3(pr_s[6]), s3(pr_s[7])[..., ROPE_LO:ROPE_LO + MLA_ROPE]))

        w_out_b = w_out[i].astype(BF16)
        wr_pad = jnp.zeros((d, LANE), F32).at[:, 0:N_EXP].set(w_router[i])
        wr_parts = _split2(wr_pad)
        br_pad = jnp.zeros((1, LANE), F32).at[0, 0:N_EXP].set(b_router[i])
        flat4 = lambda o: o.reshape(-1, N_HEADS * LANE)
        h1_p, hn_p, rt_p, cnt = _out_project([flat4(o) for o in outs_p], hp, group_gain[i], w_out_b, norm_ffn[i],
                                             wr_parts, br_pad, jnp.zeros((1, LANE), F32))
        h1_s, hn_s, rt_s, cnt = _out_project([flat4(o) for o in outs_s], hs, group_gain[i], w_out_b, norm_ffn[i],
                                             wr_parts, br_pad, cnt)

        hn_all = jnp.concatenate([hn_p, hn_s], axis=0)
        rt_all = jnp.concatenate([rt_p, rt_s], axis=0)
        y, dest, gates = _moe(hn_all, rt_all, cnt[0, 0:N_EXP].astype(jnp.int32), i, w_up, b_up, w_down, b_down)

        wg_b = w_ple_gate[i].astype(BF16)
        wp_b = w_ple_proj[i].astype(BF16)
        pp = p_prompt.reshape(depth, nb * s, -1)
        hp = None
        n_rng = nb * COMBINE_SPLIT if s % (COMBINE_SPLIT * PLE_ROWS) == 0 else nb
        rng = nb * s // n_rng
        for r in range(n_rng):
            hp = _ple(h1_p, r * rng, _pick_rows(y, dest, r * rng, rng), gates[r * rng:(r + 1) * rng], pp, i,
                      norm_ple[i], wg_b, wp_b, hp)
        hs = _ple(h1_s, 0, _pick_rows(y, dest, nb * s, nd * t), gates[nb * s:], p_sample.reshape(depth, nd * t, -1),
                  i, norm_ple[i], wg_b, wp_b)

    bk_t, bv_t, ck_t, cv_t, ckv_all, misc_t = layered_state

    def from_channel_major(a):
        return jnp.transpose(a.reshape(depth, nb, N_HEADS, HEAD_DIM, s), (0, 1, 4, 2, 3))

    state_p = [jnp.stack([st[j] for st in st_p]) for j in range(2)] + \
              [from_channel_major(a) for a in (bk_t, bv_t, ck_t, cv_t)] + \
              [jnp.swapaxes(misc_t[:, :, 0:N_HEADS, :], 2, 3), ckv_all.reshape(depth, nb, s, KV_RANK),
               jnp.swapaxes(misc_t[:, :, ROPE_LO:ROPE_LO + MLA_ROPE, :], 2, 3)]
    state_s = [jnp.stack([st[j] for st in st_s]) for j in range(9)]
    return (hp.reshape(nb, s, d), hs.reshape(nd, t, d), *state_p, *state_s)
```

```python
import functools

import numpy as np
import jax
import jax.numpy as jnp
from jax import lax
from jax.experimental import pallas as pl
from jax.experimental.pallas import tpu as pltpu

F32 = jnp.float32
BF16 = jnp.bfloat16

CHUNK = 64
HEAD_DIM = 64
N_HEADS = 4
GROUP_W = 256
A_LEFT_CHUNKS = 8
A_WINDOW = A_LEFT_CHUNKS * CHUNK
REL_CLIP = 128
MLA_NOPE = 64
MLA_ROPE = 32
KV_RANK = 128
ROPE_BASE = 10000.0
N_EXP = 32
TOP_K = 4
D_FF = 1024
SWIGLU_LIMIT = 7.0
SWIGLU_ALPHA = 1.702
MOE_BLOCK = 512
NORM_EPS = 1e-6
NEG_INF = -1e30
LOG2E = 1.4426950408889634

A_Q = 0
C_F = 2304
D_Q = 2308
D_CKV = D_Q + N_HEADS * (MLA_NOPE + MLA_ROPE)
D_KR = D_CKV + KV_RANK

LANE = 128
SEG_ABC = 0
SEG_DQ = 2304
SEG_CKV = 2816
SEG_MISC = 2944
N_COLS = 3072
ROPE_LO = 64
ROPE_HALF = MLA_ROPE // 2

VMEM_LIMIT = 56 * 1024 * 1024

PROJ_ROWS = 512
ATTN_TQ = 512
ATTN_TK = 1024
MLA_TQ = 2048
STICK_KS = 256
DENSE_ROWS = 512
PLE_ROWS = 512
MOE_GROUPS = 8
SCATTER_CHUNK = 8192

DEAD_LOG2 = -160.0
BOUND_MARGIN = 1.01


def _cparams(n_axes):
    return pltpu.CompilerParams(dimension_semantics=("arbitrary",) * n_axes,
                                vmem_limit_bytes=VMEM_LIMIT)


def _nt_dot(a, b):
    return lax.dot_general(a, b, (((1,), (1,)), ((), ())), preferred_element_type=F32)


def _dot(a, b):
    return jnp.dot(a, b, preferred_element_type=F32)


def _split2(x):
    hi = x.astype(BF16)
    lo = (x - hi.astype(F32)).astype(BF16)
    return hi, lo


def _in_col_map():
    cols = np.full((N_COLS,), -1, np.int64)
    cols[0:2304] = np.arange(2304)
    for h in range(N_HEADS):
        base = D_Q + (MLA_NOPE + MLA_ROPE) * h
        cols[SEG_DQ + LANE * h: SEG_DQ + LANE * h + MLA_NOPE + MLA_ROPE] = base + np.arange(MLA_NOPE + MLA_ROPE)
    cols[SEG_CKV:SEG_CKV + KV_RANK] = D_CKV + np.arange(KV_RANK)
    cols[SEG_MISC:SEG_MISC + N_HEADS] = C_F + np.arange(N_HEADS)
    cols[SEG_MISC + ROPE_LO:SEG_MISC + ROPE_LO + MLA_ROPE] = D_KR + np.arange(MLA_ROPE)
    return cols


def _rope_tables(pos):
    inv = ROPE_BASE ** (-jnp.arange(ROPE_HALF, dtype=F32) / ROPE_HALF)
    ang = pos.astype(F32)[:, None] * inv
    cos, sin = jnp.cos(ang), jnp.sin(ang)
    n = pos.shape[0]
    one = jnp.ones((n, ROPE_LO), F32)
    z16 = jnp.zeros((n, ROPE_HALF), F32)
    z64 = jnp.zeros((n, ROPE_LO), F32)
    z32 = jnp.zeros((n, LANE - ROPE_LO - MLA_ROPE), F32)
    cos_t = jnp.concatenate([one, cos, cos, z32 + 1.0], axis=1)
    sin_a = jnp.concatenate([z64, -sin, z16, z32], axis=1)
    sin_b = jnp.concatenate([z64, z16, sin, z32], axis=1)
    return cos_t, sin_a, sin_b


def _rope(y, cos_t, sin_a, sin_b):
    left = pltpu.roll(y, LANE - ROPE_HALF, axis=1)
    right = pltpu.roll(y, ROPE_HALF, axis=1)
    return y * cos_t + left * sin_a + right * sin_b


def _head_norm(x, m_ref, gain):
    hi, lo = _split2(x * x)
    ssq = _dot(hi, m_ref[...]) + _dot(lo, m_ref[...])
    return x * lax.rsqrt(ssq * (1.0 / HEAD_DIM) + NORM_EPS) * gain


def _value_lane0(h):
    return HEAD_DIM * (h % 2)


def _ones_lane(h):
    return HEAD_DIM - _value_lane0(h)


def _store_padded_q(q_ref, q, scale):
    lane = lax.broadcasted_iota(jnp.int32, (q.shape[0], LANE), 1)
    low = lane < HEAD_DIM
    qs = q * scale
    for h in range(N_HEADS):
        pair = qs[:, LANE * (h // 2): LANE * (h // 2) + LANE]
        keep = low if h % 2 == 0 else jnp.logical_not(low)
        q_ref[:, LANE * h: LANE * h + LANE] = jnp.where(keep, pair, 0.0).astype(BF16)


def _proj_kernel(*refs, channel_major, n_alias):
    x_ref, gmix_ref, w_ref, cos_ref, sa_ref, sb_ref, gv_ref, m64_ref, wuk_ref, wuv_ref = refs[:10]
    (ak_ref, av_ref, bk_ref, bv_ref, ck_ref, cv_ref, ckv_ref, misc_ref,
     qa_ref, ka_ref, va_ref, qb_ref, kb_ref, vb_ref, qc_ref, kc_ref, vc_ref,
     qd_ref, kd_ref, vd_ref) = refs[10 + n_alias:]

    def put_state(ref, val):
        ref[...] = val.T if channel_major else val

    x = x_ref[...]
    ms = jnp.mean(x * x, axis=-1, keepdims=True)
    hn = (x * lax.rsqrt(ms + NORM_EPS) * gmix_ref[...]).astype(BF16)
    sm_scale = (HEAD_DIM ** -0.5) * LOG2E
    rows = x.shape[0]
    lane = lax.broadcasted_iota(jnp.int32, (rows, LANE), 1)
    is_nope = lane < MLA_NOPE
    is_rope = jnp.logical_and(lane >= ROPE_LO, lane < ROPE_LO + MLA_ROPE)
    cos_t, sin_a, sin_b = cos_ref[...], sa_ref[...], sb_ref[...]

    z_a = _dot(hn, w_ref[:, 0:768])
    z_b = _dot(hn, w_ref[:, 768:1536])
    z_c = _dot(hn, w_ref[:, 1536:2304])
    z_l = _dot(hn, w_ref[:, SEG_CKV:N_COLS])
    z_q = _dot(hn, w_ref[:, SEG_DQ:SEG_CKV])

    z = z_a
    aq = _head_norm(z[:, 0:256], m64_ref, gv_ref[0:1, :])
    ak = _head_norm(z[:, 256:512], m64_ref, gv_ref[1:2, :])
    av = z[:, 512:768]
    ak_ref[...] = ak
    av_ref[...] = av
    _store_padded_q(qa_ref, aq, sm_scale)
    ka_ref[...] = ak.astype(BF16)
    va_ref[...] = av.astype(BF16)

    z = z_b
    put_state(bk_ref, z[:, 256:512])
    put_state(bv_ref, z[:, 512:768])
    _store_padded_q(qb_ref, z[:, 0:256], sm_scale)
    kb_ref[...] = z[:, 256:512].astype(BF16)
    vb_ref[...] = z[:, 512:768].astype(BF16)

    z = z_c
    cq = _head_norm(z[:, 0:256], m64_ref, gv_ref[2:3, :])
    ck = _head_norm(z[:, 256:512], m64_ref, gv_ref[3:4, :])
    cv = z[:, 512:768]
    put_state(ck_ref, ck)
    put_state(cv_ref, cv)
    _store_padded_q(qc_ref, cq, sm_scale)
    kc_ref[...] = ck.astype(BF16)
    vc_ref[...] = cv.astype(BF16)

    z = z_l
    zc = z[:, 0:KV_RANK]
    ckv = zc * lax.rsqrt(jnp.mean(zc * zc, axis=-1, keepdims=True) + NORM_EPS) * gv_ref[7:8, 0:LANE]
    ckv_ref[...] = ckv
    zm = z[:, KV_RANK:2 * KV_RANK]
    ssr = jnp.sum(jnp.where(is_rope, zm * zm, 0.0), axis=-1, keepdims=True)
    kr = zm * lax.rsqrt(ssr * (1.0 / MLA_ROPE) + NORM_EPS) * gv_ref[6:7, 0:LANE]
    kr = _rope(kr, cos_t, sin_a, sin_b)
    zf = zm + gv_ref[8:9, 0:LANE]
    clf = jnp.minimum(zf, 0.0) - jnp.log1p(jnp.exp(-jnp.abs(zf)))
    put_state(misc_ref, jnp.where(lane < N_HEADS, clf, kr))

    ckv_b = ckv.astype(BF16)
    kn = _dot(ckv_b, wuk_ref[...])
    for h in range(N_HEADS):
        hs = slice(LANE * h, LANE * h + LANE)
        ones = (lane == _ones_lane(h)).astype(F32)
        vd_ref[:, hs] = (_dot(ckv_b, wuv_ref[:, hs]) + ones).astype(BF16)
    for h in range(N_HEADS):
        xh = kn[:, LANE * h: LANE * h + LANE]
        ss = jnp.sum(xh * xh, axis=-1, keepdims=True)
        yh = xh * lax.rsqrt(ss * (1.0 / MLA_NOPE) + NORM_EPS) * gv_ref[5:6, 0:LANE]
        kd_ref[:, LANE * h: LANE * h + LANE] = (yh + kr).astype(BF16)

    z = z_q
    d_scale = ((MLA_NOPE + MLA_ROPE) ** -0.5) * LOG2E
    for h in range(N_HEADS):
        xh = z[:, LANE * h: LANE * h + LANE]
        x2 = xh * xh
        ssn = jnp.sum(jnp.where(is_nope, x2, 0.0), axis=-1, keepdims=True)
        ssr = jnp.sum(jnp.where(is_rope, x2, 0.0), axis=-1, keepdims=True)
        rn = lax.rsqrt(ssn * (1.0 / MLA_NOPE) + NORM_EPS)
        rr = lax.rsqrt(ssr * (1.0 / MLA_ROPE) + NORM_EPS)
        yh = xh * jnp.where(is_nope, rn, rr) * gv_ref[4:5, 0:LANE]
        yh = _rope(yh, cos_t, sin_a, sin_b)
        qd_ref[:, LANE * h: LANE * h + LANE] = (yh * d_scale).astype(BF16)


def _proj_weights(w_in, b_forget, qk_gain, rope_gain, kv_gain, w_uk, w_uv):
    cols = _in_col_map()
    valid = jnp.asarray(cols >= 0)
    w = jnp.where(valid[None, :], w_in[:, np.maximum(cols, 0)], 0.0).astype(BF16)

    def tile4(g):
        return jnp.tile(g, N_HEADS)

    gv = jnp.zeros((16, GROUP_W), F32)
    gv = gv.at[0].set(tile4(qk_gain[0])).at[1].set(tile4(qk_gain[1]))
    gv = gv.at[2].set(tile4(qk_gain[2])).at[3].set(tile4(qk_gain[3]))
    gv = gv.at[4, 0:MLA_NOPE].set(qk_gain[4]).at[4, ROPE_LO:ROPE_LO + MLA_ROPE].set(rope_gain[0])
    gv = gv.at[5, 0:MLA_NOPE].set(qk_gain[5])
    gv = gv.at[6, ROPE_LO:ROPE_LO + MLA_ROPE].set(rope_gain[1])
    gv = gv.at[7, 0:KV_RANK].set(kv_gain)
    gv = gv.at[8, 0:N_HEADS].set(b_forget)
    head = np.arange(GROUP_W) // HEAD_DIM
    m64 = jnp.asarray((head[:, None] == head[None, :]).astype(np.float32), BF16)
    wuk = jnp.zeros((KV_RANK, N_HEADS * LANE), F32)
    for h in range(N_HEADS):
        wuk = wuk.at[:, LANE * h: LANE * h + MLA_NOPE].set(w_uk[:, MLA_NOPE * h: MLA_NOPE * (h + 1)])
    wuv = jnp.zeros((KV_RANK, N_HEADS * LANE), F32)
    for h in range(N_HEADS):
        lo = LANE * h + _value_lane0(h)
        wuv = wuv.at[:, lo: lo + HEAD_DIM].set(w_uv[:, HEAD_DIM * h: HEAD_DIM * (h + 1)])
    return w, gv, m64, wuk.astype(BF16), wuv.astype(BF16)


STATE_SLOTS = (2, 3, 4, 5, 6, 7)


def _project(x, gmix, pw, tables, n_tab_blocks, layered=None):
    w, gv, m64, wuk, wuv = pw
    t, d = x.shape
    ts = PROJ_ROWS
    assert t % ts == 0
    row = lambda i: (i, 0)
    full = lambda i: (0, 0)
    tab = lambda i: (i % n_tab_blocks, 0)
    f32_w = [GROUP_W] * 6 + [KV_RANK, LANE]
    bf_w = [512, 256, 256, 512, 256, 256, 512, 256, 256, 512, 512, 512]
    out_shape = [jax.ShapeDtypeStruct((t, c), F32) for c in f32_w] + \
                [jax.ShapeDtypeStruct((t, c), BF16) for c in bf_w]
    out_specs = [pl.BlockSpec((ts, c), row) for c in f32_w + bf_w]
    in_specs = [pl.BlockSpec((ts, d), row), pl.BlockSpec((1, d), full), pl.BlockSpec((d, N_COLS), full),
                pl.BlockSpec((ts, LANE), tab), pl.BlockSpec((ts, LANE), tab), pl.BlockSpec((ts, LANE), tab),
                pl.BlockSpec(gv.shape, full), pl.BlockSpec(m64.shape, full),
                pl.BlockSpec(wuk.shape, full), pl.BlockSpec(wuv.shape, full)]
    operands = [x, gmix.reshape(1, d), w, *tables, gv, m64, wuk, wuv]
    aliases = {}
    if layered is not None:
        depth, layer, nb, earlier = layered
        s = t // nb
        nt = s // ts
        for slot in STATE_SLOTS:
            c = f32_w[slot]
            if slot == 6:
                out_shape[slot] = jax.ShapeDtypeStruct((depth, t, c), F32)
                out_specs[slot] = pl.BlockSpec((None, ts, c), lambda i: (layer, i, 0))
            else:
                out_shape[slot] = jax.ShapeDtypeStruct((depth, nb, c, s), F32)
                out_specs[slot] = pl.BlockSpec((None, None, c, ts), lambda i: (layer, i // nt, 0, i % nt))
        if earlier is not None:
            for k, slot in enumerate(STATE_SLOTS):
                aliases[len(operands)] = slot
                in_specs.append(pl.BlockSpec(memory_space=pl.ANY))
                operands.append(earlier[k])
    kern = functools.partial(_proj_kernel, channel_major=layered is not None, n_alias=len(aliases))
    return pl.pallas_call(
        kern, grid=(t // ts,), in_specs=in_specs, out_specs=out_specs, out_shape=out_shape,
        input_output_aliases=aliases, compiler_params=_cparams(1), name="proj",
    )(*operands)


def _softmax_step(q, kb, vb, carry, bias=None, ok=None):
    m, l, acc = carry
    s = _nt_dot(q, kb)
    if bias is not None:
        s = s + bias
    if ok is not None:
        s = jnp.where(ok, s, NEG_INF)
    m_new = jnp.maximum(m, jnp.max(s, axis=1, keepdims=True))
    alpha = jnp.exp2(m - m_new)
    p = jnp.exp2(s - m_new)
    l = alpha * l + jnp.sum(p, axis=1, keepdims=True)
    acc = alpha * acc + _dot(p.astype(BF16), vb)
    return m_new, l, acc


def _softmax_init(tq):
    return (jnp.full((tq, 1), NEG_INF, F32), jnp.zeros((tq, 1), F32), jnp.zeros((tq, LANE), F32))


def _pick_lane(block, h):
    col = lax.broadcasted_iota(jnp.int32, block.shape, 1)
    return jnp.sum(jnp.where(col == h, block, 0.0), axis=1, keepdims=True)


def _stack_pair(q_ref):
    return jnp.concatenate([q_ref[0, :, 0:LANE], q_ref[0, :, LANE:2 * LANE]], axis=0)


def _store_pair(o_ref, o, tq):
    o_ref[0, :, 0:LANE] = o[:tq].astype(o_ref.dtype)
    o_ref[0, :, LANE:2 * LANE] = o[tq:].astype(o_ref.dtype)


def _flash_forget_kernel(kmax_ref, decay_ref, q_ref, k_ref, v_ref, fq_ref, fk_ref, o_ref, *, tq, n_blk):
    b = pl.program_id(0)
    pair = pl.program_id(1)
    qi = pl.program_id(2)
    q = _stack_pair(q_ref)
    fq = jnp.concatenate([_pick_lane(fq_ref[0], 2 * pair), _pick_lane(fq_ref[0], 2 * pair + 1)], axis=0) * LOG2E

    def step(j, carry, diag):
        off = pl.multiple_of(j * tq, tq)
        kb = k_ref[0, pl.ds(off, tq), :]
        vb = v_ref[0, pl.ds(off, tq), :]
        fk = jnp.concatenate([jnp.broadcast_to(fk_ref[0, 0, pl.ds(j, 1), :], (tq, tq)),
                              jnp.broadcast_to(fk_ref[0, 1, pl.ds(j, 1), :], (tq, tq))], axis=0)
        ok = None
        if diag:
            row = lax.broadcasted_iota(jnp.int32, (2 * tq, tq), 0) % tq
            col = lax.broadcasted_iota(jnp.int32, (2 * tq, tq), 1)
            ok = col <= row
        return _softmax_step(q, kb, vb, carry, fq - fk * LOG2E, ok)

    carry = step(qi, _softmax_init(2 * tq), True)
    qf = q.astype(F32)
    q_norm = jnp.sqrt(jnp.sum(qf * qf, axis=1, keepdims=True))
    head0 = b * N_HEADS + 2 * pair
    room = q_norm * jnp.where(lax.broadcasted_iota(jnp.int32, q_norm.shape, 0) < tq,
                              kmax_ref[head0], kmax_ref[head0 + 1]) + fq - carry[0]
    slack0 = jnp.max(room[:tq]) - DEAD_LOG2
    slack1 = jnp.max(room[tq:]) - DEAD_LOG2

    def live(state):
        j = jnp.maximum(state[0], 0)
        alive = jnp.logical_or(slack0 + decay_ref[head0 * n_blk + j] >= 0.0,
                               slack1 + decay_ref[(head0 + 1) * n_blk + j] >= 0.0)
        return jnp.logical_and(state[0] >= 0, alive)

    def older(state):
        j = state[0]
        return (j - 1,) + step(j, state[1:], False)

    _, m, l, acc = lax.while_loop(live, older, (qi - 1,) + carry)
    _store_pair(o_ref, acc / l, tq)


def _flash_mla_kernel(q_ref, k_ref, v_ref, o_ref, *, tq, tk):
    qi = pl.program_id(2)
    n_sub = tq // tk

    def step(hh, j, carry, row0=None):
        m, acc = carry
        hs = slice(LANE * hh, LANE * hh + LANE)
        off = pl.multiple_of(j * tk, tk)
        qs = q_ref[0, :, hs] if row0 is None else q_ref[0, row0:, hs]
        s = _nt_dot(qs, k_ref[0, pl.ds(off, tk), hs])
        if row0 is not None:
            row = lax.broadcasted_iota(jnp.int32, s.shape, 0)
            col = lax.broadcasted_iota(jnp.int32, s.shape, 1)
            s = jnp.where(col // CHUNK <= row // CHUNK, s, NEG_INF)
        m_new = jnp.maximum(m, jnp.max(s, axis=1, keepdims=True))
        p = jnp.exp2(s - m_new)
        acc = jnp.exp2(m - m_new) * acc + _dot(p.astype(BF16), v_ref[0, pl.ds(off, tk), hs])
        return m_new, acc

    def both(g, carry):
        c0, c1 = carry
        for r in range(n_sub):
            c0, c1 = step(0, g * n_sub + r, c0), step(1, g * n_sub + r, c1)
        return c0, c1

    init = (jnp.full((tq, 1), NEG_INF, F32), jnp.zeros((tq, LANE), F32))
    carry = lax.fori_loop(0, qi, both, (init, init))
    for hh in range(2):
        m, acc = carry[hh]
        for r in range(n_sub):
            row0 = r * tk
            m_r, acc_r = step(hh, qi * n_sub + r, (m[row0:], acc[row0:]), row0)
            m = m_r if r == 0 else jnp.concatenate([m[:row0], m_r], axis=0)
            acc = acc_r if r == 0 else jnp.concatenate([acc[:row0], acc_r], axis=0)
        ones_at = _ones_lane(hh)
        o_ref[0, :, LANE * hh: LANE * hh + LANE] = (acc / acc[:, ones_at:ones_at + 1]).astype(o_ref.dtype)


def _flash_stick_kernel(q_ref, k_ref, v_ref, u_ref, o_ref, *, tq, ks):
    qi = pl.program_id(2)
    q = _stack_pair(q_ref)
    n_sub = tq // ks

    def step(jb, carry, diag):
        c, acc = carry
        off = pl.multiple_of(jb * ks, ks)
        kb = k_ref[0, pl.ds(off, ks), :]
        vb = v_ref[0, pl.ds(off, ks), :]
        vis = None
        if diag:
            row = lax.broadcasted_iota(jnp.int32, (2 * tq, ks), 0) % tq + qi * tq
            col = lax.broadcasted_iota(jnp.int32, (2 * tq, ks), 1) + jb * ks
            vis = col < row
        c_new, w = _stick_weights(_nt_dot(q, kb), c, u_ref[...], vis)
        return c_new, acc + _dot(w.astype(BF16), vb)

    carry = (jnp.zeros((2 * tq, 1), F32), jnp.zeros((2 * tq, LANE), F32))
    for r in range(n_sub):
        carry = step(qi * n_sub + (n_sub - 1 - r), carry, True)

    def live(state):
        return jnp.logical_and(state[0] >= 0, state[1] > DEAD_LOG2)

    def older(state):
        c, acc = step(state[0], state[2:], False)
        return state[0] - 1, jnp.max(c), c, acc

    _, _, _, acc = lax.while_loop(live, older, (qi * n_sub - 1, jnp.max(carry[0])) + carry)
    _store_pair(o_ref, acc, tq)


def _stick_weights(z, c, u, vis):
    sp = jnp.log(1.0 + jnp.exp2(-jnp.abs(z))) * LOG2E
    log_rest = jnp.minimum(-z, 0.0) - sp
    log_beta = log_rest + z
    if vis is not None:
        log_rest = jnp.where(vis, log_rest, 0.0)
    hi, lo = _split2(log_rest)
    between = _dot(hi, u) + _dot(lo, u)
    w = jnp.exp2(log_beta + between + c)
    if vis is not None:
        w = jnp.where(vis, w, 0.0)
    return c + jnp.sum(log_rest, axis=1, keepdims=True), w


def _band_kernel(q_ref, k_ref, v_ref, bd_ref, bp_ref, o_ref, *, tq):
    qi = pl.program_id(2)
    q = _stack_pair(q_ref)
    off = pl.multiple_of(qi * tq, tq)
    carry = _softmax_step(q, k_ref[0, pl.ds(off, tq), :], v_ref[0, pl.ds(off, tq), :],
                          _softmax_init(2 * tq), bd_ref[...].reshape(2 * tq, tq))
    offp = pl.multiple_of(jnp.maximum(qi - 1, 0) * tq, tq)
    no_prev = jnp.where(qi == 0, NEG_INF, 0.0)
    m, l, acc = _softmax_step(q, k_ref[0, pl.ds(offp, tq), :], v_ref[0, pl.ds(offp, tq), :],
                              carry, bp_ref[...].reshape(2 * tq, tq) + no_prev)
    _store_pair(o_ref, acc / l, tq)


def _pair_specs(s, tq):
    q_spec = pl.BlockSpec((1, tq, 2 * LANE), lambda b, p, i: (b, i, p))
    kv_spec = pl.BlockSpec((1, s, LANE), lambda b, p, i: (b, 0, p))
    return q_spec, kv_spec


def _pair_call(kern, nb, s, tq, in_specs, operands, name):
    return pl.pallas_call(
        kern, grid=(nb, N_HEADS // 2, s // tq), in_specs=in_specs,
        out_specs=pl.BlockSpec((1, tq, 2 * LANE), lambda b, p, i: (b, i, p)),
        out_shape=jax.ShapeDtypeStruct((nb, s, N_HEADS * LANE), BF16),
        compiler_params=_cparams(3), name=name,
    )(*operands)


def _strict_upper(n):
    idx = np.arange(n)
    return jnp.asarray((idx[:, None] > idx[None, :]).astype(np.float32), BF16)


def _toeplitz(vec, n_rows, n_cols):
    length = n_rows + n_cols - 1
    assert vec.shape[-1] == length
    lead = vec.shape[:-1]
    rev = jnp.concatenate([vec[..., ::-1], jnp.zeros(lead + (1,), vec.dtype)], axis=-1)
    flat = jnp.tile(rev, (1,) * len(lead) + (n_rows,))[..., :n_rows * length]
    return flat.reshape(lead + (n_rows, length))[..., n_rows - 1: n_rows - 1 + n_cols]


def _rel_bias_tile(rel_bias, n_rows, n_cols, rel00, ok):
    d = np.arange(n_rows + n_cols - 1) - (n_cols - 1) + rel00
    vec = rel_bias.astype(F32)[:, np.clip(d, -REL_CLIP, REL_CLIP) + REL_CLIP] * LOG2E
    return jnp.where(jnp.asarray(ok)[None], _toeplitz(vec, n_rows, n_cols), NEG_INF)


def _band_bias_tiles(rel_bias, tq):
    i = np.arange(tq)[:, None]
    j = np.arange(tq)[None, :]
    own = _rel_bias_tile(rel_bias, tq, tq, 0, (j // CHUNK) <= (i // CHUNK))
    prev = _rel_bias_tile(rel_bias, tq, tq, tq, (j // CHUNK) >= (i // CHUNK) + tq // CHUNK - A_LEFT_CHUNKS)
    return own, prev


def _prompt_attention(pr, misc_t, nb, s, rel_bias, k_gain):
    tq = ATTN_TQ
    assert s % tq == 0 and tq == A_WINDOW
    r3 = lambda a: a.reshape(nb, s, a.shape[-1])
    qa, ka, va, qb, kb, vb, qc, kc, vc, qd, kd, vd = [r3(a) for a in pr[8:20]]
    q_spec, kv_spec = _pair_specs(s, tq)

    bd, bp = _band_bias_tiles(rel_bias, tq)
    b_spec = pl.BlockSpec((2, tq, tq), lambda b, p, i: (p, 0, 0))
    o_a = _pair_call(functools.partial(_band_kernel, tq=tq), nb, s, tq,
                     [q_spec, kv_spec, kv_spec, b_spec, b_spec], (qa, ka, va, bd, bp), "attn_band")

    u = _strict_upper(STICK_KS)
    o_b = _pair_call(functools.partial(_flash_stick_kernel, tq=tq, ks=STICK_KS), nb, s, tq,
                     [q_spec, kv_spec, kv_spec, pl.BlockSpec(u.shape, lambda b, p, i: (0, 0))],
                     (qb, kb, vb, u), "attn_stick")

    n_blk = s // tq
    fk_t = jnp.cumsum(misc_t[:, 0:N_HEADS, :], axis=2)
    c_cum = jnp.swapaxes(fk_t, 1, 2)
    fk = fk_t.reshape(nb, N_HEADS, n_blk, tq)
    kmax = jnp.full((nb * N_HEADS,), HEAD_DIM ** 0.5 * BOUND_MARGIN, F32) * jnp.max(jnp.abs(k_gain))
    decay = lax.cummax(jnp.max(-fk, axis=-1), axis=2) * LOG2E
    smem = pl.BlockSpec(memory_space=pltpu.SMEM)
    fq_spec = pl.BlockSpec((1, tq, N_HEADS), lambda b, p, i: (b, i, 0))
    fk_spec = pl.BlockSpec((1, 2, n_blk, tq), lambda b, p, i: (b, p, 0, 0))
    o_c = _pair_call(functools.partial(_flash_forget_kernel, tq=tq, n_blk=n_blk), nb, s, tq,
                     [smem, smem, q_spec, kv_spec, kv_spec, fq_spec, fk_spec],
                     (kmax, decay.reshape(-1), qc, kc, vc, c_cum, fk), "attn_forget")

    tqd = MLA_TQ if s % MLA_TQ == 0 else tq
    qd_spec = pl.BlockSpec((1, tqd, 2 * LANE), lambda b, p, i: (b, i, p))
    kvd_spec = pl.BlockSpec((1, s, 2 * LANE), lambda b, p, i: (b, 0, p), pipeline_mode=pl.Buffered(1))
    o_d = _pair_call(functools.partial(_flash_mla_kernel, tq=tqd, tk=min(ATTN_TK, tqd)), nb, s, tqd,
                     [qd_spec, kvd_spec, kvd_spec], (qd, kd, vd), "attn_mla")
    return o_a, o_b, o_c, o_d


def _two_block_softmax(s1, s2, v1_t, v2):
    m = jnp.maximum(jnp.max(s1, axis=1, keepdims=True), jnp.max(s2, axis=1, keepdims=True))
    p1 = jnp.exp2(s1 - m)
    p2 = jnp.exp2(s2 - m)
    l = jnp.sum(p1, axis=1, keepdims=True) + jnp.sum(p2, axis=1, keepdims=True)
    return (_nt_dot(p1.astype(BF16), v1_t) + _dot(p2.astype(BF16), v2)) / l


def _sample_kernel(qa_ref, ka_ref, va_ref, qb_ref, kb_ref, vb_ref, qc_ref, kc_ref, vc_ref, qd_ref, kd_ref, vd_ref,
                   cak_ref, cav_ref, cbk_ref, cbv_ref, cck_ref, ccv_ref, cckv_ref, ckr_ref,
                   ba_c_ref, ba_n_ref, fq_ref, fkc_ref, fkn_ref,
                   u_ref, un_ref, wukt_ref, wuv_ref, gk_ref,
                   oa_ref, ob_ref, oc_ref, od_ref, *, t, past, ks):
    row = lax.broadcasted_iota(jnp.int32, (t, t), 0)
    col = lax.broadcasted_iota(jnp.int32, (t, t), 1)
    causal_bias = jnp.where(col <= row, 0.0, NEG_INF)
    chunk_bias = jnp.where((past + col) // CHUNK <= (past + row) // CHUNK, 0.0, NEG_INF)
    strict = col < row

    ckv_c = cckv_ref[0].astype(BF16)
    kr_t = ckr_ref[0].astype(BF16)
    pad_t = jnp.zeros((LANE - MLA_NOPE - MLA_ROPE, past), BF16)

    for h in range(N_HEADS):
        hs = slice(LANE * h, LANE * h + LANE)
        ps = slice(LANE * (h // 2), LANE * (h // 2) + LANE)

        q = qa_ref[0][:, hs]
        s1 = _dot(q, cak_ref[0, ps, :].astype(BF16)) + ba_c_ref[h]
        s2 = _nt_dot(q, ka_ref[0][:, ps]) + ba_n_ref[h]
        oa_ref[0, :, hs] = _two_block_softmax(s1, s2, cav_ref[0, ps, :].astype(BF16), va_ref[0][:, ps])

        q = qb_ref[0][:, hs]
        c, w = _stick_weights(_nt_dot(q, kb_ref[0][:, ps]), jnp.zeros((t, 1), F32), un_ref[...], strict)
        acc = _dot(w.astype(BF16), vb_ref[0][:, ps])

        def b_live(state):
            return jnp.logical_and(state[0] >= 0, state[1] > DEAD_LOG2)

        def b_older(state, q=q, ps=ps):
            jb, _, c, acc = state
            off = pl.multiple_of(jb * ks, ks)
            kb_t = cbk_ref[0, ps, pl.ds(off, ks)].astype(BF16)
            vb_t = cbv_ref[0, ps, pl.ds(off, ks)].astype(BF16)
            c, w = _stick_weights(_dot(q, kb_t), c, u_ref[...], None)
            return jb - 1, jnp.max(c), c, acc + _nt_dot(w.astype(BF16), vb_t)

        _, _, _, acc = lax.while_loop(b_live, b_older, (past // ks - 1, jnp.max(c), c, acc))
        ob_ref[0, :, hs] = acc

        q = qc_ref[0][:, hs]
        fq = fq_ref[0][:, h:h + 1] * LOG2E
        s1 = _dot(q, cck_ref[0, ps, :].astype(BF16)) + (fq - fkc_ref[0, h:h + 1, :] * LOG2E)
        s2 = _nt_dot(q, kc_ref[0][:, ps]) + (fq - fkn_ref[0, h:h + 1, :] * LOG2E) + causal_bias
        oc_ref[0, :, hs] = _two_block_softmax(s1, s2, ccv_ref[0, ps, :].astype(BF16), vc_ref[0][:, ps])

        q = qd_ref[0][:, hs]
        kn_t = _nt_dot(wukt_ref[MLA_NOPE * h: MLA_NOPE * (h + 1), :], ckv_c)
        ss = jnp.sum(kn_t * kn_t, axis=0, keepdims=True)
        kn_t = kn_t * lax.rsqrt(ss * (1.0 / MLA_NOPE) + NORM_EPS) * gk_ref[...]
        k_t = jnp.concatenate([kn_t.astype(BF16), kr_t, pad_t], axis=0)
        s1 = _dot(q, k_t)
        s2 = _nt_dot(q, kd_ref[0][:, hs]) + chunk_bias
        v_c = _dot(ckv_c, wuv_ref[:, hs]).astype(BF16)
        m = jnp.maximum(jnp.max(s1, axis=1, keepdims=True), jnp.max(s2, axis=1, keepdims=True))
        p1 = jnp.exp2(s1 - m)
        p2 = jnp.exp2(s2 - m)
        l = jnp.sum(p1, axis=1, keepdims=True) + jnp.sum(p2, axis=1, keepdims=True)
        od_ref[0, :, hs] = (_dot(p1.astype(BF16), v_c) + _dot(p2.astype(BF16), vd_ref[0][:, hs])) / l


def _channel_major(cache):
    depth, nb, rows = cache.shape[:3]
    return jnp.transpose(cache, (0, 1, 3, 4, 2)).reshape(depth, nb, GROUP_W, rows)


def _sample_attention(pr, caches, layer, pw, w_uk, k_gain, rel_bias, nb, t):
    a_k, a_v, b_k, b_v, c_k, c_v, c_lf, d_ckv, d_kr = caches
    wuv = pw[4]
    past = b_k.shape[2]
    win = a_k.shape[2]
    ks = STICK_KS
    assert past % ks == 0
    r3 = lambda a: a.reshape(nb, t, a.shape[-1])
    news = [r3(a) for a in pr[8:20]]

    qpos = past + np.arange(t)
    kpos = past - win + np.arange(win + t)
    qc, kc = qpos // CHUNK, kpos // CHUNK
    ok = (kc[None, :] <= qc[:, None]) & (kc[None, :] >= qc[:, None] - A_LEFT_CHUNKS)
    ba = _rel_bias_tile(rel_bias, t, win + t, win, ok)
    ba_c, ba_n = ba[:, :, :win], ba[:, :, win:]

    clf_new = jnp.swapaxes(r3(pr[7])[..., 0:N_HEADS], 1, 2)
    lf_c = jnp.swapaxes(c_lf[layer], 1, 2).astype(F32)
    fk = jnp.cumsum(jnp.concatenate([lf_c, clf_new], axis=2), axis=2)
    fk_c, fk_n = fk[:, :, :past], fk[:, :, past:]
    fq = jnp.swapaxes(fk_n, 1, 2)

    wukt = w_uk.T.astype(BF16)
    gk = k_gain.reshape(MLA_NOPE, 1)
    u, un = _strict_upper(ks), _strict_upper(t)

    per_b = lambda shape: pl.BlockSpec((1,) + shape, lambda b: (b,) + (0,) * len(shape))
    per_lb = lambda shape: pl.BlockSpec((None, 1) + shape, lambda b: (layer, b) + (0,) * len(shape))
    const = lambda a: pl.BlockSpec(a.shape, lambda b: (0,) * a.ndim)
    cache_ops = [_channel_major(c) for c in (a_k, a_v, b_k, b_v, c_k, c_v)] + [d_ckv, jnp.swapaxes(d_kr, 2, 3)]
    operands = news + cache_ops + [ba_c, ba_n, fq, fk_c, fk_n, u, un, wukt, wuv, gk]
    in_specs = [per_b(a.shape[1:]) for a in news] + [per_lb(a.shape[2:]) for a in cache_ops] + \
               [const(ba_c), const(ba_n)] + [per_b(a.shape[1:]) for a in (fq, fk_c, fk_n)] + \
               [const(a) for a in (u, un, wukt, wuv, gk)]
    out_shape = [jax.ShapeDtypeStruct((nb, t, N_HEADS * LANE), F32)] * 4
    out_specs = [per_b((t, N_HEADS * LANE))] * 4
    return pl.pallas_call(
        functools.partial(_sample_kernel, t=t, past=past, ks=ks), grid=(nb,),
        in_specs=in_specs, out_specs=out_specs, out_shape=out_shape,
        compiler_params=_cparams(1), name="attn_sample",
    )(*operands)


def _out_kernel(oa_ref, ob_ref, oc_ref, od_ref, h_ref, gg_ref, wo_ref, nf_ref,
                wr1_ref, wr2_ref, br_ref, tri_ref, cnt0_ref, h1_ref, hn_ref, route_ref, cnt_ref):
    rows = h_ref.shape[0]
    low = lax.broadcasted_iota(jnp.int32, (rows, LANE), 1) < HEAD_DIM
    h1 = h_ref[...]
    for g, o_ref in enumerate((oa_ref, ob_ref, oc_ref, od_ref)):
        p0 = jnp.where(low, o_ref[:, 0:LANE], o_ref[:, LANE:2 * LANE])
        p1 = jnp.where(low, o_ref[:, 2 * LANE:3 * LANE], o_ref[:, 3 * LANE:4 * LANE])
        og = jnp.concatenate([p0, p1], axis=1).astype(F32)
        ms = jnp.mean(og * og, axis=-1, keepdims=True)
        y = (og * lax.rsqrt(ms + NORM_EPS) * gg_ref[g:g + 1, :]).astype(BF16)
        h1 = h1 + _dot(y, wo_ref[GROUP_W * g: GROUP_W * (g + 1), :])
    h1_ref[...] = h1
    ms = jnp.mean(h1 * h1, axis=-1, keepdims=True)
    hn = h1 * lax.rsqrt(ms + NORM_EPS) * nf_ref[...]
    hn_ref[...] = hn.astype(BF16)
    a1, a2 = _split2(hn)
    w1, w2 = wr1_ref[...], wr2_ref[...]
    logits = _dot(a1, w1) + _dot(a1, w2) + _dot(a2, w1) + br_ref[...]

    @pl.when(pl.program_id(0) == 0)
    def _():
        cnt_ref[...] = cnt0_ref[...]

    lane = lax.broadcasted_iota(jnp.int32, (rows, LANE), 1)
    x = jnp.where(lane < N_EXP, logits, NEG_INF)
    picks, tops, ids = [], [], []
    for j in range(TOP_K):
        top = jnp.max(x, axis=1, keepdims=True)
        idx = jnp.min(jnp.where(x == top, lane, LANE), axis=1, keepdims=True)
        hit = lane == idx
        x = jnp.where(hit, NEG_INF, x)
        picks.append(hit)
        tops.append(top)
        ids.append(idx)
    chosen = functools.reduce(jnp.logical_or, picks)
    before = _dot(tri_ref[...], jnp.where(chosen, 1.0, 0.0).astype(BF16)) + cnt_ref[...]
    exps = [jnp.exp(top - tops[0]) for top in tops]
    total = functools.reduce(jnp.add, exps)
    route = jnp.zeros((rows, LANE), F32)
    for j in range(TOP_K):
        rank = jnp.sum(jnp.where(picks[j], before, 0.0), axis=1, keepdims=True)
        route = jnp.where(lane == j, ids[j].astype(F32), route)
        route = jnp.where(lane == TOP_K + j, exps[j] / total, route)
        route = jnp.where(lane == 2 * TOP_K + j, rank, route)
    route_ref[...] = route
    cnt_ref[...] = cnt_ref[...] + jnp.sum(jnp.where(chosen, 1.0, 0.0), axis=0, keepdims=True)


def _out_project(outs, h, group_gain, w_out_b, norm_ffn, wr_parts, br_pad, counts0):
    t, d = h.shape
    ts = DENSE_ROWS
    assert t % ts == 0
    row = lambda i: (i, 0)
    full = lambda i: (0, 0)
    idx = np.arange(ts)
    tri = jnp.asarray((idx[:, None] > idx[None, :]).astype(np.float32), BF16)
    o_spec = pl.BlockSpec((ts, N_HEADS * LANE), row)
    in_specs = [o_spec] * 4 + [pl.BlockSpec((ts, d), row), pl.BlockSpec(group_gain.shape, full),
                               pl.BlockSpec(w_out_b.shape, full), pl.BlockSpec((1, d), full)] + \
               [pl.BlockSpec((d, LANE), full)] * len(wr_parts) + [pl.BlockSpec((1, LANE), full)] + \
               [pl.BlockSpec((ts, ts), full), pl.BlockSpec((1, LANE), full)]
    out_shape = [jax.ShapeDtypeStruct((t, d), F32), jax.ShapeDtypeStruct((t, d), BF16),
                 jax.ShapeDtypeStruct((t, LANE), F32), jax.ShapeDtypeStruct((1, LANE), F32)]
    out_specs = [pl.BlockSpec((ts, d), row), pl.BlockSpec((ts, d), row), pl.BlockSpec((ts, LANE), row),
                 pl.BlockSpec((1, LANE), full)]
    return pl.pallas_call(
        _out_kernel, grid=(t // ts,), in_specs=in_specs, out_specs=out_specs, out_shape=out_shape,
        compiler_params=_cparams(1), name="out_proj",
    )(*outs, h, group_gain, w_out_b, norm_ffn.reshape(1, d), *wr_parts, br_pad, tri, counts0)


def _expert_kernel(be_ref, fe_ref, ne_ref, nu_ref, x_ref, wu_hbm, bu_ref, wd_hbm, bd_ref, *rest, layer):
    y_ref, wu_stage, wd_stage, wub_ref, wdb_ref, sems = rest[-6:]
    i = pl.program_id(0)
    used = i < nu_ref[0]

    def weight_copies(expert):
        return (pltpu.make_async_copy(wu_hbm.at[layer, expert], wu_stage, sems.at[0]),
                pltpu.make_async_copy(wd_hbm.at[layer, expert], wd_stage, sems.at[1]))

    @pl.when(jnp.logical_and(used, fe_ref[i] == 1))
    def _():
        @pl.when(i == 0)
        def _():
            for cp in weight_copies(be_ref[0]):
                cp.start()
        for cp in weight_copies(be_ref[i]):
            cp.wait()
        wub_ref[...] = wu_stage[...].astype(BF16)
        wdb_ref[...] = wd_stage[...].astype(BF16)

        @pl.when(ne_ref[i] >= 0)
        def _():
            for cp in weight_copies(ne_ref[i]):
                cp.start()

    @pl.when(used)
    def _():
        u = _dot(x_ref[...], wub_ref[...]) + bu_ref[0, 0]
        glu = jnp.minimum(u[:, :D_FF], SWIGLU_LIMIT)
        lin = jnp.clip(u[:, D_FF:], -SWIGLU_LIMIT, SWIGLU_LIMIT)
        act = glu * jax.nn.sigmoid(SWIGLU_ALPHA * glu) * (lin + 1.0)
        y_ref[...] = (_dot(act.astype(BF16), wdb_ref[...]) + bd_ref[0, 0]).astype(y_ref.dtype)


def _expert_ffn(y_prev, blk0, n_blk_all, x_rows, blk_exp, blk_first, blk_next, n_used, layer,
                w_up, b_up, w_down, b_down):
    rows, d = x_rows.shape
    n_blk = rows // MOE_BLOCK
    last = lambda i, nu: jnp.maximum(jnp.minimum(i, nu[0] - 1), 0)
    b_idx = lambda i, be, fe, ne, nu: (layer, be[last(i, nu)], 0, 0)
    hbm = pl.BlockSpec(memory_space=pl.ANY)
    in_specs = [pl.BlockSpec((MOE_BLOCK, d), lambda i, be, fe, ne, nu: (last(i, nu), 0)),
                hbm, pl.BlockSpec((1, 1, 1, 2 * D_FF), b_idx), hbm, pl.BlockSpec((1, 1, 1, d), b_idx)]
    depth = w_up.shape[0]
    operands = [blk_exp, blk_first, blk_next, n_used, x_rows, w_up, b_up.reshape(depth, N_EXP, 1, 2 * D_FF),
                w_down, b_down.reshape(depth, N_EXP, 1, d)]
    aliases = {}
    if y_prev is not None:
        in_specs.append(hbm)
        aliases = {len(operands): 0}
        operands.append(y_prev)
    grid_spec = pltpu.PrefetchScalarGridSpec(
        num_scalar_prefetch=4, grid=(n_blk,), in_specs=in_specs,
        out_specs=pl.BlockSpec((MOE_BLOCK, d), lambda i, be, fe, ne, nu: (blk0 + last(i, nu), 0)),
        scratch_shapes=[pltpu.VMEM((d, 2 * D_FF), F32), pltpu.VMEM((D_FF, d), F32),
                        pltpu.VMEM((d, 2 * D_FF), BF16), pltpu.VMEM((D_FF, d), BF16),
                        pltpu.SemaphoreType.DMA((2,))])
    return pl.pallas_call(
        functools.partial(_expert_kernel, layer=layer), grid_spec=grid_spec,
        out_shape=jax.ShapeDtypeStruct((n_blk_all * MOE_BLOCK, d), BF16), input_output_aliases=aliases,
        compiler_params=_cparams(1), name="expert_ffn",
    )(*operands)


SCATTER_UNROLL = 8


def _row_token_kernel(pad_ref, dest_ref, rt_ref, *, chunk):
    i = pl.program_id(0)

    @pl.when(i == 0)
    def _():
        for e in range(N_EXP + 1):
            def clear(r, carry):
                rt_ref[r] = 0
                return carry
            lax.fori_loop(pad_ref[2 * e], pad_ref[2 * e + 1], clear, 0)

    def place(g, tok):
        k = g * SCATTER_UNROLL
        for j in range(SCATTER_UNROLL):
            rt_ref[dest_ref[k + j]] = tok + j // TOP_K
        return tok + SCATTER_UNROLL // TOP_K
    lax.fori_loop(0, chunk // SCATTER_UNROLL, place, i * (chunk // TOP_K))


def _row_tokens(dest, pad_ranges, n_rows):
    n = dest.shape[0]
    chunk = int(np.gcd(n, SCATTER_CHUNK))
    assert chunk % SCATTER_UNROLL == 0 and SCATTER_UNROLL % TOP_K == 0
    smem = pl.BlockSpec(memory_space=pltpu.SMEM)
    return pl.pallas_call(
        functools.partial(_row_token_kernel, chunk=chunk), grid=(n // chunk,),
        in_specs=[smem, pl.BlockSpec((chunk,), lambda i: (i,), memory_space=pltpu.SMEM)],
        out_specs=smem, out_shape=jax.ShapeDtypeStruct((n_rows,), jnp.int32),
        compiler_params=_cparams(1), name="row_tokens",
    )(pad_ranges, dest)


def _moe(hn, route, counts, layer, w_up, b_up, w_down, b_down):
    n_tok, d = hn.shape
    top_i = route[:, 0:TOP_K].astype(jnp.int32)
    gates = route[:, TOP_K:2 * TOP_K]
    rank = route[:, 2 * TOP_K:3 * TOP_K].astype(jnp.int32)
    n = n_tok * TOP_K
    padded = (counts + MOE_BLOCK - 1) // MOE_BLOCK * MOE_BLOCK
    p_end = jnp.cumsum(padded)
    p_start = p_end - padded
    dest = (p_start[top_i] + rank).reshape(-1)
    n_blk = -(-n // MOE_BLOCK) + N_EXP
    rows = n_blk * MOE_BLOCK
    pad_lo = jnp.concatenate([p_start + counts, p_end[-1:]])
    pad_hi = jnp.concatenate([p_end, jnp.full((1,), rows, p_end.dtype)])
    pad_ranges = jnp.stack([pad_lo, pad_hi], axis=1).reshape(-1).astype(jnp.int32)
    row_tok = _row_tokens(dest.astype(jnp.int32), pad_ranges, rows)
    blk_start = jnp.arange(n_blk, dtype=jnp.int32) * MOE_BLOCK
    blk_exp = jnp.sum((p_end[None, :] <= blk_start[:, None]).astype(jnp.int32), axis=1)
    blk_exp = jnp.minimum(blk_exp, N_EXP - 1)
    n_used = (p_end[-1] // MOE_BLOCK).astype(jnp.int32)
    n_grp = MOE_GROUPS if n_blk % MOE_GROUPS == 0 else 1
    nbg = n_blk // n_grp
    y = None
    for g in range(n_grp):
        be = blk_exp[g * nbg:(g + 1) * nbg]
        fe = jnp.concatenate([jnp.ones((1,), jnp.int32), (be[1:] != be[:-1]).astype(jnp.int32)])
        nu = n_used - g * nbg
        nxt = jnp.sum((be[None, :] <= be[:, None]).astype(jnp.int32), axis=1)
        ne = jnp.where(nxt < jnp.minimum(nu, nbg), be[jnp.minimum(nxt, nbg - 1)], -1).astype(jnp.int32)
        x_rows = hn.at[row_tok[g * nbg * MOE_BLOCK:(g + 1) * nbg * MOE_BLOCK]].get(mode='promise_in_bounds')
        y = _expert_ffn(y, g * nbg, n_blk, x_rows, be, fe, ne, nu.reshape(1), layer, w_up, b_up, w_down, b_down)
    return y, dest.reshape(n_tok, TOP_K), gates


def _pick_rows(y, dest, row0, n_rows):
    return y.at[dest[row0:row0 + n_rows].T].get(mode='promise_in_bounds')


def _ple_kernel(h_ref, y_ref, g_ref, p_ref, np_ref, wg_ref, wp_ref, *rest):
    o_ref = rest[-1]
    h2 = h_ref[...]
    g = g_ref[...]
    for j in range(TOP_K):
        h2 = h2 + y_ref[j].astype(F32) * g[:, j:j + 1]
    ms = jnp.mean(h2 * h2, axis=-1, keepdims=True)
    hn = (h2 * lax.rsqrt(ms + NORM_EPS) * np_ref[...]).astype(BF16)
    gate = jax.nn.sigmoid(_dot(hn, wg_ref[...]))
    o_ref[...] = h2 + gate * _dot(p_ref[...].astype(BF16), wp_ref[...])


def _ple(h1, row0, picked, gates, p, layer, norm_ple, wg_b, wp_b, out_prev=None):
    t, d = h1.shape
    n = gates.shape[0]
    ts = PLE_ROWS
    assert n % ts == 0 and row0 % ts == 0
    off = row0 // ts
    shifted = lambda i: (i + off, 0)
    full = lambda i: (0, 0)
    in_specs = [pl.BlockSpec((ts, d), shifted), pl.BlockSpec((TOP_K, ts, d), lambda i: (0, i, 0)),
                pl.BlockSpec((ts, TOP_K), lambda i: (i, 0)),
                pl.BlockSpec((None, ts, p.shape[2]), lambda i: (layer, i + off, 0)),
                pl.BlockSpec((1, d), full), pl.BlockSpec(wg_b.shape, full), pl.BlockSpec(wp_b.shape, full)]
    operands = [h1, picked, gates, p, norm_ple.reshape(1, d), wg_b, wp_b]
    aliases = {}
    if out_prev is not None:
        aliases = {len(operands): 0}
        in_specs.append(pl.BlockSpec(memory_space=pl.ANY))
        operands.append(out_prev)
    return pl.pallas_call(
        _ple_kernel, grid=(n // ts,), in_specs=in_specs,
        out_specs=pl.BlockSpec((ts, d), shifted), out_shape=jax.ShapeDtypeStruct((t, d), F32),
        input_output_aliases=aliases, compiler_params=_cparams(1), name="ple",
    )(*operands)


def kernel(x_prompt, x_sample, p_prompt, p_sample, cache_a_k, cache_a_v, cache_b_k, cache_b_v, cache_c_k, cache_c_v, cache_c_logf, cache_d_ckv, cache_d_krope, norm_mix, w_in, b_forget, qk_gain, rope_gain, kv_gain, w_uk, w_uv, rel_bias, group_gain, w_out, norm_ffn, w_router, b_router, w_up, b_up, w_down, b_down, norm_ple, w_ple_gate, w_ple_proj):
    nb, s, d = x_prompt.shape
    nd, t, _ = x_sample.shape
    depth = w_in.shape[0]
    past = cache_b_k.shape[2]
    assert PROJ_ROWS % t == 0 and s % PROJ_ROWS == 0

    tab_p = _rope_tables(jnp.arange(s))
    tab_s = _rope_tables(past + jnp.arange(PROJ_ROWS) % t)
    hp = x_prompt.reshape(nb * s, d)
    hs = x_sample.reshape(nd * t, d)
    st_p, st_s = [], []
    keep = min(A_WINDOW, s)
    layered_state = None
    for i in range(depth):
        pw = _proj_weights(w_in[i], b_forget[i], qk_gain[i], rope_gain[i], kv_gain[i], w_uk[i], w_uv[i])
        pr_p = _project(hp, norm_mix[i], pw, tab_p, s // PROJ_ROWS, (depth, i, nb, layered_state))
        layered_state = [pr_p[slot] for slot in STATE_SLOTS]
        pr_s = _project(hs, norm_mix[i], pw, tab_s, 1)
        outs_p = _prompt_attention(pr_p, pr_p[7][i], nb, s, rel_bias[i], qk_gain[i, 3])
        caches = (cache_a_k, cache_a_v, cache_b_k, cache_b_v, cache_c_k, cache_c_v,
                  cache_c_logf, cache_d_ckv, cache_d_krope)
        outs_s = _sample_attention(pr_s, caches, i, pw, w_uk[i], qk_gain[i, 5], rel_bias[i], nd, t)

        def heads(a, n, rows):
            return a.reshape(n, rows, N_HEADS, HEAD_DIM)

        p3 = lambda a: a.reshape(nb, s, a.shape[-1])
        s3 = lambda a: a.reshape(nd, t, a.shape[-1])
        st_p.append((heads(p3(pr_p[0])[:, s - keep:], nb, keep), heads(p3(pr_p[1])[:, s - keep:], nb, keep)))
        ka_all = jnp.concatenate([cache_a_k[i], heads(pr_s[0], nd, t)], axis=1)[:, t:]
        va_all = jnp.concatenate([cache_a_v[i], heads(pr_s[1], nd, t)], axis=1)[:, t:]
        st_s.append((ka_all, va_all,
                     heads(pr_s[2], nd, t), heads(pr_s[3], nd, t), heads(pr_s[4], nd, t), heads(pr_s[5], nd, t),
                     s3(pr_s[7])[..., 0:N_HEADS], s3(pr_s[6]), s3(pr_s[7])[..., ROPE_LO:ROPE_LO + MLA_ROPE]))

        w_out_b = w_out[i].astype(BF16)
        wr_pad = jnp.zeros((d, LANE), F32).at[:, 0:N_EXP].set(w_router[i])
        wr_parts = _split2(wr_pad)
        br_pad = jnp.zeros((1, LANE), F32).at[0, 0:N_EXP].set(b_router[i])
        flat4 = lambda o: o.reshape(-1, N_HEADS * LANE)
        h1_p, hn_p, rt_p, cnt = _out_project([flat4(o) for o in outs_p], hp, group_gain[i], w_out_b, norm_ffn[i],
                                             wr_parts, br_pad, jnp.zeros((1, LANE), F32))
        h1_s, hn_s, rt_s, cnt = _out_project([flat4(o) for o in outs_s], hs, group_gain[i], w_out_b, norm_ffn[i],
                                             wr_parts, br_pad, cnt)

        hn_all = jnp.concatenate([hn_p, hn_s], axis=0)
        rt_all = jnp.concatenate([rt_p, rt_s], axis=0)
        y, dest, gates = _moe(hn_all, rt_all, cnt[0, 0:N_EXP].astype(jnp.int32), i, w_up, b_up, w_down, b_down)

        wg_b = w_ple_gate[i].astype(BF16)
        wp_b = w_ple_proj[i].astype(BF16)
        pp = p_prompt.reshape(depth, nb * s, -1)
        hp = None
        for b in range(nb):
            hp = _ple(h1_p, b * s, _pick_rows(y, dest, b * s, s), gates[b * s:(b + 1) * s], pp, i,
                      norm_ple[i], wg_b, wp_b, hp)
        hs = _ple(h1_s, 0, _pick_rows(y, dest, nb * s, nd * t), gates[nb * s:], p_sample.reshape(depth, nd * t, -1),
                  i, norm_ple[i], wg_b, wp_b)

    bk_t, bv_t, ck_t, cv_t, ckv_all, misc_t = layered_state

    def from_channel_major(a):
        return jnp.transpose(a.reshape(depth, nb, N_HEADS, HEAD_DIM, s), (0, 1, 4, 2, 3))

    state_p = [jnp.stack([st[j] for st in st_p]) for j in range(2)] + \
              [from_channel_major(a) for a in (bk_t, bv_t, ck_t, cv_t)] + \
              [jnp.swapaxes(misc_t[:, :, 0:N_HEADS, :], 2, 3), ckv_all.reshape(depth, nb, s, KV_RANK),
               jnp.swapaxes(misc_t[:, :, ROPE_LO:ROPE_LO + MLA_ROPE, :], 2, 3)]
    state_s = [jnp.stack([st[j] for st in st_s]) for j in range(9)]
    return (hp.reshape(nb, s, d), hs.reshape(nd, t, d), *state_p, *state_s)
```
